```python
import jax
import jax.numpy as jnp
from jax import lax
import numpy as np

D_MODEL = 2048
BATCH = 4
SEQ = 4096
DEPTH = 2

GRID_W = 64
CTX_LEN = 256

MLA_HEADS = 8
QK_NOPE = 128
QK_ROPE = 64
V_HEAD = 128
Q_LORA = 512
KV_LORA = 512
ROPE_THETA = 10000.0
ROPE_FREQS = QK_ROPE // 4
ATTN_SCALE = (QK_NOPE + QK_ROPE) ** -0.5
Q_BLOCK = 128

M_HEADS = 4
M_QK = 64
M_V = 128
M_CHUNK = 64
FORGET_BIAS = 3.0

CONV_WIDTH = 512
CONV_K = 3

N_EXPERTS = 32
TOP_K = 4
D_FF_EXPERT = 1024
SWIGLU_ALPHA = 1.702
SWIGLU_LIMIT = 7.0

N_BRANCH = 3
N_MOD = 6
EPS = 1e-6

SPLIT_SIZES = (Q_LORA, KV_LORA, QK_ROPE, M_HEADS * M_QK, M_HEADS * M_QK, M_HEADS * M_V, M_HEADS * M_V,
               2 * 2 * M_HEADS, CONV_WIDTH, CONV_WIDTH, CONV_WIDTH, N_BRANCH * D_MODEL)
SPLIT_POINTS = tuple(int(s) for s in np.cumsum(SPLIT_SIZES)[:-1])
D_IN = int(sum(SPLIT_SIZES))

kernel_name = 'hybrid_mla_mlstm_shortconv_moe_diffusion_block'


def rmsnorm(x, g):
    xf = x.astype(jnp.float32)
    y = xf * lax.rsqrt(jnp.mean(xf * xf, axis=-1, keepdims=True) + EPS)
    return (y * g.astype(jnp.float32)).astype(x.dtype)


def adaln_params(cond, w_mod, b_mod):
    m = jax.nn.silu(cond) @ w_mod + b_mod
    return jnp.split(m[..., None, :], N_MOD, axis=-1)


def modulate(x, g, shift, scale):
    return rmsnorm(x, g) * (1.0 + scale) + shift


def rope_half(x, ang):
    x1, x2 = jnp.split(x, 2, axis=-1)
    cos, sin = jnp.cos(ang), jnp.sin(ang)
    return jnp.concatenate([x1 * cos - x2 * sin, x1 * sin + x2 * cos], axis=-1)


def rope_2d(x, ang_row, ang_col):
    xf = x.astype(jnp.float32)
    xr, xc = jnp.split(xf, 2, axis=-1)
    return jnp.concatenate([rope_half(xr, ang_row), rope_half(xc, ang_col)], axis=-1).astype(x.dtype)


def mla_project(z_cq, z_ckv, z_kr, g_q, g_kv, w_q_up, w_kv_up):
    b, t, _ = z_cq.shape
    q = (rmsnorm(z_cq, g_q) @ w_q_up).reshape(b, t, MLA_HEADS, QK_NOPE + QK_ROPE).transpose(0, 2, 1, 3)
    kv = (rmsnorm(z_ckv, g_kv) @ w_kv_up).reshape(b, t, MLA_HEADS, QK_NOPE + V_HEAD).transpose(0, 2, 1, 3)
    return q, kv[..., :QK_NOPE], z_kr[:, None], kv[..., QK_NOPE:]


def assemble_keys(k_nope, k_rope):
    k_rope = jnp.broadcast_to(k_rope, k_nope.shape[:-1] + (QK_ROPE,))
    return jnp.concatenate([k_nope, k_rope], axis=-1)


def softmax_attend(q, k, v):
    s = jnp.einsum('bhqd,bhkd->bhqk', q, k).astype(jnp.float32) * ATTN_SCALE
    p = jax.nn.softmax(s, axis=-1)
    return jnp.einsum('bhqk,bhkd->bhqd', p.astype(v.dtype), v)


def blocked_attend(q, k, v):
    b, h, t, dq = q.shape
    nb = t // Q_BLOCK
    qb = q.reshape(b, h, nb, Q_BLOCK, dq).transpose(2, 0, 1, 3, 4)
    ob = lax.map(lambda qi: softmax_attend(qi, k, v), qb)
    return ob.transpose(1, 2, 0, 3, 4).reshape(b, h, t, v.shape[-1])


def merge_heads(o):
    b, h, t, d = o.shape
    return o.transpose(0, 2, 1, 3).reshape(b, t, h * d)


def mlstm_project(mq, mk, mv, mif, b_if):
    b, t, _ = mq.shape
    def heads(a, d):
        return a.reshape(b, t, M_HEADS, d).transpose(0, 2, 1, 3).astype(jnp.float32)
    q = heads(mq, M_QK) * (M_QK ** -0.5)
    k = heads(mk, M_QK)
    v = heads(mv, M_V)
    pre = (mif.reshape(b, t, 2, 2, M_HEADS) + b_if).astype(jnp.float32).transpose(2, 3, 0, 4, 1)
    log_i = pre[:, 0]
    log_f = jax.nn.log_sigmoid(pre[:, 1])
    return q, k, v, log_i, log_f


def zero_state(b):
    return (jnp.zeros((b, M_HEADS, M_V, M_QK), jnp.float32),
            jnp.zeros((b, M_HEADS, M_QK), jnp.float32),
            jnp.zeros((b, M_HEADS), jnp.float32))


def mlstm_scan(q, k, v, log_i, log_f, state):
    b, h, t, dk = q.shape
    dv = v.shape[-1]
    nc = t // M_CHUNK
    def to_chunks(a):
        return jnp.moveaxis(a.reshape((b, h, nc, M_CHUNK) + a.shape[3:]), 2, 0)
    tril = jnp.tril(jnp.ones((M_CHUNK, M_CHUNK), dtype=bool))
    def step(carry, inp):
        c_prev, n_prev, m_prev = carry
        qc, kc, vc, li, lf = inp
        bcum = jnp.cumsum(lf, axis=-1)
        dlog = jnp.where(tril, bcum[..., :, None] - bcum[..., None, :] + li[..., None, :], -jnp.inf)
        inter = bcum + m_prev[..., None]
        m_t = jnp.maximum(inter, jnp.max(dlog, axis=-1))
        s = jnp.einsum('bhtk,bhsk->bhts', qc, kc) * jnp.exp(dlog - m_t[..., None])
        a = jnp.exp(inter - m_t)
        num = jnp.einsum('bhts,bhsv->bhtv', s, vc) + a[..., None] * jnp.einsum('bhtk,bhvk->bhtv', qc, c_prev)
        den = jnp.sum(s, axis=-1) + a * jnp.einsum('bhtk,bhk->bht', qc, n_prev)
        hc = num / jnp.maximum(jnp.abs(den), jnp.exp(-m_t))[..., None]
        b_last = bcum[..., -1]
        w_s = b_last[..., None] - bcum + li
        m_new = jnp.maximum(b_last + m_prev, jnp.max(w_s, axis=-1))
        decay = jnp.exp(b_last + m_prev - m_new)
        e = jnp.exp(w_s - m_new[..., None])
        c_new = decay[..., None, None] * c_prev + jnp.einsum('bhs,bhsv,bhsk->bhvk', e, vc, kc)
        n_new = decay[..., None] * n_prev + jnp.einsum('bhs,bhsk->bhk', e, kc)
        return (c_new, n_new, m_new), hc
    carry, hs = lax.scan(step, state, (to_chunks(q), to_chunks(k), to_chunks(v), to_chunks(log_i), to_chunks(log_f)))
    return jnp.moveaxis(hs, 0, 2).reshape(b, h, t, dv), carry


def mlstm_dir(q, k, v, log_i, log_f, state, reverse):
    if reverse:
        q, k, v = jnp.flip(q, 2), jnp.flip(k, 2), jnp.flip(v, 2)
        log_i, log_f = jnp.flip(log_i, -1), jnp.flip(log_f, -1)
    h, st = mlstm_scan(q, k, v, log_i, log_f, state)
    if reverse:
        h = jnp.flip(h, 2)
    return h, st


def mlstm_output(h, mo, g_mlstm):
    b, nh, t, dv = h.shape
    hn = rmsnorm(h.transpose(0, 2, 1, 3), g_mlstm.reshape(M_HEADS, M_V))
    return hn.reshape(b, t, nh * dv).astype(mo.dtype) * jax.nn.sigmoid(mo)


def short_conv(cu, cb, cc, w_conv, b_conv):
    u = cc * cu
    p = jnp.pad(u, ((0, 0), (1, 1), (0, 0)))
    y = p[:, :-2] * w_conv[0] + p[:, 1:-1] * w_conv[1] + p[:, 2:] * w_conv[2] + b_conv
    return cb * y


def merge_branches(cg, a, m, s, w_proj_a, w_proj_b, w_proj_c, w_o):
    ga, gm, gs = jnp.split(jax.nn.sigmoid(cg), N_BRANCH, axis=-1)
    return (ga * (a @ w_proj_a) + gm * (m @ w_proj_b) + gs * (s @ w_proj_c)) @ w_o


def moe(tokens, w_router, b_router, w_up, b_up, w_down, b_down):
    logits = (tokens @ w_router + b_router).astype(jnp.float32)
    top_v, top_i = lax.top_k(logits, TOP_K)
    w = jax.nn.softmax(top_v, axis=-1)
    combine = jnp.sum(jax.nn.one_hot(top_i, N_EXPERTS, dtype=jnp.float32) * w[..., None], axis=1)
    acc = jnp.zeros(tokens.shape, jnp.float32)
    for e in range(N_EXPERTS):
        gu = tokens @ w_up[e] + b_up[e]
        g, u = jnp.split(gu, 2, axis=-1)
        g = jnp.minimum(g, SWIGLU_LIMIT)
        u = jnp.clip(u, -SWIGLU_LIMIT, SWIGLU_LIMIT)
        hid = g * jax.nn.sigmoid(SWIGLU_ALPHA * g) * (u + 1.0)
        acc = acc + combine[:, e:e + 1] * (hid @ w_down[e] + b_down[e]).astype(jnp.float32)
    return acc.astype(tokens.dtype)


def token_mixers(hl, hc, w_in, g_q_lora, g_kv_lora, w_q_up, w_kv_up, b_if, g_mlstm, w_conv, b_conv,
                 w_proj_a, w_proj_b, w_proj_c, w_o, ang_row, ang_col, with_ctx_out):
    cq_l, ckv_l, kr_l, mq_l, mk_l, mv_l, mo_l, mif_l, cu_l, cb_l, cc_l, cg_l = jnp.split(hl @ w_in, SPLIT_POINTS, axis=-1)
    cq_c, ckv_c, kr_c, mq_c, mk_c, mv_c, mo_c, mif_c, cu_c, cb_c, cc_c, cg_c = jnp.split(hc @ w_in, SPLIT_POINTS, axis=-1)
    ql, kl_nope, kl_rope, vl = mla_project(cq_l, ckv_l, kr_l, g_q_lora, g_kv_lora, w_q_up, w_kv_up)
    qc, kc_nope, kc_rope, vc = mla_project(cq_c, ckv_c, kr_c, g_q_lora, g_kv_lora, w_q_up, w_kv_up)
    ql = jnp.concatenate([ql[..., :QK_NOPE], rope_2d(ql[..., QK_NOPE:], ang_row, ang_col)], axis=-1)
    kl = assemble_keys(kl_nope, rope_2d(kl_rope, ang_row, ang_col))
    kc = assemble_keys(kc_nope, kc_rope)
    k_all = jnp.concatenate([kl, kc], axis=2)
    v_all = jnp.concatenate([vl, vc], axis=2)
    a_l = merge_heads(blocked_attend(ql, k_all, v_all))
    qml, kml, vml, lil, lfl = mlstm_project(mq_l, mk_l, mv_l, mif_l, b_if)
    qmc, kmc, vmc, lic, lfc = mlstm_project(mq_c, mk_c, mv_c, mif_c, b_if)
    b = hc.shape[0]
    hcf, st_f = mlstm_dir(qmc, kmc, vmc, lic[0], lfc[0], zero_state(b), False)
    hcb, st_b = mlstm_dir(qmc, kmc, vmc, lic[1], lfc[1], zero_state(b), True)
    hlf, _ = mlstm_dir(qml, kml, vml, lil[0], lfl[0], st_f, False)
    hlb, _ = mlstm_dir(qml, kml, vml, lil[1], lfl[1], st_b, True)
    m_l = mlstm_output(hlf + hlb, mo_l, g_mlstm)
    s_l = short_conv(cu_l, cb_l, cc_l, w_conv, b_conv)
    y_l = merge_branches(cg_l, a_l, m_l, s_l, w_proj_a, w_proj_b, w_proj_c, w_o)
    if not with_ctx_out:
        return y_l, None
    a_c = merge_heads(softmax_attend(qc, kc, vc))
    m_c = mlstm_output(hcf + hcb, mo_c, g_mlstm)
    s_c = short_conv(cu_c, cb_c, cc_c, w_conv, b_conv)
    y_c = merge_branches(cg_c, a_c, m_c, s_c, w_proj_a, w_proj_b, w_proj_c, w_o)
    return y_l, y_c


def layer(x, ctx, c, c_ctx, w_mod, b_mod, g_norm, w_in, g_q_lora, g_kv_lora, w_q_up, w_kv_up, b_if, g_mlstm,
          w_conv, b_conv, w_proj_a, w_proj_b, w_proj_c, w_o, w_router, b_router, w_up, b_up, w_down, b_down,
          ang_row, ang_col, update_ctx):
    sh1, sc1, gt1, sh2, sc2, gt2 = adaln_params(c, w_mod, b_mod)
    csh1, csc1, cgt1, csh2, csc2, cgt2 = adaln_params(c_ctx, w_mod, b_mod)
    hl = modulate(x, g_norm[0], sh1, sc1)
    hc = modulate(ctx, g_norm[0], csh1, csc1)
    y_l, y_c = token_mixers(hl, hc, w_in, g_q_lora, g_kv_lora, w_q_up, w_kv_up, b_if, g_mlstm, w_conv, b_conv,
                            w_proj_a, w_proj_b, w_proj_c, w_o, ang_row, ang_col, update_ctx)
    x = x + gt1 * rmsnorm(y_l, g_norm[1])
    h2 = modulate(x, g_norm[2], sh2, sc2)
    if update_ctx:
        ctx = ctx + cgt1 * rmsnorm(y_c, g_norm[1])
        h2c = modulate(ctx, g_norm[2], csh2, csc2)
        n_lat = h2.shape[0] * h2.shape[1]
        tokens = jnp.concatenate([h2.reshape(-1, D_MODEL), h2c.reshape(-1, D_MODEL)], axis=0)
        f = moe(tokens, w_router, b_router, w_up, b_up, w_down, b_down)
        f_l = f[:n_lat].reshape(h2.shape)
        ctx = ctx + cgt2 * rmsnorm(f[n_lat:].reshape(h2c.shape), g_norm[3])
    else:
        f_l = moe(h2.reshape(-1, D_MODEL), w_router, b_router, w_up, b_up, w_down, b_down).reshape(h2.shape)
    x = x + gt2 * rmsnorm(f_l, g_norm[3])
    return x, ctx


def setup_inputs(seed: int = 0) -> dict:
    key = jax.random.key(seed)
    ks = jax.random.split(key, 26)
    def nrm(k, shape, s):
        return jax.random.normal(k, shape, jnp.float32) * s
    L = DEPTH
    gate_offset = jnp.array([0.0, FORGET_BIAS], jnp.float32).reshape(1, 1, 2, 1)
    return {
        'x': nrm(ks[0], (BATCH, SEQ, D_MODEL), 1.0),
        'c': nrm(ks[1], (BATCH, D_MODEL), 1.0),
        'ctx': nrm(ks[2], (BATCH, CTX_LEN, D_MODEL), 1.0),
        'c_ctx': nrm(ks[3], (D_MODEL,), 1.0),
        'w_mod': nrm(ks[4], (L, D_MODEL, N_MOD * D_MODEL), 0.5 * D_MODEL ** -0.5),
        'b_mod': nrm(ks[5], (L, N_MOD * D_MODEL), 0.01),
        'g_norm': 1.0 + nrm(ks[6], (L, 4, D_MODEL), 0.02),
        'w_in': nrm(ks[7], (L, D_MODEL, D_IN), D_MODEL ** -0.5),
        'g_q_lora': 1.0 + nrm(ks[8], (L, Q_LORA), 0.02),
        'g_kv_lora': 1.0 + nrm(ks[9], (L, KV_LORA), 0.02),
        'w_q_up': nrm(ks[10], (L, Q_LORA, MLA_HEADS * (QK_NOPE + QK_ROPE)), Q_LORA ** -0.5),
        'w_kv_up': nrm(ks[11], (L, KV_LORA, MLA_HEADS * (QK_NOPE + V_HEAD)), KV_LORA ** -0.5),
        'b_if': gate_offset + nrm(ks[12], (L, 2, 2, M_HEADS), 0.1),
        'g_mlstm': 1.0 + nrm(ks[13], (L, M_HEADS * M_V), 0.02),
        'w_conv': nrm(ks[14], (L, CONV_K, CONV_WIDTH), CONV_K ** -0.5),
        'b_conv': nrm(ks[15], (L, CONV_WIDTH), 0.01),
        'w_proj_a': nrm(ks[16], (L, MLA_HEADS * V_HEAD, D_MODEL), (MLA_HEADS * V_HEAD) ** -0.5),
        'w_proj_b': nrm(ks[17], (L, M_HEADS * M_V, D_MODEL), (M_HEADS * M_V) ** -0.5),
        'w_proj_c': nrm(ks[18], (L, CONV_WIDTH, D_MODEL), CONV_WIDTH ** -0.5),
        'w_o': nrm(ks[19], (L, D_MODEL, D_MODEL), D_MODEL ** -0.5),
        'w_router': nrm(ks[20], (L, D_MODEL, N_EXPERTS), D_MODEL ** -0.5),
        'b_router': nrm(ks[21], (L, N_EXPERTS), 0.01),
        'w_up': nrm(ks[22], (L, N_EXPERTS, D_MODEL, 2 * D_FF_EXPERT), D_MODEL ** -0.5),
        'b_up': nrm(ks[23], (L, N_EXPERTS, 2 * D_FF_EXPERT), 0.01),
        'w_down': nrm(ks[24], (L, N_EXPERTS, D_FF_EXPERT, D_MODEL), D_FF_EXPERT ** -0.5),
        'b_down': nrm(ks[25], (L, N_EXPERTS, D_MODEL), 0.01),
    }


def reference(x, c, ctx, c_ctx, w_mod, b_mod, g_norm, w_in, g_q_lora, g_kv_lora, w_q_up, w_kv_up, b_if, g_mlstm,
              w_conv, b_conv, w_proj_a, w_proj_b, w_proj_c, w_o, w_router, b_router, w_up, b_up, w_down, b_down):
    rows = x.shape[1] // GRID_W
    row_pos = jnp.repeat(jnp.arange(rows, dtype=jnp.float32), GRID_W)
    col_pos = jnp.tile(jnp.arange(GRID_W, dtype=jnp.float32), rows)
    inv_freq = ROPE_THETA ** (-jnp.arange(ROPE_FREQS, dtype=jnp.float32) / ROPE_FREQS)
    ang_row = row_pos[:, None] * inv_freq
    ang_col = col_pos[:, None] * inv_freq
    for l in range(DEPTH):
        x, ctx = layer(x, ctx, c, c_ctx, w_mod[l], b_mod[l], g_norm[l], w_in[l], g_q_lora[l], g_kv_lora[l],
                       w_q_up[l], w_kv_up[l], b_if[l], g_mlstm[l], w_conv[l], b_conv[l], w_proj_a[l], w_proj_b[l],
                       w_proj_c[l], w_o[l], w_router[l], b_router[l], w_up[l], b_up[l], w_down[l], b_down[l],
                       ang_row, ang_col, l < DEPTH - 1)
    return x
```

```python
import functools

import numpy as np
import jax
import jax.numpy as jnp
from jax import lax
from jax.experimental import pallas as pl
from jax.experimental.pallas import tpu as pltpu

F32 = jnp.float32
MXU_DTYPE = jnp.bfloat16

D_MODEL = 2048
GRID_W = 64
MLA_HEADS = 8
QK_NOPE = 128
QK_ROPE = 64
V_HEAD = 128
Q_LORA = 512
KV_LORA = 512
ROPE_THETA = 10000.0
ROPE_FREQS = QK_ROPE // 4
ATTN_SCALE = (QK_NOPE + QK_ROPE) ** -0.5
M_HEADS = 4
M_QK = 64
M_V = 128
CONV_WIDTH = 512
N_EXPERTS = 32
TOP_K = 4
D_FF = 1024
SWIGLU_ALPHA = 1.702
SWIGLU_LIMIT = 7.0
N_MOD = 6
EPS = 1e-6

LANE = 128
MLSTM_CHUNK = 256
COL_TILE = 512
MOE_TILE = 512
NEG_BIG = -1e30
VMEM_LIMIT = 56 * 1024 * 1024

ZT_CQ, ZT_CKV, ZT_MQK, ZT_MV, ZT_MO, ZT_CU, ZT_CB, ZT_CC, ZT_CG = 0, 1, 2, 3, 4, 5, 6, 7, 8
N_MAIN_TILES = 8 + 3 * D_MODEL // COL_TILE


def _cparams(n_axes, vmem=VMEM_LIMIT):
    return pltpu.CompilerParams(dimension_semantics=("arbitrary",) * n_axes, vmem_limit_bytes=vmem)


def _rms(x, g):
    return x * lax.rsqrt(jnp.mean(x * x, axis=-1, keepdims=True) + EPS) * g


def _lane_pick(x, idx):
    lane = lax.broadcasted_iota(jnp.int32, x.shape, 1)
    return jnp.sum(jnp.where(lane == idx, x, 0.0), axis=1, keepdims=True)


def _mod_kernel(c_ref, w_ref, b_ref, o_ref):
    c = c_ref[...]
    s = c * jax.nn.sigmoid(c)
    o_ref[...] = jnp.dot(s.astype(MXU_DTYPE), w_ref[...].astype(MXU_DTYPE),
                         preferred_element_type=F32) + b_ref[...]


def _adaln(cond, w_mod, b_mod):
    n = w_mod.shape[1]
    tn = 1024
    return pl.pallas_call(
        _mod_kernel,
        grid=(n // tn,),
        in_specs=[pl.BlockSpec(cond.shape, lambda j: (0, 0)),
                  pl.BlockSpec((D_MODEL, tn), lambda j: (0, j)),
                  pl.BlockSpec((1, tn), lambda j: (0, j))],
        out_specs=pl.BlockSpec((cond.shape[0], tn), lambda j: (0, j)),
        out_shape=jax.ShapeDtypeStruct((cond.shape[0], n), F32),
        compiler_params=_cparams(1),
        name="adaln",
    )(cond, w_mod, b_mod.reshape(1, n))


def _inproj_kernel(x_ref, g_ref, sh_ref, sc_ref, w_ref, z_ref, zs_ref, h_scr):
    j = pl.program_id(1)

    @pl.when(j == 0)
    def _():
        h = _rms(x_ref[...], g_ref[0]) * (1.0 + sc_ref[0]) + sh_ref[0]
        h_scr[...] = h.astype(h_scr.dtype)

    acc = jnp.dot(h_scr[...], w_ref[...], preferred_element_type=F32)

    @pl.when(j < N_MAIN_TILES)
    def _():
        z_ref[...] = acc.astype(z_ref.dtype)

    @pl.when(j == N_MAIN_TILES)
    def _():
        zs_ref[...] = acc


def _inproj(xa, g_norm, modt, w_in_p, rt):
    rows = xa.shape[0]
    nt = rows // rt
    return pl.pallas_call(
        _inproj_kernel,
        grid=(nt, N_MAIN_TILES + 1),
        in_specs=[pl.BlockSpec((rt, D_MODEL), lambda i, j: (i, 0)),
                  pl.BlockSpec((1, 1, D_MODEL), lambda i, j: (0, 0, 0)),
                  pl.BlockSpec((1, 1, D_MODEL), lambda i, j: (i, 0, 0)),
                  pl.BlockSpec((1, 1, D_MODEL), lambda i, j: (i, 0, 1)),
                  pl.BlockSpec((D_MODEL, COL_TILE), lambda i, j: (0, j))],
        out_specs=[pl.BlockSpec((rt, COL_TILE), lambda i, j: (i, jnp.minimum(j, N_MAIN_TILES - 1))),
                   pl.BlockSpec((rt, COL_TILE), lambda i, j: (i, 0))],
        out_shape=[jax.ShapeDtypeStruct((rows, N_MAIN_TILES * COL_TILE), MXU_DTYPE),
                   jax.ShapeDtypeStruct((rows, COL_TILE), F32)],
        scratch_shapes=[pltpu.VMEM((rt, D_MODEL), MXU_DTYPE)],
        compiler_params=_cparams(2),
        name="inproj",
    )(xa, g_norm, modt, modt, w_in_p)


def _mla_kernel(cq_ref, ckv_ref, zs_ref, gq_ref, gkv_ref, wq_ref, wkv_ref, cos_ref, sin_ref,
                q_ref, k_ref, v_ref):
    def norm(z_ref, g_ref):
        return _rms(z_ref[...].astype(F32), g_ref[...]).astype(MXU_DTYPE)

    qf = jnp.dot(norm(cq_ref, gq_ref), wq_ref[...], preferred_element_type=F32) * ATTN_SCALE
    kvf = jnp.dot(norm(ckv_ref, gkv_ref), wkv_ref[...], preferred_element_type=F32)
    cos = cos_ref[...]
    sin = sin_ref[...]
    nope_w = MLA_HEADS * QK_NOPE
    rope_w = MLA_HEADS * QK_ROPE
    for p in range(MLA_HEADS // 2):
        a = nope_w + LANE * p
        b = nope_w + rope_w + LANE * p
        rp = (qf[:, a:a + LANE] * cos + qf[:, b:b + LANE] * sin).astype(MXU_DTYPE)
        for h in (2 * p, 2 * p + 1):
            q_ref[:, 256 * h:256 * h + LANE] = qf[:, LANE * h:LANE * (h + 1)].astype(MXU_DTYPE)
            q_ref[:, 256 * h + LANE:256 * (h + 1)] = rp
    kr2 = zs_ref[:, 0:LANE] * cos + zs_ref[:, LANE:2 * LANE] * sin
    lane = lax.broadcasted_iota(jnp.int32, kr2.shape, 1)
    k_lo = jnp.where(lane < QK_ROPE, kr2, 0.0).astype(MXU_DTYPE)
    k_hi = jnp.where(lane >= QK_ROPE, kr2, 0.0).astype(MXU_DTYPE)
    for h in range(MLA_HEADS):
        k_ref[:, 256 * h:256 * h + LANE] = kvf[:, LANE * h:LANE * (h + 1)].astype(MXU_DTYPE)
        k_ref[:, 256 * h + LANE:256 * (h + 1)] = k_lo if h % 2 == 0 else k_hi
    v_ref[...] = kvf[:, MLA_HEADS * QK_NOPE:].astype(MXU_DTYPE)


def _mla_project(z, zs, g_q, g_kv, wq_p, wkv_p, cos_t, sin_t, rt, n_lat_tiles, tiles_per_seq):
    rows = z.shape[0]
    nt = rows // rt

    def tab(i):
        return (jnp.where(i < n_lat_tiles, i % tiles_per_seq, tiles_per_seq), 0)

    return pl.pallas_call(
        _mla_kernel,
        grid=(nt,),
        in_specs=[pl.BlockSpec((rt, COL_TILE), lambda i: (i, ZT_CQ)),
                  pl.BlockSpec((rt, COL_TILE), lambda i: (i, ZT_CKV)),
                  pl.BlockSpec((rt, COL_TILE), lambda i: (i, 0)),
                  pl.BlockSpec((1, Q_LORA), lambda i: (0, 0)),
                  pl.BlockSpec((1, KV_LORA), lambda i: (0, 0)),
                  pl.BlockSpec(wq_p.shape, lambda i: (0, 0)),
                  pl.BlockSpec(wkv_p.shape, lambda i: (0, 0)),
                  pl.BlockSpec((rt, LANE), tab),
                  pl.BlockSpec((rt, LANE), tab)],
        out_specs=[pl.BlockSpec((rt, 256 * MLA_HEADS), lambda i: (i, 0)),
                   pl.BlockSpec((rt, 256 * MLA_HEADS), lambda i: (i, 0)),
                   pl.BlockSpec((rt, V_HEAD * MLA_HEADS), lambda i: (i, 0))],
        out_shape=[jax.ShapeDtypeStruct((rows, 256 * MLA_HEADS), MXU_DTYPE),
                   jax.ShapeDtypeStruct((rows, 256 * MLA_HEADS), MXU_DTYPE),
                   jax.ShapeDtypeStruct((rows, V_HEAD * MLA_HEADS), MXU_DTYPE)],
        compiler_params=_cparams(1),
        name="mla_project",
    )(z, z, zs, g_q.reshape(1, -1), g_kv.reshape(1, -1), wq_p, wkv_p, cos_t, sin_t)


def _attn_kernel(*refs, kv_chunks):
    q_ref = refs[0]
    o_ref = refs[-1]
    q = q_ref[...]
    tq = q.shape[0]
    m = jnp.full((tq, 1), -jnp.inf, F32)
    l = jnp.zeros((tq, 1), F32)
    acc = jnp.zeros((tq, V_HEAD), F32)
    for s, n_chunks in enumerate(kv_chunks):
        k_ref, v_ref = refs[1 + 2 * s], refs[2 + 2 * s]
        ck = k_ref.shape[0] // n_chunks
        for c in range(n_chunks):
            k = k_ref[c * ck:(c + 1) * ck, :]
            v = v_ref[c * ck:(c + 1) * ck, :]
            sc = lax.dot_general(q, k, (((1,), (1,)), ((), ())), preferred_element_type=F32)
            m_new = jnp.maximum(m, jnp.max(sc, axis=-1, keepdims=True))
            p = jnp.exp(sc - m_new)
            alpha = jnp.exp(m - m_new)
            l = alpha * l + jnp.sum(p, axis=-1, keepdims=True)
            acc = alpha * acc + jnp.dot(p.astype(MXU_DTYPE), v, preferred_element_type=F32)
            m = m_new
    o_ref[...] = (acc / l).astype(o_ref.dtype)


def _attention_latent(q, k, v, B, T, Tc, tq, ck):
    rows = q.shape[0]
    qt = T // tq
    ctx_blk0 = B * T // Tc
    return pl.pallas_call(
        functools.partial(_attn_kernel, kv_chunks=(T // ck, 1)),
        grid=(B, MLA_HEADS, qt),
        in_specs=[pl.BlockSpec((tq, 256), lambda b, h, i: (b * qt + i, h)),
                  pl.BlockSpec((T, 256), lambda b, h, i: (b, h)),
                  pl.BlockSpec((T, V_HEAD), lambda b, h, i: (b, h)),
                  pl.BlockSpec((Tc, 256), lambda b, h, i: (ctx_blk0 + b, h)),
                  pl.BlockSpec((Tc, V_HEAD), lambda b, h, i: (ctx_blk0 + b, h))],
        out_specs=pl.BlockSpec((tq, V_HEAD), lambda b, h, i: (b * qt + i, h)),
        out_shape=jax.ShapeDtypeStruct((rows, V_HEAD * MLA_HEADS), MXU_DTYPE),
        compiler_params=_cparams(3),
        name="attn_latent",
    )(q, k, v, k, v)


def _attention_ctx(q, k, v, a_buf, B, T, Tc):
    ctx_blk0 = B * T // Tc
    return pl.pallas_call(
        functools.partial(_attn_kernel, kv_chunks=(1,)),
        grid=(B, MLA_HEADS),
        in_specs=[pl.BlockSpec((Tc, 256), lambda b, h: (ctx_blk0 + b, h)),
                  pl.BlockSpec((Tc, 256), lambda b, h: (ctx_blk0 + b, h)),
                  pl.BlockSpec((Tc, V_HEAD), lambda b, h: (ctx_blk0 + b, h)),
                  pl.BlockSpec(memory_space=pl.ANY)],
        out_specs=pl.BlockSpec((Tc, V_HEAD), lambda b, h: (ctx_blk0 + b, h)),
        out_shape=jax.ShapeDtypeStruct(a_buf.shape, a_buf.dtype),
        input_output_aliases={3: 0},
        compiler_params=_cparams(2),
        name="attn_ctx",
    )(q, k, v, a_buf)


def _mlstm_chunk(qk, v, g, lf_cum, gt_ref, head, reverse, ct_ref, m_prev, consts):
    tri, row_iota, mk, ones_tile, lane = consts
    L = qk.shape[0]
    base = 8 if reverse else 0
    col_i, col_f = base + head, base + 4 + head
    ql = jnp.where(lane < M_QK, qk, jnp.zeros_like(qk))
    kl = jnp.dot(qk, mk, preferred_element_type=F32).astype(MXU_DTYPE)
    v_aug = jnp.concatenate([v, ones_tile], axis=1)
    bc = _lane_pick(lf_cum, col_f)
    li = _lane_pick(g, col_i)
    r_row = gt_ref[0, col_i:col_i + 1, :] - gt_ref[1, col_f:col_f + 1, :]
    dlog = jnp.where(tri, bc + r_row, -jnp.inf)
    inter = bc + m_prev
    m_t = jnp.maximum(inter, jnp.max(dlog, axis=1, keepdims=True))
    dexp = jnp.exp(dlog - m_t)
    s_raw = lax.dot_general(ql, kl, (((1,), (1,)), ((), ())), preferred_element_type=F32)
    s_mat = (s_raw * (M_QK ** -0.5) * dexp).astype(MXU_DTYPE)
    a = jnp.exp(inter - m_t) * (M_QK ** -0.5)
    ct = ct_ref[...]
    nd = (jnp.dot(s_mat, v_aug, preferred_element_type=F32)
          + a * jnp.dot(ql, ct.astype(MXU_DTYPE), preferred_element_type=F32))
    num, den = nd[:, :M_V], nd[:, M_V:]
    h_out = num / jnp.maximum(jnp.abs(den), jnp.exp(-m_t))
    last = 0 if reverse else L - 1
    total = jnp.sum(jnp.where(row_iota == last, bc, 0.0), axis=0, keepdims=True)
    w_col = total - bc + li
    m_new = jnp.maximum(total + m_prev, jnp.max(w_col, axis=0, keepdims=True))
    decay = jnp.exp(total + m_prev - m_new)
    e = jnp.exp(w_col - m_new)
    ek = (kl.astype(F32) * e).astype(MXU_DTYPE)
    upd = lax.dot_general(ek, v_aug, (((0,), (0,)), ((), ())), preferred_element_type=F32)
    ct_ref[...] = decay * ct + upd
    return h_out, m_new


def _mlstm_kernel(zl_qk, zl_v, zl_mo, gl_ref, zc_qk, zc_v, zc_mo, gc_ref, bif_ref, gm_ref,
                  ol_ref, oc_ref, acc_l, acc_c, ct_scr, gt_scr):
    L = MLSTM_CHUNK
    T = zl_qk.shape[0]
    nc = T // L
    r = lax.broadcasted_iota(jnp.int32, (L, L), 0)
    c = lax.broadcasted_iota(jnp.int32, (L, L), 1)
    tri_f = c <= r
    tri_b = c >= r
    tmat_f = tri_f.astype(F32)
    tmat_b = tri_b.astype(F32)
    row_iota = lax.broadcasted_iota(jnp.int32, (L, 1), 0)
    lane = lax.broadcasted_iota(jnp.int32, (L, LANE), 1)
    rr = lax.broadcasted_iota(jnp.int32, (LANE, LANE), 0)
    cc = lax.broadcasted_iota(jnp.int32, (LANE, LANE), 1)
    mk = (rr == cc + M_QK).astype(MXU_DTYPE)
    ones_tile = jnp.ones((L, LANE), MXU_DTYPE)
    consts_f = (tri_f, row_iota, mk, ones_tile, lane)
    consts_b = (tri_b, row_iota, mk, ones_tile, lane)
    bias = bif_ref[...]

    def gate_prep(g_raw, tmat, slot):
        g = g_raw + bias
        log_sig = jnp.minimum(g, 0.0) - jnp.log1p(jnp.exp(-jnp.abs(g)))
        cum = jnp.dot(tmat, log_sig, precision=lax.Precision.HIGHEST, preferred_element_type=F32)
        gt_scr[slot, 0] = g.T
        gt_scr[slot, 1] = cum.T
        return g, cum

    def step(qk_ref, v_ref, g_ref, acc_ref, row_f, row_b, ms):
        g_f, cum_f = gate_prep(g_ref[pl.ds(row_f, L), :], tmat_f, 0)
        g_b, cum_b = gate_prep(g_ref[pl.ds(row_b, L), :], tmat_b, 1)
        new_ms = []
        for h in range(M_HEADS):
            cols = slice(h * LANE, (h + 1) * LANE)
            hf, mf = _mlstm_chunk(qk_ref[pl.ds(row_f, L), cols], v_ref[pl.ds(row_f, L), cols],
                                  g_f, cum_f, gt_scr.at[0], h, False,
                                  ct_scr.at[h], ms[h], consts_f)
            acc_ref[pl.ds(row_f, L), cols] += hf
            hb, mb = _mlstm_chunk(qk_ref[pl.ds(row_b, L), cols], v_ref[pl.ds(row_b, L), cols],
                                  g_b, cum_b, gt_scr.at[1], h, True,
                                  ct_scr.at[M_HEADS + h], ms[M_HEADS + h], consts_b)
            acc_ref[pl.ds(row_b, L), cols] += hb
            new_ms.append((mf, mb))
        return tuple(x[0] for x in new_ms) + tuple(x[1] for x in new_ms)

    ct_scr[...] = jnp.zeros_like(ct_scr)
    acc_l[...] = jnp.zeros_like(acc_l)
    acc_c[...] = jnp.zeros_like(acc_c)
    zero = jnp.zeros((1, 1), F32)
    ms = step(zc_qk, zc_v, gc_ref, acc_c, 0, 0, (zero,) * (2 * M_HEADS))

    def body(i, ms):
        row_f = pl.multiple_of(i * L, L)
        row_b = pl.multiple_of((nc - 1 - i) * L, L)
        return step(zl_qk, zl_v, gl_ref, acc_l, row_f, row_b, ms)

    lax.fori_loop(0, nc, body, ms)

    gm = gm_ref[...]

    def finish(acc_ref, mo_ref, o_ref, row):
        for h in range(M_HEADS):
            cols = slice(h * LANE, (h + 1) * LANE)
            hn = _rms(acc_ref[pl.ds(row, L), cols], gm[:, cols])
            gate = jax.nn.sigmoid(mo_ref[pl.ds(row, L), cols].astype(F32))
            o_ref[pl.ds(row, L), cols] = (hn * gate).astype(o_ref.dtype)

    finish(acc_c, zc_mo, oc_ref, 0)

    def fin_body(i, carry):
        finish(acc_l, zl_mo, ol_ref, pl.multiple_of(i * L, L))
        return carry

    lax.fori_loop(0, nc, fin_body, 0)


def _mlstm(z, zs, bif_row, g_mlstm, B, T, Tc):
    assert Tc == MLSTM_CHUNK and T % MLSTM_CHUNK == 0
    rows = z.shape[0]
    ctx_blk0 = B * T // Tc
    w = M_HEADS * LANE
    out_l, out_c = pl.pallas_call(
        _mlstm_kernel,
        grid=(B,),
        in_specs=[pl.BlockSpec((T, w), lambda b: (b, ZT_MQK)),
                  pl.BlockSpec((T, w), lambda b: (b, ZT_MV)),
                  pl.BlockSpec((T, w), lambda b: (b, ZT_MO)),
                  pl.BlockSpec((T, LANE), lambda b: (b, 2)),
                  pl.BlockSpec((Tc, w), lambda b: (ctx_blk0 + b, ZT_MQK)),
                  pl.BlockSpec((Tc, w), lambda b: (ctx_blk0 + b, ZT_MV)),
                  pl.BlockSpec((Tc, w), lambda b: (ctx_blk0 + b, ZT_MO)),
                  pl.BlockSpec((Tc, LANE), lambda b: (ctx_blk0 + b, 2)),
                  pl.BlockSpec((1, LANE), lambda b: (0, 0)),
                  pl.BlockSpec((1, w), lambda b: (0, 0))],
        out_specs=[pl.BlockSpec((T, w), lambda b: (b, 0)),
                   pl.BlockSpec((Tc, w), lambda b: (b, 0))],
        out_shape=[jax.ShapeDtypeStruct((B * T, w), MXU_DTYPE),
                   jax.ShapeDtypeStruct((B * Tc, w), MXU_DTYPE)],
        scratch_shapes=[pltpu.VMEM((T, w), F32), pltpu.VMEM((Tc, w), F32),
                        pltpu.VMEM((2 * M_HEADS, LANE, 2 * LANE), F32),
                        pltpu.VMEM((2, 2, LANE, MLSTM_CHUNK), F32)],
        compiler_params=_cparams(1),
        name="mlstm",
    )(z, z, z, zs, z, z, z, zs, bif_row, g_mlstm.reshape(1, -1))
    return jnp.concatenate([out_l, out_c], axis=0)


def _merge_kernel(a_ref, m_ref, cu_ref, cb_ref, cc_ref, cu_p, cc_p, cu_n, cc_n, ga_ref, gm_ref, gs_ref,
                  wa_ref, wb_ref, wc_ref, wconv_ref, bconv_ref, t_ref, s_scr,
                  *, n_lat_tiles, tiles_per_seq, seq_len, ctx_len):
    i = pl.program_id(0)
    j = pl.program_id(1)
    rt = a_ref.shape[0]

    @pl.when(j == 0)
    def _():
        u = cc_ref[...].astype(F32) * cu_ref[...].astype(F32)
        hp = cc_p[...].astype(F32) * cu_p[...].astype(F32)
        hn = cc_n[...].astype(F32) * cu_n[...].astype(F32)
        row = lax.broadcasted_iota(jnp.int32, (rt, 1), 0)
        is_ctx = i >= n_lat_tiles
        seg = jnp.where(is_ctx, ctx_len, seq_len)
        off = jnp.where(is_ctx, (i - n_lat_tiles) * rt, (i % tiles_per_seq) * rt)
        pos = (off + row) & (seg - 1)
        up = pltpu.roll(u, 1, 0)
        up = jnp.where(row == 0, hp[hp.shape[0] - 1:, :], up)
        up = jnp.where(pos == 0, 0.0, up)
        un = pltpu.roll(u, rt - 1, 0)
        un = jnp.where(row == rt - 1, hn[0:1, :], un)
        un = jnp.where(pos == seg - 1, 0.0, un)
        wconv = wconv_ref[...]
        y = up * wconv[0:1, :] + u * wconv[1:2, :] + un * wconv[2:3, :] + bconv_ref[...]
        s_scr[...] = (cb_ref[...].astype(F32) * y).astype(s_scr.dtype)

    def sig(ref):
        return jax.nn.sigmoid(ref[...].astype(F32))

    t = (sig(ga_ref) * jnp.dot(a_ref[...], wa_ref[...], preferred_element_type=F32)
         + sig(gm_ref) * jnp.dot(m_ref[...], wb_ref[...], preferred_element_type=F32)
         + sig(gs_ref) * jnp.dot(s_scr[...], wc_ref[...], preferred_element_type=F32))
    t_ref[...] = t.astype(t_ref.dtype)


def _merge(a, m, z, wa, wb, wc, w_conv, b_conv, rt, n_tiles, n_lat_tiles, tiles_per_seq, T, Tc):
    rows = a.shape[0]
    hb = 16
    rb = rt // hb
    last_hb = rows // hb - 1
    tn = COL_TILE
    nj = D_MODEL // tn
    gt = D_MODEL // tn

    def prev_blk(col):
        return lambda i, j: (jnp.maximum(i * rb - 1, 0), col)

    def next_blk(col):
        return lambda i, j: (jnp.minimum((i + 1) * rb, last_hb), col)

    kern = functools.partial(_merge_kernel, n_lat_tiles=n_lat_tiles, tiles_per_seq=tiles_per_seq,
                             seq_len=T, ctx_len=Tc)
    return pl.pallas_call(
        kern,
        grid=(n_tiles, nj),
        in_specs=[pl.BlockSpec((rt, a.shape[1]), lambda i, j: (i, 0)),
                  pl.BlockSpec((rt, m.shape[1]), lambda i, j: (i, 0)),
                  pl.BlockSpec((rt, COL_TILE), lambda i, j: (i, ZT_CU)),
                  pl.BlockSpec((rt, COL_TILE), lambda i, j: (i, ZT_CB)),
                  pl.BlockSpec((rt, COL_TILE), lambda i, j: (i, ZT_CC)),
                  pl.BlockSpec((hb, COL_TILE), prev_blk(ZT_CU)),
                  pl.BlockSpec((hb, COL_TILE), prev_blk(ZT_CC)),
                  pl.BlockSpec((hb, COL_TILE), next_blk(ZT_CU)),
                  pl.BlockSpec((hb, COL_TILE), next_blk(ZT_CC)),
                  pl.BlockSpec((rt, tn), lambda i, j: (i, ZT_CG + j)),
                  pl.BlockSpec((rt, tn), lambda i, j: (i, ZT_CG + gt + j)),
                  pl.BlockSpec((rt, tn), lambda i, j: (i, ZT_CG + 2 * gt + j)),
                  pl.BlockSpec((wa.shape[0], tn), lambda i, j: (0, j)),
                  pl.BlockSpec((wb.shape[0], tn), lambda i, j: (0, j)),
                  pl.BlockSpec((wc.shape[0], tn), lambda i, j: (0, j)),
                  pl.BlockSpec(w_conv.shape, lambda i, j: (0, 0)),
                  pl.BlockSpec((1, CONV_WIDTH), lambda i, j: (0, 0))],
        out_specs=pl.BlockSpec((rt, tn), lambda i, j: (i, j)),
        out_shape=jax.ShapeDtypeStruct((rows, D_MODEL), MXU_DTYPE),
        scratch_shapes=[pltpu.VMEM((rt, CONV_WIDTH), MXU_DTYPE)],
        compiler_params=_cparams(2),
        name="merge",
    )(a, m, z, z, z, z, z, z, z, z, z, z, wa, wb, wc, w_conv, b_conv.reshape(1, -1))


def _dot3(x, w):
    xh = x.astype(jnp.bfloat16)
    xl = (x - xh.astype(F32)).astype(jnp.bfloat16)
    wh = w.astype(jnp.bfloat16)
    wl = (w - wh.astype(F32)).astype(jnp.bfloat16)
    return (jnp.dot(xh, wh, preferred_element_type=F32) + jnp.dot(xh, wl, preferred_element_type=F32)
            + jnp.dot(xl, wh, preferred_element_type=F32))


def _post_kernel(t_ref, wo_ref, x_ref, g1_ref, g2_ref, gt_ref, sh_ref, sc_ref, wr_ref, br_ref,
                 xo_ref, h2_ref, ri_ref, rw_ref):
    y = jnp.dot(t_ref[...], wo_ref[...], preferred_element_type=F32)
    xn = x_ref[...] + gt_ref[0] * _rms(y, g1_ref[0])
    xo_ref[...] = xn
    h2 = _rms(xn, g2_ref[0]) * (1.0 + sc_ref[0]) + sh_ref[0]
    h2_ref[...] = h2.astype(h2_ref.dtype)
    logits = _dot3(h2, wr_ref[...]) + br_ref[...]
    lane = lax.broadcasted_iota(jnp.int32, logits.shape, 1)
    lane_f = lane.astype(F32)
    vals, ids = [], []
    for _ in range(TOP_K):
        mx = jnp.max(logits, axis=1, keepdims=True)
        idx = jnp.min(jnp.where(logits == mx, lane_f, float(LANE)), axis=1, keepdims=True)
        logits = jnp.where(lane_f == idx, -jnp.inf, logits)
        vals.append(mx)
        ids.append(idx)
    es = [jnp.exp(v - vals[0]) for v in vals]
    tot = es[0] + es[1] + es[2] + es[3]
    rw = jnp.zeros(logits.shape, F32)
    ri = jnp.zeros(logits.shape, F32)
    for k in range(TOP_K):
        rw = jnp.where(lane == k, es[k] / tot, rw)
        ri = jnp.where(lane == k, ids[k], ri)
    rw_ref[...] = rw
    ri_ref[...] = ri.astype(jnp.int32)


def _post(t, wo, xa, g_norm, modt, wr_p, br_p, tm, n_tiles):
    rows = xa.shape[0]
    row_spec = lambda w: pl.BlockSpec((tm, w), lambda i: (i, 0))
    mod = lambda k: pl.BlockSpec((1, 1, D_MODEL), lambda i: (i, 0, k))
    gn = lambda k: pl.BlockSpec((1, 1, D_MODEL), lambda i: (k, 0, 0))
    return pl.pallas_call(
        _post_kernel,
        grid=(n_tiles,),
        in_specs=[row_spec(D_MODEL), pl.BlockSpec(wo.shape, lambda i: (0, 0)), row_spec(D_MODEL),
                  gn(1), gn(2), mod(2), mod(3), mod(4),
                  pl.BlockSpec(wr_p.shape, lambda i: (0, 0)), pl.BlockSpec((1, LANE), lambda i: (0, 0))],
        out_specs=[row_spec(D_MODEL), row_spec(D_MODEL), row_spec(LANE), row_spec(LANE)],
        out_shape=[jax.ShapeDtypeStruct((rows, D_MODEL), F32),
                   jax.ShapeDtypeStruct((rows, D_MODEL), MXU_DTYPE),
                   jax.ShapeDtypeStruct((rows, LANE), jnp.int32),
                   jax.ShapeDtypeStruct((rows, LANE), F32)],
        compiler_params=_cparams(1),
        name="post_mix",
    )(t, wo, xa, g_norm, g_norm, modt, modt, modt, wr_p, br_p)


def _expert_kernel(te_ref, na_ref, x_ref, wu_ref, bu_ref, wd_ref, bd_ref, y_ref):
    t = pl.program_id(0)

    @pl.when(t < na_ref[0])
    def _():
        gu = jnp.dot(x_ref[...], wu_ref[0], preferred_element_type=F32) + bu_ref[0]
        g = jnp.minimum(gu[:, :D_FF], SWIGLU_LIMIT)
        u = jnp.clip(gu[:, D_FF:], -SWIGLU_LIMIT, SWIGLU_LIMIT)
        hid = g * jax.nn.sigmoid(SWIGLU_ALPHA * g) * (u + 1.0)
        y = jnp.dot(hid.astype(MXU_DTYPE), wd_ref[0], preferred_element_type=F32) + bd_ref[0]
        y_ref[...] = y.astype(y_ref.dtype)

    @pl.when(t >= na_ref[0])
    def _():
        y_ref[...] = jnp.zeros_like(y_ref)


def _experts(tile_expert, n_active, xs, w_up, b_up, w_down, b_down):
    r = xs.shape[0]
    tm = MOE_TILE
    grid_spec = pltpu.PrefetchScalarGridSpec(
        num_scalar_prefetch=2,
        grid=(r // tm,),
        in_specs=[pl.BlockSpec((tm, D_MODEL), lambda t, te, na: (t, 0)),
                  pl.BlockSpec((1, D_MODEL, 2 * D_FF), lambda t, te, na: (te[t], 0, 0)),
                  pl.BlockSpec((1, 1, 2 * D_FF), lambda t, te, na: (te[t], 0, 0)),
                  pl.BlockSpec((1, D_FF, D_MODEL), lambda t, te, na: (te[t], 0, 0)),
                  pl.BlockSpec((1, 1, D_MODEL), lambda t, te, na: (te[t], 0, 0))],
        out_specs=pl.BlockSpec((tm, D_MODEL), lambda t, te, na: (t, 0)),
    )
    return pl.pallas_call(
        _expert_kernel,
        grid_spec=grid_spec,
        out_shape=jax.ShapeDtypeStruct((r, D_MODEL), MXU_DTYPE),
        compiler_params=_cparams(1),
        name="experts",
    )(tile_expert, n_active, xs, w_up, b_up.reshape(N_EXPERTS, 1, -1), w_down,
      b_down.reshape(N_EXPERTS, 1, -1))


def _combine_kernel(yg_ref, rw_ref, x_ref, g3_ref, gt_ref, xo_ref):
    rw = rw_ref[...]
    f = jnp.zeros(x_ref.shape, F32)
    for k in range(TOP_K):
        f = f + _lane_pick(rw, k) * yg_ref[:, k * D_MODEL:(k + 1) * D_MODEL].astype(F32)
    xo_ref[...] = x_ref[...] + gt_ref[0] * _rms(f, g3_ref[0])


def _combine(yg, rw, xa, g_norm, modt, tm, n_tiles):
    rows = xa.shape[0]
    return pl.pallas_call(
        _combine_kernel,
        grid=(n_tiles,),
        in_specs=[pl.BlockSpec((tm, TOP_K * D_MODEL), lambda i: (i, 0)),
                  pl.BlockSpec((tm, LANE), lambda i: (i, 0)),
                  pl.BlockSpec((tm, D_MODEL), lambda i: (i, 0)),
                  pl.BlockSpec((1, 1, D_MODEL), lambda i: (3, 0, 0)),
                  pl.BlockSpec((1, 1, D_MODEL), lambda i: (i, 0, 5))],
        out_specs=pl.BlockSpec((tm, D_MODEL), lambda i: (i, 0)),
        out_shape=jax.ShapeDtypeStruct((rows, D_MODEL), F32),
        compiler_params=_cparams(1),
        name="combine",
    )(yg, rw, xa, g_norm, modt)


def _swap16(idx):
    g = np.asarray(idx).reshape(-1, 2, ROPE_FREQS)
    return g[:, ::-1, :].reshape(-1)


def _in_proj_columns():
    sizes = (Q_LORA, KV_LORA, QK_ROPE, M_HEADS * M_QK, M_HEADS * M_QK, M_HEADS * M_V, M_HEADS * M_V,
             2 * 2 * M_HEADS, CONV_WIDTH, CONV_WIDTH, CONV_WIDTH, 3 * D_MODEL)
    starts = np.concatenate([[0], np.cumsum(sizes)[:-1]])
    seg = lambda k: np.arange(starts[k], starts[k] + sizes[k])
    cq, ckv, kr, mq, mk, mv, mo, mif, cu, cb, cc, cg = (seg(k) for k in range(12))
    mqk = np.concatenate([np.concatenate([mq[h * M_QK:(h + 1) * M_QK], mk[h * M_QK:(h + 1) * M_QK]])
                          for h in range(M_HEADS)])
    main = np.concatenate([cq, ckv, mqk, mv, mo, cu, cb, cc, cg])
    pad = lambda n: np.full((n,), -1)
    special = np.concatenate([kr, kr, _swap16(kr), _swap16(kr), mif, pad(LANE - mif.size), pad(LANE)])
    cols = np.concatenate([main, special])
    assert cols.size == (N_MAIN_TILES + 1) * COL_TILE
    return cols


def _take_cols(w, cols):
    cols = np.asarray(cols)
    out = jnp.take(w, jnp.asarray(np.maximum(cols, 0)), axis=1)
    return jnp.where(jnp.asarray(cols >= 0)[None, :], out, 0.0)


def _q_up_columns():
    per = QK_NOPE + QK_ROPE
    nope = np.concatenate([np.arange(h * per, h * per + QK_NOPE) for h in range(MLA_HEADS)])
    rope = np.concatenate([np.arange(h * per + QK_NOPE, (h + 1) * per) for h in range(MLA_HEADS)])
    return np.concatenate([nope, rope, _swap16(rope)])


def _kv_up_columns():
    per = QK_NOPE + V_HEAD
    kn = np.concatenate([np.arange(h * per, h * per + QK_NOPE) for h in range(MLA_HEADS)])
    vv = np.concatenate([np.arange(h * per + QK_NOPE, (h + 1) * per) for h in range(MLA_HEADS)])
    return np.concatenate([kn, vv])


def _rope_tables(T, rt):
    pos = jnp.arange(T)
    inv_freq = ROPE_THETA ** (-jnp.arange(ROPE_FREQS, dtype=F32) / ROPE_FREQS)
    ang_r = (pos // GRID_W).astype(F32)[:, None] * inv_freq
    ang_c = (pos % GRID_W).astype(F32)[:, None] * inv_freq
    cos64 = jnp.concatenate([jnp.cos(ang_r), jnp.cos(ang_r), jnp.cos(ang_c), jnp.cos(ang_c)], axis=1)
    sin64 = jnp.concatenate([-jnp.sin(ang_r), jnp.sin(ang_r), -jnp.sin(ang_c), jnp.sin(ang_c)], axis=1)
    cos_t = jnp.concatenate([jnp.tile(cos64, (1, 2)), jnp.ones((rt, LANE), F32)], axis=0)
    sin_t = jnp.concatenate([jnp.tile(sin64, (1, 2)), jnp.zeros((rt, LANE), F32)], axis=0)
    return cos_t, sin_t


def _dispatch(ri, n_tok):
    tm = MOE_TILE
    n_asg = n_tok * TOP_K
    r = n_asg + N_EXPERTS * tm
    ids = ri[:n_tok, :TOP_K].reshape(-1)
    counts = jnp.zeros((N_EXPERTS,), jnp.int32).at[ids].add(1)
    gsz = (counts + tm - 1) // tm * tm
    gend = jnp.cumsum(gsz)
    goff = gend - gsz
    cstart = jnp.cumsum(counts) - counts
    order = jnp.argsort(ids, stable=True).astype(jnp.int32)
    sid = ids[order]
    dest_sorted = goff[sid] + jnp.arange(n_asg, dtype=jnp.int32) - cstart[sid]
    src_tok = jnp.zeros((r,), jnp.int32).at[dest_sorted].set(order // TOP_K)
    dest = jnp.zeros((n_asg,), jnp.int32).at[order].set(dest_sorted)
    tile_start = jnp.arange(r // tm, dtype=jnp.int32) * tm
    tile_expert = jnp.minimum(jnp.searchsorted(gend, tile_start, side="right"), N_EXPERTS - 1).astype(jnp.int32)
    n_active = (gend[-1] // tm).astype(jnp.int32).reshape(1)
    return src_tok, dest, tile_expert, n_active


def _layer(xa, mods, B, T, Tc, rt, tm, update_ctx, p, cos_t, sin_t):
    rows = xa.shape[0]
    n_lat = B * T
    lat_rt, all_rt = n_lat // rt, rows // rt
    lat_tm, all_tm = n_lat // tm, rows // tm
    cond_rt = np.concatenate([np.arange(lat_rt) // (T // rt), np.full((all_rt - lat_rt,), B)])
    cond_tm = np.concatenate([np.arange(lat_tm) // (T // tm), np.full((all_tm - lat_tm,), B)])
    modt_rt = mods[cond_rt][:, None, :]
    modt_tm = mods[cond_tm][:, None, :]
    g_norm = p["g_norm"].reshape(4, 1, D_MODEL)

    z, zs = _inproj(xa, g_norm, modt_rt, p["w_in"], rt)
    q, k, v = _mla_project(z, zs, p["g_q_lora"], p["g_kv_lora"], p["w_q_up"], p["w_kv_up"],
                           cos_t, sin_t, rt, lat_rt, T // rt)
    a = _attention_latent(q, k, v, B, T, Tc, min(512, T), min(512, T))
    if update_ctx:
        a = _attention_ctx(q, k, v, a, B, T, Tc)
    m = _mlstm(z, zs, p["b_if"], p["g_mlstm"], B, T, Tc)
    n_rt = all_rt if update_ctx else lat_rt
    n_tm = all_tm if update_ctx else lat_tm
    t = _merge(a, m, z, p["w_proj_a"], p["w_proj_b"], p["w_proj_c"], p["w_conv"], p["b_conv"],
               rt, n_rt, lat_rt, T // rt, T, Tc)
    xa, h2, ri, rw = _post(t, p["w_o"], xa, g_norm, modt_tm, p["w_router"], p["b_router"], tm, n_tm)
    n_tok = rows if update_ctx else n_lat
    src_tok, dest, tile_expert, n_active = _dispatch(ri, n_tok)
    xs = jnp.take(h2, src_tok, axis=0)
    ys = _experts(tile_expert, n_active, xs, p["w_up"], p["b_up"], p["w_down"], p["b_down"])
    yg = jnp.take(ys, dest, axis=0).reshape(n_tok, TOP_K * D_MODEL)
    return _combine(yg, rw, xa, g_norm, modt_tm, tm, n_tm)


def kernel(x, c, ctx, c_ctx, w_mod, b_mod, g_norm, w_in, g_q_lora, g_kv_lora, w_q_up, w_kv_up, b_if, g_mlstm,
           w_conv, b_conv, w_proj_a, w_proj_b, w_proj_c, w_o, w_router, b_router, w_up, b_up, w_down, b_down):
    B, T, _ = x.shape
    Tc = ctx.shape[1]
    depth = w_mod.shape[0]
    rt = min(1024, T)
    tm = min(512, T)
    assert T % rt == 0 and (B * Tc) % rt == 0 and (B * Tc) % tm == 0
    assert T & (T - 1) == 0 and Tc & (Tc - 1) == 0 and T % GRID_W == 0

    xa = jnp.concatenate([x.reshape(B * T, D_MODEL), ctx.reshape(B * Tc, D_MODEL)], axis=0)
    cond = jnp.zeros((8, D_MODEL), F32).at[:B].set(c).at[B].set(c_ctx)
    cos_t, sin_t = _rope_tables(T, rt)
    in_cols, q_cols, kv_cols = _in_proj_columns(), _q_up_columns(), _kv_up_columns()
    cast = lambda w: w.astype(MXU_DTYPE)
    for l in range(depth):
        p = dict(
            g_norm=g_norm[l], w_in=cast(_take_cols(w_in[l], in_cols)),
            g_q_lora=g_q_lora[l], g_kv_lora=g_kv_lora[l],
            w_q_up=cast(_take_cols(w_q_up[l], q_cols)), w_kv_up=cast(_take_cols(w_kv_up[l], kv_cols)),
            b_if=jnp.zeros((1, LANE), F32).at[0, :4 * M_HEADS].set(b_if[l].reshape(-1)),
            g_mlstm=g_mlstm[l], w_conv=w_conv[l], b_conv=b_conv[l],
            w_proj_a=cast(w_proj_a[l]), w_proj_b=cast(w_proj_b[l]), w_proj_c=cast(w_proj_c[l]),
            w_o=cast(w_o[l]),
            w_router=jnp.zeros((D_MODEL, LANE), F32).at[:, :N_EXPERTS].set(w_router[l]),
            b_router=jnp.full((1, LANE), NEG_BIG, F32).at[0, :N_EXPERTS].set(b_router[l]),
            w_up=cast(w_up[l]), b_up=b_up[l], w_down=cast(w_down[l]), b_down=b_down[l],
        )
        mods = _adaln(cond, w_mod[l], b_mod[l])
        xa = _layer(xa, mods, B, T, Tc, rt, tm, l < depth - 1, p, cos_t, sin_t)
    return xa[:B * T].reshape(B, T, D_MODEL)
```

```python
import functools

import numpy as np
import jax
import jax.numpy as jnp
from jax import lax
from jax.experimental import pallas as pl
from jax.experimental.pallas import tpu as pltpu

F32 = jnp.float32
MXU_DTYPE = jnp.bfloat16

D_MODEL = 2048
GRID_W = 64
MLA_HEADS = 8
QK_NOPE = 128
QK_ROPE = 64
V_HEAD = 128
Q_LORA = 512
KV_LORA = 512
ROPE_THETA = 10000.0
ROPE_FREQS = QK_ROPE // 4
ATTN_SCALE = (QK_NOPE + QK_ROPE) ** -0.5
M_HEADS = 4
M_QK = 64
M_V = 128
CONV_WIDTH = 512
N_EXPERTS = 32
TOP_K = 4
D_FF = 1024
SWIGLU_ALPHA = 1.702
SWIGLU_LIMIT = 7.0
N_MOD = 6
EPS = 1e-6

LANE = 128
MLSTM_CHUNK = 256
COL_TILE = 512
MOE_TILE = 512
NEG_BIG = -1e30
VMEM_LIMIT = 56 * 1024 * 1024

ZT_CQ, ZT_CKV, ZT_MQK, ZT_MV, ZT_MO, ZT_CU, ZT_CB, ZT_CC, ZT_CG = 0, 1, 2, 3, 4, 5, 6, 7, 8
N_MAIN_TILES = 8 + 3 * D_MODEL // COL_TILE


def _cparams(n_axes, vmem=VMEM_LIMIT):
    return pltpu.CompilerParams(dimension_semantics=("arbitrary",) * n_axes, vmem_limit_bytes=vmem)


def _rms(x, g):
    return x * lax.rsqrt(jnp.mean(x * x, axis=-1, keepdims=True) + EPS) * g


def _lane_pick(x, idx):
    lane = lax.broadcasted_iota(jnp.int32, x.shape, 1)
    return jnp.sum(jnp.where(lane == idx, x, 0.0), axis=1, keepdims=True)


def _mod_kernel(c_ref, w_ref, b_ref, o_ref):
    c = c_ref[...]
    s = c * jax.nn.sigmoid(c)
    o_ref[0] = jnp.dot(s.astype(MXU_DTYPE), w_ref[0].astype(MXU_DTYPE),
                       preferred_element_type=F32) + b_ref[0]


def _adaln(cond, w_mod, b_mod):
    depth, _, n = w_mod.shape
    tn = 1024
    return pl.pallas_call(
        _mod_kernel,
        grid=(depth, n // tn),
        in_specs=[pl.BlockSpec(cond.shape, lambda l, j: (0, 0)),
                  pl.BlockSpec((1, D_MODEL, tn), lambda l, j: (l, 0, j)),
                  pl.BlockSpec((1, 1, tn), lambda l, j: (l, 0, j))],
        out_specs=pl.BlockSpec((1, cond.shape[0], tn), lambda l, j: (l, 0, j)),
        out_shape=jax.ShapeDtypeStruct((depth, cond.shape[0], n), F32),
        compiler_params=_cparams(2),
        name="adaln",
    )(cond, w_mod, b_mod.reshape(depth, 1, n))


def _inproj_kernel(x_ref, g_ref, sh_ref, sc_ref, w_ref, z_ref, zs_ref, h_scr):
    j = pl.program_id(1)

    @pl.when(j == 0)
    def _():
        h = _rms(x_ref[...], g_ref[0]) * (1.0 + sc_ref[0]) + sh_ref[0]
        h_scr[...] = h.astype(h_scr.dtype)

    acc = jnp.dot(h_scr[...], w_ref[0], preferred_element_type=F32)

    @pl.when(j < N_MAIN_TILES)
    def _():
        z_ref[...] = acc.astype(z_ref.dtype)

    @pl.when(j == N_MAIN_TILES)
    def _():
        zs_ref[...] = acc


def _inproj(xa, g_norm, modt, w_in_p, layer, rt):
    rows = xa.shape[0]
    nt = rows // rt
    return pl.pallas_call(
        _inproj_kernel,
        grid=(nt, N_MAIN_TILES + 1),
        in_specs=[pl.BlockSpec((rt, D_MODEL), lambda i, j: (i, 0)),
                  pl.BlockSpec((1, 1, D_MODEL), lambda i, j: (0, 0, 0)),
                  pl.BlockSpec((1, 1, D_MODEL), lambda i, j: (i, 0, 0)),
                  pl.BlockSpec((1, 1, D_MODEL), lambda i, j: (i, 0, 1)),
                  pl.BlockSpec((1, D_MODEL, COL_TILE), lambda i, j: (layer, 0, j))],
        out_specs=[pl.BlockSpec((rt, COL_TILE), lambda i, j: (i, jnp.minimum(j, N_MAIN_TILES - 1))),
                   pl.BlockSpec((rt, COL_TILE), lambda i, j: (i, 0))],
        out_shape=[jax.ShapeDtypeStruct((rows, N_MAIN_TILES * COL_TILE), MXU_DTYPE),
                   jax.ShapeDtypeStruct((rows, COL_TILE), F32)],
        scratch_shapes=[pltpu.VMEM((rt, D_MODEL), MXU_DTYPE)],
        compiler_params=_cparams(2),
        name="inproj",
    )(xa, g_norm, modt, modt, w_in_p)


def _mla_kernel(cq_ref, ckv_ref, zs_ref, gq_ref, gkv_ref, wq_ref, wkv_ref, cos_ref, sin_ref,
                q_ref, k_ref, v_ref):
    def norm(z_ref, g_ref):
        return _rms(z_ref[...].astype(F32), g_ref[...]).astype(MXU_DTYPE)

    qf = jnp.dot(norm(cq_ref, gq_ref), wq_ref[...], preferred_element_type=F32) * ATTN_SCALE
    kvf = jnp.dot(norm(ckv_ref, gkv_ref), wkv_ref[...], preferred_element_type=F32)
    cos = cos_ref[...]
    sin = sin_ref[...]
    nope_w = MLA_HEADS * QK_NOPE
    rope_w = MLA_HEADS * QK_ROPE
    for p in range(MLA_HEADS // 2):
        a = nope_w + LANE * p
        b = nope_w + rope_w + LANE * p
        rp = (qf[:, a:a + LANE] * cos + qf[:, b:b + LANE] * sin).astype(MXU_DTYPE)
        for h in (2 * p, 2 * p + 1):
            q_ref[:, 256 * h:256 * h + LANE] = qf[:, LANE * h:LANE * (h + 1)].astype(MXU_DTYPE)
            q_ref[:, 256 * h + LANE:256 * (h + 1)] = rp
    kr2 = zs_ref[:, 0:LANE] * cos + zs_ref[:, LANE:2 * LANE] * sin
    lane = lax.broadcasted_iota(jnp.int32, kr2.shape, 1)
    k_lo = jnp.where(lane < QK_ROPE, kr2, 0.0).astype(MXU_DTYPE)
    k_hi = jnp.where(lane >= QK_ROPE, kr2, 0.0).astype(MXU_DTYPE)
    for h in range(MLA_HEADS):
        k_ref[:, 256 * h:256 * h + LANE] = kvf[:, LANE * h:LANE * (h + 1)].astype(MXU_DTYPE)
        k_ref[:, 256 * h + LANE:256 * (h + 1)] = k_lo if h % 2 == 0 else k_hi
    v_ref[...] = kvf[:, MLA_HEADS * QK_NOPE:].astype(MXU_DTYPE)


def _mla_project(z, zs, g_q, g_kv, wq_p, wkv_p, cos_t, sin_t, rt, n_lat_tiles, tiles_per_seq):
    rows = z.shape[0]
    nt = rows // rt

    def tab(i):
        return (jnp.where(i < n_lat_tiles, i % tiles_per_seq, tiles_per_seq), 0)

    return pl.pallas_call(
        _mla_kernel,
        grid=(nt,),
        in_specs=[pl.BlockSpec((rt, COL_TILE), lambda i: (i, ZT_CQ)),
                  pl.BlockSpec((rt, COL_TILE), lambda i: (i, ZT_CKV)),
                  pl.BlockSpec((rt, COL_TILE), lambda i: (i, 0)),
                  pl.BlockSpec((1, Q_LORA), lambda i: (0, 0)),
                  pl.BlockSpec((1, KV_LORA), lambda i: (0, 0)),
                  pl.BlockSpec(wq_p.shape, lambda i: (0, 0)),
                  pl.BlockSpec(wkv_p.shape, lambda i: (0, 0)),
                  pl.BlockSpec((rt, LANE), tab),
                  pl.BlockSpec((rt, LANE), tab)],
        out_specs=[pl.BlockSpec((rt, 256 * MLA_HEADS), lambda i: (i, 0)),
                   pl.BlockSpec((rt, 256 * MLA_HEADS), lambda i: (i, 0)),
                   pl.BlockSpec((rt, V_HEAD * MLA_HEADS), lambda i: (i, 0))],
        out_shape=[jax.ShapeDtypeStruct((rows, 256 * MLA_HEADS), MXU_DTYPE),
                   jax.ShapeDtypeStruct((rows, 256 * MLA_HEADS), MXU_DTYPE),
                   jax.ShapeDtypeStruct((rows, V_HEAD * MLA_HEADS), MXU_DTYPE)],
        compiler_params=_cparams(1),
        name="mla_project",
    )(z, z, zs, g_q.reshape(1, -1), g_kv.reshape(1, -1), wq_p, wkv_p, cos_t, sin_t)


def _attn_kernel(*refs, kv_chunks):
    q_ref = refs[0]
    o_ref = refs[-1]
    q = q_ref[...]
    tq = q.shape[0]
    m = jnp.full((tq, 1), -jnp.inf, F32)
    l = jnp.zeros((tq, 1), F32)
    acc = jnp.zeros((tq, V_HEAD), F32)
    for s, n_chunks in enumerate(kv_chunks):
        k_ref, v_ref = refs[1 + 2 * s], refs[2 + 2 * s]
        ck = k_ref.shape[0] // n_chunks
        for c in range(n_chunks):
            k = k_ref[c * ck:(c + 1) * ck, :]
            v = v_ref[c * ck:(c + 1) * ck, :]
            sc = lax.dot_general(q, k, (((1,), (1,)), ((), ())), preferred_element_type=F32)
            m_new = jnp.maximum(m, jnp.max(sc, axis=-1, keepdims=True))
            p = jnp.exp(sc - m_new)
            alpha = jnp.exp(m - m_new)
            l = alpha * l + jnp.sum(p, axis=-1, keepdims=True)
            acc = alpha * acc + jnp.dot(p.astype(MXU_DTYPE), v, preferred_element_type=F32)
            m = m_new
    o_ref[...] = (acc / l).astype(o_ref.dtype)


def _attention_latent(q, k, v, B, T, Tc, tq, ck):
    rows = q.shape[0]
    qt = T // tq
    ctx_blk0 = B * T // Tc
    return pl.pallas_call(
        functools.partial(_attn_kernel, kv_chunks=(T // ck, 1)),
        grid=(B, MLA_HEADS, qt),
        in_specs=[pl.BlockSpec((tq, 256), lambda b, h, i: (b * qt + i, h)),
                  pl.BlockSpec((T, 256), lambda b, h, i: (b, h)),
                  pl.BlockSpec((T, V_HEAD), lambda b, h, i: (b, h)),
                  pl.BlockSpec((Tc, 256), lambda b, h, i: (ctx_blk0 + b, h)),
                  pl.BlockSpec((Tc, V_HEAD), lambda b, h, i: (ctx_blk0 + b, h))],
        out_specs=pl.BlockSpec((tq, V_HEAD), lambda b, h, i: (b * qt + i, h)),
        out_shape=jax.ShapeDtypeStruct((rows, V_HEAD * MLA_HEADS), MXU_DTYPE),
        compiler_params=_cparams(3),
        name="attn_latent",
    )(q, k, v, k, v)


def _attention_ctx(q, k, v, a_buf, B, T, Tc):
    ctx_blk0 = B * T // Tc
    return pl.pallas_call(
        functools.partial(_attn_kernel, kv_chunks=(1,)),
        grid=(B, MLA_HEADS),
        in_specs=[pl.BlockSpec((Tc, 256), lambda b, h: (ctx_blk0 + b, h)),
                  pl.BlockSpec((Tc, 256), lambda b, h: (ctx_blk0 + b, h)),
                  pl.BlockSpec((Tc, V_HEAD), lambda b, h: (ctx_blk0 + b, h)),
                  pl.BlockSpec(memory_space=pl.ANY)],
        out_specs=pl.BlockSpec((Tc, V_HEAD), lambda b, h: (ctx_blk0 + b, h)),
        out_shape=jax.ShapeDtypeStruct(a_buf.shape, a_buf.dtype),
        input_output_aliases={3: 0},
        compiler_params=_cparams(2),
        name="attn_ctx",
    )(q, k, v, a_buf)


def _mlstm_chunk(qk, v, g, lf_cum, gt_ref, head, reverse, ct_ref, m_prev, consts):
    tri, row_iota, mk, ones_tile, lane = consts
    L = qk.shape[0]
    base = 8 if reverse else 0
    col_i, col_f = base + head, base + 4 + head
    ql = jnp.where(lane < M_QK, qk, jnp.zeros_like(qk))
    kl = jnp.dot(qk, mk, preferred_element_type=F32).astype(MXU_DTYPE)
    v_aug = jnp.concatenate([v, ones_tile], axis=1)
    bc = _lane_pick(lf_cum, col_f)
    li = _lane_pick(g, col_i)
    r_row = gt_ref[0, col_i:col_i + 1, :] - gt_ref[1, col_f:col_f + 1, :]
    dlog = jnp.where(tri, bc + r_row, -jnp.inf)
    inter = bc + m_prev
    m_t = jnp.maximum(inter, jnp.max(dlog, axis=1, keepdims=True))
    dexp = jnp.exp(dlog - m_t)
    s_raw = lax.dot_general(ql, kl, (((1,), (1,)), ((), ())), preferred_element_type=F32)
    s_mat = (s_raw * (M_QK ** -0.5) * dexp).astype(MXU_DTYPE)
    a = jnp.exp(inter - m_t) * (M_QK ** -0.5)
    ct = ct_ref[...]
    nd = (jnp.dot(s_mat, v_aug, preferred_element_type=F32)
          + a * jnp.dot(ql, ct.astype(MXU_DTYPE), preferred_element_type=F32))
    num, den = nd[:, :M_V], nd[:, M_V:]
    h_out = num / jnp.maximum(jnp.abs(den), jnp.exp(-m_t))
    last = 0 if reverse else L - 1
    total = jnp.sum(jnp.where(row_iota == last, bc, 0.0), axis=0, keepdims=True)
    w_col = total - bc + li
    m_new = jnp.maximum(total + m_prev, jnp.max(w_col, axis=0, keepdims=True))
    decay = jnp.exp(total + m_prev - m_new)
    e = jnp.exp(w_col - m_new)
    ek = (kl.astype(F32) * e).astype(MXU_DTYPE)
    upd = lax.dot_general(ek, v_aug, (((0,), (0,)), ((), ())), preferred_element_type=F32)
    ct_ref[...] = decay * ct + upd
    return h_out, m_new


def _mlstm_kernel(zl_qk, zl_v, zl_mo, gl_ref, zc_qk, zc_v, zc_mo, gc_ref, bif_ref, gm_ref,
                  ol_ref, oc_ref, acc_l, acc_c, ct_scr, gt_scr):
    L = MLSTM_CHUNK
    T = zl_qk.shape[0]
    nc = T // L
    r = lax.broadcasted_iota(jnp.int32, (L, L), 0)
    c = lax.broadcasted_iota(jnp.int32, (L, L), 1)
    tri_f = c <= r
    tri_b = c >= r
    tmat_f = tri_f.astype(F32)
    tmat_b = tri_b.astype(F32)
    row_iota = lax.broadcasted_iota(jnp.int32, (L, 1), 0)
    lane = lax.broadcasted_iota(jnp.int32, (L, LANE), 1)
    rr = lax.broadcasted_iota(jnp.int32, (LANE, LANE), 0)
    cc = lax.broadcasted_iota(jnp.int32, (LANE, LANE), 1)
    mk = (rr == cc + M_QK).astype(MXU_DTYPE)
    ones_tile = jnp.ones((L, LANE), MXU_DTYPE)
    consts_f = (tri_f, row_iota, mk, ones_tile, lane)
    consts_b = (tri_b, row_iota, mk, ones_tile, lane)
    bias = bif_ref[...]

    def gate_prep(g_raw, tmat, slot):
        g = g_raw + bias
        log_sig = jnp.minimum(g, 0.0) - jnp.log1p(jnp.exp(-jnp.abs(g)))
        cum = jnp.dot(tmat, log_sig, precision=lax.Precision.HIGHEST, preferred_element_type=F32)
        gt_scr[slot, 0] = g.T
        gt_scr[slot, 1] = cum.T
        return g, cum

    def step(qk_ref, v_ref, g_ref, acc_ref, row_f, row_b, ms):
        g_f, cum_f = gate_prep(g_ref[pl.ds(row_f, L), :], tmat_f, 0)
        g_b, cum_b = gate_prep(g_ref[pl.ds(row_b, L), :], tmat_b, 1)
        new_ms = []
        for h in range(M_HEADS):
            cols = slice(h * LANE, (h + 1) * LANE)
            hf, mf = _mlstm_chunk(qk_ref[pl.ds(row_f, L), cols], v_ref[pl.ds(row_f, L), cols],
                                  g_f, cum_f, gt_scr.at[0], h, False,
                                  ct_scr.at[h], ms[h], consts_f)
            acc_ref[pl.ds(row_f, L), cols] += hf
            hb, mb = _mlstm_chunk(qk_ref[pl.ds(row_b, L), cols], v_ref[pl.ds(row_b, L), cols],
                                  g_b, cum_b, gt_scr.at[1], h, True,
                                  ct_scr.at[M_HEADS + h], ms[M_HEADS + h], consts_b)
            acc_ref[pl.ds(row_b, L), cols] += hb
            new_ms.append((mf, mb))
        return tuple(x[0] for x in new_ms) + tuple(x[1] for x in new_ms)

    ct_scr[...] = jnp.zeros_like(ct_scr)
    acc_l[...] = jnp.zeros_like(acc_l)
    acc_c[...] = jnp.zeros_like(acc_c)
    zero = jnp.zeros((1, 1), F32)
    ms = step(zc_qk, zc_v, gc_ref, acc_c, 0, 0, (zero,) * (2 * M_HEADS))

    def body(i, ms):
        row_f = pl.multiple_of(i * L, L)
        row_b = pl.multiple_of((nc - 1 - i) * L, L)
        return step(zl_qk, zl_v, gl_ref, acc_l, row_f, row_b, ms)

    lax.fori_loop(0, nc, body, ms)

    gm = gm_ref[...]

    def finish(acc_ref, mo_ref, o_ref, row):
        for h in range(M_HEADS):
            cols = slice(h * LANE, (h + 1) * LANE)
            hn = _rms(acc_ref[pl.ds(row, L), cols], gm[:, cols])
            gate = jax.nn.sigmoid(mo_ref[pl.ds(row, L), cols].astype(F32))
            o_ref[pl.ds(row, L), cols] = (hn * gate).astype(o_ref.dtype)

    finish(acc_c, zc_mo, oc_ref, 0)

    def fin_body(i, carry):
        finish(acc_l, zl_mo, ol_ref, pl.multiple_of(i * L, L))
        return carry

    lax.fori_loop(0, nc, fin_body, 0)


def _mlstm(z, zs, bif_row, g_mlstm, B, T, Tc):
    assert Tc == MLSTM_CHUNK and T % MLSTM_CHUNK == 0
    rows = z.shape[0]
    ctx_blk0 = B * T // Tc
    w = M_HEADS * LANE
    out_l, out_c = pl.pallas_call(
        _mlstm_kernel,
        grid=(B,),
        in_specs=[pl.BlockSpec((T, w), lambda b: (b, ZT_MQK)),
                  pl.BlockSpec((T, w), lambda b: (b, ZT_MV)),
                  pl.BlockSpec((T, w), lambda b: (b, ZT_MO)),
                  pl.BlockSpec((T, LANE), lambda b: (b, 2)),
                  pl.BlockSpec((Tc, w), lambda b: (ctx_blk0 + b, ZT_MQK)),
                  pl.BlockSpec((Tc, w), lambda b: (ctx_blk0 + b, ZT_MV)),
                  pl.BlockSpec((Tc, w), lambda b: (ctx_blk0 + b, ZT_MO)),
                  pl.BlockSpec((Tc, LANE), lambda b: (ctx_blk0 + b, 2)),
                  pl.BlockSpec((1, LANE), lambda b: (0, 0)),
                  pl.BlockSpec((1, w), lambda b: (0, 0))],
        out_specs=[pl.BlockSpec((T, w), lambda b: (b, 0)),
                   pl.BlockSpec((Tc, w), lambda b: (b, 0))],
        out_shape=[jax.ShapeDtypeStruct((B * T, w), MXU_DTYPE),
                   jax.ShapeDtypeStruct((B * Tc, w), MXU_DTYPE)],
        scratch_shapes=[pltpu.VMEM((T, w), F32), pltpu.VMEM((Tc, w), F32),
                        pltpu.VMEM((2 * M_HEADS, LANE, 2 * LANE), F32),
                        pltpu.VMEM((2, 2, LANE, MLSTM_CHUNK), F32)],
        compiler_params=_cparams(1),
        name="mlstm",
    )(z, z, z, zs, z, z, z, zs, bif_row, g_mlstm.reshape(1, -1))
    return jnp.concatenate([out_l, out_c], axis=0)


def _merge_kernel(a_ref, m_ref, cu_ref, cb_ref, cc_ref, cu_p, cc_p, cu_n, cc_n, ga_ref, gm_ref, gs_ref,
                  wa_ref, wb_ref, wc_ref, wconv_ref, bconv_ref, t_ref, s_scr,
                  *, n_lat_tiles, tiles_per_seq, seq_len, ctx_len):
    i = pl.program_id(0)
    j = pl.program_id(1)
    rt = a_ref.shape[0]

    @pl.when(j == 0)
    def _():
        u = cc_ref[...].astype(F32) * cu_ref[...].astype(F32)
        hp = cc_p[...].astype(F32) * cu_p[...].astype(F32)
        hn = cc_n[...].astype(F32) * cu_n[...].astype(F32)
        row = lax.broadcasted_iota(jnp.int32, (rt, 1), 0)
        is_ctx = i >= n_lat_tiles
        seg = jnp.where(is_ctx, ctx_len, seq_len)
        off = jnp.where(is_ctx, (i - n_lat_tiles) * rt, (i % tiles_per_seq) * rt)
        pos = (off + row) & (seg - 1)
        up = pltpu.roll(u, 1, 0)
        up = jnp.where(row == 0, hp[hp.shape[0] - 1:, :], up)
        up = jnp.where(pos == 0, 0.0, up)
        un = pltpu.roll(u, rt - 1, 0)
        un = jnp.where(row == rt - 1, hn[0:1, :], un)
        un = jnp.where(pos == seg - 1, 0.0, un)
        wconv = wconv_ref[...]
        y = up * wconv[0:1, :] + u * wconv[1:2, :] + un * wconv[2:3, :] + bconv_ref[...]
        s_scr[...] = (cb_ref[...].astype(F32) * y).astype(s_scr.dtype)

    def sig(ref):
        return jax.nn.sigmoid(ref[...].astype(F32))

    t = (sig(ga_ref) * jnp.dot(a_ref[...], wa_ref[...], preferred_element_type=F32)
         + sig(gm_ref) * jnp.dot(m_ref[...], wb_ref[...], preferred_element_type=F32)
         + sig(gs_ref) * jnp.dot(s_scr[...], wc_ref[...], preferred_element_type=F32))
    t_ref[...] = t.astype(t_ref.dtype)


def _merge(a, m, z, wa, wb, wc, w_conv, b_conv, rt, n_tiles, n_lat_tiles, tiles_per_seq, T, Tc):
    rows = a.shape[0]
    hb = 16
    rb = rt // hb
    last_hb = rows // hb - 1
    tn = COL_TILE
    nj = D_MODEL // tn
    gt = D_MODEL // tn

    def prev_blk(col):
        return lambda i, j: (jnp.maximum(i * rb - 1, 0), col)

    def next_blk(col):
        return lambda i, j: (jnp.minimum((i + 1) * rb, last_hb), col)

    kern = functools.partial(_merge_kernel, n_lat_tiles=n_lat_tiles, tiles_per_seq=tiles_per_seq,
                             seq_len=T, ctx_len=Tc)
    return pl.pallas_call(
        kern,
        grid=(n_tiles, nj),
        in_specs=[pl.BlockSpec((rt, a.shape[1]), lambda i, j: (i, 0)),
                  pl.BlockSpec((rt, m.shape[1]), lambda i, j: (i, 0)),
                  pl.BlockSpec((rt, COL_TILE), lambda i, j: (i, ZT_CU)),
                  pl.BlockSpec((rt, COL_TILE), lambda i, j: (i, ZT_CB)),
                  pl.BlockSpec((rt, COL_TILE), lambda i, j: (i, ZT_CC)),
                  pl.BlockSpec((hb, COL_TILE), prev_blk(ZT_CU)),
                  pl.BlockSpec((hb, COL_TILE), prev_blk(ZT_CC)),
                  pl.BlockSpec((hb, COL_TILE), next_blk(ZT_CU)),
                  pl.BlockSpec((hb, COL_TILE), next_blk(ZT_CC)),
                  pl.BlockSpec((rt, tn), lambda i, j: (i, ZT_CG + j)),
                  pl.BlockSpec((rt, tn), lambda i, j: (i, ZT_CG + gt + j)),
                  pl.BlockSpec((rt, tn), lambda i, j: (i, ZT_CG + 2 * gt + j)),
                  pl.BlockSpec((wa.shape[0], tn), lambda i, j: (0, j)),
                  pl.BlockSpec((wb.shape[0], tn), lambda i, j: (0, j)),
                  pl.BlockSpec((wc.shape[0], tn), lambda i, j: (0, j)),
                  pl.BlockSpec(w_conv.shape, lambda i, j: (0, 0)),
                  pl.BlockSpec((1, CONV_WIDTH), lambda i, j: (0, 0))],
        out_specs=pl.BlockSpec((rt, tn), lambda i, j: (i, j)),
        out_shape=jax.ShapeDtypeStruct((rows, D_MODEL), MXU_DTYPE),
        scratch_shapes=[pltpu.VMEM((rt, CONV_WIDTH), MXU_DTYPE)],
        compiler_params=_cparams(2),
        name="merge",
    )(a, m, z, z, z, z, z, z, z, z, z, z, wa, wb, wc, w_conv, b_conv.reshape(1, -1))


def _dot3(x, w):
    xh = x.astype(jnp.bfloat16)
    xl = (x - xh.astype(F32)).astype(jnp.bfloat16)
    wh = w.astype(jnp.bfloat16)
    wl = (w - wh.astype(F32)).astype(jnp.bfloat16)
    return (jnp.dot(xh, wh, preferred_element_type=F32) + jnp.dot(xh, wl, preferred_element_type=F32)
            + jnp.dot(xl, wh, preferred_element_type=F32))


def _post_kernel(t_ref, wo_ref, x_ref, g1_ref, g2_ref, gt_ref, sh_ref, sc_ref, wr_ref, br_ref,
                 xo_ref, h2_ref, ri_ref, rw_ref, cnt_ref):
    y = jnp.dot(t_ref[...], wo_ref[...], preferred_element_type=F32)
    xn = x_ref[...] + gt_ref[0] * _rms(y, g1_ref[0])
    xo_ref[...] = xn
    h2 = _rms(xn, g2_ref[0]) * (1.0 + sc_ref[0]) + sh_ref[0]
    h2_ref[...] = h2.astype(h2_ref.dtype)
    logits = _dot3(h2, wr_ref[...]) + br_ref[...]
    lane = lax.broadcasted_iota(jnp.int32, logits.shape, 1)
    lane_f = lane.astype(F32)
    vals, ids = [], []
    for _ in range(TOP_K):
        mx = jnp.max(logits, axis=1, keepdims=True)
        idx = jnp.min(jnp.where(logits == mx, lane_f, float(LANE)), axis=1, keepdims=True)
        logits = jnp.where(lane_f == idx, -jnp.inf, logits)
        vals.append(mx)
        ids.append(idx)
    es = [jnp.exp(v - vals[0]) for v in vals]
    tot = es[0] + es[1] + es[2] + es[3]
    tm = logits.shape[0]
    onehots = [lane_f == ids[k] for k in range(TOP_K)]
    asg = jnp.zeros(logits.shape, F32)
    for k in range(TOP_K):
        asg = jnp.where(onehots[k], 1.0, asg)
    earlier = (lax.broadcasted_iota(jnp.int32, (tm, tm), 1)
               < lax.broadcasted_iota(jnp.int32, (tm, tm), 0)).astype(jnp.bfloat16)
    before = jnp.dot(earlier, asg.astype(jnp.bfloat16), preferred_element_type=F32)
    rw = jnp.zeros(logits.shape, F32)
    ri = jnp.zeros(logits.shape, F32)
    for k in range(TOP_K):
        rank = jnp.sum(jnp.where(onehots[k], before, 0.0), axis=1, keepdims=True)
        rw = jnp.where(lane == k, es[k] / tot, rw)
        ri = jnp.where(lane == k, ids[k], ri)
        ri = jnp.where(lane == TOP_K + k, rank, ri)
    rw_ref[...] = rw
    ri_ref[...] = ri.astype(jnp.int32)
    cnt_ref[0] = jnp.broadcast_to(jnp.sum(asg, axis=0, keepdims=True), cnt_ref.shape[1:])


def _post(t, wo, xa, g_norm, modt, wr_p, br_p, tm, n_tiles):
    rows = xa.shape[0]
    row_spec = lambda w: pl.BlockSpec((tm, w), lambda i: (i, 0))
    mod = lambda k: pl.BlockSpec((1, 1, D_MODEL), lambda i: (i, 0, k))
    gn = lambda k: pl.BlockSpec((1, 1, D_MODEL), lambda i: (k, 0, 0))
    return pl.pallas_call(
        _post_kernel,
        grid=(n_tiles,),
        in_specs=[row_spec(D_MODEL), pl.BlockSpec(wo.shape, lambda i: (0, 0)), row_spec(D_MODEL),
                  gn(1), gn(2), mod(2), mod(3), mod(4),
                  pl.BlockSpec(wr_p.shape, lambda i: (0, 0)), pl.BlockSpec((1, LANE), lambda i: (0, 0))],
        out_specs=[row_spec(D_MODEL), row_spec(D_MODEL), row_spec(LANE), row_spec(LANE),
                   pl.BlockSpec((1, 8, LANE), lambda i: (i, 0, 0))],
        out_shape=[jax.ShapeDtypeStruct((rows, D_MODEL), F32),
                   jax.ShapeDtypeStruct((rows, D_MODEL), MXU_DTYPE),
                   jax.ShapeDtypeStruct((rows, LANE), jnp.int32),
                   jax.ShapeDtypeStruct((rows, LANE), F32),
                   jax.ShapeDtypeStruct((rows // tm, 8, LANE), F32)],
        compiler_params=_cparams(1),
        name="post_mix",
    )(t, wo, xa, g_norm, g_norm, modt, modt, modt, wr_p, br_p.reshape(1, LANE))


def _expert_kernel(te_ref, na_ref, x_ref, wu_ref, bu_ref, wd_ref, bd_ref, y_ref):
    t = pl.program_id(0)

    @pl.when(t < na_ref[0])
    def _():
        gu = jnp.dot(x_ref[...], wu_ref[0], preferred_element_type=F32) + bu_ref[0]
        g = jnp.minimum(gu[:, :D_FF], SWIGLU_LIMIT)
        u = jnp.clip(gu[:, D_FF:], -SWIGLU_LIMIT, SWIGLU_LIMIT)
        hid = g * jax.nn.sigmoid(SWIGLU_ALPHA * g) * (u + 1.0)
        y = jnp.dot(hid.astype(MXU_DTYPE), wd_ref[0], preferred_element_type=F32) + bd_ref[0]
        y_ref[...] = y.astype(y_ref.dtype)

    @pl.when(t >= na_ref[0])
    def _():
        y_ref[...] = jnp.zeros_like(y_ref)


def _experts(tile_expert, n_active, xs, w_up, b_up, w_down, b_down, layer):
    r = xs.shape[0]
    tm = MOE_TILE
    base = layer * N_EXPERTS
    grid_spec = pltpu.PrefetchScalarGridSpec(
        num_scalar_prefetch=2,
        grid=(r // tm,),
        in_specs=[pl.BlockSpec((tm, D_MODEL), lambda t, te, na: (t, 0)),
                  pl.BlockSpec((1, D_MODEL, 2 * D_FF), lambda t, te, na: (base + te[t], 0, 0)),
                  pl.BlockSpec((1, 1, 2 * D_FF), lambda t, te, na: (base + te[t], 0, 0)),
                  pl.BlockSpec((1, D_FF, D_MODEL), lambda t, te, na: (base + te[t], 0, 0)),
                  pl.BlockSpec((1, 1, D_MODEL), lambda t, te, na: (base + te[t], 0, 0))],
        out_specs=pl.BlockSpec((tm, D_MODEL), lambda t, te, na: (t, 0)),
    )
    return pl.pallas_call(
        _expert_kernel,
        grid_spec=grid_spec,
        out_shape=jax.ShapeDtypeStruct((r, D_MODEL), MXU_DTYPE),
        compiler_params=_cparams(1),
        name="experts",
    )(tile_expert, n_active, xs, w_up, b_up, w_down, b_down)


def _cast_kernel(w_ref, o_ref):
    o_ref[...] = w_ref[...].astype(o_ref.dtype)


def _cast_weights(w, dtype):
    g, k, n = w.shape
    bk = 512
    return pl.pallas_call(
        _cast_kernel,
        grid=(g, k // bk),
        in_specs=[pl.BlockSpec((1, bk, n), lambda i, j: (i, j, 0))],
        out_specs=pl.BlockSpec((1, bk, n), lambda i, j: (i, j, 0)),
        out_shape=jax.ShapeDtypeStruct(w.shape, dtype),
        compiler_params=_cparams(2),
        name="cast_weights",
    )(w)


def _combine_kernel(yg_ref, rw_ref, x_ref, g3_ref, gt_ref, xo_ref):
    rw = rw_ref[...]
    f = jnp.zeros(x_ref.shape, F32)
    for k in range(TOP_K):
        f = f + _lane_pick(rw, k) * yg_ref[k].astype(F32)
    xo_ref[...] = x_ref[...] + gt_ref[0] * _rms(f, g3_ref[0])


def _combine(yg, rw, xa, g_norm, modt, tm, n_tiles):
    return pl.pallas_call(
        _combine_kernel,
        grid=(n_tiles,),
        in_specs=[pl.BlockSpec((TOP_K, tm, D_MODEL), lambda i: (0, i, 0)),
                  pl.BlockSpec((tm, LANE), lambda i: (i, 0)),
                  pl.BlockSpec((tm, D_MODEL), lambda i: (i, 0)),
                  pl.BlockSpec((1, 1, D_MODEL), lambda i: (3, 0, 0)),
                  pl.BlockSpec((1, 1, D_MODEL), lambda i: (i, 0, 5))],
        out_specs=pl.BlockSpec((tm, D_MODEL), lambda i: (i, 0)),
        out_shape=jax.ShapeDtypeStruct((n_tiles * tm, D_MODEL), F32),
        compiler_params=_cparams(1),
        name="combine",
    )(yg, rw, xa, g_norm, modt)


def _swap16(idx):
    g = np.asarray(idx).reshape(-1, 2, ROPE_FREQS)
    return g[:, ::-1, :].reshape(-1)


def _in_proj_columns():
    sizes = (Q_LORA, KV_LORA, QK_ROPE, M_HEADS * M_QK, M_HEADS * M_QK, M_HEADS * M_V, M_HEADS * M_V,
             2 * 2 * M_HEADS, CONV_WIDTH, CONV_WIDTH, CONV_WIDTH, 3 * D_MODEL)
    starts = np.concatenate([[0], np.cumsum(sizes)[:-1]])
    seg = lambda k: np.arange(starts[k], starts[k] + sizes[k])
    cq, ckv, kr, mq, mk, mv, mo, mif, cu, cb, cc, cg = (seg(k) for k in range(12))
    mqk = np.concatenate([np.concatenate([mq[h * M_QK:(h + 1) * M_QK], mk[h * M_QK:(h + 1) * M_QK]])
                          for h in range(M_HEADS)])
    main = np.concatenate([cq, ckv, mqk, mv, mo, cu, cb, cc, cg])
    pad = lambda n: np.full((n,), -1)
    special = np.concatenate([kr, kr, _swap16(kr), _swap16(kr), mif, pad(LANE - mif.size), pad(LANE)])
    cols = np.concatenate([main, special])
    assert cols.size == (N_MAIN_TILES + 1) * COL_TILE
    return cols


def _take_cols(w, cols, dtype):
    cols = np.asarray(cols)
    breaks = np.flatnonzero((np.diff(cols) != 1) & ~((cols[1:] < 0) & (cols[:-1] < 0))) + 1
    parts = []
    for run in np.split(cols, breaks):
        if run[0] < 0:
            parts.append(jnp.zeros(w.shape[:-1] + (run.size,), dtype))
        else:
            parts.append(w[..., int(run[0]):int(run[-1]) + 1].astype(dtype))
    return jnp.concatenate(parts, axis=-1)


def _q_up_columns():
    per = QK_NOPE + QK_ROPE
    nope = np.concatenate([np.arange(h * per, h * per + QK_NOPE) for h in range(MLA_HEADS)])
    rope = np.concatenate([np.arange(h * per + QK_NOPE, (h + 1) * per) for h in range(MLA_HEADS)])
    return np.concatenate([nope, rope, _swap16(rope)])


def _kv_up_columns():
    per = QK_NOPE + V_HEAD
    kn = np.concatenate([np.arange(h * per, h * per + QK_NOPE) for h in range(MLA_HEADS)])
    vv = np.concatenate([np.arange(h * per + QK_NOPE, (h + 1) * per) for h in range(MLA_HEADS)])
    return np.concatenate([kn, vv])


def _rope_tables(T, rt):
    pos = jnp.arange(T)
    inv_freq = ROPE_THETA ** (-jnp.arange(ROPE_FREQS, dtype=F32) / ROPE_FREQS)
    ang_r = (pos // GRID_W).astype(F32)[:, None] * inv_freq
    ang_c = (pos % GRID_W).astype(F32)[:, None] * inv_freq
    cos64 = jnp.concatenate([jnp.cos(ang_r), jnp.cos(ang_r), jnp.cos(ang_c), jnp.cos(ang_c)], axis=1)
    sin64 = jnp.concatenate([-jnp.sin(ang_r), jnp.sin(ang_r), -jnp.sin(ang_c), jnp.sin(ang_c)], axis=1)
    cos_t = jnp.concatenate([jnp.tile(cos64, (1, 2)), jnp.ones((rt, LANE), F32)], axis=0)
    sin_t = jnp.concatenate([jnp.tile(sin64, (1, 2)), jnp.zeros((rt, LANE), F32)], axis=0)
    return cos_t, sin_t


def _dispatch(ri, cnt, n_tok, tok_tile):
    tm = MOE_TILE
    n_asg = n_tok * TOP_K
    r = n_asg + N_EXPERTS * tm
    counts_te = cnt[:n_tok // tok_tile, 0, :N_EXPERTS].astype(jnp.int32)
    counts = jnp.sum(counts_te, axis=0)
    gsz = (counts + tm - 1) // tm * tm
    gend = jnp.cumsum(gsz)
    base = (gend - gsz)[None, :] + jnp.cumsum(counts_te, axis=0) - counts_te
    base_tok = jnp.repeat(base, tok_tile, axis=0)
    ids, rank = ri[:n_tok, :TOP_K], ri[:n_tok, TOP_K:2 * TOP_K]
    hit = ids[:, :, None] == jnp.arange(N_EXPERTS, dtype=jnp.int32)
    dest = jnp.sum(jnp.where(hit, base_tok[:, None, :], 0), axis=-1) + rank
    filler = jnp.arange(r, dtype=jnp.int32) % n_tok
    src_tok = filler.at[dest.reshape(-1)].set(jnp.arange(n_asg, dtype=jnp.int32) // TOP_K,
                                               unique_indices=True, mode="promise_in_bounds")
    tile_start = jnp.arange(r // tm, dtype=jnp.int32) * tm
    tile_expert = jnp.minimum(jnp.sum(gend[None, :] <= tile_start[:, None], axis=1), N_EXPERTS - 1)
    n_active = (gend[-1] // tm).astype(jnp.int32).reshape(1)
    return src_tok, dest.T, tile_expert.astype(jnp.int32), n_active


def _layer(xa, mods, B, T, Tc, rt, tm, update_ctx, p, cos_t, sin_t, layer):
    rows = xa.shape[0]
    n_lat = B * T
    lat_rt, all_rt = n_lat // rt, rows // rt
    lat_tm, all_tm = n_lat // tm, rows // tm
    cond_rt = np.concatenate([np.arange(lat_rt) // (T // rt), np.full((all_rt - lat_rt,), B)])
    cond_tm = np.concatenate([np.arange(lat_tm) // (T // tm), np.full((all_tm - lat_tm,), B)])
    modt_rt = mods[cond_rt][:, None, :]
    modt_tm = mods[cond_tm][:, None, :]
    g_norm = p["g_norm"].reshape(4, 1, D_MODEL)

    z, zs = _inproj(xa, g_norm, modt_rt, p["w_in"], layer, rt)
    q, k, v = _mla_project(z, zs, p["g_q_lora"], p["g_kv_lora"], p["w_q_up"], p["w_kv_up"],
                           cos_t, sin_t, rt, lat_rt, T // rt)
    a = _attention_latent(q, k, v, B, T, Tc, min(512, T), min(512, T))
    if update_ctx:
        a = _attention_ctx(q, k, v, a, B, T, Tc)
    m = _mlstm(z, zs, p["b_if"], p["g_mlstm"], B, T, Tc)
    n_rt = all_rt if update_ctx else lat_rt
    n_tm = all_tm if update_ctx else lat_tm
    t = _merge(a, m, z, p["w_proj_a"], p["w_proj_b"], p["w_proj_c"], p["w_conv"], p["b_conv"],
               rt, n_rt, lat_rt, T // rt, T, Tc)
    xa, h2, ri, rw, cnt = _post(t, p["w_o"], xa, g_norm, modt_tm, p["w_router"], p["b_router"], tm, n_tm)
    n_tok = rows if update_ctx else n_lat
    src_tok, dest, tile_expert, n_active = _dispatch(ri, cnt, n_tok, tm)
    xs = h2.at[src_tok].get(mode="promise_in_bounds")
    ys = _experts(tile_expert, n_active, xs, p["w_up"], p["b_up"], p["w_down"], p["b_down"], layer)
    yg = ys.at[dest].get(mode="promise_in_bounds")
    return _combine(yg, rw, xa, g_norm, modt_tm, tm, n_tm)


def kernel(x, c, ctx, c_ctx, w_mod, b_mod, g_norm, w_in, g_q_lora, g_kv_lora, w_q_up, w_kv_up, b_if, g_mlstm,
           w_conv, b_conv, w_proj_a, w_proj_b, w_proj_c, w_o, w_router, b_router, w_up, b_up, w_down, b_down):
    B, T, _ = x.shape
    Tc = ctx.shape[1]
    depth = w_mod.shape[0]
    rt = min(1024, T)
    tm = min(512, T)
    assert T % rt == 0 and (B * Tc) % rt == 0 and (B * Tc) % tm == 0
    assert T & (T - 1) == 0 and Tc & (Tc - 1) == 0 and T % GRID_W == 0

    xa = jnp.concatenate([x.reshape(B * T, D_MODEL), ctx.reshape(B * Tc, D_MODEL)], axis=0)
    cond = jnp.zeros((8, D_MODEL), F32).at[:B].set(c).at[B].set(c_ctx)
    cos_t, sin_t = _rope_tables(T, rt)
    cast = lambda w: w.astype(MXU_DTYPE)
    w_in_p = _take_cols(w_in, _in_proj_columns(), MXU_DTYPE)
    w_q_p = _take_cols(w_q_up, _q_up_columns(), MXU_DTYPE)
    w_kv_p = _take_cols(w_kv_up, _kv_up_columns(), MXU_DTYPE)
    w_up_c = _cast_weights(w_up.reshape(depth * N_EXPERTS, D_MODEL, 2 * D_FF), MXU_DTYPE)
    w_down_c = _cast_weights(w_down.reshape(depth * N_EXPERTS, D_FF, D_MODEL), MXU_DTYPE)
    b_up_r = b_up.reshape(depth * N_EXPERTS, 1, 2 * D_FF)
    b_down_r = b_down.reshape(depth * N_EXPERTS, 1, D_MODEL)
    wa_c, wb_c, wc_c, wo_c = cast(w_proj_a), cast(w_proj_b), cast(w_proj_c), cast(w_o)
    w_router_p = jnp.pad(w_router, ((0, 0), (0, 0), (0, LANE - N_EXPERTS)))
    b_router_p = jnp.pad(b_router, ((0, 0), (0, LANE - N_EXPERTS)), constant_values=NEG_BIG)
    b_if_p = jnp.pad(b_if.reshape(depth, 1, -1), ((0, 0), (0, 0), (0, LANE - 4 * M_HEADS)))
    mods = _adaln(cond, w_mod, b_mod)
    for l in range(depth):
        p = dict(
            g_norm=g_norm[l], w_in=w_in_p, g_q_lora=g_q_lora[l], g_kv_lora=g_kv_lora[l],
            w_q_up=w_q_p[l], w_kv_up=w_kv_p[l], b_if=b_if_p[l],
            g_mlstm=g_mlstm[l], w_conv=w_conv[l], b_conv=b_conv[l],
            w_proj_a=wa_c[l], w_proj_b=wb_c[l], w_proj_c=wc_c[l], w_o=wo_c[l],
            w_router=w_router_p[l], b_router=b_router_p[l],
            w_up=w_up_c, b_up=b_up_r, w_down=w_down_c, b_down=b_down_r,
        )
        xa = _layer(xa, mods[l], B, T, Tc, rt, tm, l < depth - 1, p, cos_t, sin_t, l)
    return xa.reshape(B, T, D_MODEL)
```

```python
import functools

import numpy as np
import jax
import jax.numpy as jnp
from jax import lax
from jax.experimental import pallas as pl
from jax.experimental.pallas import tpu as pltpu

F32 = jnp.float32
MXU_DTYPE = jnp.bfloat16

D_MODEL = 2048
GRID_W = 64
MLA_HEADS = 8
QK_NOPE = 128
QK_ROPE = 64
V_HEAD = 128
Q_LORA = 512
KV_LORA = 512
ROPE_THETA = 10000.0
ROPE_FREQS = QK_ROPE // 4
ATTN_SCALE = (QK_NOPE + QK_ROPE) ** -0.5
M_HEADS = 4
M_QK = 64
M_V = 128
CONV_WIDTH = 512
N_EXPERTS = 32
TOP_K = 4
D_FF = 1024
SWIGLU_ALPHA = 1.702
SWIGLU_LIMIT = 7.0
N_MOD = 6
EPS = 1e-6

LANE = 128
MLSTM_CHUNK = 256
COL_TILE = 512
INPROJ_TILE = 1024
LOG2_E = 1.4426950408889634
MOE_TILE = 512
NEG_BIG = -1e30
VMEM_LIMIT = 56 * 1024 * 1024

ZT_CQ, ZT_CKV, ZT_MQK, ZT_MV, ZT_MO, ZT_CU, ZT_CB, ZT_CC, ZT_CG = 0, 1, 2, 3, 4, 5, 6, 7, 8
N_MAIN_TILES = 8 + 3 * D_MODEL // COL_TILE
assert (N_MAIN_TILES * COL_TILE) % INPROJ_TILE == 0


def _cparams(n_axes, vmem=VMEM_LIMIT):
    return pltpu.CompilerParams(dimension_semantics=("arbitrary",) * n_axes, vmem_limit_bytes=vmem)


def _rms(x, g):
    return x * lax.rsqrt(jnp.mean(x * x, axis=-1, keepdims=True) + EPS) * g


def _lane_pick(x, idx):
    lane = lax.broadcasted_iota(jnp.int32, x.shape, 1)
    return jnp.sum(jnp.where(lane == idx, x, 0.0), axis=1, keepdims=True)


def _mod_kernel(c_ref, w_ref, b_ref, o_ref):
    c = c_ref[...]
    s = c * jax.nn.sigmoid(c)
    o_ref[0] = jnp.dot(s.astype(MXU_DTYPE), w_ref[0].astype(MXU_DTYPE),
                       preferred_element_type=F32) + b_ref[0]


def _adaln(cond, w_mod, b_mod):
    depth, _, n = w_mod.shape
    tn = 1024
    return pl.pallas_call(
        _mod_kernel,
        grid=(depth, n // tn),
        in_specs=[pl.BlockSpec(cond.shape, lambda l, j: (0, 0)),
                  pl.BlockSpec((1, D_MODEL, tn), lambda l, j: (l, 0, j)),
                  pl.BlockSpec((1, 1, tn), lambda l, j: (l, 0, j))],
        out_specs=pl.BlockSpec((1, cond.shape[0], tn), lambda l, j: (l, 0, j)),
        out_shape=jax.ShapeDtypeStruct((depth, cond.shape[0], n), F32),
        compiler_params=_cparams(2),
        name="adaln",
    )(cond, w_mod, b_mod.reshape(depth, 1, n))


def _inproj_kernel(x_ref, g_ref, sh_ref, sc_ref, w_ref, ws_ref, z_ref, zs_ref, h_scr):
    @pl.when(pl.program_id(1) == 0)
    def _():
        h = _rms(x_ref[...], g_ref[0]) * (1.0 + sc_ref[0]) + sh_ref[0]
        h_scr[...] = h.astype(h_scr.dtype)
        zs_ref[...] = jnp.dot(h_scr[...], ws_ref[0], preferred_element_type=F32)

    z_ref[...] = jnp.dot(h_scr[...], w_ref[0], preferred_element_type=F32).astype(z_ref.dtype)


def _inproj(xa, g_norm, modt, w_main, w_special, layer, rt):
    rows = xa.shape[0]
    n_main = w_main.shape[2]
    tn = INPROJ_TILE
    return pl.pallas_call(
        _inproj_kernel,
        grid=(rows // rt, n_main // tn),
        in_specs=[pl.BlockSpec((rt, D_MODEL), lambda i, j: (i, 0)),
                  pl.BlockSpec((1, 1, D_MODEL), lambda i, j: (0, 0, 0)),
                  pl.BlockSpec((1, 1, D_MODEL), lambda i, j: (i, 0, 0)),
                  pl.BlockSpec((1, 1, D_MODEL), lambda i, j: (i, 0, 1)),
                  pl.BlockSpec((1, D_MODEL, tn), lambda i, j: (layer, 0, j)),
                  pl.BlockSpec((1, D_MODEL, COL_TILE), lambda i, j: (layer, 0, 0))],
        out_specs=[pl.BlockSpec((rt, tn), lambda i, j: (i, j)),
                   pl.BlockSpec((rt, COL_TILE), lambda i, j: (i, 0))],
        out_shape=[jax.ShapeDtypeStruct((rows, n_main), MXU_DTYPE),
                   jax.ShapeDtypeStruct((rows, COL_TILE), F32)],
        scratch_shapes=[pltpu.VMEM((rt, D_MODEL), MXU_DTYPE)],
        compiler_params=_cparams(2),
        name="inproj",
    )(xa, g_norm, modt, modt, w_main, w_special)


def _mla_kernel(cq_ref, ckv_ref, zs_ref, gq_ref, gkv_ref, wq_ref, wkv_ref, cos_ref, sin_ref,
                q_ref, k_ref, v_ref):
    def norm(z_ref, g_ref):
        return _rms(z_ref[...].astype(F32), g_ref[...]).astype(MXU_DTYPE)

    qf = jnp.dot(norm(cq_ref, gq_ref), wq_ref[...], preferred_element_type=F32) * (ATTN_SCALE * LOG2_E)
    kvf = jnp.dot(norm(ckv_ref, gkv_ref), wkv_ref[...], preferred_element_type=F32)
    cos = cos_ref[...]
    sin = sin_ref[...]
    nope_w = MLA_HEADS * QK_NOPE
    rope_w = MLA_HEADS * QK_ROPE
    for p in range(MLA_HEADS // 2):
        a = nope_w + LANE * p
        b = nope_w + rope_w + LANE * p
        rp = (qf[:, a:a + LANE] * cos + qf[:, b:b + LANE] * sin).astype(MXU_DTYPE)
        for h in (2 * p, 2 * p + 1):
            q_ref[:, 256 * h:256 * h + LANE] = qf[:, LANE * h:LANE * (h + 1)].astype(MXU_DTYPE)
            q_ref[:, 256 * h + LANE:256 * (h + 1)] = rp
    kr2 = zs_ref[:, 0:LANE] * cos + zs_ref[:, LANE:2 * LANE] * sin
    lane = lax.broadcasted_iota(jnp.int32, kr2.shape, 1)
    k_lo = jnp.where(lane < QK_ROPE, kr2, 0.0).astype(MXU_DTYPE)
    k_hi = jnp.where(lane >= QK_ROPE, kr2, 0.0).astype(MXU_DTYPE)
    for h in range(MLA_HEADS):
        k_ref[:, 256 * h:256 * h + LANE] = kvf[:, LANE * h:LANE * (h + 1)].astype(MXU_DTYPE)
        k_ref[:, 256 * h + LANE:256 * (h + 1)] = k_lo if h % 2 == 0 else k_hi
    ones = jnp.ones((kvf.shape[0], LANE), MXU_DTYPE)
    v0 = MLA_HEADS * QK_NOPE
    for h in range(MLA_HEADS):
        v_ref[:, 256 * h:256 * h + LANE] = kvf[:, v0 + LANE * h:v0 + LANE * (h + 1)].astype(MXU_DTYPE)
        v_ref[:, 256 * h + LANE:256 * (h + 1)] = ones


def _mla_project(z, zs, g_q, g_kv, wq_p, wkv_p, cos_t, sin_t, rt, n_lat_tiles, tiles_per_seq):
    rows = z.shape[0]
    nt = rows // rt

    def tab(i):
        return (jnp.where(i < n_lat_tiles, i % tiles_per_seq, tiles_per_seq), 0)

    return pl.pallas_call(
        _mla_kernel,
        grid=(nt,),
        in_specs=[pl.BlockSpec((rt, COL_TILE), lambda i: (i, ZT_CQ)),
                  pl.BlockSpec((rt, COL_TILE), lambda i: (i, ZT_CKV)),
                  pl.BlockSpec((rt, COL_TILE), lambda i: (i, 0)),
                  pl.BlockSpec((1, Q_LORA), lambda i: (0, 0)),
                  pl.BlockSpec((1, KV_LORA), lambda i: (0, 0)),
                  pl.BlockSpec(wq_p.shape, lambda i: (0, 0)),
                  pl.BlockSpec(wkv_p.shape, lambda i: (0, 0)),
                  pl.BlockSpec((rt, LANE), tab),
                  pl.BlockSpec((rt, LANE), tab)],
        out_specs=[pl.BlockSpec((rt, 256 * MLA_HEADS), lambda i: (i, 0)),
                   pl.BlockSpec((rt, 256 * MLA_HEADS), lambda i: (i, 0)),
                   pl.BlockSpec((rt, 256 * MLA_HEADS), lambda i: (i, 0))],
        out_shape=[jax.ShapeDtypeStruct((rows, 256 * MLA_HEADS), MXU_DTYPE)] * 3,
        compiler_params=_cparams(1),
        name="mla_project",
    )(z, z, zs, g_q.reshape(1, -1), g_kv.reshape(1, -1), wq_p, wkv_p, cos_t, sin_t)


def _attn_kernel(*refs, kv_chunks):
    q_ref = refs[0]
    o_ref = refs[-1]
    q = q_ref[...]
    tq = q.shape[0]
    m = jnp.full((tq, 1), -jnp.inf, F32)
    acc = jnp.zeros((tq, 2 * V_HEAD), F32)
    for s, n_chunks in enumerate(kv_chunks):
        k_ref, v_ref = refs[1 + 2 * s], refs[2 + 2 * s]
        ck = k_ref.shape[0] // n_chunks
        for c in range(n_chunks):
            k = k_ref[c * ck:(c + 1) * ck, :]
            v = v_ref[c * ck:(c + 1) * ck, :]
            sc = lax.dot_general(q, k, (((1,), (1,)), ((), ())), preferred_element_type=F32)
            m_new = jnp.maximum(m, jnp.max(sc, axis=-1, keepdims=True))
            p = jnp.exp2(sc - m_new).astype(MXU_DTYPE)
            acc = jnp.exp2(m - m_new) * acc + jnp.dot(p, v, preferred_element_type=F32)
            m = m_new
    o_ref[...] = (acc[:, :V_HEAD] / acc[:, V_HEAD:]).astype(o_ref.dtype)


def _attention_latent(q, k, v, B, T, Tc, tq, ck):
    rows = q.shape[0]
    qt = T // tq
    ctx_blk0 = B * T // Tc
    return pl.pallas_call(
        functools.partial(_attn_kernel, kv_chunks=(T // ck, 1)),
        grid=(B, MLA_HEADS, qt),
        in_specs=[pl.BlockSpec((tq, 256), lambda b, h, i: (b * qt + i, h)),
                  pl.BlockSpec((T, 256), lambda b, h, i: (b, h)),
                  pl.BlockSpec((T, 256), lambda b, h, i: (b, h)),
                  pl.BlockSpec((Tc, 256), lambda b, h, i: (ctx_blk0 + b, h)),
                  pl.BlockSpec((Tc, 256), lambda b, h, i: (ctx_blk0 + b, h))],
        out_specs=pl.BlockSpec((tq, V_HEAD), lambda b, h, i: (b * qt + i, h)),
        out_shape=jax.ShapeDtypeStruct((rows, V_HEAD * MLA_HEADS), MXU_DTYPE),
        compiler_params=_cparams(3),
        name="attn_latent",
    )(q, k, v, k, v)


def _attention_ctx(q, k, v, a_buf, B, T, Tc):
    ctx_blk0 = B * T // Tc
    return pl.pallas_call(
        functools.partial(_attn_kernel, kv_chunks=(1,)),
        grid=(B, MLA_HEADS),
        in_specs=[pl.BlockSpec((Tc, 256), lambda b, h: (ctx_blk0 + b, h)),
                  pl.BlockSpec((Tc, 256), lambda b, h: (ctx_blk0 + b, h)),
                  pl.BlockSpec((Tc, 256), lambda b, h: (ctx_blk0 + b, h)),
                  pl.BlockSpec(memory_space=pl.ANY)],
        out_specs=pl.BlockSpec((Tc, V_HEAD), lambda b, h: (ctx_blk0 + b, h)),
        out_shape=jax.ShapeDtypeStruct(a_buf.shape, a_buf.dtype),
        input_output_aliases={3: 0},
        compiler_params=_cparams(2),
        name="attn_ctx",
    )(q, k, v, a_buf)


def _mlstm_chunk(qk, v, g, lf_cum, gt_ref, head, reverse, ct_ref, m_prev, consts):
    tri, row_iota, mk, ones_tile, lane = consts
    L = qk.shape[0]
    base = 8 if reverse else 0
    col_i, col_f = base + head, base + 4 + head
    ql = jnp.where(lane < M_QK, qk, jnp.zeros_like(qk))
    kl = jnp.dot(qk, mk, preferred_element_type=F32).astype(MXU_DTYPE)
    v_aug = jnp.concatenate([v, ones_tile], axis=1)
    bc = _lane_pick(lf_cum, col_f)
    li = _lane_pick(g, col_i)
    r_row = gt_ref[0, col_i:col_i + 1, :] - gt_ref[1, col_f:col_f + 1, :]
    dlog = jnp.where(tri, bc + r_row, -jnp.inf)
    inter = bc + m_prev
    m_t = jnp.maximum(inter, jnp.max(dlog, axis=1, keepdims=True))
    dexp = jnp.exp(dlog - m_t)
    s_raw = lax.dot_general(ql, kl, (((1,), (1,)), ((), ())), preferred_element_type=F32)
    s_mat = (s_raw * (M_QK ** -0.5) * dexp).astype(MXU_DTYPE)
    a = jnp.exp(inter - m_t) * (M_QK ** -0.5)
    ct = ct_ref[...]
    nd = (jnp.dot(s_mat, v_aug, preferred_element_type=F32)
          + a * jnp.dot(ql, ct.astype(MXU_DTYPE), preferred_element_type=F32))
    num, den = nd[:, :M_V], nd[:, M_V:]
    h_out = num / jnp.maximum(jnp.abs(den), jnp.exp(-m_t))
    last = 0 if reverse else L - 1
    total = jnp.sum(jnp.where(row_iota == last, bc, 0.0), axis=0, keepdims=True)
    w_col = total - bc + li
    m_new = jnp.maximum(total + m_prev, jnp.max(w_col, axis=0, keepdims=True))
    decay = jnp.exp(total + m_prev - m_new)
    e = jnp.exp(w_col - m_new)
    ek = (kl.astype(F32) * e).astype(MXU_DTYPE)
    upd = lax.dot_general(ek, v_aug, (((0,), (0,)), ((), ())), preferred_element_type=F32)
    ct_ref[...] = decay * ct + upd
    return h_out, m_new


def _mlstm_kernel(zl_qk, zl_v, zl_mo, gl_ref, zc_qk, zc_v, zc_mo, gc_ref, bif_ref, gm_ref,
                  ol_ref, oc_ref, acc_l, acc_c, ct_scr, gt_scr):
    L = MLSTM_CHUNK
    T = zl_qk.shape[0]
    nc = T // L
    r = lax.broadcasted_iota(jnp.int32, (L, L), 0)
    c = lax.broadcasted_iota(jnp.int32, (L, L), 1)
    tri_f = c <= r
    tri_b = c >= r
    tmat_f = tri_f.astype(F32)
    tmat_b = tri_b.astype(F32)
    row_iota = lax.broadcasted_iota(jnp.int32, (L, 1), 0)
    lane = lax.broadcasted_iota(jnp.int32, (L, LANE), 1)
    rr = lax.broadcasted_iota(jnp.int32, (LANE, LANE), 0)
    cc = lax.broadcasted_iota(jnp.int32, (LANE, LANE), 1)
    mk = (rr == cc + M_QK).astype(MXU_DTYPE)
    ones_tile = jnp.ones((L, LANE), MXU_DTYPE)
    consts_f = (tri_f, row_iota, mk, ones_tile, lane)
    consts_b = (tri_b, row_iota, mk, ones_tile, lane)
    bias = bif_ref[...]

    def gate_prep(g_raw, tmat, slot):
        g = g_raw + bias
        log_sig = jnp.minimum(g, 0.0) - jnp.log1p(jnp.exp(-jnp.abs(g)))
        cum = jnp.dot(tmat, log_sig, precision=lax.Precision.HIGHEST, preferred_element_type=F32)
        gt_scr[slot, 0] = g.T
        gt_scr[slot, 1] = cum.T
        return g, cum

    def step(qk_ref, v_ref, g_ref, acc_ref, row_f, row_b, ms):
        g_f, cum_f = gate_prep(g_ref[pl.ds(row_f, L), :], tmat_f, 0)
        g_b, cum_b = gate_prep(g_ref[pl.ds(row_b, L), :], tmat_b, 1)
        new_ms = []
        for h in range(M_HEADS):
            cols = slice(h * LANE, (h + 1) * LANE)
            hf, mf = _mlstm_chunk(qk_ref[pl.ds(row_f, L), cols], v_ref[pl.ds(row_f, L), cols],
                                  g_f, cum_f, gt_scr.at[0], h, False,
                                  ct_scr.at[h], ms[h], consts_f)
            acc_ref[pl.ds(row_f, L), cols] += hf
            hb, mb = _mlstm_chunk(qk_ref[pl.ds(row_b, L), cols], v_ref[pl.ds(row_b, L), cols],
                                  g_b, cum_b, gt_scr.at[1], h, True,
                                  ct_scr.at[M_HEADS + h], ms[M_HEADS + h], consts_b)
            acc_ref[pl.ds(row_b, L), cols] += hb
            new_ms.append((mf, mb))
        return tuple(x[0] for x in new_ms) + tuple(x[1] for x in new_ms)

    ct_scr[...] = jnp.zeros_like(ct_scr)
    acc_l[...] = jnp.zeros_like(acc_l)
    acc_c[...] = jnp.zeros_like(acc_c)
    zero = jnp.zeros((1, 1), F32)
    ms = step(zc_qk, zc_v, gc_ref, acc_c, 0, 0, (zero,) * (2 * M_HEADS))

    def body(i, ms):
        row_f = pl.multiple_of(i * L, L)
        row_b = pl.multiple_of((nc - 1 - i) * L, L)
        return step(zl_qk, zl_v, gl_ref, acc_l, row_f, row_b, ms)

    lax.fori_loop(0, nc, body, ms)

    gm = gm_ref[...]

    def finish(acc_ref, mo_ref, o_ref, row):
        for h in range(M_HEADS):
            cols = slice(h * LANE, (h + 1) * LANE)
            hn = _rms(acc_ref[pl.ds(row, L), cols], gm[:, cols])
            gate = jax.nn.sigmoid(mo_ref[pl.ds(row, L), cols].astype(F32))
            o_ref[pl.ds(row, L), cols] = (hn * gate).astype(o_ref.dtype)

    finish(acc_c, zc_mo, oc_ref, 0)

    def fin_body(i, carry):
        finish(acc_l, zl_mo, ol_ref, pl.multiple_of(i * L, L))
        return carry

    lax.fori_loop(0, nc, fin_body, 0)


def _mlstm(z, zs, bif_row, g_mlstm, B, T, Tc):
    assert Tc == MLSTM_CHUNK and T % MLSTM_CHUNK == 0
    rows = z.shape[0]
    ctx_blk0 = B * T // Tc
    w = M_HEADS * LANE
    out_l, out_c = pl.pallas_call(
        _mlstm_kernel,
        grid=(B,),
        in_specs=[pl.BlockSpec((T, w), lambda b: (b, ZT_MQK)),
                  pl.BlockSpec((T, w), lambda b: (b, ZT_MV)),
                  pl.BlockSpec((T, w), lambda b: (b, ZT_MO)),
                  pl.BlockSpec((T, LANE), lambda b: (b, 2)),
                  pl.BlockSpec((Tc, w), lambda b: (ctx_blk0 + b, ZT_MQK)),
                  pl.BlockSpec((Tc, w), lambda b: (ctx_blk0 + b, ZT_MV)),
                  pl.BlockSpec((Tc, w), lambda b: (ctx_blk0 + b, ZT_MO)),
                  pl.BlockSpec((Tc, LANE), lambda b: (ctx_blk0 + b, 2)),
                  pl.BlockSpec((1, LANE), lambda b: (0, 0)),
                  pl.BlockSpec((1, w), lambda b: (0, 0))],
        out_specs=[pl.BlockSpec((T, w), lambda b: (b, 0)),
                   pl.BlockSpec((Tc, w), lambda b: (b, 0))],
        out_shape=[jax.ShapeDtypeStruct((B * T, w), MXU_DTYPE),
                   jax.ShapeDtypeStruct((B * Tc, w), MXU_DTYPE)],
        scratch_shapes=[pltpu.VMEM((T, w), F32), pltpu.VMEM((Tc, w), F32),
                        pltpu.VMEM((2 * M_HEADS, LANE, 2 * LANE), F32),
                        pltpu.VMEM((2, 2, LANE, MLSTM_CHUNK), F32)],
        compiler_params=_cparams(1),
        name="mlstm",
    )(z, z, z, zs, z, z, z, zs, bif_row, g_mlstm.reshape(1, -1))
    return jnp.concatenate([out_l, out_c], axis=0)


def _merge_kernel(a_ref, m_ref, cu_ref, cb_ref, cc_ref, cu_p, cc_p, cu_n, cc_n, ga_ref, gm_ref, gs_ref,
                  wa_ref, wb_ref, wc_ref, wconv_ref, bconv_ref, t_ref, s_scr,
                  *, n_lat_tiles, tiles_per_seq, seq_len, ctx_len):
    i = pl.program_id(0)
    j = pl.program_id(1)
    rt = a_ref.shape[0]

    @pl.when(j == 0)
    def _():
        u = cc_ref[...].astype(F32) * cu_ref[...].astype(F32)
        hp = cc_p[...].astype(F32) * cu_p[...].astype(F32)
        hn = cc_n[...].astype(F32) * cu_n[...].astype(F32)
        row = lax.broadcasted_iota(jnp.int32, (rt, 1), 0)
        is_ctx = i >= n_lat_tiles
        seg = jnp.where(is_ctx, ctx_len, seq_len)
        off = jnp.where(is_ctx, (i - n_lat_tiles) * rt, (i % tiles_per_seq) * rt)
        pos = (off + row) & (seg - 1)
        up = pltpu.roll(u, 1, 0)
        up = jnp.where(row == 0, hp[hp.shape[0] - 1:, :], up)
        up = jnp.where(pos == 0, 0.0, up)
        un = pltpu.roll(u, rt - 1, 0)
        un = jnp.where(row == rt - 1, hn[0:1, :], un)
        un = jnp.where(pos == seg - 1, 0.0, un)
        wconv = wconv_ref[...]
        y = up * wconv[0:1, :] + u * wconv[1:2, :] + un * wconv[2:3, :] + bconv_ref[...]
        s_scr[...] = (cb_ref[...].astype(F32) * y).astype(s_scr.dtype)

    def sig(ref):
        return jax.nn.sigmoid(ref[...].astype(F32))

    t = (sig(ga_ref) * jnp.dot(a_ref[...], wa_ref[...], preferred_element_type=F32)
         + sig(gm_ref) * jnp.dot(m_ref[...], wb_ref[...], preferred_element_type=F32)
         + sig(gs_ref) * jnp.dot(s_scr[...], wc_ref[...], preferred_element_type=F32))
    t_ref[...] = t.astype(t_ref.dtype)


def _merge(a, m, z, wa, wb, wc, w_conv, b_conv, rt, n_tiles, n_lat_tiles, tiles_per_seq, T, Tc):
    rows = a.shape[0]
    hb = 16
    rb = rt // hb
    last_hb = rows // hb - 1
    tn = COL_TILE
    nj = D_MODEL // tn
    gt = D_MODEL // tn

    def prev_blk(col):
        return lambda i, j: (jnp.maximum(i * rb - 1, 0), col)

    def next_blk(col):
        return lambda i, j: (jnp.minimum((i + 1) * rb, last_hb), col)

    kern = functools.partial(_merge_kernel, n_lat_tiles=n_lat_tiles, tiles_per_seq=tiles_per_seq,
                             seq_len=T, ctx_len=Tc)
    return pl.pallas_call(
        kern,
        grid=(n_tiles, nj),
        in_specs=[pl.BlockSpec((rt, a.shape[1]), lambda i, j: (i, 0)),
                  pl.BlockSpec((rt, m.shape[1]), lambda i, j: (i, 0)),
                  pl.BlockSpec((rt, COL_TILE), lambda i, j: (i, ZT_CU)),
                  pl.BlockSpec((rt, COL_TILE), lambda i, j: (i, ZT_CB)),
                  pl.BlockSpec((rt, COL_TILE), lambda i, j: (i, ZT_CC)),
                  pl.BlockSpec((hb, COL_TILE), prev_blk(ZT_CU)),
                  pl.BlockSpec((hb, COL_TILE), prev_blk(ZT_CC)),
                  pl.BlockSpec((hb, COL_TILE), next_blk(ZT_CU)),
                  pl.BlockSpec((hb, COL_TILE), next_blk(ZT_CC)),
                  pl.BlockSpec((rt, tn), lambda i, j: (i, ZT_CG + j)),
                  pl.BlockSpec((rt, tn), lambda i, j: (i, ZT_CG + gt + j)),
                  pl.BlockSpec((rt, tn), lambda i, j: (i, ZT_CG + 2 * gt + j)),
                  pl.BlockSpec((wa.shape[0], tn), lambda i, j: (0, j)),
                  pl.BlockSpec((wb.shape[0], tn), lambda i, j: (0, j)),
                  pl.BlockSpec((wc.shape[0], tn), lambda i, j: (0, j)),
                  pl.BlockSpec(w_conv.shape, lambda i, j: (0, 0)),
                  pl.BlockSpec((1, CONV_WIDTH), lambda i, j: (0, 0))],
        out_specs=pl.BlockSpec((rt, tn), lambda i, j: (i, j)),
        out_shape=jax.ShapeDtypeStruct((rows, D_MODEL), MXU_DTYPE),
        scratch_shapes=[pltpu.VMEM((rt, CONV_WIDTH), MXU_DTYPE)],
        compiler_params=_cparams(2),
        name="merge",
    )(a, m, z, z, z, z, z, z, z, z, z, z, wa, wb, wc, w_conv, b_conv.reshape(1, -1))


def _dot3(x, w):
    xh = x.astype(jnp.bfloat16)
    xl = (x - xh.astype(F32)).astype(jnp.bfloat16)
    wh = w.astype(jnp.bfloat16)
    wl = (w - wh.astype(F32)).astype(jnp.bfloat16)
    return (jnp.dot(xh, wh, preferred_element_type=F32) + jnp.dot(xh, wl, preferred_element_type=F32)
            + jnp.dot(xl, wh, preferred_element_type=F32))


def _post_kernel(t_ref, wo_ref, x_ref, g1_ref, g2_ref, gt_ref, sh_ref, sc_ref, wr_ref, br_ref,
                 xo_ref, h2_ref, ri_ref, rw_ref, cnt_ref):
    y = jnp.dot(t_ref[...], wo_ref[...], preferred_element_type=F32)
    xn = x_ref[...] + gt_ref[0] * _rms(y, g1_ref[0])
    xo_ref[...] = xn
    h2 = _rms(xn, g2_ref[0]) * (1.0 + sc_ref[0]) + sh_ref[0]
    h2_ref[...] = h2.astype(h2_ref.dtype)
    logits = _dot3(h2, wr_ref[...]) + br_ref[...]
    lane = lax.broadcasted_iota(jnp.int32, logits.shape, 1)
    lane_f = lane.astype(F32)
    vals, ids = [], []
    for _ in range(TOP_K):
        mx = jnp.max(logits, axis=1, keepdims=True)
        idx = jnp.min(jnp.where(logits == mx, lane_f, float(LANE)), axis=1, keepdims=True)
        logits = jnp.where(lane_f == idx, -jnp.inf, logits)
        vals.append(mx)
        ids.append(idx)
    es = [jnp.exp(v - vals[0]) for v in vals]
    tot = es[0] + es[1] + es[2] + es[3]
    tm = logits.shape[0]
    onehots = [lane_f == ids[k] for k in range(TOP_K)]
    asg = jnp.zeros(logits.shape, F32)
    for k in range(TOP_K):
        asg = jnp.where(onehots[k], 1.0, asg)
    earlier = (lax.broadcasted_iota(jnp.int32, (tm, tm), 1)
               < lax.broadcasted_iota(jnp.int32, (tm, tm), 0)).astype(jnp.bfloat16)
    before = jnp.dot(earlier, asg.astype(jnp.bfloat16), preferred_element_type=F32)
    rw = jnp.zeros(logits.shape, F32)
    ri = jnp.zeros(logits.shape, F32)
    for k in range(TOP_K):
        rank = jnp.sum(jnp.where(onehots[k], before, 0.0), axis=1, keepdims=True)
        rw = jnp.where(lane == k, es[k] / tot, rw)
        ri = jnp.where(lane == k, ids[k], ri)
        ri = jnp.where(lane == TOP_K + k, rank, ri)
    rw_ref[...] = rw
    ri_ref[...] = ri.astype(jnp.int32)
    cnt_ref[0] = jnp.broadcast_to(jnp.sum(asg, axis=0, keepdims=True), cnt_ref.shape[1:])


def _post(t, wo, xa, g_norm, modt, wr_p, br_p, tm, n_tiles):
    rows = xa.shape[0]
    row_spec = lambda w: pl.BlockSpec((tm, w), lambda i: (i, 0))
    mod = lambda k: pl.BlockSpec((1, 1, D_MODEL), lambda i: (i, 0, k))
    gn = lambda k: pl.BlockSpec((1, 1, D_MODEL), lambda i: (k, 0, 0))
    return pl.pallas_call(
        _post_kernel,
        grid=(n_tiles,),
        in_specs=[row_spec(D_MODEL), pl.BlockSpec(wo.shape, lambda i: (0, 0)), row_spec(D_MODEL),
                  gn(1), gn(2), mod(2), mod(3), mod(4),
                  pl.BlockSpec(wr_p.shape, lambda i: (0, 0)), pl.BlockSpec((1, LANE), lambda i: (0, 0))],
        out_specs=[row_spec(D_MODEL), row_spec(D_MODEL), row_spec(LANE), row_spec(LANE),
                   pl.BlockSpec((1, 8, LANE), lambda i: (i, 0, 0))],
        out_shape=[jax.ShapeDtypeStruct((rows, D_MODEL), F32),
                   jax.ShapeDtypeStruct((rows, D_MODEL), MXU_DTYPE),
                   jax.ShapeDtypeStruct((rows, LANE), jnp.int32),
                   jax.ShapeDtypeStruct((rows, LANE), F32),
                   jax.ShapeDtypeStruct((rows // tm, 8, LANE), F32)],
        compiler_params=_cparams(1),
        name="post_mix",
    )(t, wo, xa, g_norm, g_norm, modt, modt, modt, wr_p, br_p.reshape(1, LANE))


def _expert_kernel(te_ref, na_ref, x_ref, wu_ref, bu_ref, wd_ref, bd_ref, y_ref):
    t = pl.program_id(0)

    @pl.when(t < na_ref[0])
    def _():
        gu = jnp.dot(x_ref[...], wu_ref[0], preferred_element_type=F32) + bu_ref[0]
        g = jnp.minimum(gu[:, :D_FF], SWIGLU_LIMIT)
        u = jnp.clip(gu[:, D_FF:], -SWIGLU_LIMIT, SWIGLU_LIMIT)
        hid = g * jax.nn.sigmoid(SWIGLU_ALPHA * g) * (u + 1.0)
        y = jnp.dot(hid.astype(MXU_DTYPE), wd_ref[0], preferred_element_type=F32) + bd_ref[0]
        y_ref[...] = y.astype(y_ref.dtype)

    @pl.when(t >= na_ref[0])
    def _():
        y_ref[...] = jnp.zeros_like(y_ref)


def _experts(tile_expert, n_active, xs, w_up, b_up, w_down, b_down):
    r = xs.shape[0]
    tm = MOE_TILE
    grid_spec = pltpu.PrefetchScalarGridSpec(
        num_scalar_prefetch=2,
        grid=(r // tm,),
        in_specs=[pl.BlockSpec((tm, D_MODEL), lambda t, te, na: (t, 0)),
                  pl.BlockSpec((1, D_MODEL, 2 * D_FF), lambda t, te, na: (te[t], 0, 0)),
                  pl.BlockSpec((1, 1, 2 * D_FF), lambda t, te, na: (te[t], 0, 0)),
                  pl.BlockSpec((1, D_FF, D_MODEL), lambda t, te, na: (te[t], 0, 0)),
                  pl.BlockSpec((1, 1, D_MODEL), lambda t, te, na: (te[t], 0, 0))],
        out_specs=pl.BlockSpec((tm, D_MODEL), lambda t, te, na: (t, 0)),
    )
    return pl.pallas_call(
        _expert_kernel,
        grid_spec=grid_spec,
        out_shape=jax.ShapeDtypeStruct((r, D_MODEL), MXU_DTYPE),
        compiler_params=_cparams(1),
        name="experts",
    )(tile_expert, n_active, xs, w_up, b_up, w_down, b_down)


def _cast_kernel(w_ref, o_ref):
    o_ref[...] = w_ref[...].astype(o_ref.dtype)


def _cast_weights(w, layer, dtype):
    _, k, n = w.shape
    bk = 512
    return pl.pallas_call(
        _cast_kernel,
        grid=(N_EXPERTS, k // bk),
        in_specs=[pl.BlockSpec((1, bk, n), lambda i, j: (layer * N_EXPERTS + i, j, 0))],
        out_specs=pl.BlockSpec((1, bk, n), lambda i, j: (i, j, 0)),
        out_shape=jax.ShapeDtypeStruct((N_EXPERTS, k, n), dtype),
        compiler_params=_cparams(2),
        name="cast_weights",
    )(w)


def _combine_kernel(yg_ref, rw_ref, x_ref, g3_ref, gt_ref, xo_ref):
    rw = rw_ref[...]
    f = jnp.zeros(x_ref.shape, F32)
    for k in range(TOP_K):
        f = f + _lane_pick(rw, k) * yg_ref[k].astype(F32)
    xo_ref[...] = x_ref[...] + gt_ref[0] * _rms(f, g3_ref[0])


def _combine(yg, rw, xa, g_norm, modt, tm, n_tiles):
    return pl.pallas_call(
        _combine_kernel,
        grid=(n_tiles,),
        in_specs=[pl.BlockSpec((TOP_K, tm, D_MODEL), lambda i: (0, i, 0)),
                  pl.BlockSpec((tm, LANE), lambda i: (i, 0)),
                  pl.BlockSpec((tm, D_MODEL), lambda i: (i, 0)),
                  pl.BlockSpec((1, 1, D_MODEL), lambda i: (3, 0, 0)),
                  pl.BlockSpec((1, 1, D_MODEL), lambda i: (i, 0, 5))],
        out_specs=pl.BlockSpec((tm, D_MODEL), lambda i: (i, 0)),
        out_shape=jax.ShapeDtypeStruct((n_tiles * tm, D_MODEL), F32),
        compiler_params=_cparams(1),
        name="combine",
    )(yg, rw, xa, g_norm, modt)


def _swap16(idx):
    g = np.asarray(idx).reshape(-1, 2, ROPE_FREQS)
    return g[:, ::-1, :].reshape(-1)


def _in_proj_columns():
    sizes = (Q_LORA, KV_LORA, QK_ROPE, M_HEADS * M_QK, M_HEADS * M_QK, M_HEADS * M_V, M_HEADS * M_V,
             2 * 2 * M_HEADS, CONV_WIDTH, CONV_WIDTH, CONV_WIDTH, 3 * D_MODEL)
    starts = np.concatenate([[0], np.cumsum(sizes)[:-1]])
    seg = lambda k: np.arange(starts[k], starts[k] + sizes[k])
    cq, ckv, kr, mq, mk, mv, mo, mif, cu, cb, cc, cg = (seg(k) for k in range(12))
    mqk = np.concatenate([np.concatenate([mq[h * M_QK:(h + 1) * M_QK], mk[h * M_QK:(h + 1) * M_QK]])
                          for h in range(M_HEADS)])
    main = np.concatenate([cq, ckv, mqk, mv, mo, cu, cb, cc, cg])
    pad = lambda n: np.full((n,), -1)
    special = np.concatenate([kr, kr, _swap16(kr), _swap16(kr), mif, pad(LANE - mif.size), pad(LANE)])
    cols = np.concatenate([main, special])
    assert cols.size == (N_MAIN_TILES + 1) * COL_TILE
    return cols


def _take_cols(w, cols, dtype):
    cols = np.asarray(cols)
    breaks = np.flatnonzero((np.diff(cols) != 1) & ~((cols[1:] < 0) & (cols[:-1] < 0))) + 1
    parts = []
    for run in np.split(cols, breaks):
        if run[0] < 0:
            parts.append(jnp.zeros(w.shape[:-1] + (run.size,), dtype))
        else:
            parts.append(w[..., int(run[0]):int(run[-1]) + 1].astype(dtype))
    return jnp.concatenate(parts, axis=-1)


def _q_up_columns():
    per = QK_NOPE + QK_ROPE
    nope = np.concatenate([np.arange(h * per, h * per + QK_NOPE) for h in range(MLA_HEADS)])
    rope = np.concatenate([np.arange(h * per + QK_NOPE, (h + 1) * per) for h in range(MLA_HEADS)])
    return np.concatenate([nope, rope, _swap16(rope)])


def _kv_up_columns():
    per = QK_NOPE + V_HEAD
    kn = np.concatenate([np.arange(h * per, h * per + QK_NOPE) for h in range(MLA_HEADS)])
    vv = np.concatenate([np.arange(h * per + QK_NOPE, (h + 1) * per) for h in range(MLA_HEADS)])
    return np.concatenate([kn, vv])


def _rope_tables(T, rt):
    pos = jnp.arange(T)
    inv_freq = ROPE_THETA ** (-jnp.arange(ROPE_FREQS, dtype=F32) / ROPE_FREQS)
    ang_r = (pos // GRID_W).astype(F32)[:, None] * inv_freq
    ang_c = (pos % GRID_W).astype(F32)[:, None] * inv_freq
    cos64 = jnp.concatenate([jnp.cos(ang_r), jnp.cos(ang_r), jnp.cos(ang_c), jnp.cos(ang_c)], axis=1)
    sin64 = jnp.concatenate([-jnp.sin(ang_r), jnp.sin(ang_r), -jnp.sin(ang_c), jnp.sin(ang_c)], axis=1)
    cos_t = jnp.concatenate([jnp.tile(cos64, (1, 2)), jnp.ones((rt, LANE), F32)], axis=0)
    sin_t = jnp.concatenate([jnp.tile(sin64, (1, 2)), jnp.zeros((rt, LANE), F32)], axis=0)
    return cos_t, sin_t


def _dispatch(ri, cnt, n_tok, tok_tile):
    tm = MOE_TILE
    n_asg = n_tok * TOP_K
    r = n_asg + N_EXPERTS * tm
    counts_te = cnt[:n_tok // tok_tile, 0, :N_EXPERTS].astype(jnp.int32)
    counts = jnp.sum(counts_te, axis=0)
    gsz = (counts + tm - 1) // tm * tm
    gend = jnp.cumsum(gsz)
    base = (gend - gsz)[None, :] + jnp.cumsum(counts_te, axis=0) - counts_te
    base_tok = jnp.repeat(base, tok_tile, axis=0)
    ids, rank = ri[:n_tok, :TOP_K], ri[:n_tok, TOP_K:2 * TOP_K]
    hit = ids[:, :, None] == jnp.arange(N_EXPERTS, dtype=jnp.int32)
    dest = jnp.sum(jnp.where(hit, base_tok[:, None, :], 0), axis=-1) + rank
    filler = jnp.arange(r, dtype=jnp.int32) % n_tok
    src_tok = filler.at[dest.reshape(-1)].set(jnp.arange(n_asg, dtype=jnp.int32) // TOP_K,
                                               unique_indices=True, mode="promise_in_bounds")
    tile_start = jnp.arange(r // tm, dtype=jnp.int32) * tm
    tile_expert = jnp.minimum(jnp.sum(gend[None, :] <= tile_start[:, None], axis=1), N_EXPERTS - 1)
    n_active = (gend[-1] // tm).astype(jnp.int32).reshape(1)
    return src_tok, dest.T, tile_expert.astype(jnp.int32), n_active


def _layer(xa, mods, B, T, Tc, rt, tm, update_ctx, p, cos_t, sin_t, layer):
    rows = xa.shape[0]
    n_lat = B * T
    lat_rt, all_rt = n_lat // rt, rows // rt
    lat_tm, all_tm = n_lat // tm, rows // tm
    cond_rt = np.concatenate([np.arange(lat_rt) // (T // rt), np.full((all_rt - lat_rt,), B)])
    cond_tm = np.concatenate([np.arange(lat_tm) // (T // tm), np.full((all_tm - lat_tm,), B)])
    modt_rt = mods[cond_rt][:, None, :]
    modt_tm = mods[cond_tm][:, None, :]
    g_norm = p["g_norm"].reshape(4, 1, D_MODEL)

    z, zs = _inproj(xa, g_norm, modt_rt, p["w_in_main"], p["w_in_special"], layer, rt)
    q, k, v = _mla_project(z, zs, p["g_q_lora"], p["g_kv_lora"], p["w_q_up"], p["w_kv_up"],
                           cos_t, sin_t, rt, lat_rt, T // rt)
    a = _attention_latent(q, k, v, B, T, Tc, min(512, T), min(512, T))
    if update_ctx:
        a = _attention_ctx(q, k, v, a, B, T, Tc)
    m = _mlstm(z, zs, p["b_if"], p["g_mlstm"], B, T, Tc)
    n_rt = all_rt if update_ctx else lat_rt
    n_tm = all_tm if update_ctx else lat_tm
    t = _merge(a, m, z, p["w_proj_a"], p["w_proj_b"], p["w_proj_c"], p["w_conv"], p["b_conv"],
               rt, n_rt, lat_rt, T // rt, T, Tc)
    xa, h2, ri, rw, cnt = _post(t, p["w_o"], xa, g_norm, modt_tm, p["w_router"], p["b_router"], tm, n_tm)
    n_tok = rows if update_ctx else n_lat
    src_tok, dest, tile_expert, n_active = _dispatch(ri, cnt, n_tok, tm)
    xs = h2.at[src_tok].get(mode="promise_in_bounds")
    w_up_c = _cast_weights(p["w_up"], layer, MXU_DTYPE)
    w_down_c = _cast_weights(p["w_down"], layer, MXU_DTYPE)
    ys = _experts(tile_expert, n_active, xs, w_up_c, p["b_up"], w_down_c, p["b_down"])
    yg = ys.at[dest].get(mode="promise_in_bounds")
    return _combine(yg, rw, xa, g_norm, modt_tm, tm, n_tm)


def kernel(x, c, ctx, c_ctx, w_mod, b_mod, g_norm, w_in, g_q_lora, g_kv_lora, w_q_up, w_kv_up, b_if, g_mlstm,
           w_conv, b_conv, w_proj_a, w_proj_b, w_proj_c, w_o, w_router, b_router, w_up, b_up, w_down, b_down):
    B, T, _ = x.shape
    Tc = ctx.shape[1]
    depth = w_mod.shape[0]
    rt = min(1024, T)
    tm = min(512, T)
    assert T % rt == 0 and (B * Tc) % rt == 0 and (B * Tc) % tm == 0
    assert T & (T - 1) == 0 and Tc & (Tc - 1) == 0 and T % GRID_W == 0

    xa = jnp.concatenate([x.reshape(B * T, D_MODEL), ctx.reshape(B * Tc, D_MODEL)], axis=0)
    cond = jnp.zeros((8, D_MODEL), F32).at[:B].set(c).at[B].set(c_ctx)
    cos_t, sin_t = _rope_tables(T, rt)
    cast = lambda w: w.astype(MXU_DTYPE)
    in_cols = _in_proj_columns()
    n_main = N_MAIN_TILES * COL_TILE
    w_in_main = _take_cols(w_in, in_cols[:n_main], MXU_DTYPE)
    w_in_special = _take_cols(w_in, in_cols[n_main:], MXU_DTYPE)
    w_q_p = _take_cols(w_q_up, _q_up_columns(), MXU_DTYPE)
    w_kv_p = _take_cols(w_kv_up, _kv_up_columns(), MXU_DTYPE)
    w_up_r = w_up.reshape(depth * N_EXPERTS, D_MODEL, 2 * D_FF)
    w_down_r = w_down.reshape(depth * N_EXPERTS, D_FF, D_MODEL)
    wa_c, wb_c, wc_c, wo_c = cast(w_proj_a), cast(w_proj_b), cast(w_proj_c), cast(w_o)
    w_router_p = jnp.pad(w_router, ((0, 0), (0, 0), (0, LANE - N_EXPERTS)))
    b_router_p = jnp.pad(b_router, ((0, 0), (0, LANE - N_EXPERTS)), constant_values=NEG_BIG)
    b_if_p = jnp.pad(b_if.reshape(depth, 1, -1), ((0, 0), (0, 0), (0, LANE - 4 * M_HEADS)))
    mods = _adaln(cond, w_mod, b_mod)
    for l in range(depth):
        p = dict(
            g_norm=g_norm[l], w_in_main=w_in_main, w_in_special=w_in_special,
            g_q_lora=g_q_lora[l], g_kv_lora=g_kv_lora[l],
            w_q_up=w_q_p[l], w_kv_up=w_kv_p[l], b_if=b_if_p[l],
            g_mlstm=g_mlstm[l], w_conv=w_conv[l], b_conv=b_conv[l],
            w_proj_a=wa_c[l], w_proj_b=wb_c[l], w_proj_c=wc_c[l], w_o=wo_c[l],
            w_router=w_router_p[l], b_router=b_router_p[l],
            w_up=w_up_r, b_up=b_up[l].reshape(N_EXPERTS, 1, -1),
            w_down=w_down_r, b_down=b_down[l].reshape(N_EXPERTS, 1, -1),
        )
        xa = _layer(xa, mods[l], B, T, Tc, rt, tm, l < depth - 1, p, cos_t, sin_t, l)
    return xa.reshape(B, T, D_MODEL)
```

```python
import functools

import numpy as np
import jax
import jax.numpy as jnp
from jax import lax
from jax.experimental import pallas as pl
from jax.experimental.pallas import tpu as pltpu

F32 = jnp.float32
MXU_DTYPE = jnp.bfloat16

D_MODEL = 2048
GRID_W = 64
MLA_HEADS = 8
QK_NOPE = 128
QK_ROPE = 64
V_HEAD = 128
Q_LORA = 512
KV_LORA = 512
ROPE_THETA = 10000.0
ROPE_FREQS = QK_ROPE // 4
ATTN_SCALE = (QK_NOPE + QK_ROPE) ** -0.5
M_HEADS = 4
M_QK = 64
M_V = 128
CONV_WIDTH = 512
N_EXPERTS = 32
TOP_K = 4
D_FF = 1024
SWIGLU_ALPHA = 1.702
SWIGLU_LIMIT = 7.0
N_MOD = 6
EPS = 1e-6

LANE = 128
MLSTM_CHUNK = 256
COL_TILE = 512
INPROJ_TILE = 1024
LOG2_E = 1.4426950408889634
MOE_TILE = 512
NEG_BIG = -1e30
VMEM_LIMIT = 56 * 1024 * 1024

ZT_CQ, ZT_CKV, ZT_MQK, ZT_MV, ZT_MO, ZT_CU, ZT_CB, ZT_CC, ZT_CG = 0, 1, 2, 3, 4, 5, 6, 7, 8
N_MAIN_TILES = 8 + 3 * D_MODEL // COL_TILE
assert (N_MAIN_TILES * COL_TILE) % INPROJ_TILE == 0


def _cparams(n_axes, vmem=VMEM_LIMIT):
    return pltpu.CompilerParams(dimension_semantics=("arbitrary",) * n_axes, vmem_limit_bytes=vmem)


def _rms(x, g):
    return x * lax.rsqrt(jnp.mean(x * x, axis=-1, keepdims=True) + EPS) * g


def _lane_pick(x, idx):
    lane = lax.broadcasted_iota(jnp.int32, x.shape, 1)
    return jnp.sum(jnp.where(lane == idx, x, 0.0), axis=1, keepdims=True)


def _mod_kernel(c_ref, w_ref, b_ref, o_ref):
    c = c_ref[...]
    s = c * jax.nn.sigmoid(c)
    o_ref[0] = jnp.dot(s.astype(MXU_DTYPE), w_ref[0].astype(MXU_DTYPE),
                       preferred_element_type=F32) + b_ref[0]


def _adaln(cond, w_mod, b_mod):
    depth, _, n = w_mod.shape
    tn = 1024
    return pl.pallas_call(
        _mod_kernel,
        grid=(depth, n // tn),
        in_specs=[pl.BlockSpec(cond.shape, lambda l, j: (0, 0)),
                  pl.BlockSpec((1, D_MODEL, tn), lambda l, j: (l, 0, j)),
                  pl.BlockSpec((1, 1, tn), lambda l, j: (l, 0, j))],
        out_specs=pl.BlockSpec((1, cond.shape[0], tn), lambda l, j: (l, 0, j)),
        out_shape=jax.ShapeDtypeStruct((depth, cond.shape[0], n), F32),
        compiler_params=_cparams(2),
        name="adaln",
    )(cond, w_mod, b_mod.reshape(depth, 1, n))


def _inproj_kernel(x_ref, g_ref, sh_ref, sc_ref, w_ref, ws_ref, z_ref, zs_ref, h_scr):
    @pl.when(pl.program_id(1) == 0)
    def _():
        h = _rms(x_ref[...], g_ref[0]) * (1.0 + sc_ref[0]) + sh_ref[0]
        h_scr[...] = h.astype(h_scr.dtype)
        zs_ref[...] = jnp.dot(h_scr[...], ws_ref[0], preferred_element_type=F32)

    z_ref[...] = jnp.dot(h_scr[...], w_ref[0], preferred_element_type=F32).astype(z_ref.dtype)


def _inproj(xa, g_norm, modt, w_main, w_special, layer, rt):
    rows = xa.shape[0]
    n_main = w_main.shape[2]
    tn = INPROJ_TILE
    return pl.pallas_call(
        _inproj_kernel,
        grid=(rows // rt, n_main // tn),
        in_specs=[pl.BlockSpec((rt, D_MODEL), lambda i, j: (i, 0)),
                  pl.BlockSpec((1, 1, D_MODEL), lambda i, j: (0, 0, 0)),
                  pl.BlockSpec((1, 1, D_MODEL), lambda i, j: (i, 0, 0)),
                  pl.BlockSpec((1, 1, D_MODEL), lambda i, j: (i, 0, 1)),
                  pl.BlockSpec((1, D_MODEL, tn), lambda i, j: (layer, 0, j)),
                  pl.BlockSpec((1, D_MODEL, COL_TILE), lambda i, j: (layer, 0, 0))],
        out_specs=[pl.BlockSpec((rt, tn), lambda i, j: (i, j)),
                   pl.BlockSpec((rt, COL_TILE), lambda i, j: (i, 0))],
        out_shape=[jax.ShapeDtypeStruct((rows, n_main), MXU_DTYPE),
                   jax.ShapeDtypeStruct((rows, COL_TILE), F32)],
        scratch_shapes=[pltpu.VMEM((rt, D_MODEL), MXU_DTYPE)],
        compiler_params=_cparams(2),
        name="inproj",
    )(xa, g_norm, modt, modt, w_main, w_special)


def _mla_kernel(cq_ref, ckv_ref, zs_ref, gq_ref, gkv_ref, wq_ref, wkv_ref, cos_ref, sin_ref,
                q_ref, k_ref, v_ref):
    def norm(z_ref, g_ref):
        return _rms(z_ref[...].astype(F32), g_ref[...]).astype(MXU_DTYPE)

    qf = jnp.dot(norm(cq_ref, gq_ref), wq_ref[...], preferred_element_type=F32) * (ATTN_SCALE * LOG2_E)
    kvf = jnp.dot(norm(ckv_ref, gkv_ref), wkv_ref[...], preferred_element_type=F32)
    cos = cos_ref[...]
    sin = sin_ref[...]
    nope_w = MLA_HEADS * QK_NOPE
    rope_w = MLA_HEADS * QK_ROPE
    for p in range(MLA_HEADS // 2):
        a = nope_w + LANE * p
        b = nope_w + rope_w + LANE * p
        rp = (qf[:, a:a + LANE] * cos + qf[:, b:b + LANE] * sin).astype(MXU_DTYPE)
        for h in (2 * p, 2 * p + 1):
            q_ref[:, 256 * h:256 * h + LANE] = qf[:, LANE * h:LANE * (h + 1)].astype(MXU_DTYPE)
            q_ref[:, 256 * h + LANE:256 * (h + 1)] = rp
    kr2 = zs_ref[:, 0:LANE] * cos + zs_ref[:, LANE:2 * LANE] * sin
    lane = lax.broadcasted_iota(jnp.int32, kr2.shape, 1)
    k_lo = jnp.where(lane < QK_ROPE, kr2, 0.0).astype(MXU_DTYPE)
    k_hi = jnp.where(lane >= QK_ROPE, kr2, 0.0).astype(MXU_DTYPE)
    for h in range(MLA_HEADS):
        k_ref[:, 256 * h:256 * h + LANE] = kvf[:, LANE * h:LANE * (h + 1)].astype(MXU_DTYPE)
        k_ref[:, 256 * h + LANE:256 * (h + 1)] = k_lo if h % 2 == 0 else k_hi
    ones = jnp.ones((kvf.shape[0], LANE), MXU_DTYPE)
    v0 = MLA_HEADS * QK_NOPE
    for h in range(MLA_HEADS):
        v_ref[:, 256 * h:256 * h + LANE] = kvf[:, v0 + LANE * h:v0 + LANE * (h + 1)].astype(MXU_DTYPE)
        v_ref[:, 256 * h + LANE:256 * (h + 1)] = ones


def _mla_project(z, zs, g_q, g_kv, wq_p, wkv_p, cos_t, sin_t, rt, n_lat_tiles, tiles_per_seq):
    rows = z.shape[0]
    nt = rows // rt

    def tab(i):
        return (jnp.where(i < n_lat_tiles, i % tiles_per_seq, tiles_per_seq), 0)

    return pl.pallas_call(
        _mla_kernel,
        grid=(nt,),
        in_specs=[pl.BlockSpec((rt, COL_TILE), lambda i: (i, ZT_CQ)),
                  pl.BlockSpec((rt, COL_TILE), lambda i: (i, ZT_CKV)),
                  pl.BlockSpec((rt, COL_TILE), lambda i: (i, 0)),
                  pl.BlockSpec((1, Q_LORA), lambda i: (0, 0)),
                  pl.BlockSpec((1, KV_LORA), lambda i: (0, 0)),
                  pl.BlockSpec(wq_p.shape, lambda i: (0, 0)),
                  pl.BlockSpec(wkv_p.shape, lambda i: (0, 0)),
                  pl.BlockSpec((rt, LANE), tab),
                  pl.BlockSpec((rt, LANE), tab)],
        out_specs=[pl.BlockSpec((rt, 256 * MLA_HEADS), lambda i: (i, 0)),
                   pl.BlockSpec((rt, 256 * MLA_HEADS), lambda i: (i, 0)),
                   pl.BlockSpec((rt, 256 * MLA_HEADS), lambda i: (i, 0))],
        out_shape=[jax.ShapeDtypeStruct((rows, 256 * MLA_HEADS), MXU_DTYPE)] * 3,
        compiler_params=_cparams(1),
        name="mla_project",
    )(z, z, zs, g_q.reshape(1, -1), g_kv.reshape(1, -1), wq_p, wkv_p, cos_t, sin_t)


def _attn_kernel(*refs, kv_chunks, n_cast=0):
    q_ref = refs[0]
    o_ref = refs[len(refs) - 1 - n_cast]
    for c in range(n_cast):
        src = refs[1 + 2 * len(kv_chunks) + c]
        dst = refs[len(refs) - n_cast + c]
        dst[...] = src[...].astype(dst.dtype)
    q = q_ref[...]
    tq = q.shape[0]
    m = jnp.full((tq, 1), -jnp.inf, F32)
    acc = jnp.zeros((tq, 2 * V_HEAD), F32)
    for s, n_chunks in enumerate(kv_chunks):
        k_ref, v_ref = refs[1 + 2 * s], refs[2 + 2 * s]
        ck = k_ref.shape[0] // n_chunks
        for c in range(n_chunks):
            k = k_ref[c * ck:(c + 1) * ck, :]
            v = v_ref[c * ck:(c + 1) * ck, :]
            sc = lax.dot_general(q, k, (((1,), (1,)), ((), ())), preferred_element_type=F32)
            m_new = jnp.maximum(m, jnp.max(sc, axis=-1, keepdims=True))
            p = jnp.exp2(sc - m_new).astype(MXU_DTYPE)
            acc = jnp.exp2(m - m_new) * acc + jnp.dot(p, v, preferred_element_type=F32)
            m = m_new
    o_ref[...] = (acc[:, :V_HEAD] / acc[:, V_HEAD:]).astype(o_ref.dtype)


def _attention_latent(q, k, v, w_up, w_down, layer, depth, B, T, Tc, tq, ck):
    rows = q.shape[0]
    qt = T // tq
    ctx_blk0 = B * T // Tc
    steps = B * MLA_HEADS * qt
    step = lambda b, h, i: (b * MLA_HEADS + h) * qt + i
    cast_specs_in, cast_specs_out, cast_shapes = [], [], []
    for w in (w_up, w_down):
        n = w.shape[1]
        layer_rows = w.shape[0] // depth
        slab = layer_rows // steps
        assert slab * steps == layer_rows and slab % 16 == 0
        cast_specs_in.append(pl.BlockSpec((slab, n), lambda b, h, i: (layer * steps + step(b, h, i), 0)))
        cast_specs_out.append(pl.BlockSpec((slab, n), lambda b, h, i: (step(b, h, i), 0)))
        cast_shapes.append(jax.ShapeDtypeStruct((layer_rows, n), MXU_DTYPE))
    return pl.pallas_call(
        functools.partial(_attn_kernel, kv_chunks=(T // ck, 1), n_cast=2),
        grid=(B, MLA_HEADS, qt),
        in_specs=[pl.BlockSpec((tq, 256), lambda b, h, i: (b * qt + i, h)),
                  pl.BlockSpec((T, 256), lambda b, h, i: (b, h)),
                  pl.BlockSpec((T, 256), lambda b, h, i: (b, h)),
                  pl.BlockSpec((Tc, 256), lambda b, h, i: (ctx_blk0 + b, h)),
                  pl.BlockSpec((Tc, 256), lambda b, h, i: (ctx_blk0 + b, h))] + cast_specs_in,
        out_specs=[pl.BlockSpec((tq, V_HEAD), lambda b, h, i: (b * qt + i, h))] + cast_specs_out,
        out_shape=[jax.ShapeDtypeStruct((rows, V_HEAD * MLA_HEADS), MXU_DTYPE)] + cast_shapes,
        compiler_params=_cparams(3),
        name="attn_latent",
    )(q, k, v, k, v, w_up, w_down)


def _attention_ctx(q, k, v, a_buf, B, T, Tc):
    ctx_blk0 = B * T // Tc
    return pl.pallas_call(
        functools.partial(_attn_kernel, kv_chunks=(1,)),
        grid=(B, MLA_HEADS),
        in_specs=[pl.BlockSpec((Tc, 256), lambda b, h: (ctx_blk0 + b, h)),
                  pl.BlockSpec((Tc, 256), lambda b, h: (ctx_blk0 + b, h)),
                  pl.BlockSpec((Tc, 256), lambda b, h: (ctx_blk0 + b, h)),
                  pl.BlockSpec(memory_space=pl.ANY)],
        out_specs=pl.BlockSpec((Tc, V_HEAD), lambda b, h: (ctx_blk0 + b, h)),
        out_shape=jax.ShapeDtypeStruct(a_buf.shape, a_buf.dtype),
        input_output_aliases={3: 0},
        compiler_params=_cparams(2),
        name="attn_ctx",
    )(q, k, v, a_buf)


def _mlstm_chunk(qk, v, g, lf_cum, gt_ref, head, reverse, ct_ref, m_prev, consts):
    tri, row_iota, mk, ones_tile, lane = consts
    L = qk.shape[0]
    base = 8 if reverse else 0
    col_i, col_f = base + head, base + 4 + head
    ql = jnp.where(lane < M_QK, qk, jnp.zeros_like(qk))
    kl = jnp.dot(qk, mk, preferred_element_type=F32).astype(MXU_DTYPE)
    v_aug = jnp.concatenate([v, ones_tile], axis=1)
    bc = _lane_pick(lf_cum, col_f)
    li = _lane_pick(g, col_i)
    r_row = gt_ref[0, col_i:col_i + 1, :] - gt_ref[1, col_f:col_f + 1, :]
    dlog = jnp.where(tri, bc + r_row, -jnp.inf)
    inter = bc + m_prev
    m_t = jnp.maximum(inter, jnp.max(dlog, axis=1, keepdims=True))
    dexp = jnp.exp(dlog - m_t)
    s_raw = lax.dot_general(ql, kl, (((1,), (1,)), ((), ())), preferred_element_type=F32)
    s_mat = (s_raw * (M_QK ** -0.5) * dexp).astype(MXU_DTYPE)
    a = jnp.exp(inter - m_t) * (M_QK ** -0.5)
    ct = ct_ref[...]
    nd = (jnp.dot(s_mat, v_aug, preferred_element_type=F32)
          + a * jnp.dot(ql, ct.astype(MXU_DTYPE), preferred_element_type=F32))
    num, den = nd[:, :M_V], nd[:, M_V:]
    h_out = num / jnp.maximum(jnp.abs(den), jnp.exp(-m_t))
    last = 0 if reverse else L - 1
    total = jnp.sum(jnp.where(row_iota == last, bc, 0.0), axis=0, keepdims=True)
    w_col = total - bc + li
    m_new = jnp.maximum(total + m_prev, jnp.max(w_col, axis=0, keepdims=True))
    decay = jnp.exp(total + m_prev - m_new)
    e = jnp.exp(w_col - m_new)
    ek = (kl.astype(F32) * e).astype(MXU_DTYPE)
    upd = lax.dot_general(ek, v_aug, (((0,), (0,)), ((), ())), preferred_element_type=F32)
    ct_ref[...] = decay * ct + upd
    return h_out, m_new


def _mlstm_kernel(zl_qk, zl_v, zl_mo, gl_ref, zc_qk, zc_v, zc_mo, gc_ref, bif_ref, gm_ref,
                  ol_ref, oc_ref, acc_l, acc_c, ct_scr, gt_scr):
    L = MLSTM_CHUNK
    T = zl_qk.shape[0]
    nc = T // L
    r = lax.broadcasted_iota(jnp.int32, (L, L), 0)
    c = lax.broadcasted_iota(jnp.int32, (L, L), 1)
    tri_f = c <= r
    tri_b = c >= r
    tmat_f = tri_f.astype(F32)
    tmat_b = tri_b.astype(F32)
    row_iota = lax.broadcasted_iota(jnp.int32, (L, 1), 0)
    lane = lax.broadcasted_iota(jnp.int32, (L, LANE), 1)
    rr = lax.broadcasted_iota(jnp.int32, (LANE, LANE), 0)
    cc = lax.broadcasted_iota(jnp.int32, (LANE, LANE), 1)
    mk = (rr == cc + M_QK).astype(MXU_DTYPE)
    ones_tile = jnp.ones((L, LANE), MXU_DTYPE)
    consts_f = (tri_f, row_iota, mk, ones_tile, lane)
    consts_b = (tri_b, row_iota, mk, ones_tile, lane)
    bias = bif_ref[...]

    def gate_prep(g_raw, tmat, slot):
        g = g_raw + bias
        log_sig = jnp.minimum(g, 0.0) - jnp.log1p(jnp.exp(-jnp.abs(g)))
        cum = jnp.dot(tmat, log_sig, precision=lax.Precision.HIGHEST, preferred_element_type=F32)
        gt_scr[slot, 0] = g.T
        gt_scr[slot, 1] = cum.T
        return g, cum

    def step(qk_ref, v_ref, g_ref, acc_ref, row_f, row_b, ms):
        g_f, cum_f = gate_prep(g_ref[pl.ds(row_f, L), :], tmat_f, 0)
        g_b, cum_b = gate_prep(g_ref[pl.ds(row_b, L), :], tmat_b, 1)
        new_ms = []
        for h in range(M_HEADS):
            cols = slice(h * LANE, (h + 1) * LANE)
            hf, mf = _mlstm_chunk(qk_ref[pl.ds(row_f, L), cols], v_ref[pl.ds(row_f, L), cols],
                                  g_f, cum_f, gt_scr.at[0], h, False,
                                  ct_scr.at[h], ms[h], consts_f)
            acc_ref[pl.ds(row_f, L), cols] += hf
            hb, mb = _mlstm_chunk(qk_ref[pl.ds(row_b, L), cols], v_ref[pl.ds(row_b, L), cols],
                                  g_b, cum_b, gt_scr.at[1], h, True,
                                  ct_scr.at[M_HEADS + h], ms[M_HEADS + h], consts_b)
            acc_ref[pl.ds(row_b, L), cols] += hb
            new_ms.append((mf, mb))
        return tuple(x[0] for x in new_ms) + tuple(x[1] for x in new_ms)

    ct_scr[...] = jnp.zeros_like(ct_scr)
    acc_l[...] = jnp.zeros_like(acc_l)
    acc_c[...] = jnp.zeros_like(acc_c)
    zero = jnp.zeros((1, 1), F32)
    ms = step(zc_qk, zc_v, gc_ref, acc_c, 0, 0, (zero,) * (2 * M_HEADS))

    def body(i, ms):
        row_f = pl.multiple_of(i * L, L)
        row_b = pl.multiple_of((nc - 1 - i) * L, L)
        return step(zl_qk, zl_v, gl_ref, acc_l, row_f, row_b, ms)

    lax.fori_loop(0, nc, body, ms)

    gm = gm_ref[...]

    def finish(acc_ref, mo_ref, o_ref, row):
        for h in range(M_HEADS):
            cols = slice(h * LANE, (h + 1) * LANE)
            hn = _rms(acc_ref[pl.ds(row, L), cols], gm[:, cols])
            gate = jax.nn.sigmoid(mo_ref[pl.ds(row, L), cols].astype(F32))
            o_ref[pl.ds(row, L), cols] = (hn * gate).astype(o_ref.dtype)

    finish(acc_c, zc_mo, oc_ref, 0)

    def fin_body(i, carry):
        finish(acc_l, zl_mo, ol_ref, pl.multiple_of(i * L, L))
        return carry

    lax.fori_loop(0, nc, fin_body, 0)


def _mlstm(z, zs, bif_row, g_mlstm, B, T, Tc):
    assert Tc == MLSTM_CHUNK and T % MLSTM_CHUNK == 0
    rows = z.shape[0]
    ctx_blk0 = B * T // Tc
    w = M_HEADS * LANE
    out_l, out_c = pl.pallas_call(
        _mlstm_kernel,
        grid=(B,),
        in_specs=[pl.BlockSpec((T, w), lambda b: (b, ZT_MQK)),
                  pl.BlockSpec((T, w), lambda b: (b, ZT_MV)),
                  pl.BlockSpec((T, w), lambda b: (b, ZT_MO)),
                  pl.BlockSpec((T, LANE), lambda b: (b, 2)),
                  pl.BlockSpec((Tc, w), lambda b: (ctx_blk0 + b, ZT_MQK)),
                  pl.BlockSpec((Tc, w), lambda b: (ctx_blk0 + b, ZT_MV)),
                  pl.BlockSpec((Tc, w), lambda b: (ctx_blk0 + b, ZT_MO)),
                  pl.BlockSpec((Tc, LANE), lambda b: (ctx_blk0 + b, 2)),
                  pl.BlockSpec((1, LANE), lambda b: (0, 0)),
                  pl.BlockSpec((1, w), lambda b: (0, 0))],
        out_specs=[pl.BlockSpec((T, w), lambda b: (b, 0)),
                   pl.BlockSpec((Tc, w), lambda b: (b, 0))],
        out_shape=[jax.ShapeDtypeStruct((B * T, w), MXU_DTYPE),
                   jax.ShapeDtypeStruct((B * Tc, w), MXU_DTYPE)],
        scratch_shapes=[pltpu.VMEM((T, w), F32), pltpu.VMEM((Tc, w), F32),
                        pltpu.VMEM((2 * M_HEADS, LANE, 2 * LANE), F32),
                        pltpu.VMEM((2, 2, LANE, MLSTM_CHUNK), F32)],
        compiler_params=_cparams(1),
        name="mlstm",
    )(z, z, z, zs, z, z, z, zs, bif_row, g_mlstm.reshape(1, -1))
    return jnp.concatenate([out_l, out_c], axis=0)


def _merge_kernel(a_ref, m_ref, cu_ref, cb_ref, cc_ref, cu_p, cc_p, cu_n, cc_n, ga_ref, gm_ref, gs_ref,
                  wa_ref, wb_ref, wc_ref, wconv_ref, bconv_ref, t_ref, s_scr,
                  *, n_lat_tiles, tiles_per_seq, seq_len, ctx_len):
    i = pl.program_id(0)
    j = pl.program_id(1)
    rt = a_ref.shape[0]

    @pl.when(j == 0)
    def _():
        u = cc_ref[...].astype(F32) * cu_ref[...].astype(F32)
        hp = cc_p[...].astype(F32) * cu_p[...].astype(F32)
        hn = cc_n[...].astype(F32) * cu_n[...].astype(F32)
        row = lax.broadcasted_iota(jnp.int32, (rt, 1), 0)
        is_ctx = i >= n_lat_tiles
        seg = jnp.where(is_ctx, ctx_len, seq_len)
        off = jnp.where(is_ctx, (i - n_lat_tiles) * rt, (i % tiles_per_seq) * rt)
        pos = (off + row) & (seg - 1)
        up = pltpu.roll(u, 1, 0)
        up = jnp.where(row == 0, hp[hp.shape[0] - 1:, :], up)
        up = jnp.where(pos == 0, 0.0, up)
        un = pltpu.roll(u, rt - 1, 0)
        un = jnp.where(row == rt - 1, hn[0:1, :], un)
        un = jnp.where(pos == seg - 1, 0.0, un)
        wconv = wconv_ref[...]
        y = up * wconv[0:1, :] + u * wconv[1:2, :] + un * wconv[2:3, :] + bconv_ref[...]
        s_scr[...] = (cb_ref[...].astype(F32) * y).astype(s_scr.dtype)

    def sig(ref):
        return jax.nn.sigmoid(ref[...].astype(F32))

    t = (sig(ga_ref) * jnp.dot(a_ref[...], wa_ref[...], preferred_element_type=F32)
         + sig(gm_ref) * jnp.dot(m_ref[...], wb_ref[...], preferred_element_type=F32)
         + sig(gs_ref) * jnp.dot(s_scr[...], wc_ref[...], preferred_element_type=F32))
    t_ref[...] = t.astype(t_ref.dtype)


def _merge(a, m, z, wa, wb, wc, w_conv, b_conv, rt, n_tiles, n_lat_tiles, tiles_per_seq, T, Tc):
    rows = a.shape[0]
    hb = 16
    rb = rt // hb
    last_hb = rows // hb - 1
    tn = COL_TILE
    nj = D_MODEL // tn
    gt = D_MODEL // tn

    def prev_blk(col):
        return lambda i, j: (jnp.maximum(i * rb - 1, 0), col)

    def next_blk(col):
        return lambda i, j: (jnp.minimum((i + 1) * rb, last_hb), col)

    kern = functools.partial(_merge_kernel, n_lat_tiles=n_lat_tiles, tiles_per_seq=tiles_per_seq,
                             seq_len=T, ctx_len=Tc)
    return pl.pallas_call(
        kern,
        grid=(n_tiles, nj),
        in_specs=[pl.BlockSpec((rt, a.shape[1]), lambda i, j: (i, 0)),
                  pl.BlockSpec((rt, m.shape[1]), lambda i, j: (i, 0)),
                  pl.BlockSpec((rt, COL_TILE), lambda i, j: (i, ZT_CU)),
                  pl.BlockSpec((rt, COL_TILE), lambda i, j: (i, ZT_CB)),
                  pl.BlockSpec((rt, COL_TILE), lambda i, j: (i, ZT_CC)),
                  pl.BlockSpec((hb, COL_TILE), prev_blk(ZT_CU)),
                  pl.BlockSpec((hb, COL_TILE), prev_blk(ZT_CC)),
                  pl.BlockSpec((hb, COL_TILE), next_blk(ZT_CU)),
                  pl.BlockSpec((hb, COL_TILE), next_blk(ZT_CC)),
                  pl.BlockSpec((rt, tn), lambda i, j: (i, ZT_CG + j)),
                  pl.BlockSpec((rt, tn), lambda i, j: (i, ZT_CG + gt + j)),
                  pl.BlockSpec((rt, tn), lambda i, j: (i, ZT_CG + 2 * gt + j)),
                  pl.BlockSpec((wa.shape[0], tn), lambda i, j: (0, j)),
                  pl.BlockSpec((wb.shape[0], tn), lambda i, j: (0, j)),
                  pl.BlockSpec((wc.shape[0], tn), lambda i, j: (0, j)),
                  pl.BlockSpec(w_conv.shape, lambda i, j: (0, 0)),
                  pl.BlockSpec((1, CONV_WIDTH), lambda i, j: (0, 0))],
        out_specs=pl.BlockSpec((rt, tn), lambda i, j: (i, j)),
        out_shape=jax.ShapeDtypeStruct((rows, D_MODEL), MXU_DTYPE),
        scratch_shapes=[pltpu.VMEM((rt, CONV_WIDTH), MXU_DTYPE)],
        compiler_params=_cparams(2),
        name="merge",
    )(a, m, z, z, z, z, z, z, z, z, z, z, wa, wb, wc, w_conv, b_conv.reshape(1, -1))


def _dot3(x, w):
    xh = x.astype(jnp.bfloat16)
    xl = (x - xh.astype(F32)).astype(jnp.bfloat16)
    wh = w.astype(jnp.bfloat16)
    wl = (w - wh.astype(F32)).astype(jnp.bfloat16)
    return (jnp.dot(xh, wh, preferred_element_type=F32) + jnp.dot(xh, wl, preferred_element_type=F32)
            + jnp.dot(xl, wh, preferred_element_type=F32))


def _post_kernel(t_ref, wo_ref, x_ref, g1_ref, g2_ref, gt_ref, sh_ref, sc_ref, wr_ref, br_ref,
                 xo_ref, h2_ref, ri_ref, rw_ref, cnt_ref):
    y = jnp.dot(t_ref[...], wo_ref[...], preferred_element_type=F32)
    xn = x_ref[...] + gt_ref[0] * _rms(y, g1_ref[0])
    xo_ref[...] = xn
    h2 = _rms(xn, g2_ref[0]) * (1.0 + sc_ref[0]) + sh_ref[0]
    h2_ref[...] = h2.astype(h2_ref.dtype)
    logits = _dot3(h2, wr_ref[...]) + br_ref[...]
    lane = lax.broadcasted_iota(jnp.int32, logits.shape, 1)
    lane_f = lane.astype(F32)
    vals, ids = [], []
    for _ in range(TOP_K):
        mx = jnp.max(logits, axis=1, keepdims=True)
        idx = jnp.min(jnp.where(logits == mx, lane_f, float(LANE)), axis=1, keepdims=True)
        logits = jnp.where(lane_f == idx, -jnp.inf, logits)
        vals.append(mx)
        ids.append(idx)
    es = [jnp.exp(v - vals[0]) for v in vals]
    tot = es[0] + es[1] + es[2] + es[3]
    tm = logits.shape[0]
    onehots = [lane_f == ids[k] for k in range(TOP_K)]
    asg = jnp.zeros(logits.shape, F32)
    for k in range(TOP_K):
        asg = jnp.where(onehots[k], 1.0, asg)
    earlier = (lax.broadcasted_iota(jnp.int32, (tm, tm), 1)
               < lax.broadcasted_iota(jnp.int32, (tm, tm), 0)).astype(jnp.bfloat16)
    before = jnp.dot(earlier, asg.astype(jnp.bfloat16), preferred_element_type=F32)
    rw = jnp.zeros(logits.shape, F32)
    ri = jnp.zeros(logits.shape, F32)
    for k in range(TOP_K):
        rank = jnp.sum(jnp.where(onehots[k], before, 0.0), axis=1, keepdims=True)
        rw = jnp.where(lane == k, es[k] / tot, rw)
        ri = jnp.where(lane == k, ids[k], ri)
        ri = jnp.where(lane == TOP_K + k, rank, ri)
    rw_ref[...] = rw
    ri_ref[...] = ri.astype(jnp.int32)
    cnt_ref[0] = jnp.broadcast_to(jnp.sum(asg, axis=0, keepdims=True), cnt_ref.shape[1:])


def _post(t, wo, xa, g_norm, modt, wr_p, br_p, tm, n_tiles):
    rows = xa.shape[0]
    row_spec = lambda w: pl.BlockSpec((tm, w), lambda i: (i, 0))
    mod = lambda k: pl.BlockSpec((1, 1, D_MODEL), lambda i: (i, 0, k))
    gn = lambda k: pl.BlockSpec((1, 1, D_MODEL), lambda i: (k, 0, 0))
    return pl.pallas_call(
        _post_kernel,
        grid=(n_tiles,),
        in_specs=[row_spec(D_MODEL), pl.BlockSpec(wo.shape, lambda i: (0, 0)), row_spec(D_MODEL),
                  gn(1), gn(2), mod(2), mod(3), mod(4),
                  pl.BlockSpec(wr_p.shape, lambda i: (0, 0)), pl.BlockSpec((1, LANE), lambda i: (0, 0))],
        out_specs=[row_spec(D_MODEL), row_spec(D_MODEL), row_spec(LANE), row_spec(LANE),
                   pl.BlockSpec((1, 8, LANE), lambda i: (i, 0, 0))],
        out_shape=[jax.ShapeDtypeStruct((rows, D_MODEL), F32),
                   jax.ShapeDtypeStruct((rows, D_MODEL), MXU_DTYPE),
                   jax.ShapeDtypeStruct((rows, LANE), jnp.int32),
                   jax.ShapeDtypeStruct((rows, LANE), F32),
                   jax.ShapeDtypeStruct((rows // tm, 8, LANE), F32)],
        compiler_params=_cparams(1),
        name="post_mix",
    )(t, wo, xa, g_norm, g_norm, modt, modt, modt, wr_p, br_p.reshape(1, LANE))


def _expert_kernel(te_ref, na_ref, x_ref, wu_ref, bu_ref, wd_ref, bd_ref, y_ref):
    t = pl.program_id(0)

    @pl.when(t < na_ref[0])
    def _():
        gu = jnp.dot(x_ref[...], wu_ref[0], preferred_element_type=F32) + bu_ref[0]
        g = jnp.minimum(gu[:, :D_FF], SWIGLU_LIMIT)
        u = jnp.clip(gu[:, D_FF:], -SWIGLU_LIMIT, SWIGLU_LIMIT)
        hid = g * jax.nn.sigmoid(SWIGLU_ALPHA * g) * (u + 1.0)
        y = jnp.dot(hid.astype(MXU_DTYPE), wd_ref[0], preferred_element_type=F32) + bd_ref[0]
        y_ref[...] = y.astype(y_ref.dtype)

    @pl.when(t >= na_ref[0])
    def _():
        y_ref[...] = jnp.zeros_like(y_ref)


def _experts(tile_expert, n_active, xs, w_up, b_up, w_down, b_down):
    r = xs.shape[0]
    tm = MOE_TILE
    grid_spec = pltpu.PrefetchScalarGridSpec(
        num_scalar_prefetch=2,
        grid=(r // tm,),
        in_specs=[pl.BlockSpec((tm, D_MODEL), lambda t, te, na: (t, 0)),
                  pl.BlockSpec((1, D_MODEL, 2 * D_FF), lambda t, te, na: (te[t], 0, 0)),
                  pl.BlockSpec((1, 1, 2 * D_FF), lambda t, te, na: (te[t], 0, 0)),
                  pl.BlockSpec((1, D_FF, D_MODEL), lambda t, te, na: (te[t], 0, 0)),
                  pl.BlockSpec((1, 1, D_MODEL), lambda t, te, na: (te[t], 0, 0))],
        out_specs=pl.BlockSpec((tm, D_MODEL), lambda t, te, na: (t, 0)),
    )
    return pl.pallas_call(
        _expert_kernel,
        grid_spec=grid_spec,
        out_shape=jax.ShapeDtypeStruct((r, D_MODEL), MXU_DTYPE),
        compiler_params=_cparams(1),
        name="experts",
    )(tile_expert, n_active, xs, w_up, b_up, w_down, b_down)


def _combine_kernel(yg_ref, rw_ref, x_ref, g3_ref, gt_ref, xo_ref):
    rw = rw_ref[...]
    f = jnp.zeros(x_ref.shape, F32)
    for k in range(TOP_K):
        f = f + _lane_pick(rw, k) * yg_ref[k].astype(F32)
    xo_ref[...] = x_ref[...] + gt_ref[0] * _rms(f, g3_ref[0])


def _combine(yg, rw, xa, g_norm, modt, tm, n_tiles):
    return pl.pallas_call(
        _combine_kernel,
        grid=(n_tiles,),
        in_specs=[pl.BlockSpec((TOP_K, tm, D_MODEL), lambda i: (0, i, 0)),
                  pl.BlockSpec((tm, LANE), lambda i: (i, 0)),
                  pl.BlockSpec((tm, D_MODEL), lambda i: (i, 0)),
                  pl.BlockSpec((1, 1, D_MODEL), lambda i: (3, 0, 0)),
                  pl.BlockSpec((1, 1, D_MODEL), lambda i: (i, 0, 5))],
        out_specs=pl.BlockSpec((tm, D_MODEL), lambda i: (i, 0)),
        out_shape=jax.ShapeDtypeStruct((n_tiles * tm, D_MODEL), F32),
        compiler_params=_cparams(1),
        name="combine",
    )(yg, rw, xa, g_norm, modt)


def _swap16(idx):
    g = np.asarray(idx).reshape(-1, 2, ROPE_FREQS)
    return g[:, ::-1, :].reshape(-1)


def _in_proj_columns():
    sizes = (Q_LORA, KV_LORA, QK_ROPE, M_HEADS * M_QK, M_HEADS * M_QK, M_HEADS * M_V, M_HEADS * M_V,
             2 * 2 * M_HEADS, CONV_WIDTH, CONV_WIDTH, CONV_WIDTH, 3 * D_MODEL)
    starts = np.concatenate([[0], np.cumsum(sizes)[:-1]])
    seg = lambda k: np.arange(starts[k], starts[k] + sizes[k])
    cq, ckv, kr, mq, mk, mv, mo, mif, cu, cb, cc, cg = (seg(k) for k in range(12))
    mqk = np.concatenate([np.concatenate([mq[h * M_QK:(h + 1) * M_QK], mk[h * M_QK:(h + 1) * M_QK]])
                          for h in range(M_HEADS)])
    main = np.concatenate([cq, ckv, mqk, mv, mo, cu, cb, cc, cg])
    pad = lambda n: np.full((n,), -1)
    special = np.concatenate([kr, kr, _swap16(kr), _swap16(kr), mif, pad(LANE - mif.size), pad(LANE)])
    cols = np.concatenate([main, special])
    assert cols.size == (N_MAIN_TILES + 1) * COL_TILE
    return cols


def _take_cols(w, cols, dtype):
    cols = np.asarray(cols)
    breaks = np.flatnonzero((np.diff(cols) != 1) & ~((cols[1:] < 0) & (cols[:-1] < 0))) + 1
    parts = []
    for run in np.split(cols, breaks):
        if run[0] < 0:
            parts.append(jnp.zeros(w.shape[:-1] + (run.size,), dtype))
        else:
            parts.append(w[..., int(run[0]):int(run[-1]) + 1].astype(dtype))
    return jnp.concatenate(parts, axis=-1)


def _q_up_columns():
    per = QK_NOPE + QK_ROPE
    nope = np.concatenate([np.arange(h * per, h * per + QK_NOPE) for h in range(MLA_HEADS)])
    rope = np.concatenate([np.arange(h * per + QK_NOPE, (h + 1) * per) for h in range(MLA_HEADS)])
    return np.concatenate([nope, rope, _swap16(rope)])


def _kv_up_columns():
    per = QK_NOPE + V_HEAD
    kn = np.concatenate([np.arange(h * per, h * per + QK_NOPE) for h in range(MLA_HEADS)])
    vv = np.concatenate([np.arange(h * per + QK_NOPE, (h + 1) * per) for h in range(MLA_HEADS)])
    return np.concatenate([kn, vv])


def _rope_tables(T, rt):
    pos = jnp.arange(T)
    inv_freq = ROPE_THETA ** (-jnp.arange(ROPE_FREQS, dtype=F32) / ROPE_FREQS)
    ang_r = (pos // GRID_W).astype(F32)[:, None] * inv_freq
    ang_c = (pos % GRID_W).astype(F32)[:, None] * inv_freq
    cos64 = jnp.concatenate([jnp.cos(ang_r), jnp.cos(ang_r), jnp.cos(ang_c), jnp.cos(ang_c)], axis=1)
    sin64 = jnp.concatenate([-jnp.sin(ang_r), jnp.sin(ang_r), -jnp.sin(ang_c), jnp.sin(ang_c)], axis=1)
    cos_t = jnp.concatenate([jnp.tile(cos64, (1, 2)), jnp.ones((rt, LANE), F32)], axis=0)
    sin_t = jnp.concatenate([jnp.tile(sin64, (1, 2)), jnp.zeros((rt, LANE), F32)], axis=0)
    return cos_t, sin_t


def _dispatch(ri, cnt, n_tok, tok_tile):
    tm = MOE_TILE
    n_asg = n_tok * TOP_K
    r = n_asg + N_EXPERTS * tm
    counts_te = cnt[:n_tok // tok_tile, 0, :N_EXPERTS].astype(jnp.int32)
    counts = jnp.sum(counts_te, axis=0)
    gsz = (counts + tm - 1) // tm * tm
    gend = jnp.cumsum(gsz)
    base = (gend - gsz)[None, :] + jnp.cumsum(counts_te, axis=0) - counts_te
    base_tok = jnp.repeat(base, tok_tile, axis=0)
    ids, rank = ri[:n_tok, :TOP_K], ri[:n_tok, TOP_K:2 * TOP_K]
    hit = ids[:, :, None] == jnp.arange(N_EXPERTS, dtype=jnp.int32)
    dest = jnp.sum(jnp.where(hit, base_tok[:, None, :], 0), axis=-1) + rank
    filler = jnp.arange(r, dtype=jnp.int32) % n_tok
    src_tok = filler.at[dest.reshape(-1)].set(jnp.arange(n_asg, dtype=jnp.int32) // TOP_K,
                                               unique_indices=True, mode="promise_in_bounds")
    tile_start = jnp.arange(r // tm, dtype=jnp.int32) * tm
    tile_expert = jnp.minimum(jnp.sum(gend[None, :] <= tile_start[:, None], axis=1), N_EXPERTS - 1)
    n_active = (gend[-1] // tm).astype(jnp.int32).reshape(1)
    return src_tok, dest.T, tile_expert.astype(jnp.int32), n_active


def _layer(xa, mods, B, T, Tc, rt, tm, update_ctx, p, cos_t, sin_t, layer):
    rows = xa.shape[0]
    n_lat = B * T
    lat_rt, all_rt = n_lat // rt, rows // rt
    lat_tm, all_tm = n_lat // tm, rows // tm
    cond_rt = np.concatenate([np.arange(lat_rt) // (T // rt), np.full((all_rt - lat_rt,), B)])
    cond_tm = np.concatenate([np.arange(lat_tm) // (T // tm), np.full((all_tm - lat_tm,), B)])
    modt_rt = mods[cond_rt][:, None, :]
    modt_tm = mods[cond_tm][:, None, :]
    g_norm = p["g_norm"].reshape(4, 1, D_MODEL)

    z, zs = _inproj(xa, g_norm, modt_rt, p["w_in_main"], p["w_in_special"], layer, rt)
    q, k, v = _mla_project(z, zs, p["g_q_lora"], p["g_kv_lora"], p["w_q_up"], p["w_kv_up"],
                           cos_t, sin_t, rt, lat_rt, T // rt)
    a, w_up_c, w_down_c = _attention_latent(q, k, v, p["w_up"], p["w_down"], layer, p["depth"],
                                            B, T, Tc, min(512, T), min(512, T))
    w_up_c = w_up_c.reshape(N_EXPERTS, D_MODEL, 2 * D_FF)
    w_down_c = w_down_c.reshape(N_EXPERTS, D_FF, D_MODEL)
    if update_ctx:
        a = _attention_ctx(q, k, v, a, B, T, Tc)
    m = _mlstm(z, zs, p["b_if"], p["g_mlstm"], B, T, Tc)
    n_rt = all_rt if update_ctx else lat_rt
    n_tm = all_tm if update_ctx else lat_tm
    t = _merge(a, m, z, p["w_proj_a"], p["w_proj_b"], p["w_proj_c"], p["w_conv"], p["b_conv"],
               rt, n_rt, lat_rt, T // rt, T, Tc)
    xa, h2, ri, rw, cnt = _post(t, p["w_o"], xa, g_norm, modt_tm, p["w_router"], p["b_router"], tm, n_tm)
    n_tok = rows if update_ctx else n_lat
    src_tok, dest, tile_expert, n_active = _dispatch(ri, cnt, n_tok, tm)
    xs = h2.at[src_tok].get(mode="promise_in_bounds")
    ys = _experts(tile_expert, n_active, xs, w_up_c, p["b_up"], w_down_c, p["b_down"])
    yg = ys.at[dest].get(mode="promise_in_bounds")
    return _combine(yg, rw, xa, g_norm, modt_tm, tm, n_tm)


def kernel(x, c, ctx, c_ctx, w_mod, b_mod, g_norm, w_in, g_q_lora, g_kv_lora, w_q_up, w_kv_up, b_if, g_mlstm,
           w_conv, b_conv, w_proj_a, w_proj_b, w_proj_c, w_o, w_router, b_router, w_up, b_up, w_down, b_down):
    B, T, _ = x.shape
    Tc = ctx.shape[1]
    depth = w_mod.shape[0]
    rt = min(1024, T)
    tm = min(512, T)
    assert T % rt == 0 and (B * Tc) % rt == 0 and (B * Tc) % tm == 0
    assert T & (T - 1) == 0 and Tc & (Tc - 1) == 0 and T % GRID_W == 0

    xa = jnp.concatenate([x.reshape(B * T, D_MODEL), ctx.reshape(B * Tc, D_MODEL)], axis=0)
    cond = jnp.zeros((8, D_MODEL), F32).at[:B].set(c).at[B].set(c_ctx)
    cos_t, sin_t = _rope_tables(T, rt)
    cast = lambda w: w.astype(MXU_DTYPE)
    in_cols = _in_proj_columns()
    n_main = N_MAIN_TILES * COL_TILE
    w_in_main = _take_cols(w_in, in_cols[:n_main], MXU_DTYPE)
    w_in_special = _take_cols(w_in, in_cols[n_main:], MXU_DTYPE)
    w_q_p = _take_cols(w_q_up, _q_up_columns(), MXU_DTYPE)
    w_kv_p = _take_cols(w_kv_up, _kv_up_columns(), MXU_DTYPE)
    w_up_r = w_up.reshape(depth * N_EXPERTS * D_MODEL, 2 * D_FF)
    w_down_r = w_down.reshape(depth * N_EXPERTS * D_FF, D_MODEL)
    wa_c, wb_c, wc_c, wo_c = cast(w_proj_a), cast(w_proj_b), cast(w_proj_c), cast(w_o)
    w_router_p = jnp.pad(w_router, ((0, 0), (0, 0), (0, LANE - N_EXPERTS)))
    b_router_p = jnp.pad(b_router, ((0, 0), (0, LANE - N_EXPERTS)), constant_values=NEG_BIG)
    b_if_p = jnp.pad(b_if.reshape(depth, 1, -1), ((0, 0), (0, 0), (0, LANE - 4 * M_HEADS)))
    mods = _adaln(cond, w_mod, b_mod)
    for l in range(depth):
        p = dict(
            g_norm=g_norm[l], w_in_main=w_in_main, w_in_special=w_in_special,
            g_q_lora=g_q_lora[l], g_kv_lora=g_kv_lora[l],
            w_q_up=w_q_p[l], w_kv_up=w_kv_p[l], b_if=b_if_p[l],
            g_mlstm=g_mlstm[l], w_conv=w_conv[l], b_conv=b_conv[l],
            w_proj_a=wa_c[l], w_proj_b=wb_c[l], w_proj_c=wc_c[l], w_o=wo_c[l],
            w_router=w_router_p[l], b_router=b_router_p[l],
            depth=depth, w_up=w_up_r, b_up=b_up[l].reshape(N_EXPERTS, 1, -1),
            w_down=w_down_r, b_down=b_down[l].reshape(N_EXPERTS, 1, -1),
        )
        xa = _layer(xa, mods[l], B, T, Tc, rt, tm, l < depth - 1, p, cos_t, sin_t, l)
    return xa.reshape(B, T, D_MODEL)
```

```python
import functools

import numpy as np
import jax
import jax.numpy as jnp
from jax import lax
from jax.experimental import pallas as pl
from jax.experimental.pallas import tpu as pltpu

F32 = jnp.float32
MXU_DTYPE = jnp.bfloat16

D_MODEL = 2048
GRID_W = 64
MLA_HEADS = 8
QK_NOPE = 128
QK_ROPE = 64
V_HEAD = 128
Q_LORA = 512
KV_LORA = 512
ROPE_THETA = 10000.0
ROPE_FREQS = QK_ROPE // 4
ATTN_SCALE = (QK_NOPE + QK_ROPE) ** -0.5
M_HEADS = 4
M_QK = 64
M_V = 128
CONV_WIDTH = 512
N_EXPERTS = 32
TOP_K = 4
D_FF = 1024
SWIGLU_ALPHA = 1.702
SWIGLU_LIMIT = 7.0
N_MOD = 6
EPS = 1e-6

LANE = 128
MLSTM_CHUNK = 256
COL_TILE = 512
INPROJ_TILE = 1024
LOG2_E = 1.4426950408889634
MOE_TILE = 512
SEG_ALIGN = 16
NEG_BIG = -1e30
VMEM_LIMIT = 56 * 1024 * 1024

ZT_CQ, ZT_CKV, ZT_MQK, ZT_MV, ZT_MO, ZT_CU, ZT_CB, ZT_CC, ZT_CG = 0, 1, 2, 3, 4, 5, 6, 7, 8
N_MAIN_TILES = 8 + 3 * D_MODEL // COL_TILE
assert (N_MAIN_TILES * COL_TILE) % INPROJ_TILE == 0


def _cparams(n_axes, vmem=VMEM_LIMIT):
    return pltpu.CompilerParams(dimension_semantics=("arbitrary",) * n_axes, vmem_limit_bytes=vmem)


def _rms(x, g):
    return x * lax.rsqrt(jnp.mean(x * x, axis=-1, keepdims=True) + EPS) * g


def _lane_pick(x, idx):
    lane = lax.broadcasted_iota(jnp.int32, x.shape, 1)
    return jnp.sum(jnp.where(lane == idx, x, 0.0), axis=1, keepdims=True)


def _mod_kernel(c_ref, w_ref, b_ref, o_ref):
    c = c_ref[...]
    s = c * jax.nn.sigmoid(c)
    o_ref[0] = jnp.dot(s.astype(MXU_DTYPE), w_ref[0].astype(MXU_DTYPE),
                       preferred_element_type=F32) + b_ref[0]


def _adaln(cond, w_mod, b_mod):
    depth, _, n = w_mod.shape
    tn = 1024
    return pl.pallas_call(
        _mod_kernel,
        grid=(depth, n // tn),
        in_specs=[pl.BlockSpec(cond.shape, lambda l, j: (0, 0)),
                  pl.BlockSpec((1, D_MODEL, tn), lambda l, j: (l, 0, j)),
                  pl.BlockSpec((1, 1, tn), lambda l, j: (l, 0, j))],
        out_specs=pl.BlockSpec((1, cond.shape[0], tn), lambda l, j: (l, 0, j)),
        out_shape=jax.ShapeDtypeStruct((depth, cond.shape[0], n), F32),
        compiler_params=_cparams(2),
        name="adaln",
    )(cond, w_mod, b_mod.reshape(depth, 1, n))


def _inproj_kernel(x_ref, g_ref, sh_ref, sc_ref, w_ref, ws_ref, z_ref, zs_ref, h_scr):
    @pl.when(pl.program_id(1) == 0)
    def _():
        h = _rms(x_ref[...], g_ref[0]) * (1.0 + sc_ref[0]) + sh_ref[0]
        h_scr[...] = h.astype(h_scr.dtype)
        zs_ref[...] = jnp.dot(h_scr[...], ws_ref[0], preferred_element_type=F32)

    z_ref[...] = jnp.dot(h_scr[...], w_ref[0], preferred_element_type=F32).astype(z_ref.dtype)


def _inproj(xa, g_norm, modt, w_main, w_special, layer, rt):
    rows = xa.shape[0]
    n_main = w_main.shape[2]
    tn = INPROJ_TILE
    return pl.pallas_call(
        _inproj_kernel,
        grid=(rows // rt, n_main // tn),
        in_specs=[pl.BlockSpec((rt, D_MODEL), lambda i, j: (i, 0)),
                  pl.BlockSpec((1, 1, D_MODEL), lambda i, j: (0, 0, 0)),
                  pl.BlockSpec((1, 1, D_MODEL), lambda i, j: (i, 0, 0)),
                  pl.BlockSpec((1, 1, D_MODEL), lambda i, j: (i, 0, 1)),
                  pl.BlockSpec((1, D_MODEL, tn), lambda i, j: (layer, 0, j)),
                  pl.BlockSpec((1, D_MODEL, COL_TILE), lambda i, j: (layer, 0, 0))],
        out_specs=[pl.BlockSpec((rt, tn), lambda i, j: (i, j)),
                   pl.BlockSpec((rt, COL_TILE), lambda i, j: (i, 0))],
        out_shape=[jax.ShapeDtypeStruct((rows, n_main), MXU_DTYPE),
                   jax.ShapeDtypeStruct((rows, COL_TILE), F32)],
        scratch_shapes=[pltpu.VMEM((rt, D_MODEL), MXU_DTYPE)],
        compiler_params=_cparams(2),
        name="inproj",
    )(xa, g_norm, modt, modt, w_main, w_special)


def _mla_kernel(cq_ref, ckv_ref, zs_ref, gq_ref, gkv_ref, wq_ref, wkv_ref, cos_ref, sin_ref,
                q_ref, k_ref, v_ref):
    def norm(z_ref, g_ref):
        return _rms(z_ref[...].astype(F32), g_ref[...]).astype(MXU_DTYPE)

    qf = jnp.dot(norm(cq_ref, gq_ref), wq_ref[...], preferred_element_type=F32) * (ATTN_SCALE * LOG2_E)
    kvf = jnp.dot(norm(ckv_ref, gkv_ref), wkv_ref[...], preferred_element_type=F32)
    cos = cos_ref[...]
    sin = sin_ref[...]
    nope_w = MLA_HEADS * QK_NOPE
    rope_w = MLA_HEADS * QK_ROPE
    for p in range(MLA_HEADS // 2):
        a = nope_w + LANE * p
        b = nope_w + rope_w + LANE * p
        rp = (qf[:, a:a + LANE] * cos + qf[:, b:b + LANE] * sin).astype(MXU_DTYPE)
        for h in (2 * p, 2 * p + 1):
            q_ref[:, 256 * h:256 * h + LANE] = qf[:, LANE * h:LANE * (h + 1)].astype(MXU_DTYPE)
            q_ref[:, 256 * h + LANE:256 * (h + 1)] = rp
    kr2 = zs_ref[:, 0:LANE] * cos + zs_ref[:, LANE:2 * LANE] * sin
    lane = lax.broadcasted_iota(jnp.int32, kr2.shape, 1)
    k_lo = jnp.where(lane < QK_ROPE, kr2, 0.0).astype(MXU_DTYPE)
    k_hi = jnp.where(lane >= QK_ROPE, kr2, 0.0).astype(MXU_DTYPE)
    for h in range(MLA_HEADS):
        k_ref[:, 256 * h:256 * h + LANE] = kvf[:, LANE * h:LANE * (h + 1)].astype(MXU_DTYPE)
        k_ref[:, 256 * h + LANE:256 * (h + 1)] = k_lo if h % 2 == 0 else k_hi
    ones = jnp.ones((kvf.shape[0], LANE), MXU_DTYPE)
    v0 = MLA_HEADS * QK_NOPE
    for h in range(MLA_HEADS):
        v_ref[:, 256 * h:256 * h + LANE] = kvf[:, v0 + LANE * h:v0 + LANE * (h + 1)].astype(MXU_DTYPE)
        v_ref[:, 256 * h + LANE:256 * (h + 1)] = ones


def _mla_project(z, zs, g_q, g_kv, wq_p, wkv_p, cos_t, sin_t, rt, n_lat_tiles, tiles_per_seq):
    rows = z.shape[0]
    nt = rows // rt

    def tab(i):
        return (jnp.where(i < n_lat_tiles, i % tiles_per_seq, tiles_per_seq), 0)

    return pl.pallas_call(
        _mla_kernel,
        grid=(nt,),
        in_specs=[pl.BlockSpec((rt, COL_TILE), lambda i: (i, ZT_CQ)),
                  pl.BlockSpec((rt, COL_TILE), lambda i: (i, ZT_CKV)),
                  pl.BlockSpec((rt, COL_TILE), lambda i: (i, 0)),
                  pl.BlockSpec((1, Q_LORA), lambda i: (0, 0)),
                  pl.BlockSpec((1, KV_LORA), lambda i: (0, 0)),
                  pl.BlockSpec(wq_p.shape, lambda i: (0, 0)),
                  pl.BlockSpec(wkv_p.shape, lambda i: (0, 0)),
                  pl.BlockSpec((rt, LANE), tab),
                  pl.BlockSpec((rt, LANE), tab)],
        out_specs=[pl.BlockSpec((rt, 256 * MLA_HEADS), lambda i: (i, 0)),
                   pl.BlockSpec((rt, 256 * MLA_HEADS), lambda i: (i, 0)),
                   pl.BlockSpec((rt, 256 * MLA_HEADS), lambda i: (i, 0))],
        out_shape=[jax.ShapeDtypeStruct((rows, 256 * MLA_HEADS), MXU_DTYPE)] * 3,
        compiler_params=_cparams(1),
        name="mla_project",
    )(z, z, zs, g_q.reshape(1, -1), g_kv.reshape(1, -1), wq_p, wkv_p, cos_t, sin_t)


def _attn_kernel(*refs, kv_chunks, n_cast=0):
    q_ref = refs[0]
    o_ref = refs[len(refs) - 1 - n_cast]
    for c in range(n_cast):
        src = refs[1 + 2 * len(kv_chunks) + c]
        dst = refs[len(refs) - n_cast + c]
        dst[...] = src[...].astype(dst.dtype)
    q = q_ref[...]
    tq = q.shape[0]
    m = jnp.full((tq, 1), -jnp.inf, F32)
    acc = jnp.zeros((tq, 2 * V_HEAD), F32)
    for s, n_chunks in enumerate(kv_chunks):
        k_ref, v_ref = refs[1 + 2 * s], refs[2 + 2 * s]
        ck = k_ref.shape[0] // n_chunks
        for c in range(n_chunks):
            k = k_ref[c * ck:(c + 1) * ck, :]
            v = v_ref[c * ck:(c + 1) * ck, :]
            sc = lax.dot_general(q, k, (((1,), (1,)), ((), ())), preferred_element_type=F32)
            m_new = jnp.maximum(m, jnp.max(sc, axis=-1, keepdims=True))
            p = jnp.exp2(sc - m_new).astype(MXU_DTYPE)
            acc = jnp.exp2(m - m_new) * acc + jnp.dot(p, v, preferred_element_type=F32)
            m = m_new
    o_ref[...] = (acc[:, :V_HEAD] / acc[:, V_HEAD:]).astype(o_ref.dtype)


def _attention_latent(q, k, v, w_up, w_down, layer, depth, B, T, Tc, tq, ck, rows):
    qt = T // tq
    ctx_blk0 = B * T // Tc
    steps = B * MLA_HEADS * qt
    step = lambda b, h, i: (b * MLA_HEADS + h) * qt + i
    cast_specs_in, cast_specs_out, cast_shapes = [], [], []
    for w in (w_up, w_down):
        n = w.shape[1]
        layer_rows = w.shape[0] // depth
        slab = layer_rows // steps
        assert slab * steps == layer_rows and slab % 16 == 0
        cast_specs_in.append(pl.BlockSpec((slab, n), lambda b, h, i: (layer * steps + step(b, h, i), 0)))
        cast_specs_out.append(pl.BlockSpec((slab, n), lambda b, h, i: (step(b, h, i), 0)))
        cast_shapes.append(jax.ShapeDtypeStruct((layer_rows, n), MXU_DTYPE))
    return pl.pallas_call(
        functools.partial(_attn_kernel, kv_chunks=(T // ck, 1), n_cast=2),
        grid=(B, MLA_HEADS, qt),
        in_specs=[pl.BlockSpec((tq, 256), lambda b, h, i: (b * qt + i, h)),
                  pl.BlockSpec((T, 256), lambda b, h, i: (b, h)),
                  pl.BlockSpec((T, 256), lambda b, h, i: (b, h)),
                  pl.BlockSpec((Tc, 256), lambda b, h, i: (ctx_blk0 + b, h)),
                  pl.BlockSpec((Tc, 256), lambda b, h, i: (ctx_blk0 + b, h))] + cast_specs_in,
        out_specs=[pl.BlockSpec((tq, V_HEAD), lambda b, h, i: (b * qt + i, h))] + cast_specs_out,
        out_shape=[jax.ShapeDtypeStruct((rows, V_HEAD * MLA_HEADS), MXU_DTYPE)] + cast_shapes,
        compiler_params=_cparams(3),
        name="attn_latent",
    )(q, k, v, k, v, w_up, w_down)


def _attention_ctx(q, k, v, a_buf, B, T, Tc):
    ctx_blk0 = B * T // Tc
    return pl.pallas_call(
        functools.partial(_attn_kernel, kv_chunks=(1,)),
        grid=(B, MLA_HEADS),
        in_specs=[pl.BlockSpec((Tc, 256), lambda b, h: (ctx_blk0 + b, h)),
                  pl.BlockSpec((Tc, 256), lambda b, h: (ctx_blk0 + b, h)),
                  pl.BlockSpec((Tc, 256), lambda b, h: (ctx_blk0 + b, h)),
                  pl.BlockSpec(memory_space=pl.ANY)],
        out_specs=pl.BlockSpec((Tc, V_HEAD), lambda b, h: (ctx_blk0 + b, h)),
        out_shape=jax.ShapeDtypeStruct(a_buf.shape, a_buf.dtype),
        input_output_aliases={3: 0},
        compiler_params=_cparams(2),
        name="attn_ctx",
    )(q, k, v, a_buf)


def _mlstm_chunk(qk, v, g, lf_cum, gt_ref, head, reverse, ct_ref, m_prev, consts):
    tri, row_iota, mk, ones_tile, lane = consts
    L = qk.shape[0]
    base = 8 if reverse else 0
    col_i, col_f = base + head, base + 4 + head
    ql = jnp.where(lane < M_QK, qk, jnp.zeros_like(qk))
    kl = jnp.dot(qk, mk, preferred_element_type=F32).astype(MXU_DTYPE)
    v_aug = jnp.concatenate([v, ones_tile], axis=1)
    bc = _lane_pick(lf_cum, col_f)
    li = _lane_pick(g, col_i)
    r_row = gt_ref[0, col_i:col_i + 1, :] - gt_ref[1, col_f:col_f + 1, :]
    dlog = jnp.where(tri, bc + r_row, -jnp.inf)
    inter = bc + m_prev
    m_t = jnp.maximum(inter, jnp.max(dlog, axis=1, keepdims=True))
    dexp = jnp.exp(dlog - m_t)
    s_raw = lax.dot_general(ql, kl, (((1,), (1,)), ((), ())), preferred_element_type=F32)
    s_mat = (s_raw * (M_QK ** -0.5) * dexp).astype(MXU_DTYPE)
    a = jnp.exp(inter - m_t) * (M_QK ** -0.5)
    ct = ct_ref[...]
    nd = (jnp.dot(s_mat, v_aug, preferred_element_type=F32)
          + a * jnp.dot(ql, ct.astype(MXU_DTYPE), preferred_element_type=F32))
    num, den = nd[:, :M_V], nd[:, M_V:]
    h_out = num / jnp.maximum(jnp.abs(den), jnp.exp(-m_t))
    last = 0 if reverse else L - 1
    total = jnp.sum(jnp.where(row_iota == last, bc, 0.0), axis=0, keepdims=True)
    w_col = total - bc + li
    m_new = jnp.maximum(total + m_prev, jnp.max(w_col, axis=0, keepdims=True))
    decay = jnp.exp(total + m_prev - m_new)
    e = jnp.exp(w_col - m_new)
    ek = (kl.astype(F32) * e).astype(MXU_DTYPE)
    upd = lax.dot_general(ek, v_aug, (((0,), (0,)), ((), ())), preferred_element_type=F32)
    ct_ref[...] = decay * ct + upd
    return h_out, m_new


def _mlstm_kernel(zl_qk, zl_v, zl_mo, gl_ref, zc_qk, zc_v, zc_mo, gc_ref, bif_ref, gm_ref,
                  ol_ref, oc_ref, acc_l, acc_c, ct_scr, gt_scr):
    L = MLSTM_CHUNK
    T = zl_qk.shape[0]
    nc = T // L
    r = lax.broadcasted_iota(jnp.int32, (L, L), 0)
    c = lax.broadcasted_iota(jnp.int32, (L, L), 1)
    tri_f = c <= r
    tri_b = c >= r
    tmat_f = tri_f.astype(F32)
    tmat_b = tri_b.astype(F32)
    row_iota = lax.broadcasted_iota(jnp.int32, (L, 1), 0)
    lane = lax.broadcasted_iota(jnp.int32, (L, LANE), 1)
    rr = lax.broadcasted_iota(jnp.int32, (LANE, LANE), 0)
    cc = lax.broadcasted_iota(jnp.int32, (LANE, LANE), 1)
    mk = (rr == cc + M_QK).astype(MXU_DTYPE)
    ones_tile = jnp.ones((L, LANE), MXU_DTYPE)
    consts_f = (tri_f, row_iota, mk, ones_tile, lane)
    consts_b = (tri_b, row_iota, mk, ones_tile, lane)
    bias = bif_ref[...]

    def gate_prep(g_raw, tmat, slot):
        g = g_raw + bias
        log_sig = jnp.minimum(g, 0.0) - jnp.log1p(jnp.exp(-jnp.abs(g)))
        cum = jnp.dot(tmat, log_sig, precision=lax.Precision.HIGHEST, preferred_element_type=F32)
        gt_scr[slot, 0] = g.T
        gt_scr[slot, 1] = cum.T
        return g, cum

    def step(qk_ref, v_ref, g_ref, acc_ref, row_f, row_b, ms):
        g_f, cum_f = gate_prep(g_ref[pl.ds(row_f, L), :], tmat_f, 0)
        g_b, cum_b = gate_prep(g_ref[pl.ds(row_b, L), :], tmat_b, 1)
        new_ms = []
        for h in range(M_HEADS):
            cols = slice(h * LANE, (h + 1) * LANE)
            hf, mf = _mlstm_chunk(qk_ref[pl.ds(row_f, L), cols], v_ref[pl.ds(row_f, L), cols],
                                  g_f, cum_f, gt_scr.at[0], h, False,
                                  ct_scr.at[h], ms[h], consts_f)
            acc_ref[pl.ds(row_f, L), cols] += hf
            hb, mb = _mlstm_chunk(qk_ref[pl.ds(row_b, L), cols], v_ref[pl.ds(row_b, L), cols],
                                  g_b, cum_b, gt_scr.at[1], h, True,
                                  ct_scr.at[M_HEADS + h], ms[M_HEADS + h], consts_b)
            acc_ref[pl.ds(row_b, L), cols] += hb
            new_ms.append((mf, mb))
        return tuple(x[0] for x in new_ms) + tuple(x[1] for x in new_ms)

    ct_scr[...] = jnp.zeros_like(ct_scr)
    acc_l[...] = jnp.zeros_like(acc_l)
    acc_c[...] = jnp.zeros_like(acc_c)
    zero = jnp.zeros((1, 1), F32)
    ms = step(zc_qk, zc_v, gc_ref, acc_c, 0, 0, (zero,) * (2 * M_HEADS))

    def body(i, ms):
        row_f = pl.multiple_of(i * L, L)
        row_b = pl.multiple_of((nc - 1 - i) * L, L)
        return step(zl_qk, zl_v, gl_ref, acc_l, row_f, row_b, ms)

    lax.fori_loop(0, nc, body, ms)

    gm = gm_ref[...]

    def finish(acc_ref, mo_ref, o_ref, row):
        for h in range(M_HEADS):
            cols = slice(h * LANE, (h + 1) * LANE)
            hn = _rms(acc_ref[pl.ds(row, L), cols], gm[:, cols])
            gate = jax.nn.sigmoid(mo_ref[pl.ds(row, L), cols].astype(F32))
            o_ref[pl.ds(row, L), cols] = (hn * gate).astype(o_ref.dtype)

    finish(acc_c, zc_mo, oc_ref, 0)

    def fin_body(i, carry):
        finish(acc_l, zl_mo, ol_ref, pl.multiple_of(i * L, L))
        return carry

    lax.fori_loop(0, nc, fin_body, 0)


def _mlstm(z, zs, bif_row, g_mlstm, B, T, Tc):
    assert Tc == MLSTM_CHUNK and T % MLSTM_CHUNK == 0
    rows = z.shape[0]
    ctx_blk0 = B * T // Tc
    w = M_HEADS * LANE
    out_l, out_c = pl.pallas_call(
        _mlstm_kernel,
        grid=(B,),
        in_specs=[pl.BlockSpec((T, w), lambda b: (b, ZT_MQK)),
                  pl.BlockSpec((T, w), lambda b: (b, ZT_MV)),
                  pl.BlockSpec((T, w), lambda b: (b, ZT_MO)),
                  pl.BlockSpec((T, LANE), lambda b: (b, 2)),
                  pl.BlockSpec((Tc, w), lambda b: (ctx_blk0 + b, ZT_MQK)),
                  pl.BlockSpec((Tc, w), lambda b: (ctx_blk0 + b, ZT_MV)),
                  pl.BlockSpec((Tc, w), lambda b: (ctx_blk0 + b, ZT_MO)),
                  pl.BlockSpec((Tc, LANE), lambda b: (ctx_blk0 + b, 2)),
                  pl.BlockSpec((1, LANE), lambda b: (0, 0)),
                  pl.BlockSpec((1, w), lambda b: (0, 0))],
        out_specs=[pl.BlockSpec((T, w), lambda b: (b, 0)),
                   pl.BlockSpec((Tc, w), lambda b: (b, 0))],
        out_shape=[jax.ShapeDtypeStruct((B * T, w), MXU_DTYPE),
                   jax.ShapeDtypeStruct((B * Tc, w), MXU_DTYPE)],
        scratch_shapes=[pltpu.VMEM((T, w), F32), pltpu.VMEM((Tc, w), F32),
                        pltpu.VMEM((2 * M_HEADS, LANE, 2 * LANE), F32),
                        pltpu.VMEM((2, 2, LANE, MLSTM_CHUNK), F32)],
        compiler_params=_cparams(1),
        name="mlstm",
    )(z, z, z, zs, z, z, z, zs, bif_row, g_mlstm.reshape(1, -1))
    return jnp.concatenate([out_l, out_c], axis=0)


def _merge_kernel(a_ref, m_ref, cu_ref, cb_ref, cc_ref, cu_p, cc_p, cu_n, cc_n, ga_ref, gm_ref, gs_ref,
                  wa_ref, wb_ref, wc_ref, wconv_ref, bconv_ref, t_ref, s_scr,
                  *, n_lat_tiles, tiles_per_seq, seq_len, ctx_len):
    i = pl.program_id(0)
    j = pl.program_id(1)
    rt = a_ref.shape[0]

    @pl.when(j == 0)
    def _():
        u = cc_ref[...].astype(F32) * cu_ref[...].astype(F32)
        hp = cc_p[...].astype(F32) * cu_p[...].astype(F32)
        hn = cc_n[...].astype(F32) * cu_n[...].astype(F32)
        row = lax.broadcasted_iota(jnp.int32, (rt, 1), 0)
        is_ctx = i >= n_lat_tiles
        seg = jnp.where(is_ctx, ctx_len, seq_len)
        off = jnp.where(is_ctx, (i - n_lat_tiles) * rt, (i % tiles_per_seq) * rt)
        pos = (off + row) & (seg - 1)
        up = pltpu.roll(u, 1, 0)
        up = jnp.where(row == 0, hp[hp.shape[0] - 1:, :], up)
        up = jnp.where(pos == 0, 0.0, up)
        un = pltpu.roll(u, rt - 1, 0)
        un = jnp.where(row == rt - 1, hn[0:1, :], un)
        un = jnp.where(pos == seg - 1, 0.0, un)
        wconv = wconv_ref[...]
        y = up * wconv[0:1, :] + u * wconv[1:2, :] + un * wconv[2:3, :] + bconv_ref[...]
        s_scr[...] = (cb_ref[...].astype(F32) * y).astype(s_scr.dtype)

    def sig(ref):
        return jax.nn.sigmoid(ref[...].astype(F32))

    t = (sig(ga_ref) * jnp.dot(a_ref[...], wa_ref[...], preferred_element_type=F32)
         + sig(gm_ref) * jnp.dot(m_ref[...], wb_ref[...], preferred_element_type=F32)
         + sig(gs_ref) * jnp.dot(s_scr[...], wc_ref[...], preferred_element_type=F32))
    t_ref[...] = t.astype(t_ref.dtype)


def _merge(a, m, z, wa, wb, wc, w_conv, b_conv, rt, n_tiles, n_lat_tiles, tiles_per_seq, T, Tc):
    rows = n_tiles * rt
    hb = 16
    rb = rt // hb
    last_hb = z.shape[0] // hb - 1
    tn = COL_TILE
    nj = D_MODEL // tn
    gt = D_MODEL // tn

    def prev_blk(col):
        return lambda i, j: (jnp.maximum(i * rb - 1, 0), col)

    def next_blk(col):
        return lambda i, j: (jnp.minimum((i + 1) * rb, last_hb), col)

    kern = functools.partial(_merge_kernel, n_lat_tiles=n_lat_tiles, tiles_per_seq=tiles_per_seq,
                             seq_len=T, ctx_len=Tc)
    return pl.pallas_call(
        kern,
        grid=(n_tiles, nj),
        in_specs=[pl.BlockSpec((rt, a.shape[1]), lambda i, j: (i, 0)),
                  pl.BlockSpec((rt, m.shape[1]), lambda i, j: (i, 0)),
                  pl.BlockSpec((rt, COL_TILE), lambda i, j: (i, ZT_CU)),
                  pl.BlockSpec((rt, COL_TILE), lambda i, j: (i, ZT_CB)),
                  pl.BlockSpec((rt, COL_TILE), lambda i, j: (i, ZT_CC)),
                  pl.BlockSpec((hb, COL_TILE), prev_blk(ZT_CU)),
                  pl.BlockSpec((hb, COL_TILE), prev_blk(ZT_CC)),
                  pl.BlockSpec((hb, COL_TILE), next_blk(ZT_CU)),
                  pl.BlockSpec((hb, COL_TILE), next_blk(ZT_CC)),
                  pl.BlockSpec((rt, tn), lambda i, j: (i, ZT_CG + j)),
                  pl.BlockSpec((rt, tn), lambda i, j: (i, ZT_CG + gt + j)),
                  pl.BlockSpec((rt, tn), lambda i, j: (i, ZT_CG + 2 * gt + j)),
                  pl.BlockSpec((wa.shape[0], tn), lambda i, j: (0, j)),
                  pl.BlockSpec((wb.shape[0], tn), lambda i, j: (0, j)),
                  pl.BlockSpec((wc.shape[0], tn), lambda i, j: (0, j)),
                  pl.BlockSpec(w_conv.shape, lambda i, j: (0, 0)),
                  pl.BlockSpec((1, CONV_WIDTH), lambda i, j: (0, 0))],
        out_specs=pl.BlockSpec((rt, tn), lambda i, j: (i, j)),
        out_shape=jax.ShapeDtypeStruct((rows, D_MODEL), MXU_DTYPE),
        scratch_shapes=[pltpu.VMEM((rt, CONV_WIDTH), MXU_DTYPE)],
        compiler_params=_cparams(2),
        name="merge",
    )(a, m, z, z, z, z, z, z, z, z, z, z, wa, wb, wc, w_conv, b_conv.reshape(1, -1))


def _dot3(x, w):
    xh = x.astype(jnp.bfloat16)
    xl = (x - xh.astype(F32)).astype(jnp.bfloat16)
    wh = w.astype(jnp.bfloat16)
    wl = (w - wh.astype(F32)).astype(jnp.bfloat16)
    return (jnp.dot(xh, wh, preferred_element_type=F32) + jnp.dot(xh, wl, preferred_element_type=F32)
            + jnp.dot(xl, wh, preferred_element_type=F32))


def _seg_pad(counts):
    return jnp.ceil(counts * (1.0 / SEG_ALIGN)) * SEG_ALIGN


def _exclusive_lane_cumsum(row):
    upper = (lax.broadcasted_iota(jnp.int32, (LANE, LANE), 0)
             < lax.broadcasted_iota(jnp.int32, (LANE, LANE), 1)).astype(jnp.bfloat16)
    out = jnp.dot(jnp.broadcast_to(row, (8, LANE)).astype(jnp.bfloat16), upper, preferred_element_type=F32)
    return out[0:1, :]


def _post_kernel(t_ref, wo_ref, x_ref, g1_ref, g2_ref, gt_ref, sh_ref, sc_ref, wr_ref, br_ref,
                 xo_ref, h2_ref, ri_ref, rw_ref, cnt_ref):
    y = jnp.dot(t_ref[...], wo_ref[...], preferred_element_type=F32)
    xn = x_ref[...] + gt_ref[0] * _rms(y, g1_ref[0])
    xo_ref[...] = xn
    h2 = _rms(xn, g2_ref[0]) * (1.0 + sc_ref[0]) + sh_ref[0]
    h2_ref[...] = h2.astype(h2_ref.dtype)
    logits = _dot3(h2, wr_ref[...]) + br_ref[...]
    lane = lax.broadcasted_iota(jnp.int32, logits.shape, 1)
    lane_f = lane.astype(F32)
    vals, ids = [], []
    for _ in range(TOP_K):
        mx = jnp.max(logits, axis=1, keepdims=True)
        idx = jnp.min(jnp.where(logits == mx, lane_f, float(LANE)), axis=1, keepdims=True)
        logits = jnp.where(lane_f == idx, -jnp.inf, logits)
        vals.append(mx)
        ids.append(idx)
    es = [jnp.exp(v - vals[0]) for v in vals]
    tot = es[0] + es[1] + es[2] + es[3]
    tm = logits.shape[0]
    onehots = [lane_f == ids[k] for k in range(TOP_K)]
    asg = jnp.zeros(logits.shape, F32)
    for k in range(TOP_K):
        asg = jnp.where(onehots[k], 1.0, asg)
    earlier = (lax.broadcasted_iota(jnp.int32, (tm, tm), 1)
               < lax.broadcasted_iota(jnp.int32, (tm, tm), 0)).astype(jnp.bfloat16)
    before = jnp.dot(earlier, asg.astype(jnp.bfloat16), preferred_element_type=F32)
    counts = jnp.sum(asg, axis=0, keepdims=True)
    seg_off = _exclusive_lane_cumsum(_seg_pad(counts))
    rw = jnp.zeros(logits.shape, F32)
    ri = jnp.zeros(logits.shape, F32)
    for k in range(TOP_K):
        pos = jnp.sum(jnp.where(onehots[k], before + seg_off, 0.0), axis=1, keepdims=True)
        rw = jnp.where(lane == k, es[k] / tot, rw)
        ri = jnp.where(lane == k, ids[k], ri)
        ri = jnp.where(lane == TOP_K + k, pos, ri)
    rw_ref[...] = rw
    ri_ref[...] = ri.astype(jnp.int32)
    cnt_ref[0] = jnp.broadcast_to(counts, cnt_ref.shape[1:])


def _post(t, wo, xa, g_norm, modt, wr_p, br_p, tm, n_tiles):
    rows = n_tiles * tm
    row_spec = lambda w: pl.BlockSpec((tm, w), lambda i: (i, 0))
    mod = lambda k: pl.BlockSpec((1, 1, D_MODEL), lambda i: (i, 0, k))
    gn = lambda k: pl.BlockSpec((1, 1, D_MODEL), lambda i: (k, 0, 0))
    return pl.pallas_call(
        _post_kernel,
        grid=(n_tiles,),
        in_specs=[row_spec(D_MODEL), pl.BlockSpec(wo.shape, lambda i: (0, 0)), row_spec(D_MODEL),
                  gn(1), gn(2), mod(2), mod(3), mod(4),
                  pl.BlockSpec(wr_p.shape, lambda i: (0, 0)), pl.BlockSpec((1, LANE), lambda i: (0, 0))],
        out_specs=[row_spec(D_MODEL), row_spec(D_MODEL), row_spec(LANE), row_spec(LANE),
                   pl.BlockSpec((1, 8, LANE), lambda i: (i, 0, 0))],
        out_shape=[jax.ShapeDtypeStruct((rows, D_MODEL), F32),
                   jax.ShapeDtypeStruct((rows, D_MODEL), MXU_DTYPE),
                   jax.ShapeDtypeStruct((rows, LANE), jnp.int32),
                   jax.ShapeDtypeStruct((rows, LANE), F32),
                   jax.ShapeDtypeStruct((rows // tm, 8, LANE), F32)],
        compiler_params=_cparams(1),
        name="post_mix",
    )(t, wo, xa, g_norm, g_norm, modt, modt, modt, wr_p, br_p.reshape(1, LANE))


def _sort_kernel(h2_ref, ri_ref, xs_ref):
    tm = h2_ref.shape[0]
    rif = ri_ref[...].astype(F32)
    pos = [_lane_pick(rif, TOP_K + k) for k in range(TOP_K)]
    h2 = h2_ref[...]
    for c in range(xs_ref.shape[0] // tm):
        col = (lax.broadcasted_iota(jnp.int32, (tm, tm), 1) + c * tm).astype(F32)
        onehot = jnp.zeros((tm, tm), F32)
        for k in range(TOP_K):
            onehot = jnp.where(pos[k] == col, 1.0, onehot)
        rows = lax.dot_general(onehot.astype(MXU_DTYPE), h2, (((0,), (0,)), ((), ())),
                               preferred_element_type=F32)
        xs_ref[c * tm:(c + 1) * tm, :] = rows.astype(xs_ref.dtype)


def _sort_rows(h2, ri, tm, n_tiles):
    seg_rows = TOP_K * tm + N_EXPERTS * SEG_ALIGN
    assert seg_rows % tm == 0
    return pl.pallas_call(
        _sort_kernel,
        grid=(n_tiles,),
        in_specs=[pl.BlockSpec((tm, D_MODEL), lambda i: (i, 0)),
                  pl.BlockSpec((tm, LANE), lambda i: (i, 0))],
        out_specs=pl.BlockSpec((seg_rows, D_MODEL), lambda i: (i, 0)),
        out_shape=jax.ShapeDtypeStruct((n_tiles * seg_rows, D_MODEL), MXU_DTYPE),
        compiler_params=_cparams(1),
        name="sort_rows",
    )(h2, ri)


def _expert_kernel(te_ref, r0_ref, na_ref, seg_ref, src_ref, glen_ref,
                   xs_hbm, wu_ref, bu_ref, wd_ref, bd_ref, ys_hbm,
                   xbuf, ybuf, sem_in, sem_out, *, n_tok_tiles):
    w = pl.program_id(0)
    n_active = na_ref[0]
    tm = ybuf.shape[0]
    nt = n_tok_tiles
    sizes = [s for s in (512, 256, 128, 64, 32, 16) if s <= tm]

    def for_pieces(length, fn):
        off = jnp.int32(0)
        for size in sizes:
            hit = (length & size) != 0

            @pl.when(hit)
            def _(off=off, size=size):
                fn(off, size)

            off = off + jnp.where(hit, size, 0)

    def for_segments(wt, fn):
        e = te_ref[wt]
        r0 = r0_ref[wt]

        def body(t, carry):
            s0 = seg_ref[e * (nt + 1) + t]
            s1 = seg_ref[e * (nt + 1) + t + 1]
            lo = jnp.maximum(s0, r0)
            length = jnp.maximum(jnp.minimum(s1, r0 + tm) - lo, 0)
            src = src_ref[e * nt + t] + (lo - s0)
            dst = lo - r0
            for_pieces(length, lambda off, size: fn(src + off, dst + off, size))
            return carry

        lax.fori_loop(0, nt, body, 0)

    def rows_in_use(wt):
        return jnp.minimum(glen_ref[te_ref[wt]] - r0_ref[wt], tm)

    def aligned(row):
        return row if isinstance(row, int) else pl.multiple_of(row, SEG_ALIGN)

    def in_copy(hbm_row, buf_row, size, slot):
        return pltpu.make_async_copy(xs_hbm.at[pl.ds(aligned(hbm_row), size)],
                                     xbuf.at[slot, pl.ds(aligned(buf_row), size)], sem_in.at[slot])

    def out_copy(hbm_row, buf_row, size):
        return pltpu.make_async_copy(ybuf.at[pl.ds(aligned(buf_row), size)],
                                     ys_hbm.at[pl.ds(aligned(hbm_row), size)], sem_out.at[0])

    def start_in(wt, slot):
        for_segments(wt, lambda h, b, size: in_copy(h, b, size, slot).start())

    def wait_in(wt, slot):
        for_pieces(rows_in_use(wt), lambda off, size: in_copy(0, 0, size, slot).wait())

    def wait_out(wt):
        for_pieces(rows_in_use(wt), lambda off, size: out_copy(0, 0, size).wait())

    @pl.when(w < n_active)
    def _():
        slot = w % 2

        @pl.when(w == 0)
        def _():
            xbuf[...] = jnp.zeros_like(xbuf)
            start_in(0, 0)

        wait_in(w, slot)

        @pl.when(w + 1 < n_active)
        def _():
            start_in(w + 1, 1 - slot)

        gu = jnp.dot(xbuf[slot], wu_ref[0], preferred_element_type=F32) + bu_ref[0]
        g = jnp.minimum(gu[:, :D_FF], SWIGLU_LIMIT)
        u = jnp.clip(gu[:, D_FF:], -SWIGLU_LIMIT, SWIGLU_LIMIT)
        hid = g * jax.nn.sigmoid(SWIGLU_ALPHA * g) * (u + 1.0)
        y = jnp.dot(hid.astype(MXU_DTYPE), wd_ref[0], preferred_element_type=F32) + bd_ref[0]

        @pl.when(w > 0)
        def _():
            wait_out(w - 1)

        ybuf[...] = y.astype(ybuf.dtype)
        for_segments(w, lambda h, b, size: out_copy(h, b, size).start())

        @pl.when(w == n_active - 1)
        def _():
            wait_out(w)


def _experts(tables, xs, w_up, b_up, w_down, b_down, n_tok_tiles):
    tile_expert, tile_row0, n_active, seg_start, src_row, group_len = tables
    tm = MOE_TILE
    grid_spec = pltpu.PrefetchScalarGridSpec(
        num_scalar_prefetch=6,
        grid=(tile_expert.shape[0],),
        in_specs=[pl.BlockSpec(memory_space=pl.ANY),
                  pl.BlockSpec((1, D_MODEL, 2 * D_FF), lambda t, te, *_: (te[t], 0, 0)),
                  pl.BlockSpec((1, 1, 2 * D_FF), lambda t, te, *_: (te[t], 0, 0)),
                  pl.BlockSpec((1, D_FF, D_MODEL), lambda t, te, *_: (te[t], 0, 0)),
                  pl.BlockSpec((1, 1, D_MODEL), lambda t, te, *_: (te[t], 0, 0))],
        out_specs=pl.BlockSpec(memory_space=pl.ANY),
        scratch_shapes=[pltpu.VMEM((2, tm, D_MODEL), xs.dtype), pltpu.VMEM((tm, D_MODEL), xs.dtype),
                        pltpu.SemaphoreType.DMA((2,)), pltpu.SemaphoreType.DMA((1,))],
    )
    return pl.pallas_call(
        functools.partial(_expert_kernel, n_tok_tiles=n_tok_tiles),
        grid_spec=grid_spec,
        out_shape=jax.ShapeDtypeStruct(xs.shape, xs.dtype),
        compiler_params=_cparams(1),
        name="experts",
    )(tile_expert, tile_row0, n_active, seg_start, src_row, group_len, xs, w_up, b_up, w_down, b_down)


def _combine_kernel(ys_ref, ri_ref, rw_ref, cnt_ref, x_ref, g3_ref, gt_ref, xo_ref):
    tm = x_ref.shape[0]
    rif = ri_ref[...].astype(F32)
    rw = rw_ref[...]
    pos = [_lane_pick(rif, TOP_K + k) for k in range(TOP_K)]
    wgt = [_lane_pick(rw, k) for k in range(TOP_K)]
    used = jnp.sum(_seg_pad(cnt_ref[0][0:1, :]), axis=1, keepdims=True)
    f = jnp.zeros(x_ref.shape, F32)
    for c in range(ys_ref.shape[0] // tm):
        col = (lax.broadcasted_iota(jnp.int32, (tm, tm), 1) + c * tm).astype(F32)
        scat = jnp.zeros((tm, tm), F32)
        for k in range(TOP_K):
            scat = jnp.where(pos[k] == col, wgt[k], scat)
        row = (lax.broadcasted_iota(jnp.int32, (tm, 1), 0) + c * tm).astype(F32)
        ys = ys_ref[c * tm:(c + 1) * tm, :]
        ys = jnp.where(row < used, ys, jnp.zeros_like(ys))
        f = f + jnp.dot(scat.astype(MXU_DTYPE), ys, preferred_element_type=F32)
    xo_ref[...] = x_ref[...] + gt_ref[0] * _rms(f, g3_ref[0])


def _combine(ys, ri, rw, cnt, xa, g_norm, modt, tm, n_tiles):
    seg_rows = ys.shape[0] // n_tiles
    return pl.pallas_call(
        _combine_kernel,
        grid=(n_tiles,),
        in_specs=[pl.BlockSpec((seg_rows, D_MODEL), lambda i: (i, 0)),
                  pl.BlockSpec((tm, LANE), lambda i: (i, 0)),
                  pl.BlockSpec((tm, LANE), lambda i: (i, 0)),
                  pl.BlockSpec((1, 8, LANE), lambda i: (i, 0, 0)),
                  pl.BlockSpec((tm, D_MODEL), lambda i: (i, 0)),
                  pl.BlockSpec((1, 1, D_MODEL), lambda i: (3, 0, 0)),
                  pl.BlockSpec((1, 1, D_MODEL), lambda i: (i, 0, 5))],
        out_specs=pl.BlockSpec((tm, D_MODEL), lambda i: (i, 0)),
        out_shape=jax.ShapeDtypeStruct((n_tiles * tm, D_MODEL), F32),
        compiler_params=_cparams(1),
        name="combine",
    )(ys, ri, rw, cnt, xa, g_norm, modt)


def _swap16(idx):
    g = np.asarray(idx).reshape(-1, 2, ROPE_FREQS)
    return g[:, ::-1, :].reshape(-1)


def _in_proj_columns():
    sizes = (Q_LORA, KV_LORA, QK_ROPE, M_HEADS * M_QK, M_HEADS * M_QK, M_HEADS * M_V, M_HEADS * M_V,
             2 * 2 * M_HEADS, CONV_WIDTH, CONV_WIDTH, CONV_WIDTH, 3 * D_MODEL)
    starts = np.concatenate([[0], np.cumsum(sizes)[:-1]])
    seg = lambda k: np.arange(starts[k], starts[k] + sizes[k])
    cq, ckv, kr, mq, mk, mv, mo, mif, cu, cb, cc, cg = (seg(k) for k in range(12))
    mqk = np.concatenate([np.concatenate([mq[h * M_QK:(h + 1) * M_QK], mk[h * M_QK:(h + 1) * M_QK]])
                          for h in range(M_HEADS)])
    main = np.concatenate([cq, ckv, mqk, mv, mo, cu, cb, cc, cg])
    pad = lambda n: np.full((n,), -1)
    special = np.concatenate([kr, kr, _swap16(kr), _swap16(kr), mif, pad(LANE - mif.size), pad(LANE)])
    cols = np.concatenate([main, special])
    assert cols.size == (N_MAIN_TILES + 1) * COL_TILE
    return cols


def _take_cols(w, cols, dtype):
    cols = np.asarray(cols)
    breaks = np.flatnonzero((np.diff(cols) != 1) & ~((cols[1:] < 0) & (cols[:-1] < 0))) + 1
    parts = []
    for run in np.split(cols, breaks):
        if run[0] < 0:
            parts.append(jnp.zeros(w.shape[:-1] + (run.size,), dtype))
        else:
            parts.append(w[..., int(run[0]):int(run[-1]) + 1].astype(dtype))
    return jnp.concatenate(parts, axis=-1)


def _q_up_columns():
    per = QK_NOPE + QK_ROPE
    nope = np.concatenate([np.arange(h * per, h * per + QK_NOPE) for h in range(MLA_HEADS)])
    rope = np.concatenate([np.arange(h * per + QK_NOPE, (h + 1) * per) for h in range(MLA_HEADS)])
    return np.concatenate([nope, rope, _swap16(rope)])


def _kv_up_columns():
    per = QK_NOPE + V_HEAD
    kn = np.concatenate([np.arange(h * per, h * per + QK_NOPE) for h in range(MLA_HEADS)])
    vv = np.concatenate([np.arange(h * per + QK_NOPE, (h + 1) * per) for h in range(MLA_HEADS)])
    return np.concatenate([kn, vv])


def _rope_tables(T, rt):
    pos = jnp.arange(T)
    inv_freq = ROPE_THETA ** (-jnp.arange(ROPE_FREQS, dtype=F32) / ROPE_FREQS)
    ang_r = (pos // GRID_W).astype(F32)[:, None] * inv_freq
    ang_c = (pos % GRID_W).astype(F32)[:, None] * inv_freq
    cos64 = jnp.concatenate([jnp.cos(ang_r), jnp.cos(ang_r), jnp.cos(ang_c), jnp.cos(ang_c)], axis=1)
    sin64 = jnp.concatenate([-jnp.sin(ang_r), jnp.sin(ang_r), -jnp.sin(ang_c), jnp.sin(ang_c)], axis=1)
    cos_t = jnp.concatenate([jnp.tile(cos64, (1, 2)), jnp.ones((rt, LANE), F32)], axis=0)
    sin_t = jnp.concatenate([jnp.tile(sin64, (1, 2)), jnp.zeros((rt, LANE), F32)], axis=0)
    return cos_t, sin_t


def _moe_tables(cnt, n_tiles, tok_tile):
    tm = MOE_TILE
    seg_rows = TOP_K * tok_tile + N_EXPERTS * SEG_ALIGN
    n_work = n_tiles * seg_rows // tm + N_EXPERTS
    counts = cnt[:n_tiles, 0, :N_EXPERTS].astype(jnp.int32)
    padded = (counts + SEG_ALIGN - 1) // SEG_ALIGN * SEG_ALIGN
    in_tile = jnp.cumsum(padded, axis=1) - padded
    src_row = (jnp.arange(n_tiles, dtype=jnp.int32)[:, None] * seg_rows + in_tile).T
    seg_start = jnp.concatenate([jnp.zeros((N_EXPERTS, 1), jnp.int32), jnp.cumsum(padded.T, axis=1)], axis=1)
    group_len = seg_start[:, -1]
    work_end = jnp.cumsum((group_len + tm - 1) // tm)
    work = jnp.arange(n_work, dtype=jnp.int32)
    tile_expert = jnp.minimum(jnp.sum(work_end[None, :] <= work[:, None], axis=1), N_EXPERTS - 1)
    first_work = jnp.concatenate([jnp.zeros((1,), jnp.int32), work_end[:-1]])
    tile_row0 = (work - first_work[tile_expert]) * tm
    i32 = lambda a: a.astype(jnp.int32)
    return (i32(tile_expert), i32(tile_row0), i32(work_end[-1:]), i32(seg_start.reshape(-1)),
            i32(src_row.reshape(-1)), i32(group_len))


def _layer(xa, mods, B, T, Tc, rt, tm, update_ctx, p, cos_t, sin_t, layer):
    rows = xa.shape[0]
    n_lat = B * T
    lat_rt, all_rt = n_lat // rt, rows // rt
    lat_tm, all_tm = n_lat // tm, rows // tm
    cond_rt = np.concatenate([np.arange(lat_rt) // (T // rt), np.full((all_rt - lat_rt,), B)])
    cond_tm = np.concatenate([np.arange(lat_tm) // (T // tm), np.full((all_tm - lat_tm,), B)])
    modt_rt = mods[cond_rt][:, None, :]
    modt_tm = mods[cond_tm][:, None, :]
    g_norm = p["g_norm"].reshape(4, 1, D_MODEL)

    z, zs = _inproj(xa, g_norm, modt_rt, p["w_in_main"], p["w_in_special"], layer, rt)
    q, k, v = _mla_project(z, zs, p["g_q_lora"], p["g_kv_lora"], p["w_q_up"], p["w_kv_up"],
                           cos_t, sin_t, rt, lat_rt, T // rt)
    a, w_up_c, w_down_c = _attention_latent(q, k, v, p["w_up"], p["w_down"], layer, p["depth"],
                                            B, T, Tc, min(512, T), min(512, T),
                                            rows if update_ctx else n_lat)
    w_up_c = w_up_c.reshape(N_EXPERTS, D_MODEL, 2 * D_FF)
    w_down_c = w_down_c.reshape(N_EXPERTS, D_FF, D_MODEL)
    if update_ctx:
        a = _attention_ctx(q, k, v, a, B, T, Tc)
    m = _mlstm(z, zs, p["b_if"], p["g_mlstm"], B, T, Tc)
    n_rt = all_rt if update_ctx else lat_rt
    n_tm = all_tm if update_ctx else lat_tm
    t = _merge(a, m, z, p["w_proj_a"], p["w_proj_b"], p["w_proj_c"], p["w_conv"], p["b_conv"],
               rt, n_rt, lat_rt, T // rt, T, Tc)
    xa, h2, ri, rw, cnt = _post(t, p["w_o"], xa, g_norm, modt_tm, p["w_router"], p["b_router"], tm, n_tm)
    xs = _sort_rows(h2, ri, tm, n_tm)
    ys = _experts(_moe_tables(cnt, n_tm, tm), xs, w_up_c, p["b_up"], w_down_c, p["b_down"], n_tm)
    return _combine(ys, ri, rw, cnt, xa, g_norm, modt_tm, tm, n_tm)


def kernel(x, c, ctx, c_ctx, w_mod, b_mod, g_norm, w_in, g_q_lora, g_kv_lora, w_q_up, w_kv_up, b_if, g_mlstm,
           w_conv, b_conv, w_proj_a, w_proj_b, w_proj_c, w_o, w_router, b_router, w_up, b_up, w_down, b_down):
    B, T, _ = x.shape
    Tc = ctx.shape[1]
    depth = w_mod.shape[0]
    rt = min(1024, T)
    tm = min(512, T)
    assert T % rt == 0 and (B * Tc) % rt == 0 and (B * Tc) % tm == 0
    assert T & (T - 1) == 0 and Tc & (Tc - 1) == 0 and T % GRID_W == 0

    xa = jnp.concatenate([x.reshape(B * T, D_MODEL), ctx.reshape(B * Tc, D_MODEL)], axis=0)
    cond = jnp.zeros((8, D_MODEL), F32).at[:B].set(c).at[B].set(c_ctx)
    cos_t, sin_t = _rope_tables(T, rt)
    cast = lambda w: w.astype(MXU_DTYPE)
    in_cols = _in_proj_columns()
    n_main = N_MAIN_TILES * COL_TILE
    w_in_main = _take_cols(w_in, in_cols[:n_main], MXU_DTYPE)
    w_in_special = _take_cols(w_in, in_cols[n_main:], MXU_DTYPE)
    w_q_p = _take_cols(w_q_up, _q_up_columns(), MXU_DTYPE)
    w_kv_p = _take_cols(w_kv_up, _kv_up_columns(), MXU_DTYPE)
    w_up_r = w_up.reshape(depth * N_EXPERTS * D_MODEL, 2 * D_FF)
    w_down_r = w_down.reshape(depth * N_EXPERTS * D_FF, D_MODEL)
    wa_c, wb_c, wc_c, wo_c = cast(w_proj_a), cast(w_proj_b), cast(w_proj_c), cast(w_o)
    w_router_p = jnp.pad(w_router, ((0, 0), (0, 0), (0, LANE - N_EXPERTS)))
    b_router_p = jnp.pad(b_router, ((0, 0), (0, LANE - N_EXPERTS)), constant_values=NEG_BIG)
    b_if_p = jnp.pad(b_if.reshape(depth, 1, -1), ((0, 0), (0, 0), (0, LANE - 4 * M_HEADS)))
    mods = _adaln(cond, w_mod, b_mod)
    for l in range(depth):
        p = dict(
            g_norm=g_norm[l], w_in_main=w_in_main, w_in_special=w_in_special,
            g_q_lora=g_q_lora[l], g_kv_lora=g_kv_lora[l],
            w_q_up=w_q_p[l], w_kv_up=w_kv_p[l], b_if=b_if_p[l],
            g_mlstm=g_mlstm[l], w_conv=w_conv[l], b_conv=b_conv[l],
            w_proj_a=wa_c[l], w_proj_b=wb_c[l], w_proj_c=wc_c[l], w_o=wo_c[l],
            w_router=w_router_p[l], b_router=b_router_p[l],
            depth=depth, w_up=w_up_r, b_up=b_up[l].reshape(N_EXPERTS, 1, -1),
            w_down=w_down_r, b_down=b_down[l].reshape(N_EXPERTS, 1, -1),
        )
        xa = _layer(xa, mods[l], B, T, Tc, rt, tm, l < depth - 1, p, cos_t, sin_t, l)
    return xa.reshape(B, T, D_MODEL)
```

```python
import functools

import numpy as np
import jax
import jax.numpy as jnp
from jax import lax
from jax.experimental import pallas as pl
from jax.experimental.pallas import tpu as pltpu

F32 = jnp.float32
MXU_DTYPE = jnp.bfloat16

D_MODEL = 2048
GRID_W = 64
MLA_HEADS = 8
QK_NOPE = 128
QK_ROPE = 64
V_HEAD = 128
Q_LORA = 512
KV_LORA = 512
ROPE_THETA = 10000.0
ROPE_FREQS = QK_ROPE // 4
ATTN_SCALE = (QK_NOPE + QK_ROPE) ** -0.5
M_HEADS = 4
M_QK = 64
M_V = 128
CONV_WIDTH = 512
N_EXPERTS = 32
TOP_K = 4
D_FF = 1024
SWIGLU_ALPHA = 1.702
SWIGLU_LIMIT = 7.0
N_MOD = 6
EPS = 1e-6

LANE = 128
MLSTM_CHUNK = 256
COL_TILE = 512
INPROJ_TILE = 1024
LOG2_E = 1.4426950408889634
MOE_TILE = 512
SEG_ALIGN = 16
NEG_BIG = -1e30
VMEM_LIMIT = 56 * 1024 * 1024

ZT_CQ, ZT_CKV, ZT_MQK, ZT_MV, ZT_MO, ZT_CU, ZT_CB, ZT_CC, ZT_CG = 0, 1, 2, 3, 4, 5, 6, 7, 8
N_MAIN_TILES = 8 + 3 * D_MODEL // COL_TILE
assert (N_MAIN_TILES * COL_TILE) % INPROJ_TILE == 0


def _cparams(n_axes, vmem=VMEM_LIMIT):
    return pltpu.CompilerParams(dimension_semantics=("arbitrary",) * n_axes, vmem_limit_bytes=vmem)


def _rms(x, g):
    return x * lax.rsqrt(jnp.mean(x * x, axis=-1, keepdims=True) + EPS) * g


def _lane_pick(x, idx):
    lane = lax.broadcasted_iota(jnp.int32, x.shape, 1)
    return jnp.sum(jnp.where(lane == idx, x, 0.0), axis=1, keepdims=True)


def _mod_kernel(c_ref, w_ref, b_ref, o_ref):
    c = c_ref[...]
    s = c * jax.nn.sigmoid(c)
    o_ref[0] = jnp.dot(s.astype(MXU_DTYPE), w_ref[0].astype(MXU_DTYPE),
                       preferred_element_type=F32) + b_ref[0]


def _adaln(cond, w_mod, b_mod):
    depth, _, n = w_mod.shape
    tn = 1024
    return pl.pallas_call(
        _mod_kernel,
        grid=(depth, n // tn),
        in_specs=[pl.BlockSpec(cond.shape, lambda l, j: (0, 0)),
                  pl.BlockSpec((1, D_MODEL, tn), lambda l, j: (l, 0, j)),
                  pl.BlockSpec((1, 1, tn), lambda l, j: (l, 0, j))],
        out_specs=pl.BlockSpec((1, cond.shape[0], tn), lambda l, j: (l, 0, j)),
        out_shape=jax.ShapeDtypeStruct((depth, cond.shape[0], n), F32),
        compiler_params=_cparams(2),
        name="adaln",
    )(cond, w_mod, b_mod.reshape(depth, 1, n))


def _inproj_kernel(x_ref, g_ref, sh_ref, sc_ref, w_ref, ws_ref, z_ref, zs_ref, h_scr):
    @pl.when(pl.program_id(1) == 0)
    def _():
        h = _rms(x_ref[...], g_ref[0]) * (1.0 + sc_ref[0]) + sh_ref[0]
        h_scr[...] = h.astype(h_scr.dtype)
        zs_ref[...] = jnp.dot(h_scr[...], ws_ref[0], preferred_element_type=F32)

    z_ref[...] = jnp.dot(h_scr[...], w_ref[0], preferred_element_type=F32).astype(z_ref.dtype)


def _inproj(xa, g_norm, modt, w_main, w_special, layer, rt):
    rows = xa.shape[0]
    n_main = w_main.shape[2]
    tn = INPROJ_TILE
    return pl.pallas_call(
        _inproj_kernel,
        grid=(rows // rt, n_main // tn),
        in_specs=[pl.BlockSpec((rt, D_MODEL), lambda i, j: (i, 0)),
                  pl.BlockSpec((1, 1, D_MODEL), lambda i, j: (0, 0, 0)),
                  pl.BlockSpec((1, 1, D_MODEL), lambda i, j: (i, 0, 0)),
                  pl.BlockSpec((1, 1, D_MODEL), lambda i, j: (i, 0, 1)),
                  pl.BlockSpec((1, D_MODEL, tn), lambda i, j: (layer, 0, j)),
                  pl.BlockSpec((1, D_MODEL, COL_TILE), lambda i, j: (layer, 0, 0))],
        out_specs=[pl.BlockSpec((rt, tn), lambda i, j: (i, j)),
                   pl.BlockSpec((rt, COL_TILE), lambda i, j: (i, 0))],
        out_shape=[jax.ShapeDtypeStruct((rows, n_main), MXU_DTYPE),
                   jax.ShapeDtypeStruct((rows, COL_TILE), F32)],
        scratch_shapes=[pltpu.VMEM((rt, D_MODEL), MXU_DTYPE)],
        compiler_params=_cparams(2),
        name="inproj",
    )(xa, g_norm, modt, modt, w_main, w_special)


def _mla_kernel(cq_ref, ckv_ref, zs_ref, gq_ref, gkv_ref, wq_ref, wkv_ref, cos_ref, sin_ref,
                q_ref, k_ref, v_ref):
    def norm(z_ref, g_ref):
        return _rms(z_ref[...].astype(F32), g_ref[...]).astype(MXU_DTYPE)

    qf = jnp.dot(norm(cq_ref, gq_ref), wq_ref[...], preferred_element_type=F32) * (ATTN_SCALE * LOG2_E)
    kvf = jnp.dot(norm(ckv_ref, gkv_ref), wkv_ref[...], preferred_element_type=F32)
    cos = cos_ref[...]
    sin = sin_ref[...]
    nope_w = MLA_HEADS * QK_NOPE
    rope_w = MLA_HEADS * QK_ROPE
    for p in range(MLA_HEADS // 2):
        a = nope_w + LANE * p
        b = nope_w + rope_w + LANE * p
        rp = (qf[:, a:a + LANE] * cos + qf[:, b:b + LANE] * sin).astype(MXU_DTYPE)
        for h in (2 * p, 2 * p + 1):
            q_ref[:, 256 * h:256 * h + LANE] = qf[:, LANE * h:LANE * (h + 1)].astype(MXU_DTYPE)
            q_ref[:, 256 * h + LANE:256 * (h + 1)] = rp
    kr2 = zs_ref[:, 0:LANE] * cos + zs_ref[:, LANE:2 * LANE] * sin
    lane = lax.broadcasted_iota(jnp.int32, kr2.shape, 1)
    k_lo = jnp.where(lane < QK_ROPE, kr2, 0.0).astype(MXU_DTYPE)
    k_hi = jnp.where(lane >= QK_ROPE, kr2, 0.0).astype(MXU_DTYPE)
    for h in range(MLA_HEADS):
        k_ref[:, 256 * h:256 * h + LANE] = kvf[:, LANE * h:LANE * (h + 1)].astype(MXU_DTYPE)
        k_ref[:, 256 * h + LANE:256 * (h + 1)] = k_lo if h % 2 == 0 else k_hi
    ones = jnp.ones((kvf.shape[0], LANE), MXU_DTYPE)
    v0 = MLA_HEADS * QK_NOPE
    for h in range(MLA_HEADS):
        v_ref[:, 256 * h:256 * h + LANE] = kvf[:, v0 + LANE * h:v0 + LANE * (h + 1)].astype(MXU_DTYPE)
        v_ref[:, 256 * h + LANE:256 * (h + 1)] = ones


def _mla_project(z, zs, g_q, g_kv, wq_p, wkv_p, cos_t, sin_t, rt, n_lat_tiles, tiles_per_seq):
    rows = z.shape[0]
    nt = rows // rt

    def tab(i):
        return (jnp.where(i < n_lat_tiles, i % tiles_per_seq, tiles_per_seq), 0)

    return pl.pallas_call(
        _mla_kernel,
        grid=(nt,),
        in_specs=[pl.BlockSpec((rt, COL_TILE), lambda i: (i, ZT_CQ)),
                  pl.BlockSpec((rt, COL_TILE), lambda i: (i, ZT_CKV)),
                  pl.BlockSpec((rt, COL_TILE), lambda i: (i, 0)),
                  pl.BlockSpec((1, Q_LORA), lambda i: (0, 0)),
                  pl.BlockSpec((1, KV_LORA), lambda i: (0, 0)),
                  pl.BlockSpec(wq_p.shape, lambda i: (0, 0)),
                  pl.BlockSpec(wkv_p.shape, lambda i: (0, 0)),
                  pl.BlockSpec((rt, LANE), tab),
                  pl.BlockSpec((rt, LANE), tab)],
        out_specs=[pl.BlockSpec((rt, 256 * MLA_HEADS), lambda i: (i, 0)),
                   pl.BlockSpec((rt, 256 * MLA_HEADS), lambda i: (i, 0)),
                   pl.BlockSpec((rt, 256 * MLA_HEADS), lambda i: (i, 0))],
        out_shape=[jax.ShapeDtypeStruct((rows, 256 * MLA_HEADS), MXU_DTYPE)] * 3,
        compiler_params=_cparams(1),
        name="mla_project",
    )(z, z, zs, g_q.reshape(1, -1), g_kv.reshape(1, -1), wq_p, wkv_p, cos_t, sin_t)


def _attn_kernel(*refs, kv_chunks, n_cast=0):
    q_ref = refs[0]
    o_ref = refs[len(refs) - 1 - n_cast]
    for c in range(n_cast):
        src = refs[1 + 2 * len(kv_chunks) + c]
        dst = refs[len(refs) - n_cast + c]
        dst[...] = src[...].astype(dst.dtype)
    q = q_ref[...]
    tq = q.shape[0]
    m = jnp.full((tq, 1), -jnp.inf, F32)
    acc = jnp.zeros((tq, 2 * V_HEAD), F32)
    for s, n_chunks in enumerate(kv_chunks):
        k_ref, v_ref = refs[1 + 2 * s], refs[2 + 2 * s]
        ck = k_ref.shape[0] // n_chunks
        for c in range(n_chunks):
            k = k_ref[c * ck:(c + 1) * ck, :]
            v = v_ref[c * ck:(c + 1) * ck, :]
            sc = lax.dot_general(q, k, (((1,), (1,)), ((), ())), preferred_element_type=F32)
            m_new = jnp.maximum(m, jnp.max(sc, axis=-1, keepdims=True))
            p = jnp.exp2(sc - m_new).astype(MXU_DTYPE)
            acc = jnp.exp2(m - m_new) * acc + jnp.dot(p, v, preferred_element_type=F32)
            m = m_new
    o_ref[...] = (acc[:, :V_HEAD] / acc[:, V_HEAD:]).astype(o_ref.dtype)


def _attention_latent(q, k, v, w_up, w_down, layer, depth, B, T, Tc, tq, ck, rows):
    qt = T // tq
    ctx_blk0 = B * T // Tc
    steps = B * MLA_HEADS * qt
    step = lambda b, h, i: (b * MLA_HEADS + h) * qt + i
    cast_specs_in, cast_specs_out, cast_shapes = [], [], []
    for w in (w_up, w_down):
        n = w.shape[1]
        layer_rows = w.shape[0] // depth
        slab = layer_rows // steps
        assert slab * steps == layer_rows and slab % 16 == 0
        cast_specs_in.append(pl.BlockSpec((slab, n), lambda b, h, i: (layer * steps + step(b, h, i), 0)))
        cast_specs_out.append(pl.BlockSpec((slab, n), lambda b, h, i: (step(b, h, i), 0)))
        cast_shapes.append(jax.ShapeDtypeStruct((layer_rows, n), MXU_DTYPE))
    return pl.pallas_call(
        functools.partial(_attn_kernel, kv_chunks=(T // ck, 1), n_cast=2),
        grid=(B, MLA_HEADS, qt),
        in_specs=[pl.BlockSpec((tq, 256), lambda b, h, i: (b * qt + i, h)),
                  pl.BlockSpec((T, 256), lambda b, h, i: (b, h)),
                  pl.BlockSpec((T, 256), lambda b, h, i: (b, h)),
                  pl.BlockSpec((Tc, 256), lambda b, h, i: (ctx_blk0 + b, h)),
                  pl.BlockSpec((Tc, 256), lambda b, h, i: (ctx_blk0 + b, h))] + cast_specs_in,
        out_specs=[pl.BlockSpec((tq, V_HEAD), lambda b, h, i: (b * qt + i, h))] + cast_specs_out,
        out_shape=[jax.ShapeDtypeStruct((rows, V_HEAD * MLA_HEADS), MXU_DTYPE)] + cast_shapes,
        compiler_params=_cparams(3),
        name="attn_latent",
    )(q, k, v, k, v, w_up, w_down)


def _attention_ctx(q, k, v, a_buf, B, T, Tc):
    ctx_blk0 = B * T // Tc
    return pl.pallas_call(
        functools.partial(_attn_kernel, kv_chunks=(1,)),
        grid=(B, MLA_HEADS),
        in_specs=[pl.BlockSpec((Tc, 256), lambda b, h: (ctx_blk0 + b, h)),
                  pl.BlockSpec((Tc, 256), lambda b, h: (ctx_blk0 + b, h)),
                  pl.BlockSpec((Tc, 256), lambda b, h: (ctx_blk0 + b, h)),
                  pl.BlockSpec(memory_space=pl.ANY)],
        out_specs=pl.BlockSpec((Tc, V_HEAD), lambda b, h: (ctx_blk0 + b, h)),
        out_shape=jax.ShapeDtypeStruct(a_buf.shape, a_buf.dtype),
        input_output_aliases={3: 0},
        compiler_params=_cparams(2),
        name="attn_ctx",
    )(q, k, v, a_buf)


def _mlstm_chunk(qk, v, g, lf_cum, gt_ref, head, reverse, ct_ref, m_prev, consts):
    tri, row_iota, mk, ones_tile, lane = consts
    L = qk.shape[0]
    base = 8 if reverse else 0
    col_i, col_f = base + head, base + 4 + head
    ql = jnp.where(lane < M_QK, qk, jnp.zeros_like(qk))
    kl = jnp.dot(qk, mk, preferred_element_type=F32).astype(MXU_DTYPE)
    v_aug = jnp.concatenate([v, ones_tile], axis=1)
    bc = _lane_pick(lf_cum, col_f)
    li = _lane_pick(g, col_i)
    r_row = gt_ref[0, col_i:col_i + 1, :] - gt_ref[1, col_f:col_f + 1, :]
    dlog = jnp.where(tri, bc + r_row, -jnp.inf)
    inter = bc + m_prev
    m_t = jnp.maximum(inter, jnp.max(dlog, axis=1, keepdims=True))
    dexp = jnp.exp(dlog - m_t)
    s_raw = lax.dot_general(ql, kl, (((1,), (1,)), ((), ())), preferred_element_type=F32)
    s_mat = (s_raw * (M_QK ** -0.5) * dexp).astype(MXU_DTYPE)
    a = jnp.exp(inter - m_t) * (M_QK ** -0.5)
    ct = ct_ref[...]
    nd = (jnp.dot(s_mat, v_aug, preferred_element_type=F32)
          + a * jnp.dot(ql, ct.astype(MXU_DTYPE), preferred_element_type=F32))
    num, den = nd[:, :M_V], nd[:, M_V:]
    h_out = num / jnp.maximum(jnp.abs(den), jnp.exp(-m_t))
    last = 0 if reverse else L - 1
    total = jnp.sum(jnp.where(row_iota == last, bc, 0.0), axis=0, keepdims=True)
    w_col = total - bc + li
    m_new = jnp.maximum(total + m_prev, jnp.max(w_col, axis=0, keepdims=True))
    decay = jnp.exp(total + m_prev - m_new)
    e = jnp.exp(w_col - m_new)
    ek = (kl.astype(F32) * e).astype(MXU_DTYPE)
    upd = lax.dot_general(ek, v_aug, (((0,), (0,)), ((), ())), preferred_element_type=F32)
    ct_ref[...] = decay * ct + upd
    return h_out, m_new


def _mlstm_kernel(zl_qk, zl_v, zl_mo, gl_ref, zc_qk, zc_v, zc_mo, gc_ref, bif_ref, gm_ref,
                  ol_ref, oc_ref, acc_l, acc_c, ct_scr, gt_scr):
    L = MLSTM_CHUNK
    T = zl_qk.shape[0]
    nc = T // L
    r = lax.broadcasted_iota(jnp.int32, (L, L), 0)
    c = lax.broadcasted_iota(jnp.int32, (L, L), 1)
    tri_f = c <= r
    tri_b = c >= r
    tmat_f = tri_f.astype(F32)
    tmat_b = tri_b.astype(F32)
    row_iota = lax.broadcasted_iota(jnp.int32, (L, 1), 0)
    lane = lax.broadcasted_iota(jnp.int32, (L, LANE), 1)
    rr = lax.broadcasted_iota(jnp.int32, (LANE, LANE), 0)
    cc = lax.broadcasted_iota(jnp.int32, (LANE, LANE), 1)
    mk = (rr == cc + M_QK).astype(MXU_DTYPE)
    ones_tile = jnp.ones((L, LANE), MXU_DTYPE)
    consts_f = (tri_f, row_iota, mk, ones_tile, lane)
    consts_b = (tri_b, row_iota, mk, ones_tile, lane)
    bias = bif_ref[...]

    def gate_prep(g_raw, tmat, slot):
        g = g_raw + bias
        log_sig = jnp.minimum(g, 0.0) - jnp.log1p(jnp.exp(-jnp.abs(g)))
        cum = jnp.dot(tmat, log_sig, precision=lax.Precision.HIGHEST, preferred_element_type=F32)
        gt_scr[slot, 0] = g.T
        gt_scr[slot, 1] = cum.T
        return g, cum

    def step(qk_ref, v_ref, g_ref, acc_ref, row_f, row_b, ms):
        g_f, cum_f = gate_prep(g_ref[pl.ds(row_f, L), :], tmat_f, 0)
        g_b, cum_b = gate_prep(g_ref[pl.ds(row_b, L), :], tmat_b, 1)
        new_ms = []
        for h in range(M_HEADS):
            cols = slice(h * LANE, (h + 1) * LANE)
            hf, mf = _mlstm_chunk(qk_ref[pl.ds(row_f, L), cols], v_ref[pl.ds(row_f, L), cols],
                                  g_f, cum_f, gt_scr.at[0], h, False,
                                  ct_scr.at[h], ms[h], consts_f)
            acc_ref[pl.ds(row_f, L), cols] += hf
            hb, mb = _mlstm_chunk(qk_ref[pl.ds(row_b, L), cols], v_ref[pl.ds(row_b, L), cols],
                                  g_b, cum_b, gt_scr.at[1], h, True,
                                  ct_scr.at[M_HEADS + h], ms[M_HEADS + h], consts_b)
            acc_ref[pl.ds(row_b, L), cols] += hb
            new_ms.append((mf, mb))
        return tuple(x[0] for x in new_ms) + tuple(x[1] for x in new_ms)

    ct_scr[...] = jnp.zeros_like(ct_scr)
    acc_l[...] = jnp.zeros_like(acc_l)
    acc_c[...] = jnp.zeros_like(acc_c)
    zero = jnp.zeros((1, 1), F32)
    ms = step(zc_qk, zc_v, gc_ref, acc_c, 0, 0, (zero,) * (2 * M_HEADS))

    def body(i, ms):
        row_f = pl.multiple_of(i * L, L)
        row_b = pl.multiple_of((nc - 1 - i) * L, L)
        return step(zl_qk, zl_v, gl_ref, acc_l, row_f, row_b, ms)

    lax.fori_loop(0, nc, body, ms)

    gm = gm_ref[...]

    def finish(acc_ref, mo_ref, o_ref, row):
        for h in range(M_HEADS):
            cols = slice(h * LANE, (h + 1) * LANE)
            hn = _rms(acc_ref[pl.ds(row, L), cols], gm[:, cols])
            gate = jax.nn.sigmoid(mo_ref[pl.ds(row, L), cols].astype(F32))
            o_ref[pl.ds(row, L), cols] = (hn * gate).astype(o_ref.dtype)

    finish(acc_c, zc_mo, oc_ref, 0)

    def fin_body(i, carry):
        finish(acc_l, zl_mo, ol_ref, pl.multiple_of(i * L, L))
        return carry

    lax.fori_loop(0, nc, fin_body, 0)


def _mlstm(z, zs, bif_row, g_mlstm, B, T, Tc):
    assert Tc == MLSTM_CHUNK and T % MLSTM_CHUNK == 0
    rows = z.shape[0]
    ctx_blk0 = B * T // Tc
    w = M_HEADS * LANE
    out_l, out_c = pl.pallas_call(
        _mlstm_kernel,
        grid=(B,),
        in_specs=[pl.BlockSpec((T, w), lambda b: (b, ZT_MQK)),
                  pl.BlockSpec((T, w), lambda b: (b, ZT_MV)),
                  pl.BlockSpec((T, w), lambda b: (b, ZT_MO)),
                  pl.BlockSpec((T, LANE), lambda b: (b, 2)),
                  pl.BlockSpec((Tc, w), lambda b: (ctx_blk0 + b, ZT_MQK)),
                  pl.BlockSpec((Tc, w), lambda b: (ctx_blk0 + b, ZT_MV)),
                  pl.BlockSpec((Tc, w), lambda b: (ctx_blk0 + b, ZT_MO)),
                  pl.BlockSpec((Tc, LANE), lambda b: (ctx_blk0 + b, 2)),
                  pl.BlockSpec((1, LANE), lambda b: (0, 0)),
                  pl.BlockSpec((1, w), lambda b: (0, 0))],
        out_specs=[pl.BlockSpec((T, w), lambda b: (b, 0)),
                   pl.BlockSpec((Tc, w), lambda b: (b, 0))],
        out_shape=[jax.ShapeDtypeStruct((B * T, w), MXU_DTYPE),
                   jax.ShapeDtypeStruct((B * Tc, w), MXU_DTYPE)],
        scratch_shapes=[pltpu.VMEM((T, w), F32), pltpu.VMEM((Tc, w), F32),
                        pltpu.VMEM((2 * M_HEADS, LANE, 2 * LANE), F32),
                        pltpu.VMEM((2, 2, LANE, MLSTM_CHUNK), F32)],
        compiler_params=_cparams(1),
        name="mlstm",
    )(z, z, z, zs, z, z, z, zs, bif_row, g_mlstm.reshape(1, -1))
    return jnp.concatenate([out_l, out_c], axis=0)


def _merge_kernel(a_ref, m_ref, cu_ref, cb_ref, cc_ref, cu_p, cc_p, cu_n, cc_n, ga_ref, gm_ref, gs_ref,
                  wa_ref, wb_ref, wc_ref, wconv_ref, bconv_ref, t_ref, s_scr,
                  *, n_lat_tiles, tiles_per_seq, seq_len, ctx_len):
    i = pl.program_id(0)
    j = pl.program_id(1)
    rt = a_ref.shape[0]

    @pl.when(j == 0)
    def _():
        u = cc_ref[...].astype(F32) * cu_ref[...].astype(F32)
        hp = cc_p[...].astype(F32) * cu_p[...].astype(F32)
        hn = cc_n[...].astype(F32) * cu_n[...].astype(F32)
        row = lax.broadcasted_iota(jnp.int32, (rt, 1), 0)
        is_ctx = i >= n_lat_tiles
        seg = jnp.where(is_ctx, ctx_len, seq_len)
        off = jnp.where(is_ctx, (i - n_lat_tiles) * rt, (i % tiles_per_seq) * rt)
        pos = (off + row) & (seg - 1)
        up = pltpu.roll(u, 1, 0)
        up = jnp.where(row == 0, hp[hp.shape[0] - 1:, :], up)
        up = jnp.where(pos == 0, 0.0, up)
        un = pltpu.roll(u, rt - 1, 0)
        un = jnp.where(row == rt - 1, hn[0:1, :], un)
        un = jnp.where(pos == seg - 1, 0.0, un)
        wconv = wconv_ref[...]
        y = up * wconv[0:1, :] + u * wconv[1:2, :] + un * wconv[2:3, :] + bconv_ref[...]
        s_scr[...] = (cb_ref[...].astype(F32) * y).astype(s_scr.dtype)

    def sig(ref):
        return jax.nn.sigmoid(ref[...])

    t = (sig(ga_ref) * jnp.dot(a_ref[...], wa_ref[...], preferred_element_type=F32)
         + sig(gm_ref) * jnp.dot(m_ref[...], wb_ref[...], preferred_element_type=F32)
         + sig(gs_ref) * jnp.dot(s_scr[...], wc_ref[...], preferred_element_type=F32))
    t_ref[...] = t.astype(t_ref.dtype)


def _merge(a, m, z, wa, wb, wc, w_conv, b_conv, rt, n_tiles, n_lat_tiles, tiles_per_seq, T, Tc):
    rows = n_tiles * rt
    hb = 16
    rb = rt // hb
    last_hb = z.shape[0] // hb - 1
    tn = COL_TILE
    nj = D_MODEL // tn
    gt = D_MODEL // tn

    def prev_blk(col):
        return lambda i, j: (jnp.maximum(i * rb - 1, 0), col)

    def next_blk(col):
        return lambda i, j: (jnp.minimum((i + 1) * rb, last_hb), col)

    kern = functools.partial(_merge_kernel, n_lat_tiles=n_lat_tiles, tiles_per_seq=tiles_per_seq,
                             seq_len=T, ctx_len=Tc)
    return pl.pallas_call(
        kern,
        grid=(n_tiles, nj),
        in_specs=[pl.BlockSpec((rt, a.shape[1]), lambda i, j: (i, 0)),
                  pl.BlockSpec((rt, m.shape[1]), lambda i, j: (i, 0)),
                  pl.BlockSpec((rt, COL_TILE), lambda i, j: (i, ZT_CU)),
                  pl.BlockSpec((rt, COL_TILE), lambda i, j: (i, ZT_CB)),
                  pl.BlockSpec((rt, COL_TILE), lambda i, j: (i, ZT_CC)),
                  pl.BlockSpec((hb, COL_TILE), prev_blk(ZT_CU)),
                  pl.BlockSpec((hb, COL_TILE), prev_blk(ZT_CC)),
                  pl.BlockSpec((hb, COL_TILE), next_blk(ZT_CU)),
                  pl.BlockSpec((hb, COL_TILE), next_blk(ZT_CC)),
                  pl.BlockSpec((rt, tn), lambda i, j: (i, ZT_CG + j)),
                  pl.BlockSpec((rt, tn), lambda i, j: (i, ZT_CG + gt + j)),
                  pl.BlockSpec((rt, tn), lambda i, j: (i, ZT_CG + 2 * gt + j)),
                  pl.BlockSpec((wa.shape[0], tn), lambda i, j: (0, j)),
                  pl.BlockSpec((wb.shape[0], tn), lambda i, j: (0, j)),
                  pl.BlockSpec((wc.shape[0], tn), lambda i, j: (0, j)),
                  pl.BlockSpec(w_conv.shape, lambda i, j: (0, 0)),
                  pl.BlockSpec((1, CONV_WIDTH), lambda i, j: (0, 0))],
        out_specs=pl.BlockSpec((rt, tn), lambda i, j: (i, j)),
        out_shape=jax.ShapeDtypeStruct((rows, D_MODEL), MXU_DTYPE),
        scratch_shapes=[pltpu.VMEM((rt, CONV_WIDTH), MXU_DTYPE)],
        compiler_params=_cparams(2),
        name="merge",
    )(a, m, z, z, z, z, z, z, z, z, z, z, wa, wb, wc, w_conv, b_conv.reshape(1, -1))


def _dot3(x, w):
    xh = x.astype(jnp.bfloat16)
    xl = (x - xh.astype(F32)).astype(jnp.bfloat16)
    wh = w.astype(jnp.bfloat16)
    wl = (w - wh.astype(F32)).astype(jnp.bfloat16)
    return (jnp.dot(xh, wh, preferred_element_type=F32) + jnp.dot(xh, wl, preferred_element_type=F32)
            + jnp.dot(xl, wh, preferred_element_type=F32))


def _seg_pad(counts):
    return jnp.ceil(counts * (1.0 / SEG_ALIGN)) * SEG_ALIGN


def _exclusive_lane_cumsum(row):
    upper = (lax.broadcasted_iota(jnp.int32, (LANE, LANE), 0)
             < lax.broadcasted_iota(jnp.int32, (LANE, LANE), 1)).astype(jnp.bfloat16)
    out = jnp.dot(jnp.broadcast_to(row, (8, LANE)).astype(jnp.bfloat16), upper, preferred_element_type=F32)
    return out[0:1, :]


def _post_kernel(t_ref, wo_ref, x_ref, g1_ref, g2_ref, gt_ref, sh_ref, sc_ref, wr_ref, br_ref,
                 xo_ref, h2_ref, ri_ref, rw_ref, cnt_ref):
    y = jnp.dot(t_ref[...], wo_ref[...], preferred_element_type=F32)
    xn = x_ref[...] + gt_ref[0] * _rms(y, g1_ref[0])
    xo_ref[...] = xn
    h2 = _rms(xn, g2_ref[0]) * (1.0 + sc_ref[0]) + sh_ref[0]
    h2_ref[...] = h2.astype(h2_ref.dtype)
    logits = _dot3(h2, wr_ref[...]) + br_ref[...]
    lane = lax.broadcasted_iota(jnp.int32, logits.shape, 1)
    lane_f = lane.astype(F32)
    vals, ids = [], []
    for _ in range(TOP_K):
        mx = jnp.max(logits, axis=1, keepdims=True)
        idx = jnp.min(jnp.where(logits == mx, lane_f, float(LANE)), axis=1, keepdims=True)
        logits = jnp.where(lane_f == idx, -jnp.inf, logits)
        vals.append(mx)
        ids.append(idx)
    es = [jnp.exp(v - vals[0]) for v in vals]
    tot = es[0] + es[1] + es[2] + es[3]
    tm = logits.shape[0]
    onehots = [lane_f == ids[k] for k in range(TOP_K)]
    asg = jnp.zeros(logits.shape, F32)
    for k in range(TOP_K):
        asg = jnp.where(onehots[k], 1.0, asg)
    earlier = (lax.broadcasted_iota(jnp.int32, (tm, tm), 1)
               < lax.broadcasted_iota(jnp.int32, (tm, tm), 0)).astype(jnp.bfloat16)
    before = jnp.dot(earlier, asg.astype(jnp.bfloat16), preferred_element_type=F32)
    counts = jnp.sum(asg, axis=0, keepdims=True)
    seg_off = _exclusive_lane_cumsum(_seg_pad(counts))
    rw = jnp.zeros(logits.shape, F32)
    ri = jnp.zeros(logits.shape, F32)
    for k in range(TOP_K):
        pos = jnp.sum(jnp.where(onehots[k], before + seg_off, 0.0), axis=1, keepdims=True)
        rw = jnp.where(lane == k, es[k] / tot, rw)
        ri = jnp.where(lane == k, ids[k], ri)
        ri = jnp.where(lane == TOP_K + k, pos, ri)
    rw_ref[...] = rw
    ri_ref[...] = ri.astype(jnp.int32)
    cnt_ref[0] = jnp.broadcast_to(counts, cnt_ref.shape[1:])


def _post(t, wo, xa, g_norm, modt, wr_p, br_p, tm, n_tiles):
    rows = n_tiles * tm
    row_spec = lambda w: pl.BlockSpec((tm, w), lambda i: (i, 0))
    mod = lambda k: pl.BlockSpec((1, 1, D_MODEL), lambda i: (i, 0, k))
    gn = lambda k: pl.BlockSpec((1, 1, D_MODEL), lambda i: (k, 0, 0))
    return pl.pallas_call(
        _post_kernel,
        grid=(n_tiles,),
        in_specs=[row_spec(D_MODEL), pl.BlockSpec(wo.shape, lambda i: (0, 0)), row_spec(D_MODEL),
                  gn(1), gn(2), mod(2), mod(3), mod(4),
                  pl.BlockSpec(wr_p.shape, lambda i: (0, 0)), pl.BlockSpec((1, LANE), lambda i: (0, 0))],
        out_specs=[row_spec(D_MODEL), row_spec(D_MODEL), row_spec(LANE), row_spec(LANE),
                   pl.BlockSpec((1, 8, LANE), lambda i: (i, 0, 0))],
        out_shape=[jax.ShapeDtypeStruct((rows, D_MODEL), F32),
                   jax.ShapeDtypeStruct((rows, D_MODEL), MXU_DTYPE),
                   jax.ShapeDtypeStruct((rows, LANE), jnp.int32),
                   jax.ShapeDtypeStruct((rows, LANE), F32),
                   jax.ShapeDtypeStruct((rows // tm, 8, LANE), F32)],
        compiler_params=_cparams(1),
        name="post_mix",
    )(t, wo, xa, g_norm, g_norm, modt, modt, modt, wr_p, br_p.reshape(1, LANE))


def _sort_kernel(h2_ref, ri_ref, xs_ref):
    tm = h2_ref.shape[0]
    rif = ri_ref[...].astype(F32)
    pos = [_lane_pick(rif, TOP_K + k) for k in range(TOP_K)]
    h2 = h2_ref[...]
    for c in range(xs_ref.shape[0] // tm):
        col = (lax.broadcasted_iota(jnp.int32, (tm, tm), 1) + c * tm).astype(F32)
        onehot = jnp.zeros((tm, tm), F32)
        for k in range(TOP_K):
            onehot = jnp.where(pos[k] == col, 1.0, onehot)
        rows = lax.dot_general(onehot.astype(MXU_DTYPE), h2, (((0,), (0,)), ((), ())),
                               preferred_element_type=F32)
        xs_ref[c * tm:(c + 1) * tm, :] = rows.astype(xs_ref.dtype)


def _sort_rows(h2, ri, tm, n_tiles):
    seg_rows = TOP_K * tm + N_EXPERTS * SEG_ALIGN
    assert seg_rows % tm == 0
    return pl.pallas_call(
        _sort_kernel,
        grid=(n_tiles,),
        in_specs=[pl.BlockSpec((tm, D_MODEL), lambda i: (i, 0)),
                  pl.BlockSpec((tm, LANE), lambda i: (i, 0))],
        out_specs=pl.BlockSpec((seg_rows, D_MODEL), lambda i: (i, 0)),
        out_shape=jax.ShapeDtypeStruct((n_tiles * seg_rows, D_MODEL), MXU_DTYPE),
        compiler_params=_cparams(1),
        name="sort_rows",
    )(h2, ri)


def _expert_kernel(te_ref, r0_ref, na_ref, seg_ref, src_ref, used_ref, tlo_ref, thi_ref,
                   xs_hbm, wu_ref, bu_ref, wd_ref, bd_ref, ys_hbm,
                   xbuf, ybuf, sem_in, sem_out, *, n_tok_tiles):
    w = pl.program_id(0)
    n_active = na_ref[0]
    tm = ybuf.shape[0]
    nt = n_tok_tiles
    sizes = [s for s in (512, 256, 128, 64, 32, 16) if s <= tm]

    def for_pieces(length, fn):
        off = jnp.int32(0)
        for size in sizes:
            hit = (length & size) != 0

            @pl.when(hit)
            def _(off=off, size=size):
                fn(off, size)

            off = off + jnp.where(hit, size, 0)

    def for_segments(wt, fn):
        e = te_ref[wt]
        r0 = r0_ref[wt]

        def body(t, carry):
            s0 = seg_ref[e * (nt + 1) + t]
            s1 = seg_ref[e * (nt + 1) + t + 1]
            lo = jnp.maximum(s0, r0)
            length = jnp.maximum(jnp.minimum(s1, r0 + tm) - lo, 0)
            src = src_ref[e * nt + t] + (lo - s0)
            dst = lo - r0
            for_pieces(length, lambda off, size: fn(src + off, dst + off, size))
            return carry

        lax.fori_loop(tlo_ref[wt], thi_ref[wt], body, 0)

    def rows_in_use(wt):
        return used_ref[wt]

    def aligned(row):
        return row if isinstance(row, int) else pl.multiple_of(row, SEG_ALIGN)

    def in_copy(hbm_row, buf_row, size, slot):
        return pltpu.make_async_copy(xs_hbm.at[pl.ds(aligned(hbm_row), size)],
                                     xbuf.at[slot, pl.ds(aligned(buf_row), size)], sem_in.at[slot])

    def out_copy(hbm_row, buf_row, size):
        return pltpu.make_async_copy(ybuf.at[pl.ds(aligned(buf_row), size)],
                                     ys_hbm.at[pl.ds(aligned(hbm_row), size)], sem_out.at[0])

    def start_in(wt, slot):
        for_segments(wt, lambda h, b, size: in_copy(h, b, size, slot).start())

    def wait_in(wt, slot):
        for_pieces(rows_in_use(wt), lambda off, size: in_copy(0, 0, size, slot).wait())

    def wait_out(wt):
        for_pieces(rows_in_use(wt), lambda off, size: out_copy(0, 0, size).wait())

    @pl.when(w < n_active)
    def _():
        slot = w % 2

        @pl.when(w == 0)
        def _():
            xbuf[...] = jnp.zeros_like(xbuf)
            start_in(0, 0)

        wait_in(w, slot)

        @pl.when(w + 1 < n_active)
        def _():
            start_in(w + 1, 1 - slot)

        gu = jnp.dot(xbuf[slot], wu_ref[0], preferred_element_type=F32) + bu_ref[0]
        g = jnp.minimum(gu[:, :D_FF], SWIGLU_LIMIT)
        u = jnp.clip(gu[:, D_FF:], -SWIGLU_LIMIT, SWIGLU_LIMIT)
        hid = g * jax.nn.sigmoid(SWIGLU_ALPHA * g) * (u + 1.0)
        y = jnp.dot(hid.astype(MXU_DTYPE), wd_ref[0], preferred_element_type=F32) + bd_ref[0]

        @pl.when(w > 0)
        def _():
            wait_out(w - 1)

        ybuf[...] = y.astype(ybuf.dtype)
        for_segments(w, lambda h, b, size: out_copy(h, b, size).start())

        @pl.when(w == n_active - 1)
        def _():
            wait_out(w)


def _experts(tables, xs, w_up, b_up, w_down, b_down, n_tok_tiles):
    tile_expert = tables[0]
    tm = MOE_TILE
    grid_spec = pltpu.PrefetchScalarGridSpec(
        num_scalar_prefetch=len(tables),
        grid=(tile_expert.shape[0],),
        in_specs=[pl.BlockSpec(memory_space=pl.ANY),
                  pl.BlockSpec((1, D_MODEL, 2 * D_FF), lambda t, te, *_: (te[t], 0, 0)),
                  pl.BlockSpec((1, 1, 2 * D_FF), lambda t, te, *_: (te[t], 0, 0)),
                  pl.BlockSpec((1, D_FF, D_MODEL), lambda t, te, *_: (te[t], 0, 0)),
                  pl.BlockSpec((1, 1, D_MODEL), lambda t, te, *_: (te[t], 0, 0))],
        out_specs=pl.BlockSpec(memory_space=pl.ANY),
        scratch_shapes=[pltpu.VMEM((2, tm, D_MODEL), xs.dtype), pltpu.VMEM((tm, D_MODEL), xs.dtype),
                        pltpu.SemaphoreType.DMA((2,)), pltpu.SemaphoreType.DMA((1,))],
    )
    return pl.pallas_call(
        functools.partial(_expert_kernel, n_tok_tiles=n_tok_tiles),
        grid_spec=grid_spec,
        out_shape=jax.ShapeDtypeStruct(xs.shape, xs.dtype),
        compiler_params=_cparams(1),
        name="experts",
    )(*tables, xs, w_up, b_up, w_down, b_down)


def _combine_kernel(ys_ref, ri_ref, rw_ref, cnt_ref, x_ref, g3_ref, gt_ref, xo_ref):
    tm = x_ref.shape[0]
    rif = ri_ref[...].astype(F32)
    rw = rw_ref[...]
    pos = [_lane_pick(rif, TOP_K + k) for k in range(TOP_K)]
    wgt = [_lane_pick(rw, k) for k in range(TOP_K)]
    used = jnp.sum(_seg_pad(cnt_ref[0][0:1, :]), axis=1, keepdims=True)
    f = jnp.zeros(x_ref.shape, F32)
    for c in range(ys_ref.shape[0] // tm):
        col = (lax.broadcasted_iota(jnp.int32, (tm, tm), 1) + c * tm).astype(F32)
        scat = jnp.zeros((tm, tm), F32)
        for k in range(TOP_K):
            scat = jnp.where(pos[k] == col, wgt[k], scat)
        row = (lax.broadcasted_iota(jnp.int32, (tm, 1), 0) + c * tm).astype(F32)
        ys = ys_ref[c * tm:(c + 1) * tm, :]
        ys = jnp.where(row < used, ys, jnp.zeros_like(ys))
        f = f + jnp.dot(scat.astype(MXU_DTYPE), ys, preferred_element_type=F32)
    xo_ref[...] = x_ref[...] + gt_ref[0] * _rms(f, g3_ref[0])


def _combine(ys, ri, rw, cnt, xa, g_norm, modt, tm, n_tiles):
    seg_rows = ys.shape[0] // n_tiles
    return pl.pallas_call(
        _combine_kernel,
        grid=(n_tiles,),
        in_specs=[pl.BlockSpec((seg_rows, D_MODEL), lambda i: (i, 0)),
                  pl.BlockSpec((tm, LANE), lambda i: (i, 0)),
                  pl.BlockSpec((tm, LANE), lambda i: (i, 0)),
                  pl.BlockSpec((1, 8, LANE), lambda i: (i, 0, 0)),
                  pl.BlockSpec((tm, D_MODEL), lambda i: (i, 0)),
                  pl.BlockSpec((1, 1, D_MODEL), lambda i: (3, 0, 0)),
                  pl.BlockSpec((1, 1, D_MODEL), lambda i: (i, 0, 5))],
        out_specs=pl.BlockSpec((tm, D_MODEL), lambda i: (i, 0)),
        out_shape=jax.ShapeDtypeStruct((n_tiles * tm, D_MODEL), F32),
        compiler_params=_cparams(1),
        name="combine",
    )(ys, ri, rw, cnt, xa, g_norm, modt)


def _swap16(idx):
    g = np.asarray(idx).reshape(-1, 2, ROPE_FREQS)
    return g[:, ::-1, :].reshape(-1)


def _in_proj_columns():
    sizes = (Q_LORA, KV_LORA, QK_ROPE, M_HEADS * M_QK, M_HEADS * M_QK, M_HEADS * M_V, M_HEADS * M_V,
             2 * 2 * M_HEADS, CONV_WIDTH, CONV_WIDTH, CONV_WIDTH, 3 * D_MODEL)
    starts = np.concatenate([[0], np.cumsum(sizes)[:-1]])
    seg = lambda k: np.arange(starts[k], starts[k] + sizes[k])
    cq, ckv, kr, mq, mk, mv, mo, mif, cu, cb, cc, cg = (seg(k) for k in range(12))
    mqk = np.concatenate([np.concatenate([mq[h * M_QK:(h + 1) * M_QK], mk[h * M_QK:(h + 1) * M_QK]])
                          for h in range(M_HEADS)])
    main = np.concatenate([cq, ckv, mqk, mv, mo, cu, cb, cc, cg])
    pad = lambda n: np.full((n,), -1)
    special = np.concatenate([kr, kr, _swap16(kr), _swap16(kr), mif, pad(LANE - mif.size), pad(LANE)])
    cols = np.concatenate([main, special])
    assert cols.size == (N_MAIN_TILES + 1) * COL_TILE
    return cols


def _take_cols(w, cols, dtype):
    cols = np.asarray(cols)
    breaks = np.flatnonzero((np.diff(cols) != 1) & ~((cols[1:] < 0) & (cols[:-1] < 0))) + 1
    parts = []
    for run in np.split(cols, breaks):
        if run[0] < 0:
            parts.append(jnp.zeros(w.shape[:-1] + (run.size,), dtype))
        else:
            parts.append(w[..., int(run[0]):int(run[-1]) + 1].astype(dtype))
    return jnp.concatenate(parts, axis=-1)


def _q_up_columns():
    per = QK_NOPE + QK_ROPE
    nope = np.concatenate([np.arange(h * per, h * per + QK_NOPE) for h in range(MLA_HEADS)])
    rope = np.concatenate([np.arange(h * per + QK_NOPE, (h + 1) * per) for h in range(MLA_HEADS)])
    return np.concatenate([nope, rope, _swap16(rope)])


def _kv_up_columns():
    per = QK_NOPE + V_HEAD
    kn = np.concatenate([np.arange(h * per, h * per + QK_NOPE) for h in range(MLA_HEADS)])
    vv = np.concatenate([np.arange(h * per + QK_NOPE, (h + 1) * per) for h in range(MLA_HEADS)])
    return np.concatenate([kn, vv])


def _rope_tables(T, rt):
    pos = jnp.arange(T)
    inv_freq = ROPE_THETA ** (-jnp.arange(ROPE_FREQS, dtype=F32) / ROPE_FREQS)
    ang_r = (pos // GRID_W).astype(F32)[:, None] * inv_freq
    ang_c = (pos % GRID_W).astype(F32)[:, None] * inv_freq
    cos64 = jnp.concatenate([jnp.cos(ang_r), jnp.cos(ang_r), jnp.cos(ang_c), jnp.cos(ang_c)], axis=1)
    sin64 = jnp.concatenate([-jnp.sin(ang_r), jnp.sin(ang_r), -jnp.sin(ang_c), jnp.sin(ang_c)], axis=1)
    cos_t = jnp.concatenate([jnp.tile(cos64, (1, 2)), jnp.ones((rt, LANE), F32)], axis=0)
    sin_t = jnp.concatenate([jnp.tile(sin64, (1, 2)), jnp.zeros((rt, LANE), F32)], axis=0)
    return cos_t, sin_t


def _moe_tables(cnt, n_tiles, tok_tile):
    tm = MOE_TILE
    seg_rows = TOP_K * tok_tile + N_EXPERTS * SEG_ALIGN
    n_work = n_tiles * seg_rows // tm + N_EXPERTS
    counts = cnt[:n_tiles, 0, :N_EXPERTS].astype(jnp.int32)
    padded = (counts + SEG_ALIGN - 1) // SEG_ALIGN * SEG_ALIGN
    in_tile = jnp.cumsum(padded, axis=1) - padded
    src_row = (jnp.arange(n_tiles, dtype=jnp.int32)[:, None] * seg_rows + in_tile).T
    seg_start = jnp.concatenate([jnp.zeros((N_EXPERTS, 1), jnp.int32), jnp.cumsum(padded.T, axis=1)], axis=1)
    group_len = seg_start[:, -1]
    work_end = jnp.cumsum((group_len + tm - 1) // tm)
    work = jnp.arange(n_work, dtype=jnp.int32)
    tile_expert = jnp.minimum(jnp.sum(work_end[None, :] <= work[:, None], axis=1), N_EXPERTS - 1)
    first_work = jnp.concatenate([jnp.zeros((1,), jnp.int32), work_end[:-1]])
    tile_row0 = (work - first_work[tile_expert]) * tm
    used = jnp.clip(group_len[tile_expert] - tile_row0, 0, tm)
    starts = seg_start[tile_expert]
    t_lo = jnp.sum(starts[:, 1:] <= tile_row0[:, None], axis=1)
    t_hi = jnp.sum(starts[:, :-1] < (tile_row0 + tm)[:, None], axis=1)
    i32 = lambda a: a.astype(jnp.int32)
    return (i32(tile_expert), i32(tile_row0), i32(work_end[-1:]), i32(seg_start.reshape(-1)),
            i32(src_row.reshape(-1)), i32(used), i32(t_lo), i32(t_hi))


def _layer(xa, mods, B, T, Tc, rt, tm, update_ctx, p, cos_t, sin_t, layer):
    rows = xa.shape[0]
    n_lat = B * T
    lat_rt, all_rt = n_lat // rt, rows // rt
    lat_tm, all_tm = n_lat // tm, rows // tm
    cond_rt = np.concatenate([np.arange(lat_rt) // (T // rt), np.full((all_rt - lat_rt,), B)])
    cond_tm = np.concatenate([np.arange(lat_tm) // (T // tm), np.full((all_tm - lat_tm,), B)])
    modt_rt = mods[cond_rt][:, None, :]
    modt_tm = mods[cond_tm][:, None, :]
    g_norm = p["g_norm"].reshape(4, 1, D_MODEL)

    z, zs = _inproj(xa, g_norm, modt_rt, p["w_in_main"], p["w_in_special"], layer, rt)
    q, k, v = _mla_project(z, zs, p["g_q_lora"], p["g_kv_lora"], p["w_q_up"], p["w_kv_up"],
                           cos_t, sin_t, rt, lat_rt, T // rt)
    a, w_up_c, w_down_c = _attention_latent(q, k, v, p["w_up"], p["w_down"], layer, p["depth"],
                                            B, T, Tc, min(512, T), min(512, T),
                                            rows if update_ctx else n_lat)
    w_up_c = w_up_c.reshape(N_EXPERTS, D_MODEL, 2 * D_FF)
    w_down_c = w_down_c.reshape(N_EXPERTS, D_FF, D_MODEL)
    if update_ctx:
        a = _attention_ctx(q, k, v, a, B, T, Tc)
    m = _mlstm(z, zs, p["b_if"], p["g_mlstm"], B, T, Tc)
    n_rt = all_rt if update_ctx else lat_rt
    n_tm = all_tm if update_ctx else lat_tm
    t = _merge(a, m, z, p["w_proj_a"], p["w_proj_b"], p["w_proj_c"], p["w_conv"], p["b_conv"],
               rt, n_rt, lat_rt, T // rt, T, Tc)
    xa, h2, ri, rw, cnt = _post(t, p["w_o"], xa, g_norm, modt_tm, p["w_router"], p["b_router"], tm, n_tm)
    xs = _sort_rows(h2, ri, tm, n_tm)
    ys = _experts(_moe_tables(cnt, n_tm, tm), xs, w_up_c, p["b_up"], w_down_c, p["b_down"], n_tm)
    return _combine(ys, ri, rw, cnt, xa, g_norm, modt_tm, tm, n_tm)


def kernel(x, c, ctx, c_ctx, w_mod, b_mod, g_norm, w_in, g_q_lora, g_kv_lora, w_q_up, w_kv_up, b_if, g_mlstm,
           w_conv, b_conv, w_proj_a, w_proj_b, w_proj_c, w_o, w_router, b_router, w_up, b_up, w_down, b_down):
    B, T, _ = x.shape
    Tc = ctx.shape[1]
    depth = w_mod.shape[0]
    rt = min(1024, T)
    tm = min(512, T)
    assert T % rt == 0 and (B * Tc) % rt == 0 and (B * Tc) % tm == 0
    assert T & (T - 1) == 0 and Tc & (Tc - 1) == 0 and T % GRID_W == 0

    xa = jnp.concatenate([x.reshape(B * T, D_MODEL), ctx.reshape(B * Tc, D_MODEL)], axis=0)
    cond = jnp.zeros((8, D_MODEL), F32).at[:B].set(c).at[B].set(c_ctx)
    cos_t, sin_t = _rope_tables(T, rt)
    cast = lambda w: w.astype(MXU_DTYPE)
    in_cols = _in_proj_columns()
    n_main = N_MAIN_TILES * COL_TILE
    w_in_main = _take_cols(w_in, in_cols[:n_main], MXU_DTYPE)
    w_in_special = _take_cols(w_in, in_cols[n_main:], MXU_DTYPE)
    w_q_p = _take_cols(w_q_up, _q_up_columns(), MXU_DTYPE)
    w_kv_p = _take_cols(w_kv_up, _kv_up_columns(), MXU_DTYPE)
    w_up_r = w_up.reshape(depth * N_EXPERTS * D_MODEL, 2 * D_FF)
    w_down_r = w_down.reshape(depth * N_EXPERTS * D_FF, D_MODEL)
    wa_c, wb_c, wc_c, wo_c = cast(w_proj_a), cast(w_proj_b), cast(w_proj_c), cast(w_o)
    w_router_p = jnp.pad(w_router, ((0, 0), (0, 0), (0, LANE - N_EXPERTS)))
    b_router_p = jnp.pad(b_router, ((0, 0), (0, LANE - N_EXPERTS)), constant_values=NEG_BIG)
    b_if_p = jnp.pad(b_if.reshape(depth, 1, -1), ((0, 0), (0, 0), (0, LANE - 4 * M_HEADS)))
    mods = _adaln(cond, w_mod, b_mod)
    for l in range(depth):
        p = dict(
            g_norm=g_norm[l], w_in_main=w_in_main, w_in_special=w_in_special,
            g_q_lora=g_q_lora[l], g_kv_lora=g_kv_lora[l],
            w_q_up=w_q_p[l], w_kv_up=w_kv_p[l], b_if=b_if_p[l],
            g_mlstm=g_mlstm[l], w_conv=w_conv[l], b_conv=b_conv[l],
            w_proj_a=wa_c[l], w_proj_b=wb_c[l], w_proj_c=wc_c[l], w_o=wo_c[l],
            w_router=w_router_p[l], b_router=b_router_p[l],
            depth=depth, w_up=w_up_r, b_up=b_up[l].reshape(N_EXPERTS, 1, -1),
            w_down=w_down_r, b_down=b_down[l].reshape(N_EXPERTS, 1, -1),
        )
        xa = _layer(xa, mods[l], B, T, Tc, rt, tm, l < depth - 1, p, cos_t, sin_t, l)
    return xa.reshape(B, T, D_MODEL)
```

```python
import functools

import numpy as np
import jax
import jax.numpy as jnp
from jax import lax
from jax.experimental import pallas as pl
from jax.experimental.pallas import tpu as pltpu

F32 = jnp.float32
MXU_DTYPE = jnp.bfloat16

D_MODEL = 2048
GRID_W = 64
MLA_HEADS = 8
QK_NOPE = 128
QK_ROPE = 64
V_HEAD = 128
Q_LORA = 512
KV_LORA = 512
ROPE_THETA = 10000.0
ROPE_FREQS = QK_ROPE // 4
ATTN_SCALE = (QK_NOPE + QK_ROPE) ** -0.5
M_HEADS = 4
M_QK = 64
M_V = 128
CONV_WIDTH = 512
N_EXPERTS = 32
TOP_K = 4
D_FF = 1024
SWIGLU_ALPHA = 1.702
SWIGLU_LIMIT = 7.0
N_MOD = 6
EPS = 1e-6

LANE = 128
MLSTM_CHUNK = 256
COL_TILE = 512
INPROJ_TILE = 1024
LOG2_E = 1.4426950408889634
ATTN_Q_TILE = 1024
ATTN_KV_CHUNK = 512
MOE_TILE = 512
SEG_ALIGN = 16
NEG_BIG = -1e30
VMEM_LIMIT = 56 * 1024 * 1024

ZT_CQ, ZT_CKV, ZT_MQK, ZT_MV, ZT_MO, ZT_CU, ZT_CB, ZT_CC, ZT_CG = 0, 1, 2, 3, 4, 5, 6, 7, 8
N_MAIN_TILES = 8 + 3 * D_MODEL // COL_TILE
assert (N_MAIN_TILES * COL_TILE) % INPROJ_TILE == 0


def _cparams(n_axes, vmem=VMEM_LIMIT):
    return pltpu.CompilerParams(dimension_semantics=("arbitrary",) * n_axes, vmem_limit_bytes=vmem)


def _rms(x, g):
    return x * lax.rsqrt(jnp.mean(x * x, axis=-1, keepdims=True) + EPS) * g


def _sigmoid(x):
    return 0.5 * jnp.tanh(0.5 * x) + 0.5


def _lane_pick(x, idx):
    lane = lax.broadcasted_iota(jnp.int32, x.shape, 1)
    return jnp.sum(jnp.where(lane == idx, x, 0.0), axis=1, keepdims=True)


def _mod_kernel(c_ref, w_ref, b_ref, o_ref):
    c = c_ref[...]
    s = c * jax.nn.sigmoid(c)
    o_ref[0] = jnp.dot(s.astype(MXU_DTYPE), w_ref[0].astype(MXU_DTYPE),
                       preferred_element_type=F32) + b_ref[0]


def _adaln(cond, w_mod, b_mod):
    depth, _, n = w_mod.shape
    tn = 1024
    return pl.pallas_call(
        _mod_kernel,
        grid=(depth, n // tn),
        in_specs=[pl.BlockSpec(cond.shape, lambda l, j: (0, 0)),
                  pl.BlockSpec((1, D_MODEL, tn), lambda l, j: (l, 0, j)),
                  pl.BlockSpec((1, 1, tn), lambda l, j: (l, 0, j))],
        out_specs=pl.BlockSpec((1, cond.shape[0], tn), lambda l, j: (l, 0, j)),
        out_shape=jax.ShapeDtypeStruct((depth, cond.shape[0], n), F32),
        compiler_params=_cparams(2),
        name="adaln",
    )(cond, w_mod, b_mod.reshape(depth, 1, n))


def _inproj_kernel(x_ref, g_ref, sh_ref, sc_ref, w_ref, ws_ref, z_ref, zs_ref, h_scr):
    @pl.when(pl.program_id(1) == 0)
    def _():
        h = _rms(x_ref[...], g_ref[0]) * (1.0 + sc_ref[0]) + sh_ref[0]
        h_scr[...] = h.astype(h_scr.dtype)
        zs_ref[...] = jnp.dot(h_scr[...], ws_ref[0], preferred_element_type=F32)

    z_ref[...] = jnp.dot(h_scr[...], w_ref[0], preferred_element_type=F32).astype(z_ref.dtype)


def _inproj(xa, g_norm, modt, w_main, w_special, layer, rt):
    rows = xa.shape[0]
    n_main = w_main.shape[2]
    tn = INPROJ_TILE
    return pl.pallas_call(
        _inproj_kernel,
        grid=(rows // rt, n_main // tn),
        in_specs=[pl.BlockSpec((rt, D_MODEL), lambda i, j: (i, 0)),
                  pl.BlockSpec((1, 1, D_MODEL), lambda i, j: (0, 0, 0)),
                  pl.BlockSpec((1, 1, D_MODEL), lambda i, j: (i, 0, 0)),
                  pl.BlockSpec((1, 1, D_MODEL), lambda i, j: (i, 0, 1)),
                  pl.BlockSpec((1, D_MODEL, tn), lambda i, j: (layer, 0, j)),
                  pl.BlockSpec((1, D_MODEL, COL_TILE), lambda i, j: (layer, 0, 0))],
        out_specs=[pl.BlockSpec((rt, tn), lambda i, j: (i, j)),
                   pl.BlockSpec((rt, COL_TILE), lambda i, j: (i, 0))],
        out_shape=[jax.ShapeDtypeStruct((rows, n_main), MXU_DTYPE),
                   jax.ShapeDtypeStruct((rows, COL_TILE), F32)],
        scratch_shapes=[pltpu.VMEM((rt, D_MODEL), MXU_DTYPE)],
        compiler_params=_cparams(2),
        name="inproj",
    )(xa, g_norm, modt, modt, w_main, w_special)


def _mla_kernel(cq_ref, ckv_ref, zs_ref, gq_ref, gkv_ref, wq_ref, wkv_ref, cos_ref, sin_ref,
                q_ref, k_ref, v_ref):
    def norm(z_ref, g_ref):
        return _rms(z_ref[...].astype(F32), g_ref[...]).astype(MXU_DTYPE)

    qf = jnp.dot(norm(cq_ref, gq_ref), wq_ref[...], preferred_element_type=F32) * (ATTN_SCALE * LOG2_E)
    kvf = jnp.dot(norm(ckv_ref, gkv_ref), wkv_ref[...], preferred_element_type=F32)
    cos = cos_ref[...]
    sin = sin_ref[...]
    nope_w = MLA_HEADS * QK_NOPE
    rope_w = MLA_HEADS * QK_ROPE
    for p in range(MLA_HEADS // 2):
        a = nope_w + LANE * p
        b = nope_w + rope_w + LANE * p
        rp = (qf[:, a:a + LANE] * cos + qf[:, b:b + LANE] * sin).astype(MXU_DTYPE)
        for h in (2 * p, 2 * p + 1):
            q_ref[:, 256 * h:256 * h + LANE] = qf[:, LANE * h:LANE * (h + 1)].astype(MXU_DTYPE)
            q_ref[:, 256 * h + LANE:256 * (h + 1)] = rp
    kr2 = zs_ref[:, 0:LANE] * cos + zs_ref[:, LANE:2 * LANE] * sin
    lane = lax.broadcasted_iota(jnp.int32, kr2.shape, 1)
    k_lo = jnp.where(lane < QK_ROPE, kr2, 0.0).astype(MXU_DTYPE)
    k_hi = jnp.where(lane >= QK_ROPE, kr2, 0.0).astype(MXU_DTYPE)
    for h in range(MLA_HEADS):
        k_ref[:, 256 * h:256 * h + LANE] = kvf[:, LANE * h:LANE * (h + 1)].astype(MXU_DTYPE)
        k_ref[:, 256 * h + LANE:256 * (h + 1)] = k_lo if h % 2 == 0 else k_hi
    ones = jnp.ones((kvf.shape[0], LANE), MXU_DTYPE)
    v0 = MLA_HEADS * QK_NOPE
    for h in range(MLA_HEADS):
        v_ref[:, 256 * h:256 * h + LANE] = kvf[:, v0 + LANE * h:v0 + LANE * (h + 1)].astype(MXU_DTYPE)
        v_ref[:, 256 * h + LANE:256 * (h + 1)] = ones


def _mla_project(z, zs, g_q, g_kv, wq_p, wkv_p, cos_t, sin_t, rt, n_lat_tiles, tiles_per_seq):
    rows = z.shape[0]
    nt = rows // rt

    def tab(i):
        return (jnp.where(i < n_lat_tiles, i % tiles_per_seq, tiles_per_seq), 0)

    return pl.pallas_call(
        _mla_kernel,
        grid=(nt,),
        in_specs=[pl.BlockSpec((rt, COL_TILE), lambda i: (i, ZT_CQ)),
                  pl.BlockSpec((rt, COL_TILE), lambda i: (i, ZT_CKV)),
                  pl.BlockSpec((rt, COL_TILE), lambda i: (i, 0)),
                  pl.BlockSpec((1, Q_LORA), lambda i: (0, 0)),
                  pl.BlockSpec((1, KV_LORA), lambda i: (0, 0)),
                  pl.BlockSpec(wq_p.shape, lambda i: (0, 0)),
                  pl.BlockSpec(wkv_p.shape, lambda i: (0, 0)),
                  pl.BlockSpec((rt, LANE), tab),
                  pl.BlockSpec((rt, LANE), tab)],
        out_specs=[pl.BlockSpec((rt, 256 * MLA_HEADS), lambda i: (i, 0)),
                   pl.BlockSpec((rt, 256 * MLA_HEADS), lambda i: (i, 0)),
                   pl.BlockSpec((rt, 256 * MLA_HEADS), lambda i: (i, 0))],
        out_shape=[jax.ShapeDtypeStruct((rows, 256 * MLA_HEADS), MXU_DTYPE)] * 3,
        compiler_params=_cparams(1),
        name="mla_project",
    )(z, z, zs, g_q.reshape(1, -1), g_kv.reshape(1, -1), wq_p, wkv_p, cos_t, sin_t)


def _attn_kernel(*refs, kv_chunks, n_cast=0):
    q_ref = refs[0]
    o_ref = refs[len(refs) - 1 - n_cast]
    for c in range(n_cast):
        src = refs[1 + 2 * len(kv_chunks) + c]
        dst = refs[len(refs) - n_cast + c]
        dst[...] = src[...].astype(dst.dtype)
    q = q_ref[...]
    tq = q.shape[0]
    m = jnp.full((tq, 1), -jnp.inf, F32)
    acc = jnp.zeros((tq, 2 * V_HEAD), F32)
    for s, n_chunks in enumerate(kv_chunks):
        k_ref, v_ref = refs[1 + 2 * s], refs[2 + 2 * s]
        ck = k_ref.shape[0] // n_chunks
        for c in range(n_chunks):
            k = k_ref[c * ck:(c + 1) * ck, :]
            v = v_ref[c * ck:(c + 1) * ck, :]
            sc = lax.dot_general(q, k, (((1,), (1,)), ((), ())), preferred_element_type=F32)
            m_new = jnp.maximum(m, jnp.max(sc, axis=-1, keepdims=True))
            p = jnp.exp2(sc - m_new).astype(MXU_DTYPE)
            acc = jnp.exp2(m - m_new) * acc + jnp.dot(p, v, preferred_element_type=F32)
            m = m_new
    o_ref[...] = (acc[:, :V_HEAD] / acc[:, V_HEAD:]).astype(o_ref.dtype)


def _attention_latent(q, k, v, w_up, w_down, layer, depth, B, T, Tc, tq, ck, rows):
    qt = T // tq
    ctx_blk0 = B * T // Tc
    steps = B * MLA_HEADS * qt
    step = lambda b, h, i: (b * MLA_HEADS + h) * qt + i
    cast_specs_in, cast_specs_out, cast_shapes = [], [], []
    for w in (w_up, w_down):
        n = w.shape[1]
        layer_rows = w.shape[0] // depth
        slab = layer_rows // steps
        assert slab * steps == layer_rows and slab % 16 == 0
        cast_specs_in.append(pl.BlockSpec((slab, n), lambda b, h, i: (layer * steps + step(b, h, i), 0)))
        cast_specs_out.append(pl.BlockSpec((slab, n), lambda b, h, i: (step(b, h, i), 0)))
        cast_shapes.append(jax.ShapeDtypeStruct((layer_rows, n), MXU_DTYPE))
    return pl.pallas_call(
        functools.partial(_attn_kernel, kv_chunks=(T // ck, 1), n_cast=2),
        grid=(B, MLA_HEADS, qt),
        in_specs=[pl.BlockSpec((tq, 256), lambda b, h, i: (b * qt + i, h)),
                  pl.BlockSpec((T, 256), lambda b, h, i: (b, h)),
                  pl.BlockSpec((T, 256), lambda b, h, i: (b, h)),
                  pl.BlockSpec((Tc, 256), lambda b, h, i: (ctx_blk0 + b, h)),
                  pl.BlockSpec((Tc, 256), lambda b, h, i: (ctx_blk0 + b, h))] + cast_specs_in,
        out_specs=[pl.BlockSpec((tq, V_HEAD), lambda b, h, i: (b * qt + i, h))] + cast_specs_out,
        out_shape=[jax.ShapeDtypeStruct((rows, V_HEAD * MLA_HEADS), MXU_DTYPE)] + cast_shapes,
        compiler_params=_cparams(3),
        name="attn_latent",
    )(q, k, v, k, v, w_up, w_down)


def _attention_ctx(q, k, v, a_buf, B, T, Tc):
    ctx_blk0 = B * T // Tc
    return pl.pallas_call(
        functools.partial(_attn_kernel, kv_chunks=(1,)),
        grid=(B, MLA_HEADS),
        in_specs=[pl.BlockSpec((Tc, 256), lambda b, h: (ctx_blk0 + b, h)),
                  pl.BlockSpec((Tc, 256), lambda b, h: (ctx_blk0 + b, h)),
                  pl.BlockSpec((Tc, 256), lambda b, h: (ctx_blk0 + b, h)),
                  pl.BlockSpec(memory_space=pl.ANY)],
        out_specs=pl.BlockSpec((Tc, V_HEAD), lambda b, h: (ctx_blk0 + b, h)),
        out_shape=jax.ShapeDtypeStruct(a_buf.shape, a_buf.dtype),
        input_output_aliases={3: 0},
        compiler_params=_cparams(2),
        name="attn_ctx",
    )(q, k, v, a_buf)


def _mlstm_chunk(qk, v, g, lf_cum, gt_ref, head, reverse, ct_ref, m_prev, consts):
    tri, row_iota, mk, ones_tile, lane = consts
    L = qk.shape[0]
    base = 8 if reverse else 0
    col_i, col_f = base + head, base + 4 + head
    ql = jnp.where(lane < M_QK, qk, jnp.zeros_like(qk))
    kl = jnp.dot(qk, mk, preferred_element_type=F32).astype(MXU_DTYPE)
    v_aug = jnp.concatenate([v, ones_tile], axis=1)
    bc = _lane_pick(lf_cum, col_f)
    li = _lane_pick(g, col_i)
    r_row = gt_ref[0, col_i:col_i + 1, :] - gt_ref[1, col_f:col_f + 1, :]
    dlog = jnp.where(tri, bc + r_row, -jnp.inf)
    inter = bc + m_prev
    m_t = jnp.maximum(inter, jnp.max(dlog, axis=1, keepdims=True))
    dexp = jnp.exp(dlog - m_t)
    s_raw = lax.dot_general(ql, kl, (((1,), (1,)), ((), ())), preferred_element_type=F32)
    s_mat = (s_raw * (M_QK ** -0.5) * dexp).astype(MXU_DTYPE)
    a = jnp.exp(inter - m_t) * (M_QK ** -0.5)
    ct = ct_ref[...]
    nd = (jnp.dot(s_mat, v_aug, preferred_element_type=F32)
          + a * jnp.dot(ql, ct.astype(MXU_DTYPE), preferred_element_type=F32))
    num, den = nd[:, :M_V], nd[:, M_V:]
    h_out = num / jnp.maximum(jnp.abs(den), jnp.exp(-m_t))
    last = 0 if reverse else L - 1
    total = jnp.sum(jnp.where(row_iota == last, bc, 0.0), axis=0, keepdims=True)
    w_col = total - bc + li
    m_new = jnp.maximum(total + m_prev, jnp.max(w_col, axis=0, keepdims=True))
    decay = jnp.exp(total + m_prev - m_new)
    e = jnp.exp(w_col - m_new)
    ek = (kl.astype(F32) * e).astype(MXU_DTYPE)
    upd = lax.dot_general(ek, v_aug, (((0,), (0,)), ((), ())), preferred_element_type=F32)
    ct_ref[...] = decay * ct + upd
    return h_out, m_new


def _mlstm_kernel(zl_qk, zl_v, zl_mo, gl_ref, zc_qk, zc_v, zc_mo, gc_ref, bif_ref, gm_ref,
                  ol_ref, oc_ref, acc_l, acc_c, ct_scr, gt_scr):
    L = MLSTM_CHUNK
    T = zl_qk.shape[0]
    nc = T // L
    r = lax.broadcasted_iota(jnp.int32, (L, L), 0)
    c = lax.broadcasted_iota(jnp.int32, (L, L), 1)
    tri_f = c <= r
    tri_b = c >= r
    tmat_f = tri_f.astype(F32)
    tmat_b = tri_b.astype(F32)
    row_iota = lax.broadcasted_iota(jnp.int32, (L, 1), 0)
    lane = lax.broadcasted_iota(jnp.int32, (L, LANE), 1)
    rr = lax.broadcasted_iota(jnp.int32, (LANE, LANE), 0)
    cc = lax.broadcasted_iota(jnp.int32, (LANE, LANE), 1)
    mk = (rr == cc + M_QK).astype(MXU_DTYPE)
    ones_tile = jnp.ones((L, LANE), MXU_DTYPE)
    consts_f = (tri_f, row_iota, mk, ones_tile, lane)
    consts_b = (tri_b, row_iota, mk, ones_tile, lane)
    bias = bif_ref[...]

    def gate_prep(g_raw, tmat, slot):
        g = g_raw + bias
        log_sig = jnp.minimum(g, 0.0) - jnp.log1p(jnp.exp(-jnp.abs(g)))
        cum = jnp.dot(tmat, log_sig, precision=lax.Precision.HIGHEST, preferred_element_type=F32)
        gt_scr[slot, 0] = g.T
        gt_scr[slot, 1] = cum.T
        return g, cum

    def step(qk_ref, v_ref, g_ref, acc_ref, row_f, row_b, ms):
        g_f, cum_f = gate_prep(g_ref[pl.ds(row_f, L), :], tmat_f, 0)
        g_b, cum_b = gate_prep(g_ref[pl.ds(row_b, L), :], tmat_b, 1)
        new_ms = []
        for h in range(M_HEADS):
            cols = slice(h * LANE, (h + 1) * LANE)
            hf, mf = _mlstm_chunk(qk_ref[pl.ds(row_f, L), cols], v_ref[pl.ds(row_f, L), cols],
                                  g_f, cum_f, gt_scr.at[0], h, False,
                                  ct_scr.at[h], ms[h], consts_f)
            acc_ref[pl.ds(row_f, L), cols] += hf
            hb, mb = _mlstm_chunk(qk_ref[pl.ds(row_b, L), cols], v_ref[pl.ds(row_b, L), cols],
                                  g_b, cum_b, gt_scr.at[1], h, True,
                                  ct_scr.at[M_HEADS + h], ms[M_HEADS + h], consts_b)
            acc_ref[pl.ds(row_b, L), cols] += hb
            new_ms.append((mf, mb))
        return tuple(x[0] for x in new_ms) + tuple(x[1] for x in new_ms)

    ct_scr[...] = jnp.zeros_like(ct_scr)
    acc_l[...] = jnp.zeros_like(acc_l)
    acc_c[...] = jnp.zeros_like(acc_c)
    zero = jnp.zeros((1, 1), F32)
    ms = step(zc_qk, zc_v, gc_ref, acc_c, 0, 0, (zero,) * (2 * M_HEADS))

    def body(i, ms):
        row_f = pl.multiple_of(i * L, L)
        row_b = pl.multiple_of((nc - 1 - i) * L, L)
        return step(zl_qk, zl_v, gl_ref, acc_l, row_f, row_b, ms)

    lax.fori_loop(0, nc, body, ms)

    gm = gm_ref[...]

    def finish(acc_ref, mo_ref, o_ref, row):
        for h in range(M_HEADS):
            cols = slice(h * LANE, (h + 1) * LANE)
            hn = _rms(acc_ref[pl.ds(row, L), cols], gm[:, cols])
            gate = jax.nn.sigmoid(mo_ref[pl.ds(row, L), cols].astype(F32))
            o_ref[pl.ds(row, L), cols] = (hn * gate).astype(o_ref.dtype)

    finish(acc_c, zc_mo, oc_ref, 0)

    def fin_body(i, carry):
        finish(acc_l, zl_mo, ol_ref, pl.multiple_of(i * L, L))
        return carry

    lax.fori_loop(0, nc, fin_body, 0)


def _mlstm(z, zs, bif_row, g_mlstm, B, T, Tc):
    assert Tc == MLSTM_CHUNK and T % MLSTM_CHUNK == 0
    rows = z.shape[0]
    ctx_blk0 = B * T // Tc
    w = M_HEADS * LANE
    out_l, out_c = pl.pallas_call(
        _mlstm_kernel,
        grid=(B,),
        in_specs=[pl.BlockSpec((T, w), lambda b: (b, ZT_MQK)),
                  pl.BlockSpec((T, w), lambda b: (b, ZT_MV)),
                  pl.BlockSpec((T, w), lambda b: (b, ZT_MO)),
                  pl.BlockSpec((T, LANE), lambda b: (b, 2)),
                  pl.BlockSpec((Tc, w), lambda b: (ctx_blk0 + b, ZT_MQK)),
                  pl.BlockSpec((Tc, w), lambda b: (ctx_blk0 + b, ZT_MV)),
                  pl.BlockSpec((Tc, w), lambda b: (ctx_blk0 + b, ZT_MO)),
                  pl.BlockSpec((Tc, LANE), lambda b: (ctx_blk0 + b, 2)),
                  pl.BlockSpec((1, LANE), lambda b: (0, 0)),
                  pl.BlockSpec((1, w), lambda b: (0, 0))],
        out_specs=[pl.BlockSpec((T, w), lambda b: (b, 0)),
                   pl.BlockSpec((Tc, w), lambda b: (b, 0))],
        out_shape=[jax.ShapeDtypeStruct((B * T, w), MXU_DTYPE),
                   jax.ShapeDtypeStruct((B * Tc, w), MXU_DTYPE)],
        scratch_shapes=[pltpu.VMEM((T, w), F32), pltpu.VMEM((Tc, w), F32),
                        pltpu.VMEM((2 * M_HEADS, LANE, 2 * LANE), F32),
                        pltpu.VMEM((2, 2, LANE, MLSTM_CHUNK), F32)],
        compiler_params=_cparams(1),
        name="mlstm",
    )(z, z, z, zs, z, z, z, zs, bif_row, g_mlstm.reshape(1, -1))
    return jnp.concatenate([out_l, out_c], axis=0)


def _merge_kernel(a_ref, m_ref, cu_ref, cb_ref, cc_ref, cu_p, cc_p, cu_n, cc_n, ga_ref, gm_ref, gs_ref,
                  wa_ref, wb_ref, wc_ref, wconv_ref, bconv_ref, t_ref, s_scr,
                  *, n_lat_tiles, tiles_per_seq, seq_len, ctx_len):
    i = pl.program_id(0)
    j = pl.program_id(1)
    rt = a_ref.shape[0]

    @pl.when(j == 0)
    def _():
        u = cc_ref[...].astype(F32) * cu_ref[...].astype(F32)
        hp = cc_p[...].astype(F32) * cu_p[...].astype(F32)
        hn = cc_n[...].astype(F32) * cu_n[...].astype(F32)
        row = lax.broadcasted_iota(jnp.int32, (rt, 1), 0)
        is_ctx = i >= n_lat_tiles
        seg = jnp.where(is_ctx, ctx_len, seq_len)
        off = jnp.where(is_ctx, (i - n_lat_tiles) * rt, (i % tiles_per_seq) * rt)
        pos = (off + row) & (seg - 1)
        up = pltpu.roll(u, 1, 0)
        up = jnp.where(row == 0, hp[hp.shape[0] - 1:, :], up)
        up = jnp.where(pos == 0, 0.0, up)
        un = pltpu.roll(u, rt - 1, 0)
        un = jnp.where(row == rt - 1, hn[0:1, :], un)
        un = jnp.where(pos == seg - 1, 0.0, un)
        wconv = wconv_ref[...]
        y = up * wconv[0:1, :] + u * wconv[1:2, :] + un * wconv[2:3, :] + bconv_ref[...]
        s_scr[...] = (cb_ref[...].astype(F32) * y).astype(s_scr.dtype)

    def sig(ref):
        return _sigmoid(ref[...].astype(F32))

    t = (sig(ga_ref) * jnp.dot(a_ref[...], wa_ref[...], preferred_element_type=F32)
         + sig(gm_ref) * jnp.dot(m_ref[...], wb_ref[...], preferred_element_type=F32)
         + sig(gs_ref) * jnp.dot(s_scr[...], wc_ref[...], preferred_element_type=F32))
    t_ref[...] = t.astype(t_ref.dtype)


def _merge(a, m, z, wa, wb, wc, w_conv, b_conv, rt, n_tiles, n_lat_tiles, tiles_per_seq, T, Tc):
    rows = n_tiles * rt
    hb = 16
    rb = rt // hb
    last_hb = z.shape[0] // hb - 1
    tn = COL_TILE
    nj = D_MODEL // tn
    gt = D_MODEL // tn

    def prev_blk(col):
        return lambda i, j: (jnp.maximum(i * rb - 1, 0), col)

    def next_blk(col):
        return lambda i, j: (jnp.minimum((i + 1) * rb, last_hb), col)

    kern = functools.partial(_merge_kernel, n_lat_tiles=n_lat_tiles, tiles_per_seq=tiles_per_seq,
                             seq_len=T, ctx_len=Tc)
    return pl.pallas_call(
        kern,
        grid=(n_tiles, nj),
        in_specs=[pl.BlockSpec((rt, a.shape[1]), lambda i, j: (i, 0)),
                  pl.BlockSpec((rt, m.shape[1]), lambda i, j: (i, 0)),
                  pl.BlockSpec((rt, COL_TILE), lambda i, j: (i, ZT_CU)),
                  pl.BlockSpec((rt, COL_TILE), lambda i, j: (i, ZT_CB)),
                  pl.BlockSpec((rt, COL_TILE), lambda i, j: (i, ZT_CC)),
                  pl.BlockSpec((hb, COL_TILE), prev_blk(ZT_CU)),
                  pl.BlockSpec((hb, COL_TILE), prev_blk(ZT_CC)),
                  pl.BlockSpec((hb, COL_TILE), next_blk(ZT_CU)),
                  pl.BlockSpec((hb, COL_TILE), next_blk(ZT_CC)),
                  pl.BlockSpec((rt, tn), lambda i, j: (i, ZT_CG + j)),
                  pl.BlockSpec((rt, tn), lambda i, j: (i, ZT_CG + gt + j)),
                  pl.BlockSpec((rt, tn), lambda i, j: (i, ZT_CG + 2 * gt + j)),
                  pl.BlockSpec((wa.shape[0], tn), lambda i, j: (0, j)),
                  pl.BlockSpec((wb.shape[0], tn), lambda i, j: (0, j)),
                  pl.BlockSpec((wc.shape[0], tn), lambda i, j: (0, j)),
                  pl.BlockSpec(w_conv.shape, lambda i, j: (0, 0)),
                  pl.BlockSpec((1, CONV_WIDTH), lambda i, j: (0, 0))],
        out_specs=pl.BlockSpec((rt, tn), lambda i, j: (i, j)),
        out_shape=jax.ShapeDtypeStruct((rows, D_MODEL), MXU_DTYPE),
        scratch_shapes=[pltpu.VMEM((rt, CONV_WIDTH), MXU_DTYPE)],
        compiler_params=_cparams(2),
        name="merge",
    )(a, m, z, z, z, z, z, z, z, z, z, z, wa, wb, wc, w_conv, b_conv.reshape(1, -1))


def _dot3(x, w):
    xh = x.astype(jnp.bfloat16)
    xl = (x - xh.astype(F32)).astype(jnp.bfloat16)
    wh = w.astype(jnp.bfloat16)
    wl = (w - wh.astype(F32)).astype(jnp.bfloat16)
    n = w.shape[1]
    hi = jnp.dot(xh, jnp.concatenate([wh, wl], axis=1), preferred_element_type=F32)
    return hi[:, :n] + hi[:, n:] + jnp.dot(xl, wh, preferred_element_type=F32)


def _seg_pad(counts):
    return jnp.ceil(counts * (1.0 / SEG_ALIGN)) * SEG_ALIGN


def _exclusive_lane_cumsum(row):
    upper = (lax.broadcasted_iota(jnp.int32, (LANE, LANE), 0)
             < lax.broadcasted_iota(jnp.int32, (LANE, LANE), 1)).astype(jnp.bfloat16)
    out = jnp.dot(jnp.broadcast_to(row, (8, LANE)).astype(jnp.bfloat16), upper, preferred_element_type=F32)
    return out[0:1, :]


def _post_kernel(t_ref, wo_ref, x_ref, g1_ref, g2_ref, gt_ref, sh_ref, sc_ref, wr_ref, br_ref,
                 xo_ref, h2_ref, ri_ref, rw_ref, cnt_ref):
    y = jnp.dot(t_ref[...], wo_ref[...], preferred_element_type=F32)
    xn = x_ref[...] + gt_ref[0] * _rms(y, g1_ref[0])
    xo_ref[...] = xn
    h2 = _rms(xn, g2_ref[0]) * (1.0 + sc_ref[0]) + sh_ref[0]
    h2_ref[...] = h2.astype(h2_ref.dtype)
    logits = _dot3(h2, wr_ref[...]) + br_ref[...]
    lane = lax.broadcasted_iota(jnp.int32, logits.shape, 1)
    lane_f = lane.astype(F32)
    vals, ids = [], []
    for _ in range(TOP_K):
        mx = jnp.max(logits, axis=1, keepdims=True)
        idx = jnp.min(jnp.where(logits == mx, lane_f, float(LANE)), axis=1, keepdims=True)
        logits = jnp.where(lane_f == idx, -jnp.inf, logits)
        vals.append(mx)
        ids.append(idx)
    es = [jnp.exp(v - vals[0]) for v in vals]
    tot = es[0] + es[1] + es[2] + es[3]
    tm = logits.shape[0]
    onehots = [lane_f == ids[k] for k in range(TOP_K)]
    asg = jnp.zeros(logits.shape, F32)
    for k in range(TOP_K):
        asg = jnp.where(onehots[k], 1.0, asg)
    earlier = (lax.broadcasted_iota(jnp.int32, (tm, tm), 1)
               < lax.broadcasted_iota(jnp.int32, (tm, tm), 0)).astype(jnp.bfloat16)
    before = jnp.dot(earlier, asg.astype(jnp.bfloat16), preferred_element_type=F32)
    counts = jnp.sum(asg, axis=0, keepdims=True)
    seg_off = _exclusive_lane_cumsum(_seg_pad(counts))
    rw = jnp.zeros(logits.shape, F32)
    ri = jnp.zeros(logits.shape, F32)
    for k in range(TOP_K):
        pos = jnp.sum(jnp.where(onehots[k], before + seg_off, 0.0), axis=1, keepdims=True)
        rw = jnp.where(lane == k, es[k] / tot, rw)
        ri = jnp.where(lane == k, ids[k], ri)
        ri = jnp.where(lane == TOP_K + k, pos, ri)
    rw_ref[...] = rw
    ri_ref[...] = ri.astype(jnp.int32)
    cnt_ref[0] = jnp.broadcast_to(counts, cnt_ref.shape[1:])


def _post(t, wo, xa, g_norm, modt, wr_p, br_p, tm, n_tiles):
    rows = n_tiles * tm
    row_spec = lambda w: pl.BlockSpec((tm, w), lambda i: (i, 0))
    mod = lambda k: pl.BlockSpec((1, 1, D_MODEL), lambda i: (i, 0, k))
    gn = lambda k: pl.BlockSpec((1, 1, D_MODEL), lambda i: (k, 0, 0))
    return pl.pallas_call(
        _post_kernel,
        grid=(n_tiles,),
        in_specs=[row_spec(D_MODEL), pl.BlockSpec(wo.shape, lambda i: (0, 0)), row_spec(D_MODEL),
                  gn(1), gn(2), mod(2), mod(3), mod(4),
                  pl.BlockSpec(wr_p.shape, lambda i: (0, 0)), pl.BlockSpec((1, LANE), lambda i: (0, 0))],
        out_specs=[row_spec(D_MODEL), row_spec(D_MODEL), row_spec(LANE), row_spec(LANE),
                   pl.BlockSpec((1, 8, LANE), lambda i: (i, 0, 0))],
        out_shape=[jax.ShapeDtypeStruct((rows, D_MODEL), F32),
                   jax.ShapeDtypeStruct((rows, D_MODEL), MXU_DTYPE),
                   jax.ShapeDtypeStruct((rows, LANE), jnp.int32),
                   jax.ShapeDtypeStruct((rows, LANE), F32),
                   jax.ShapeDtypeStruct((rows // tm, 8, LANE), F32)],
        compiler_params=_cparams(1),
        name="post_mix",
    )(t, wo, xa, g_norm, g_norm, modt, modt, modt, wr_p, br_p.reshape(1, LANE))


def _sort_kernel(h2_ref, ri_ref, xs_ref):
    tm = h2_ref.shape[0]
    rif = ri_ref[...].astype(F32)
    pos = [_lane_pick(rif, TOP_K + k) for k in range(TOP_K)]
    h2 = h2_ref[...]
    for c in range(xs_ref.shape[0] // tm):
        col = (lax.broadcasted_iota(jnp.int32, (tm, tm), 1) + c * tm).astype(F32)
        onehot = jnp.zeros((tm, tm), F32)
        for k in range(TOP_K):
            onehot = jnp.where(pos[k] == col, 1.0, onehot)
        rows = lax.dot_general(onehot.astype(MXU_DTYPE), h2, (((0,), (0,)), ((), ())),
                               preferred_element_type=F32)
        xs_ref[c * tm:(c + 1) * tm, :] = rows.astype(xs_ref.dtype)


def _sort_rows(h2, ri, tm, n_tiles):
    seg_rows = TOP_K * tm + N_EXPERTS * SEG_ALIGN
    assert seg_rows % tm == 0
    return pl.pallas_call(
        _sort_kernel,
        grid=(n_tiles,),
        in_specs=[pl.BlockSpec((tm, D_MODEL), lambda i: (i, 0)),
                  pl.BlockSpec((tm, LANE), lambda i: (i, 0))],
        out_specs=pl.BlockSpec((seg_rows, D_MODEL), lambda i: (i, 0)),
        out_shape=jax.ShapeDtypeStruct((n_tiles * seg_rows, D_MODEL), MXU_DTYPE),
        compiler_params=_cparams(1),
        name="sort_rows",
    )(h2, ri)


def _expert_kernel(te_ref, r0_ref, na_ref, seg_ref, src_ref, used_ref, tlo_ref, thi_ref,
                   xs_hbm, wu_ref, bu_ref, wd_ref, bd_ref, ys_hbm,
                   xbuf, ybuf, sem_in, sem_out, *, n_tok_tiles):
    w = pl.program_id(0)
    n_active = na_ref[0]
    tm = ybuf.shape[0]
    nt = n_tok_tiles
    sizes = [s for s in (512, 256, 128, 64, 32, 16) if s <= tm]

    def for_pieces(length, fn):
        off = jnp.int32(0)
        for size in sizes:
            hit = (length & size) != 0

            @pl.when(hit)
            def _(off=off, size=size):
                fn(off, size)

            off = off + jnp.where(hit, size, 0)

    def for_segments(wt, fn):
        e = te_ref[wt]
        r0 = r0_ref[wt]

        def body(t, carry):
            s0 = seg_ref[e * (nt + 1) + t]
            s1 = seg_ref[e * (nt + 1) + t + 1]
            lo = jnp.maximum(s0, r0)
            length = jnp.maximum(jnp.minimum(s1, r0 + tm) - lo, 0)
            src = src_ref[e * nt + t] + (lo - s0)
            dst = lo - r0
            for_pieces(length, lambda off, size: fn(src + off, dst + off, size))
            return carry

        lax.fori_loop(tlo_ref[wt], thi_ref[wt], body, 0)

    def rows_in_use(wt):
        return used_ref[wt]

    def aligned(row):
        return row if isinstance(row, int) else pl.multiple_of(row, SEG_ALIGN)

    def in_copy(hbm_row, buf_row, size, slot):
        return pltpu.make_async_copy(xs_hbm.at[pl.ds(aligned(hbm_row), size)],
                                     xbuf.at[slot, pl.ds(aligned(buf_row), size)], sem_in.at[slot])

    def out_copy(hbm_row, buf_row, size):
        return pltpu.make_async_copy(ybuf.at[pl.ds(aligned(buf_row), size)],
                                     ys_hbm.at[pl.ds(aligned(hbm_row), size)], sem_out.at[0])

    def start_in(wt, slot):
        for_segments(wt, lambda h, b, size: in_copy(h, b, size, slot).start())

    def wait_in(wt, slot):
        for_pieces(rows_in_use(wt), lambda off, size: in_copy(0, 0, size, slot).wait())

    def wait_out(wt):
        for_pieces(rows_in_use(wt), lambda off, size: out_copy(0, 0, size).wait())

    @pl.when(w < n_active)
    def _():
        slot = w % 2

        @pl.when(w == 0)
        def _():
            xbuf[...] = jnp.zeros_like(xbuf)
            start_in(0, 0)

        wait_in(w, slot)

        @pl.when(w + 1 < n_active)
        def _():
            start_in(w + 1, 1 - slot)

        gu = jnp.dot(xbuf[slot], wu_ref[0], preferred_element_type=F32) + bu_ref[0]
        g = jnp.minimum(gu[:, :D_FF], SWIGLU_LIMIT)
        u = jnp.clip(gu[:, D_FF:], -SWIGLU_LIMIT, SWIGLU_LIMIT)
        hid = g * _sigmoid(SWIGLU_ALPHA * g) * (u + 1.0)
        y = jnp.dot(hid.astype(MXU_DTYPE), wd_ref[0], preferred_element_type=F32) + bd_ref[0]

        @pl.when(w > 0)
        def _():
            wait_out(w - 1)

        ybuf[...] = y.astype(ybuf.dtype)
        for_segments(w, lambda h, b, size: out_copy(h, b, size).start())

        @pl.when(w == n_active - 1)
        def _():
            wait_out(w)


def _experts(tables, xs, w_up, b_up, w_down, b_down, n_tok_tiles):
    tile_expert = tables[0]
    tm = MOE_TILE
    grid_spec = pltpu.PrefetchScalarGridSpec(
        num_scalar_prefetch=len(tables),
        grid=(tile_expert.shape[0],),
        in_specs=[pl.BlockSpec(memory_space=pl.ANY),
                  pl.BlockSpec((1, D_MODEL, 2 * D_FF), lambda t, te, *_: (te[t], 0, 0)),
                  pl.BlockSpec((1, 1, 2 * D_FF), lambda t, te, *_: (te[t], 0, 0)),
                  pl.BlockSpec((1, D_FF, D_MODEL), lambda t, te, *_: (te[t], 0, 0)),
                  pl.BlockSpec((1, 1, D_MODEL), lambda t, te, *_: (te[t], 0, 0))],
        out_specs=pl.BlockSpec(memory_space=pl.ANY),
        scratch_shapes=[pltpu.VMEM((2, tm, D_MODEL), xs.dtype), pltpu.VMEM((tm, D_MODEL), xs.dtype),
                        pltpu.SemaphoreType.DMA((2,)), pltpu.SemaphoreType.DMA((1,))],
    )
    return pl.pallas_call(
        functools.partial(_expert_kernel, n_tok_tiles=n_tok_tiles),
        grid_spec=grid_spec,
        out_shape=jax.ShapeDtypeStruct(xs.shape, xs.dtype),
        compiler_params=_cparams(1),
        name="experts",
    )(*tables, xs, w_up, b_up, w_down, b_down)


def _combine_kernel(ys_ref, ri_ref, rw_ref, cnt_ref, x_ref, g3_ref, gt_ref, xo_ref):
    tm = x_ref.shape[0]
    rif = ri_ref[...].astype(F32)
    rw = rw_ref[...]
    pos = [_lane_pick(rif, TOP_K + k) for k in range(TOP_K)]
    wgt = [_lane_pick(rw, k) for k in range(TOP_K)]
    used = jnp.sum(_seg_pad(cnt_ref[0][0:1, :]), axis=1, keepdims=True)
    f = jnp.zeros(x_ref.shape, F32)
    for c in range(ys_ref.shape[0] // tm):
        col = (lax.broadcasted_iota(jnp.int32, (tm, tm), 1) + c * tm).astype(F32)
        scat = jnp.zeros((tm, tm), F32)
        for k in range(TOP_K):
            scat = jnp.where(pos[k] == col, wgt[k], scat)
        row = (lax.broadcasted_iota(jnp.int32, (tm, 1), 0) + c * tm).astype(F32)
        ys = ys_ref[c * tm:(c + 1) * tm, :]
        ys = jnp.where(row < used, ys, jnp.zeros_like(ys))
        f = f + jnp.dot(scat.astype(MXU_DTYPE), ys, preferred_element_type=F32)
    xo_ref[...] = x_ref[...] + gt_ref[0] * _rms(f, g3_ref[0])


def _combine(ys, ri, rw, cnt, xa, g_norm, modt, tm, n_tiles):
    seg_rows = ys.shape[0] // n_tiles
    return pl.pallas_call(
        _combine_kernel,
        grid=(n_tiles,),
        in_specs=[pl.BlockSpec((seg_rows, D_MODEL), lambda i: (i, 0)),
                  pl.BlockSpec((tm, LANE), lambda i: (i, 0)),
                  pl.BlockSpec((tm, LANE), lambda i: (i, 0)),
                  pl.BlockSpec((1, 8, LANE), lambda i: (i, 0, 0)),
                  pl.BlockSpec((tm, D_MODEL), lambda i: (i, 0)),
                  pl.BlockSpec((1, 1, D_MODEL), lambda i: (3, 0, 0)),
                  pl.BlockSpec((1, 1, D_MODEL), lambda i: (i, 0, 5))],
        out_specs=pl.BlockSpec((tm, D_MODEL), lambda i: (i, 0)),
        out_shape=jax.ShapeDtypeStruct((n_tiles * tm, D_MODEL), F32),
        compiler_params=_cparams(1),
        name="combine",
    )(ys, ri, rw, cnt, xa, g_norm, modt)


def _swap16(idx):
    g = np.asarray(idx).reshape(-1, 2, ROPE_FREQS)
    return g[:, ::-1, :].reshape(-1)


def _in_proj_columns():
    sizes = (Q_LORA, KV_LORA, QK_ROPE, M_HEADS * M_QK, M_HEADS * M_QK, M_HEADS * M_V, M_HEADS * M_V,
             2 * 2 * M_HEADS, CONV_WIDTH, CONV_WIDTH, CONV_WIDTH, 3 * D_MODEL)
    starts = np.concatenate([[0], np.cumsum(sizes)[:-1]])
    seg = lambda k: np.arange(starts[k], starts[k] + sizes[k])
    cq, ckv, kr, mq, mk, mv, mo, mif, cu, cb, cc, cg = (seg(k) for k in range(12))
    mqk = np.concatenate([np.concatenate([mq[h * M_QK:(h + 1) * M_QK], mk[h * M_QK:(h + 1) * M_QK]])
                          for h in range(M_HEADS)])
    main = np.concatenate([cq, ckv, mqk, mv, mo, cu, cb, cc, cg])
    pad = lambda n: np.full((n,), -1)
    special = np.concatenate([kr, kr, _swap16(kr), _swap16(kr), mif, pad(LANE - mif.size), pad(LANE)])
    cols = np.concatenate([main, special])
    assert cols.size == (N_MAIN_TILES + 1) * COL_TILE
    return cols


def _take_cols(w, cols, dtype):
    cols = np.asarray(cols)
    breaks = np.flatnonzero((np.diff(cols) != 1) & ~((cols[1:] < 0) & (cols[:-1] < 0))) + 1
    parts = []
    for run in np.split(cols, breaks):
        if run[0] < 0:
            parts.append(jnp.zeros(w.shape[:-1] + (run.size,), dtype))
        else:
            parts.append(w[..., int(run[0]):int(run[-1]) + 1].astype(dtype))
    return jnp.concatenate(parts, axis=-1)


def _q_up_columns():
    per = QK_NOPE + QK_ROPE
    nope = np.concatenate([np.arange(h * per, h * per + QK_NOPE) for h in range(MLA_HEADS)])
    rope = np.concatenate([np.arange(h * per + QK_NOPE, (h + 1) * per) for h in range(MLA_HEADS)])
    return np.concatenate([nope, rope, _swap16(rope)])


def _kv_up_columns():
    per = QK_NOPE + V_HEAD
    kn = np.concatenate([np.arange(h * per, h * per + QK_NOPE) for h in range(MLA_HEADS)])
    vv = np.concatenate([np.arange(h * per + QK_NOPE, (h + 1) * per) for h in range(MLA_HEADS)])
    return np.concatenate([kn, vv])


def _rope_tables(T, rt):
    pos = jnp.arange(T)
    inv_freq = ROPE_THETA ** (-jnp.arange(ROPE_FREQS, dtype=F32) / ROPE_FREQS)
    ang_r = (pos // GRID_W).astype(F32)[:, None] * inv_freq
    ang_c = (pos % GRID_W).astype(F32)[:, None] * inv_freq
    cos64 = jnp.concatenate([jnp.cos(ang_r), jnp.cos(ang_r), jnp.cos(ang_c), jnp.cos(ang_c)], axis=1)
    sin64 = jnp.concatenate([-jnp.sin(ang_r), jnp.sin(ang_r), -jnp.sin(ang_c), jnp.sin(ang_c)], axis=1)
    cos_t = jnp.concatenate([jnp.tile(cos64, (1, 2)), jnp.ones((rt, LANE), F32)], axis=0)
    sin_t = jnp.concatenate([jnp.tile(sin64, (1, 2)), jnp.zeros((rt, LANE), F32)], axis=0)
    return cos_t, sin_t


def _moe_tables(cnt, n_tiles, tok_tile):
    tm = MOE_TILE
    seg_rows = TOP_K * tok_tile + N_EXPERTS * SEG_ALIGN
    n_work = n_tiles * seg_rows // tm + N_EXPERTS
    counts = cnt[:n_tiles, 0, :N_EXPERTS].astype(jnp.int32)
    padded = (counts + SEG_ALIGN - 1) // SEG_ALIGN * SEG_ALIGN
    in_tile = jnp.cumsum(padded, axis=1) - padded
    src_row = (jnp.arange(n_tiles, dtype=jnp.int32)[:, None] * seg_rows + in_tile).T
    seg_start = jnp.concatenate([jnp.zeros((N_EXPERTS, 1), jnp.int32), jnp.cumsum(padded.T, axis=1)], axis=1)
    group_len = seg_start[:, -1]
    work_end = jnp.cumsum((group_len + tm - 1) // tm)
    work = jnp.arange(n_work, dtype=jnp.int32)
    tile_expert = jnp.minimum(jnp.sum(work_end[None, :] <= work[:, None], axis=1), N_EXPERTS - 1)
    first_work = jnp.concatenate([jnp.zeros((1,), jnp.int32), work_end[:-1]])
    tile_row0 = (work - first_work[tile_expert]) * tm
    used = jnp.clip(group_len[tile_expert] - tile_row0, 0, tm)
    starts = seg_start[tile_expert]
    t_lo = jnp.sum(starts[:, 1:] <= tile_row0[:, None], axis=1)
    t_hi = jnp.sum(starts[:, :-1] < (tile_row0 + tm)[:, None], axis=1)
    i32 = lambda a: a.astype(jnp.int32)
    return (i32(tile_expert), i32(tile_row0), i32(work_end[-1:]), i32(seg_start.reshape(-1)),
            i32(src_row.reshape(-1)), i32(used), i32(t_lo), i32(t_hi))


def _layer(xa, mods, B, T, Tc, rt, tm, update_ctx, p, cos_t, sin_t, layer):
    rows = xa.shape[0]
    n_lat = B * T
    lat_rt, all_rt = n_lat // rt, rows // rt
    lat_tm, all_tm = n_lat // tm, rows // tm
    cond_rt = np.concatenate([np.arange(lat_rt) // (T // rt), np.full((all_rt - lat_rt,), B)])
    cond_tm = np.concatenate([np.arange(lat_tm) // (T // tm), np.full((all_tm - lat_tm,), B)])
    modt_rt = mods[cond_rt][:, None, :]
    modt_tm = mods[cond_tm][:, None, :]
    g_norm = p["g_norm"].reshape(4, 1, D_MODEL)

    z, zs = _inproj(xa, g_norm, modt_rt, p["w_in_main"], p["w_in_special"], layer, rt)
    q, k, v = _mla_project(z, zs, p["g_q_lora"], p["g_kv_lora"], p["w_q_up"], p["w_kv_up"],
                           cos_t, sin_t, rt, lat_rt, T // rt)
    a, w_up_c, w_down_c = _attention_latent(q, k, v, p["w_up"], p["w_down"], layer, p["depth"],
                                            B, T, Tc, min(ATTN_Q_TILE, T), min(ATTN_KV_CHUNK, T),
                                            rows if update_ctx else n_lat)
    w_up_c = w_up_c.reshape(N_EXPERTS, D_MODEL, 2 * D_FF)
    w_down_c = w_down_c.reshape(N_EXPERTS, D_FF, D_MODEL)
    if update_ctx:
        a = _attention_ctx(q, k, v, a, B, T, Tc)
    m = _mlstm(z, zs, p["b_if"], p["g_mlstm"], B, T, Tc)
    n_rt = all_rt if update_ctx else lat_rt
    n_tm = all_tm if update_ctx else lat_tm
    t = _merge(a, m, z, p["w_proj_a"], p["w_proj_b"], p["w_proj_c"], p["w_conv"], p["b_conv"],
               rt, n_rt, lat_rt, T // rt, T, Tc)
    xa, h2, ri, rw, cnt = _post(t, p["w_o"], xa, g_norm, modt_tm, p["w_router"], p["b_router"], tm, n_tm)
    xs = _sort_rows(h2, ri, tm, n_tm)
    ys = _experts(_moe_tables(cnt, n_tm, tm), xs, w_up_c, p["b_up"], w_down_c, p["b_down"], n_tm)
    return _combine(ys, ri, rw, cnt, xa, g_norm, modt_tm, tm, n_tm)


def kernel(x, c, ctx, c_ctx, w_mod, b_mod, g_norm, w_in, g_q_lora, g_kv_lora, w_q_up, w_kv_up, b_if, g_mlstm,
           w_conv, b_conv, w_proj_a, w_proj_b, w_proj_c, w_o, w_router, b_router, w_up, b_up, w_down, b_down):
    B, T, _ = x.shape
    Tc = ctx.shape[1]
    depth = w_mod.shape[0]
    rt = min(1024, T)
    tm = min(512, T)
    assert T % rt == 0 and (B * Tc) % rt == 0 and (B * Tc) % tm == 0
    assert T & (T - 1) == 0 and Tc & (Tc - 1) == 0 and T % GRID_W == 0

    xa = jnp.concatenate([x.reshape(B * T, D_MODEL), ctx.reshape(B * Tc, D_MODEL)], axis=0)
    cond = jnp.zeros((8, D_MODEL), F32).at[:B].set(c).at[B].set(c_ctx)
    cos_t, sin_t = _rope_tables(T, rt)
    cast = lambda w: w.astype(MXU_DTYPE)
    in_cols = _in_proj_columns()
    n_main = N_MAIN_TILES * COL_TILE
    w_in_main = _take_cols(w_in, in_cols[:n_main], MXU_DTYPE)
    w_in_special = _take_cols(w_in, in_cols[n_main:], MXU_DTYPE)
    w_q_p = _take_cols(w_q_up, _q_up_columns(), MXU_DTYPE)
    w_kv_p = _take_cols(w_kv_up, _kv_up_columns(), MXU_DTYPE)
    w_up_r = w_up.reshape(depth * N_EXPERTS * D_MODEL, 2 * D_FF)
    w_down_r = w_down.reshape(depth * N_EXPERTS * D_FF, D_MODEL)
    wa_c, wb_c, wc_c, wo_c = cast(w_proj_a), cast(w_proj_b), cast(w_proj_c), cast(w_o)
    w_router_p = jnp.pad(w_router, ((0, 0), (0, 0), (0, LANE - N_EXPERTS)))
    b_router_p = jnp.pad(b_router, ((0, 0), (0, LANE - N_EXPERTS)), constant_values=NEG_BIG)
    b_if_p = jnp.pad(b_if.reshape(depth, 1, -1), ((0, 0), (0, 0), (0, LANE - 4 * M_HEADS)))
    mods = _adaln(cond, w_mod, b_mod)
    for l in range(depth):
        p = dict(
            g_norm=g_norm[l], w_in_main=w_in_main, w_in_special=w_in_special,
            g_q_lora=g_q_lora[l], g_kv_lora=g_kv_lora[l],
            w_q_up=w_q_p[l], w_kv_up=w_kv_p[l], b_if=b_if_p[l],
            g_mlstm=g_mlstm[l], w_conv=w_conv[l], b_conv=b_conv[l],
            w_proj_a=wa_c[l], w_proj_b=wb_c[l], w_proj_c=wc_c[l], w_o=wo_c[l],
            w_router=w_router_p[l], b_router=b_router_p[l],
            depth=depth, w_up=w_up_r, b_up=b_up[l].reshape(N_EXPERTS, 1, -1),
            w_down=w_down_r, b_down=b_down[l].reshape(N_EXPERTS, 1, -1),
        )
        xa = _layer(xa, mods[l], B, T, Tc, rt, tm, l < depth - 1, p, cos_t, sin_t, l)
    return xa.reshape(B, T, D_MODEL)
```

```python
import functools

import numpy as np
import jax
import jax.numpy as jnp
from jax import lax
from jax.experimental import pallas as pl
from jax.experimental.pallas import tpu as pltpu

F32 = jnp.float32
MXU_DTYPE = jnp.bfloat16

D_MODEL = 2048
GRID_W = 64
MLA_HEADS = 8
QK_NOPE = 128
QK_ROPE = 64
V_HEAD = 128
Q_LORA = 512
KV_LORA = 512
ROPE_THETA = 10000.0
ROPE_FREQS = QK_ROPE // 4
ATTN_SCALE = (QK_NOPE + QK_ROPE) ** -0.5
M_HEADS = 4
M_QK = 64
M_V = 128
CONV_WIDTH = 512
N_EXPERTS = 32
TOP_K = 4
D_FF = 1024
SWIGLU_ALPHA = 1.702
SWIGLU_LIMIT = 7.0
N_MOD = 6
EPS = 1e-6

LANE = 128
HEAD_W = 2 * LANE
MLSTM_CHUNK = 256
COL_TILE = 512
INPROJ_TILE = 1024
LOG2_E = 1.4426950408889634
MERGE_TILE = 1024
ATTN_Q_TILE = 1024
ATTN_KV_CHUNK = 512
MOE_TILE = 512
SEG_ALIGN = 16
NEG_BIG = -1e30
VMEM_LIMIT = 56 * 1024 * 1024

ZT_CQ, ZT_CKV, ZT_MQK, ZT_MV, ZT_MO, ZT_CU, ZT_CB, ZT_CC, ZT_CG = 0, 1, 2, 3, 4, 5, 6, 7, 8
N_MAIN_TILES = 8 + 3 * D_MODEL // COL_TILE
assert (N_MAIN_TILES * COL_TILE) % INPROJ_TILE == 0


def _cparams(n_axes, vmem=VMEM_LIMIT):
    return pltpu.CompilerParams(dimension_semantics=("arbitrary",) * n_axes, vmem_limit_bytes=vmem)


def _rms(x, g):
    return x * lax.rsqrt(jnp.mean(x * x, axis=-1, keepdims=True) + EPS) * g


def _sigmoid(x):
    return 0.5 * jnp.tanh(0.5 * x) + 0.5


def _lane_pick(x, idx):
    lane = lax.broadcasted_iota(jnp.int32, x.shape, 1)
    return jnp.sum(jnp.where(lane == idx, x, 0.0), axis=1, keepdims=True)


def _mod_kernel(c_ref, w_ref, b_ref, o_ref):
    c = c_ref[...]
    s = c * jax.nn.sigmoid(c)
    o_ref[0] = jnp.dot(s.astype(MXU_DTYPE), w_ref[0].astype(MXU_DTYPE),
                       preferred_element_type=F32) + b_ref[0]


def _adaln(cond, w_mod, b_mod):
    depth, _, n = w_mod.shape
    tn = 1024
    return pl.pallas_call(
        _mod_kernel,
        grid=(depth, n // tn),
        in_specs=[pl.BlockSpec(cond.shape, lambda l, j: (0, 0)),
                  pl.BlockSpec((1, D_MODEL, tn), lambda l, j: (l, 0, j)),
                  pl.BlockSpec((1, 1, tn), lambda l, j: (l, 0, j))],
        out_specs=pl.BlockSpec((1, cond.shape[0], tn), lambda l, j: (l, 0, j)),
        out_shape=jax.ShapeDtypeStruct((depth, cond.shape[0], n), F32),
        compiler_params=_cparams(2),
        name="adaln",
    )(cond, w_mod, b_mod.reshape(depth, 1, n))


def _inproj_kernel(x_ref, g_ref, sh_ref, sc_ref, w_ref, ws_ref, z_ref, zs_ref, h_scr):
    @pl.when(pl.program_id(1) == 0)
    def _():
        h = _rms(x_ref[...], g_ref[0]) * (1.0 + sc_ref[0]) + sh_ref[0]
        h_scr[...] = h.astype(h_scr.dtype)
        zs_ref[...] = jnp.dot(h_scr[...], ws_ref[0], preferred_element_type=F32)

    z_ref[...] = jnp.dot(h_scr[...], w_ref[0], preferred_element_type=F32).astype(z_ref.dtype)


def _inproj(xa, g_norm, modt, w_main, w_special, layer, rt):
    rows = xa.shape[0]
    n_main = w_main.shape[2]
    tn = INPROJ_TILE
    return pl.pallas_call(
        _inproj_kernel,
        grid=(rows // rt, n_main // tn),
        in_specs=[pl.BlockSpec((rt, D_MODEL), lambda i, j: (i, 0)),
                  pl.BlockSpec((1, 1, D_MODEL), lambda i, j: (0, 0, 0)),
                  pl.BlockSpec((1, 1, D_MODEL), lambda i, j: (i, 0, 0)),
                  pl.BlockSpec((1, 1, D_MODEL), lambda i, j: (i, 0, 1)),
                  pl.BlockSpec((1, D_MODEL, tn), lambda i, j: (layer, 0, j)),
                  pl.BlockSpec((1, D_MODEL, COL_TILE), lambda i, j: (layer, 0, 0))],
        out_specs=[pl.BlockSpec((rt, tn), lambda i, j: (i, j)),
                   pl.BlockSpec((rt, COL_TILE), lambda i, j: (i, 0))],
        out_shape=[jax.ShapeDtypeStruct((rows, n_main), MXU_DTYPE),
                   jax.ShapeDtypeStruct((rows, COL_TILE), F32)],
        scratch_shapes=[pltpu.VMEM((rt, D_MODEL), MXU_DTYPE)],
        compiler_params=_cparams(2),
        name="inproj",
    )(xa, g_norm, modt, modt, w_main, w_special)


def _mla_kernel(cq_ref, ckv_ref, zs_ref, gq_ref, gkv_ref, wq_ref, wkv_ref, cos_ref, sin_ref,
                q_ref, k_ref, v_ref):
    def norm(z_ref, g_ref):
        return _rms(z_ref[...].astype(F32), g_ref[...]).astype(MXU_DTYPE)

    qf = jnp.dot(norm(cq_ref, gq_ref), wq_ref[...], preferred_element_type=F32) * (ATTN_SCALE * LOG2_E)
    kvf = jnp.dot(norm(ckv_ref, gkv_ref), wkv_ref[...], preferred_element_type=F32)
    cos = cos_ref[...]
    sin = sin_ref[...]
    nope_w = MLA_HEADS * QK_NOPE
    rope_w = MLA_HEADS * QK_ROPE
    for p in range(MLA_HEADS // 2):
        a = nope_w + LANE * p
        b = nope_w + rope_w + LANE * p
        rp = (qf[:, a:a + LANE] * cos + qf[:, b:b + LANE] * sin).astype(MXU_DTYPE)
        for h in (2 * p, 2 * p + 1):
            q_ref[:, HEAD_W *h:HEAD_W *h + LANE] = qf[:, LANE * h:LANE * (h + 1)].astype(MXU_DTYPE)
            q_ref[:, HEAD_W *h + LANE:HEAD_W *(h + 1)] = rp
    kr2 = zs_ref[:, 0:LANE] * cos + zs_ref[:, LANE:2 * LANE] * sin
    lane = lax.broadcasted_iota(jnp.int32, kr2.shape, 1)
    k_lo = jnp.where(lane < QK_ROPE, kr2, 0.0).astype(MXU_DTYPE)
    k_hi = jnp.where(lane >= QK_ROPE, kr2, 0.0).astype(MXU_DTYPE)
    for h in range(MLA_HEADS):
        k_ref[:, HEAD_W *h:HEAD_W *h + LANE] = kvf[:, LANE * h:LANE * (h + 1)].astype(MXU_DTYPE)
        k_ref[:, HEAD_W *h + LANE:HEAD_W *(h + 1)] = k_lo if h % 2 == 0 else k_hi
    ones = jnp.ones((kvf.shape[0], LANE), MXU_DTYPE)
    v0 = MLA_HEADS * QK_NOPE
    for h in range(MLA_HEADS):
        v_ref[:, HEAD_W *h:HEAD_W *h + LANE] = kvf[:, v0 + LANE * h:v0 + LANE * (h + 1)].astype(MXU_DTYPE)
        v_ref[:, HEAD_W *h + LANE:HEAD_W *(h + 1)] = ones


def _mla_project(z, zs, g_q, g_kv, wq_p, wkv_p, cos_t, sin_t, rt, n_lat_tiles, tiles_per_seq):
    rows = z.shape[0]
    nt = rows // rt

    def tab(i):
        return (jnp.where(i < n_lat_tiles, i % tiles_per_seq, tiles_per_seq), 0)

    return pl.pallas_call(
        _mla_kernel,
        grid=(nt,),
        in_specs=[pl.BlockSpec((rt, COL_TILE), lambda i: (i, ZT_CQ)),
                  pl.BlockSpec((rt, COL_TILE), lambda i: (i, ZT_CKV)),
                  pl.BlockSpec((rt, COL_TILE), lambda i: (i, 0)),
                  pl.BlockSpec((1, Q_LORA), lambda i: (0, 0)),
                  pl.BlockSpec((1, KV_LORA), lambda i: (0, 0)),
                  pl.BlockSpec(wq_p.shape, lambda i: (0, 0)),
                  pl.BlockSpec(wkv_p.shape, lambda i: (0, 0)),
                  pl.BlockSpec((rt, LANE), tab),
                  pl.BlockSpec((rt, LANE), tab)],
        out_specs=[pl.BlockSpec((rt, HEAD_W *MLA_HEADS), lambda i: (i, 0)),
                   pl.BlockSpec((rt, HEAD_W *MLA_HEADS), lambda i: (i, 0)),
                   pl.BlockSpec((rt, HEAD_W *MLA_HEADS), lambda i: (i, 0))],
        out_shape=[jax.ShapeDtypeStruct((rows, HEAD_W *MLA_HEADS), MXU_DTYPE)] * 3,
        compiler_params=_cparams(1),
        name="mla_project",
    )(z, z, zs, g_q.reshape(1, -1), g_kv.reshape(1, -1), wq_p, wkv_p, cos_t, sin_t)


def _attn_kernel(*refs, kv_chunks, n_cast=0):
    q_ref = refs[0]
    o_ref = refs[len(refs) - 1 - n_cast]
    for c in range(n_cast):
        src = refs[1 + 2 * len(kv_chunks) + c]
        dst = refs[len(refs) - n_cast + c]
        dst[...] = src[...].astype(dst.dtype)
    q = q_ref[...]
    tq = q.shape[0]
    m = jnp.full((tq, 1), -jnp.inf, F32)
    acc = jnp.zeros((tq, 2 * V_HEAD), F32)
    for s, n_chunks in enumerate(kv_chunks):
        k_ref, v_ref = refs[1 + 2 * s], refs[2 + 2 * s]
        ck = k_ref.shape[0] // n_chunks
        for c in range(n_chunks):
            k = k_ref[c * ck:(c + 1) * ck, :]
            v = v_ref[c * ck:(c + 1) * ck, :]
            sc = lax.dot_general(q, k, (((1,), (1,)), ((), ())), preferred_element_type=F32)
            m_new = jnp.maximum(m, jnp.max(sc, axis=-1, keepdims=True))
            p = jnp.exp2(sc - m_new).astype(MXU_DTYPE)
            acc = jnp.exp2(m - m_new) * acc + jnp.dot(p, v, preferred_element_type=F32)
            m = m_new
    o_ref[...] = (acc[:, :V_HEAD] / acc[:, V_HEAD:]).astype(o_ref.dtype)


def _attention_latent(q, k, v, w_up, w_down, layer, depth, B, T, Tc, tq, ck, rows):
    qt = T // tq
    ctx_blk0 = B * T // Tc
    steps = B * MLA_HEADS * qt
    step = lambda b, h, i: (b * MLA_HEADS + h) * qt + i
    cast_specs_in, cast_specs_out, cast_shapes = [], [], []
    for w in (w_up, w_down):
        n = w.shape[1]
        layer_rows = w.shape[0] // depth
        slab = layer_rows // steps
        assert slab * steps == layer_rows and slab % 16 == 0
        cast_specs_in.append(pl.BlockSpec((slab, n), lambda b, h, i: (layer * steps + step(b, h, i), 0)))
        cast_specs_out.append(pl.BlockSpec((slab, n), lambda b, h, i: (step(b, h, i), 0)))
        cast_shapes.append(jax.ShapeDtypeStruct((layer_rows, n), MXU_DTYPE))
    return pl.pallas_call(
        functools.partial(_attn_kernel, kv_chunks=(T // ck, 1), n_cast=2),
        grid=(B, MLA_HEADS, qt),
        in_specs=[pl.BlockSpec((tq, HEAD_W), lambda b, h, i: (b * qt + i, h)),
                  pl.BlockSpec((T, HEAD_W), lambda b, h, i: (b, h)),
                  pl.BlockSpec((T, HEAD_W), lambda b, h, i: (b, h)),
                  pl.BlockSpec((Tc, HEAD_W), lambda b, h, i: (ctx_blk0 + b, h)),
                  pl.BlockSpec((Tc, HEAD_W), lambda b, h, i: (ctx_blk0 + b, h))] + cast_specs_in,
        out_specs=[pl.BlockSpec((tq, V_HEAD), lambda b, h, i: (b * qt + i, h))] + cast_specs_out,
        out_shape=[jax.ShapeDtypeStruct((rows, V_HEAD * MLA_HEADS), MXU_DTYPE)] + cast_shapes,
        compiler_params=_cparams(3),
        name="attn_latent",
    )(q, k, v, k, v, w_up, w_down)


def _attention_ctx(q, k, v, a_buf, B, T, Tc):
    ctx_blk0 = B * T // Tc
    return pl.pallas_call(
        functools.partial(_attn_kernel, kv_chunks=(1,)),
        grid=(B, MLA_HEADS),
        in_specs=[pl.BlockSpec((Tc, HEAD_W), lambda b, h: (ctx_blk0 + b, h)),
                  pl.BlockSpec((Tc, HEAD_W), lambda b, h: (ctx_blk0 + b, h)),
                  pl.BlockSpec((Tc, HEAD_W), lambda b, h: (ctx_blk0 + b, h)),
                  pl.BlockSpec(memory_space=pl.ANY)],
        out_specs=pl.BlockSpec((Tc, V_HEAD), lambda b, h: (ctx_blk0 + b, h)),
        out_shape=jax.ShapeDtypeStruct(a_buf.shape, a_buf.dtype),
        input_output_aliases={3: 0},
        compiler_params=_cparams(2),
        name="attn_ctx",
    )(q, k, v, a_buf)


def _mlstm_chunk(qk, v, g, lf_cum, gt_ref, head, reverse, ct_ref, m_prev, consts):
    tri, row_iota, mk, ones_tile, lane = consts
    L = qk.shape[0]
    base = 8 if reverse else 0
    col_i, col_f = base + head, base + 4 + head
    ql = jnp.where(lane < M_QK, qk, jnp.zeros_like(qk))
    kl = jnp.dot(qk, mk, preferred_element_type=F32).astype(MXU_DTYPE)
    v_aug = jnp.concatenate([v, ones_tile], axis=1)
    bc = _lane_pick(lf_cum, col_f)
    li = _lane_pick(g, col_i)
    r_row = gt_ref[0, col_i:col_i + 1, :] - gt_ref[1, col_f:col_f + 1, :]
    dlog = jnp.where(tri, bc + r_row, -jnp.inf)
    inter = bc + m_prev
    m_t = jnp.maximum(inter, jnp.max(dlog, axis=1, keepdims=True))
    dexp = jnp.exp(dlog - m_t)
    s_raw = lax.dot_general(ql, kl, (((1,), (1,)), ((), ())), preferred_element_type=F32)
    s_mat = (s_raw * (M_QK ** -0.5) * dexp).astype(MXU_DTYPE)
    a = jnp.exp(inter - m_t) * (M_QK ** -0.5)
    ct = ct_ref[...]
    nd = (jnp.dot(s_mat, v_aug, preferred_element_type=F32)
          + a * jnp.dot(ql, ct.astype(MXU_DTYPE), preferred_element_type=F32))
    num, den = nd[:, :M_V], nd[:, M_V:]
    h_out = num / jnp.maximum(jnp.abs(den), jnp.exp(-m_t))
    last = 0 if reverse else L - 1
    total = jnp.sum(jnp.where(row_iota == last, bc, 0.0), axis=0, keepdims=True)
    w_col = total - bc + li
    m_new = jnp.maximum(total + m_prev, jnp.max(w_col, axis=0, keepdims=True))
    decay = jnp.exp(total + m_prev - m_new)
    e = jnp.exp(w_col - m_new)
    ek = (kl.astype(F32) * e).astype(MXU_DTYPE)
    upd = lax.dot_general(ek, v_aug, (((0,), (0,)), ((), ())), preferred_element_type=F32)
    ct_ref[...] = decay * ct + upd
    return h_out, m_new


def _mlstm_kernel(zl_qk, zl_v, zl_mo, gl_ref, zc_qk, zc_v, zc_mo, gc_ref, bif_ref, gm_ref,
                  ol_ref, oc_ref, acc_l, acc_c, ct_scr, gt_scr):
    L = MLSTM_CHUNK
    T = zl_qk.shape[0]
    nc = T // L
    r = lax.broadcasted_iota(jnp.int32, (L, L), 0)
    c = lax.broadcasted_iota(jnp.int32, (L, L), 1)
    tri_f = c <= r
    tri_b = c >= r
    tmat_f = tri_f.astype(F32)
    tmat_b = tri_b.astype(F32)
    row_iota = lax.broadcasted_iota(jnp.int32, (L, 1), 0)
    lane = lax.broadcasted_iota(jnp.int32, (L, LANE), 1)
    rr = lax.broadcasted_iota(jnp.int32, (LANE, LANE), 0)
    cc = lax.broadcasted_iota(jnp.int32, (LANE, LANE), 1)
    mk = (rr == cc + M_QK).astype(MXU_DTYPE)
    ones_tile = jnp.ones((L, LANE), MXU_DTYPE)
    consts_f = (tri_f, row_iota, mk, ones_tile, lane)
    consts_b = (tri_b, row_iota, mk, ones_tile, lane)
    bias = bif_ref[...]

    def gate_prep(g_raw, tmat, slot):
        g = g_raw + bias
        log_sig = jnp.minimum(g, 0.0) - jnp.log1p(jnp.exp(-jnp.abs(g)))
        cum = jnp.dot(tmat, log_sig, precision=lax.Precision.HIGHEST, preferred_element_type=F32)
        gt_scr[slot, 0] = g.T
        gt_scr[slot, 1] = cum.T
        return g, cum

    def step(qk_ref, v_ref, g_ref, acc_ref, row_f, row_b, ms):
        g_f, cum_f = gate_prep(g_ref[pl.ds(row_f, L), :], tmat_f, 0)
        g_b, cum_b = gate_prep(g_ref[pl.ds(row_b, L), :], tmat_b, 1)
        new_ms = []
        for h in range(M_HEADS):
            cols = slice(h * LANE, (h + 1) * LANE)
            hf, mf = _mlstm_chunk(qk_ref[pl.ds(row_f, L), cols], v_ref[pl.ds(row_f, L), cols],
                                  g_f, cum_f, gt_scr.at[0], h, False,
                                  ct_scr.at[h], ms[h], consts_f)
            acc_ref[pl.ds(row_f, L), cols] += hf
            hb, mb = _mlstm_chunk(qk_ref[pl.ds(row_b, L), cols], v_ref[pl.ds(row_b, L), cols],
                                  g_b, cum_b, gt_scr.at[1], h, True,
                                  ct_scr.at[M_HEADS + h], ms[M_HEADS + h], consts_b)
            acc_ref[pl.ds(row_b, L), cols] += hb
            new_ms.append((mf, mb))
        return tuple(x[0] for x in new_ms) + tuple(x[1] for x in new_ms)

    ct_scr[...] = jnp.zeros_like(ct_scr)
    acc_l[...] = jnp.zeros_like(acc_l)
    acc_c[...] = jnp.zeros_like(acc_c)
    zero = jnp.zeros((1, 1), F32)
    ms = step(zc_qk, zc_v, gc_ref, acc_c, 0, 0, (zero,) * (2 * M_HEADS))

    def body(i, ms):
        row_f = pl.multiple_of(i * L, L)
        row_b = pl.multiple_of((nc - 1 - i) * L, L)
        return step(zl_qk, zl_v, gl_ref, acc_l, row_f, row_b, ms)

    lax.fori_loop(0, nc, body, ms)

    gm = gm_ref[...]

    def finish(acc_ref, mo_ref, o_ref, row):
        for h in range(M_HEADS):
            cols = slice(h * LANE, (h + 1) * LANE)
            hn = _rms(acc_ref[pl.ds(row, L), cols], gm[:, cols])
            gate = jax.nn.sigmoid(mo_ref[pl.ds(row, L), cols].astype(F32))
            o_ref[pl.ds(row, L), cols] = (hn * gate).astype(o_ref.dtype)

    finish(acc_c, zc_mo, oc_ref, 0)

    def fin_body(i, carry):
        finish(acc_l, zl_mo, ol_ref, pl.multiple_of(i * L, L))
        return carry

    lax.fori_loop(0, nc, fin_body, 0)


def _mlstm(z, zs, bif_row, g_mlstm, B, T, Tc):
    assert Tc == MLSTM_CHUNK and T % MLSTM_CHUNK == 0
    rows = z.shape[0]
    ctx_blk0 = B * T // Tc
    w = M_HEADS * LANE
    out_l, out_c = pl.pallas_call(
        _mlstm_kernel,
        grid=(B,),
        in_specs=[pl.BlockSpec((T, w), lambda b: (b, ZT_MQK)),
                  pl.BlockSpec((T, w), lambda b: (b, ZT_MV)),
                  pl.BlockSpec((T, w), lambda b: (b, ZT_MO)),
                  pl.BlockSpec((T, LANE), lambda b: (b, 2)),
                  pl.BlockSpec((Tc, w), lambda b: (ctx_blk0 + b, ZT_MQK)),
                  pl.BlockSpec((Tc, w), lambda b: (ctx_blk0 + b, ZT_MV)),
                  pl.BlockSpec((Tc, w), lambda b: (ctx_blk0 + b, ZT_MO)),
                  pl.BlockSpec((Tc, LANE), lambda b: (ctx_blk0 + b, 2)),
                  pl.BlockSpec((1, LANE), lambda b: (0, 0)),
                  pl.BlockSpec((1, w), lambda b: (0, 0))],
        out_specs=[pl.BlockSpec((T, w), lambda b: (b, 0)),
                   pl.BlockSpec((Tc, w), lambda b: (b, 0))],
        out_shape=[jax.ShapeDtypeStruct((B * T, w), MXU_DTYPE),
                   jax.ShapeDtypeStruct((B * Tc, w), MXU_DTYPE)],
        scratch_shapes=[pltpu.VMEM((T, w), F32), pltpu.VMEM((Tc, w), F32),
                        pltpu.VMEM((2 * M_HEADS, LANE, 2 * LANE), F32),
                        pltpu.VMEM((2, 2, LANE, MLSTM_CHUNK), F32)],
        compiler_params=_cparams(1),
        name="mlstm",
    )(z, z, z, zs, z, z, z, zs, bif_row, g_mlstm.reshape(1, -1))
    return jnp.concatenate([out_l, out_c], axis=0)


def _merge_kernel(a_ref, m_ref, cu_ref, cb_ref, cc_ref, cu_p, cc_p, cu_n, cc_n, ga_ref, gm_ref, gs_ref,
                  wa_ref, wb_ref, wc_ref, wconv_ref, bconv_ref, t_ref, s_scr,
                  *, n_lat_tiles, tiles_per_seq, seq_len, ctx_len):
    i = pl.program_id(0)
    j = pl.program_id(1)
    rt = a_ref.shape[0]

    @pl.when(j == 0)
    def _():
        u = cc_ref[...].astype(F32) * cu_ref[...].astype(F32)
        hp = cc_p[...].astype(F32) * cu_p[...].astype(F32)
        hn = cc_n[...].astype(F32) * cu_n[...].astype(F32)
        row = lax.broadcasted_iota(jnp.int32, (rt, 1), 0)
        is_ctx = i >= n_lat_tiles
        seg = jnp.where(is_ctx, ctx_len, seq_len)
        off = jnp.where(is_ctx, (i - n_lat_tiles) * rt, (i % tiles_per_seq) * rt)
        pos = (off + row) & (seg - 1)
        up = pltpu.roll(u, 1, 0)
        up = jnp.where(row == 0, hp[hp.shape[0] - 1:, :], up)
        up = jnp.where(pos == 0, 0.0, up)
        un = pltpu.roll(u, rt - 1, 0)
        un = jnp.where(row == rt - 1, hn[0:1, :], un)
        un = jnp.where(pos == seg - 1, 0.0, un)
        wconv = wconv_ref[...]
        y = up * wconv[0:1, :] + u * wconv[1:2, :] + un * wconv[2:3, :] + bconv_ref[...]
        s_scr[...] = (cb_ref[...].astype(F32) * y).astype(s_scr.dtype)

    def sig(ref):
        return _sigmoid(ref[...].astype(F32))

    t = (sig(ga_ref) * jnp.dot(a_ref[...], wa_ref[...], preferred_element_type=F32)
         + sig(gm_ref) * jnp.dot(m_ref[...], wb_ref[...], preferred_element_type=F32)
         + sig(gs_ref) * jnp.dot(s_scr[...], wc_ref[...], preferred_element_type=F32))
    t_ref[...] = t.astype(t_ref.dtype)


def _merge(a, m, z, wa, wb, wc, w_conv, b_conv, rt, n_tiles, n_lat_tiles, tiles_per_seq, T, Tc):
    rows = n_tiles * rt
    hb = 16
    rb = rt // hb
    last_hb = z.shape[0] // hb - 1
    tn = MERGE_TILE
    nj = D_MODEL // tn
    gt = D_MODEL // tn
    g0 = ZT_CG * COL_TILE // tn
    assert (ZT_CG * COL_TILE) % tn == 0

    def prev_blk(col):
        return lambda i, j: (jnp.maximum(i * rb - 1, 0), col)

    def next_blk(col):
        return lambda i, j: (jnp.minimum((i + 1) * rb, last_hb), col)

    kern = functools.partial(_merge_kernel, n_lat_tiles=n_lat_tiles, tiles_per_seq=tiles_per_seq,
                             seq_len=T, ctx_len=Tc)
    return pl.pallas_call(
        kern,
        grid=(n_tiles, nj),
        in_specs=[pl.BlockSpec((rt, a.shape[1]), lambda i, j: (i, 0)),
                  pl.BlockSpec((rt, m.shape[1]), lambda i, j: (i, 0)),
                  pl.BlockSpec((rt, COL_TILE), lambda i, j: (i, ZT_CU)),
                  pl.BlockSpec((rt, COL_TILE), lambda i, j: (i, ZT_CB)),
                  pl.BlockSpec((rt, COL_TILE), lambda i, j: (i, ZT_CC)),
                  pl.BlockSpec((hb, COL_TILE), prev_blk(ZT_CU)),
                  pl.BlockSpec((hb, COL_TILE), prev_blk(ZT_CC)),
                  pl.BlockSpec((hb, COL_TILE), next_blk(ZT_CU)),
                  pl.BlockSpec((hb, COL_TILE), next_blk(ZT_CC)),
                  pl.BlockSpec((rt, tn), lambda i, j: (i, g0 + j)),
                  pl.BlockSpec((rt, tn), lambda i, j: (i, g0 + gt + j)),
                  pl.BlockSpec((rt, tn), lambda i, j: (i, g0 + 2 * gt + j)),
                  pl.BlockSpec((wa.shape[0], tn), lambda i, j: (0, j)),
                  pl.BlockSpec((wb.shape[0], tn), lambda i, j: (0, j)),
                  pl.BlockSpec((wc.shape[0], tn), lambda i, j: (0, j)),
                  pl.BlockSpec(w_conv.shape, lambda i, j: (0, 0)),
                  pl.BlockSpec((1, CONV_WIDTH), lambda i, j: (0, 0))],
        out_specs=pl.BlockSpec((rt, tn), lambda i, j: (i, j)),
        out_shape=jax.ShapeDtypeStruct((rows, D_MODEL), MXU_DTYPE),
        scratch_shapes=[pltpu.VMEM((rt, CONV_WIDTH), MXU_DTYPE)],
        compiler_params=_cparams(2),
        name="merge",
    )(a, m, z, z, z, z, z, z, z, z, z, z, wa, wb, wc, w_conv, b_conv.reshape(1, -1))


def _dot3(x, w):
    xh = x.astype(jnp.bfloat16)
    xl = (x - xh.astype(F32)).astype(jnp.bfloat16)
    wh = w.astype(jnp.bfloat16)
    wl = (w - wh.astype(F32)).astype(jnp.bfloat16)
    n = w.shape[1]
    hi = jnp.dot(xh, jnp.concatenate([wh, wl], axis=1), preferred_element_type=F32)
    return hi[:, :n] + hi[:, n:] + jnp.dot(xl, wh, preferred_element_type=F32)


def _seg_pad(counts):
    return jnp.ceil(counts * (1.0 / SEG_ALIGN)) * SEG_ALIGN


def _exclusive_lane_cumsum(row):
    upper = (lax.broadcasted_iota(jnp.int32, (LANE, LANE), 0)
             < lax.broadcasted_iota(jnp.int32, (LANE, LANE), 1)).astype(jnp.bfloat16)
    out = jnp.dot(jnp.broadcast_to(row, (8, LANE)).astype(jnp.bfloat16), upper, preferred_element_type=F32)
    return out[0:1, :]


def _post_kernel(t_ref, wo_ref, x_ref, g1_ref, g2_ref, gt_ref, sh_ref, sc_ref, wr_ref, br_ref,
                 xo_ref, h2_ref, ri_ref, rw_ref, cnt_ref):
    half = t_ref.shape[0] // 2
    parts = []
    for r in (slice(0, half), slice(half, 2 * half)):
        y = jnp.dot(t_ref[r, :], wo_ref[...], preferred_element_type=F32)
        xn = x_ref[r, :] + gt_ref[0] * _rms(y, g1_ref[0])
        xo_ref[r, :] = xn
        h2 = _rms(xn, g2_ref[0]) * (1.0 + sc_ref[0]) + sh_ref[0]
        h2_ref[r, :] = h2.astype(h2_ref.dtype)
        parts.append(_dot3(h2, wr_ref[...]) + br_ref[...])
    logits = jnp.concatenate(parts, axis=0)
    lane = lax.broadcasted_iota(jnp.int32, logits.shape, 1)
    lane_f = lane.astype(F32)
    vals, ids = [], []
    for _ in range(TOP_K):
        mx = jnp.max(logits, axis=1, keepdims=True)
        idx = jnp.min(jnp.where(logits == mx, lane_f, float(LANE)), axis=1, keepdims=True)
        logits = jnp.where(lane_f == idx, -jnp.inf, logits)
        vals.append(mx)
        ids.append(idx)
    es = [jnp.exp(v - vals[0]) for v in vals]
    tot = es[0] + es[1] + es[2] + es[3]
    tm = logits.shape[0]
    onehots = [lane_f == ids[k] for k in range(TOP_K)]
    asg = jnp.zeros(logits.shape, F32)
    for k in range(TOP_K):
        asg = jnp.where(onehots[k], 1.0, asg)
    earlier = (lax.broadcasted_iota(jnp.int32, (tm, tm), 1)
               < lax.broadcasted_iota(jnp.int32, (tm, tm), 0)).astype(jnp.bfloat16)
    before = jnp.dot(earlier, asg.astype(jnp.bfloat16), preferred_element_type=F32)
    counts = jnp.sum(asg, axis=0, keepdims=True)
    seg_off = _exclusive_lane_cumsum(_seg_pad(counts))
    rw = jnp.zeros(logits.shape, F32)
    ri = jnp.zeros(logits.shape, F32)
    for k in range(TOP_K):
        pos = jnp.sum(jnp.where(onehots[k], before + seg_off, 0.0), axis=1, keepdims=True)
        rw = jnp.where(lane == k, es[k] / tot, rw)
        ri = jnp.where(lane == k, ids[k], ri)
        ri = jnp.where(lane == TOP_K + k, pos, ri)
    rw_ref[...] = rw
    ri_ref[...] = ri.astype(jnp.int32)
    cnt_ref[0] = jnp.broadcast_to(counts, cnt_ref.shape[1:])


def _post(t, wo, xa, g_norm, modt, wr_p, br_p, tm, n_tiles):
    rows = n_tiles * tm
    row_spec = lambda w: pl.BlockSpec((tm, w), lambda i: (i, 0))
    mod = lambda k: pl.BlockSpec((1, 1, D_MODEL), lambda i: (i, 0, k))
    gn = lambda k: pl.BlockSpec((1, 1, D_MODEL), lambda i: (k, 0, 0))
    return pl.pallas_call(
        _post_kernel,
        grid=(n_tiles,),
        in_specs=[row_spec(D_MODEL), pl.BlockSpec(wo.shape, lambda i: (0, 0)), row_spec(D_MODEL),
                  gn(1), gn(2), mod(2), mod(3), mod(4),
                  pl.BlockSpec(wr_p.shape, lambda i: (0, 0)), pl.BlockSpec((1, LANE), lambda i: (0, 0))],
        out_specs=[row_spec(D_MODEL), row_spec(D_MODEL), row_spec(LANE), row_spec(LANE),
                   pl.BlockSpec((1, 8, LANE), lambda i: (i, 0, 0))],
        out_shape=[jax.ShapeDtypeStruct((rows, D_MODEL), F32),
                   jax.ShapeDtypeStruct((rows, D_MODEL), MXU_DTYPE),
                   jax.ShapeDtypeStruct((rows, LANE), jnp.int32),
                   jax.ShapeDtypeStruct((rows, LANE), F32),
                   jax.ShapeDtypeStruct((rows // tm, 8, LANE), F32)],
        compiler_params=_cparams(1),
        name="post_mix",
    )(t, wo, xa, g_norm, g_norm, modt, modt, modt, wr_p, br_p.reshape(1, LANE))


def _sort_kernel(h2_ref, ri_ref, xs_ref):
    tm = h2_ref.shape[0]
    rif = ri_ref[...].astype(F32)
    pos = [_lane_pick(rif, TOP_K + k) for k in range(TOP_K)]
    h2 = h2_ref[...]
    for c in range(xs_ref.shape[0] // tm):
        col = (lax.broadcasted_iota(jnp.int32, (tm, tm), 1) + c * tm).astype(F32)
        onehot = jnp.zeros((tm, tm), F32)
        for k in range(TOP_K):
            onehot = jnp.where(pos[k] == col, 1.0, onehot)
        rows = lax.dot_general(onehot.astype(MXU_DTYPE), h2, (((0,), (0,)), ((), ())),
                               preferred_element_type=F32)
        xs_ref[c * tm:(c + 1) * tm, :] = rows.astype(xs_ref.dtype)


def _sort_rows(h2, ri, tm, n_tiles):
    seg_rows = TOP_K * tm + N_EXPERTS * SEG_ALIGN
    assert seg_rows % tm == 0
    return pl.pallas_call(
        _sort_kernel,
        grid=(n_tiles,),
        in_specs=[pl.BlockSpec((tm, D_MODEL), lambda i: (i, 0)),
                  pl.BlockSpec((tm, LANE), lambda i: (i, 0))],
        out_specs=pl.BlockSpec((seg_rows, D_MODEL), lambda i: (i, 0)),
        out_shape=jax.ShapeDtypeStruct((n_tiles * seg_rows, D_MODEL), MXU_DTYPE),
        compiler_params=_cparams(1),
        name="sort_rows",
    )(h2, ri)


def _expert_kernel(te_ref, r0_ref, na_ref, seg_ref, src_ref, used_ref, tlo_ref, thi_ref,
                   xs_hbm, wu_ref, bu_ref, wd_ref, bd_ref, ys_hbm,
                   xbuf, ybuf, sem_in, sem_out, *, n_tok_tiles):
    w = pl.program_id(0)
    n_active = na_ref[0]
    tm = ybuf.shape[0]
    nt = n_tok_tiles
    sizes = [s for s in (512, 256, 128, 64, 32, 16) if s <= tm]

    def for_pieces(length, fn):
        off = jnp.int32(0)
        for size in sizes:
            hit = (length & size) != 0

            @pl.when(hit)
            def _(off=off, size=size):
                fn(off, size)

            off = off + jnp.where(hit, size, 0)

    def for_segments(wt, fn):
        e = te_ref[wt]
        r0 = r0_ref[wt]

        def body(t, carry):
            s0 = seg_ref[e * (nt + 1) + t]
            s1 = seg_ref[e * (nt + 1) + t + 1]
            lo = jnp.maximum(s0, r0)
            length = jnp.maximum(jnp.minimum(s1, r0 + tm) - lo, 0)
            src = src_ref[e * nt + t] + (lo - s0)
            dst = lo - r0
            for_pieces(length, lambda off, size: fn(src + off, dst + off, size))
            return carry

        lax.fori_loop(tlo_ref[wt], thi_ref[wt], body, 0)

    def rows_in_use(wt):
        return used_ref[wt]

    def aligned(row):
        return row if isinstance(row, int) else pl.multiple_of(row, SEG_ALIGN)

    def in_copy(hbm_row, buf_row, size, slot):
        return pltpu.make_async_copy(xs_hbm.at[pl.ds(aligned(hbm_row), size)],
                                     xbuf.at[slot, pl.ds(aligned(buf_row), size)], sem_in.at[slot])

    def out_copy(hbm_row, buf_row, size):
        return pltpu.make_async_copy(ybuf.at[pl.ds(aligned(buf_row), size)],
                                     ys_hbm.at[pl.ds(aligned(hbm_row), size)], sem_out.at[0])

    def start_in(wt, slot):
        for_segments(wt, lambda h, b, size: in_copy(h, b, size, slot).start())

    def wait_in(wt, slot):
        for_pieces(rows_in_use(wt), lambda off, size: in_copy(0, 0, size, slot).wait())

    def wait_out(wt):
        for_pieces(rows_in_use(wt), lambda off, size: out_copy(0, 0, size).wait())

    @pl.when(w < n_active)
    def _():
        slot = w % 2

        @pl.when(w == 0)
        def _():
            xbuf[...] = jnp.zeros_like(xbuf)
            start_in(0, 0)

        wait_in(w, slot)

        @pl.when(w + 1 < n_active)
        def _():
            start_in(w + 1, 1 - slot)

        gu = jnp.dot(xbuf[slot], wu_ref[0], preferred_element_type=F32) + bu_ref[0]
        g = jnp.minimum(gu[:, :D_FF], SWIGLU_LIMIT)
        u = jnp.clip(gu[:, D_FF:], -SWIGLU_LIMIT, SWIGLU_LIMIT)
        hid = g * _sigmoid(SWIGLU_ALPHA * g) * (u + 1.0)
        y = jnp.dot(hid.astype(MXU_DTYPE), wd_ref[0], preferred_element_type=F32) + bd_ref[0]

        @pl.when(w > 0)
        def _():
            wait_out(w - 1)

        ybuf[...] = y.astype(ybuf.dtype)
        for_segments(w, lambda h, b, size: out_copy(h, b, size).start())

        @pl.when(w == n_active - 1)
        def _():
            wait_out(w)


def _experts(tables, xs, w_up, b_up, w_down, b_down, n_tok_tiles):
    tile_expert = tables[0]
    tm = MOE_TILE
    grid_spec = pltpu.PrefetchScalarGridSpec(
        num_scalar_prefetch=len(tables),
        grid=(tile_expert.shape[0],),
        in_specs=[pl.BlockSpec(memory_space=pl.ANY),
                  pl.BlockSpec((1, D_MODEL, 2 * D_FF), lambda t, te, *_: (te[t], 0, 0)),
                  pl.BlockSpec((1, 1, 2 * D_FF), lambda t, te, *_: (te[t], 0, 0)),
                  pl.BlockSpec((1, D_FF, D_MODEL), lambda t, te, *_: (te[t], 0, 0)),
                  pl.BlockSpec((1, 1, D_MODEL), lambda t, te, *_: (te[t], 0, 0))],
        out_specs=pl.BlockSpec(memory_space=pl.ANY),
        scratch_shapes=[pltpu.VMEM((2, tm, D_MODEL), xs.dtype), pltpu.VMEM((tm, D_MODEL), xs.dtype),
                        pltpu.SemaphoreType.DMA((2,)), pltpu.SemaphoreType.DMA((1,))],
    )
    return pl.pallas_call(
        functools.partial(_expert_kernel, n_tok_tiles=n_tok_tiles),
        grid_spec=grid_spec,
        out_shape=jax.ShapeDtypeStruct(xs.shape, xs.dtype),
        compiler_params=_cparams(1),
        name="experts",
    )(*tables, xs, w_up, b_up, w_down, b_down)


def _combine_kernel(ys_ref, ri_ref, rw_ref, cnt_ref, x_ref, g3_ref, gt_ref, xo_ref):
    tm = x_ref.shape[0]
    rif = ri_ref[...].astype(F32)
    rw = rw_ref[...]
    pos = [_lane_pick(rif, TOP_K + k) for k in range(TOP_K)]
    wgt = [_lane_pick(rw, k) for k in range(TOP_K)]
    used = jnp.sum(_seg_pad(cnt_ref[0][0:1, :]), axis=1, keepdims=True)
    f = jnp.zeros(x_ref.shape, F32)
    for c in range(ys_ref.shape[0] // tm):
        col = (lax.broadcasted_iota(jnp.int32, (tm, tm), 1) + c * tm).astype(F32)
        scat = jnp.zeros((tm, tm), F32)
        for k in range(TOP_K):
            scat = jnp.where(pos[k] == col, wgt[k], scat)
        row = (lax.broadcasted_iota(jnp.int32, (tm, 1), 0) + c * tm).astype(F32)
        ys = ys_ref[c * tm:(c + 1) * tm, :]
        ys = jnp.where(row < used, ys, jnp.zeros_like(ys))
        f = f + jnp.dot(scat.astype(MXU_DTYPE), ys, preferred_element_type=F32)
    xo_ref[...] = x_ref[...] + gt_ref[0] * _rms(f, g3_ref[0])


def _combine(ys, ri, rw, cnt, xa, g_norm, modt, tm, n_tiles):
    seg_rows = ys.shape[0] // n_tiles
    return pl.pallas_call(
        _combine_kernel,
        grid=(n_tiles,),
        in_specs=[pl.BlockSpec((seg_rows, D_MODEL), lambda i: (i, 0)),
                  pl.BlockSpec((tm, LANE), lambda i: (i, 0)),
                  pl.BlockSpec((tm, LANE), lambda i: (i, 0)),
                  pl.BlockSpec((1, 8, LANE), lambda i: (i, 0, 0)),
                  pl.BlockSpec((tm, D_MODEL), lambda i: (i, 0)),
                  pl.BlockSpec((1, 1, D_MODEL), lambda i: (3, 0, 0)),
                  pl.BlockSpec((1, 1, D_MODEL), lambda i: (i, 0, 5))],
        out_specs=pl.BlockSpec((tm, D_MODEL), lambda i: (i, 0)),
        out_shape=jax.ShapeDtypeStruct((n_tiles * tm, D_MODEL), F32),
        compiler_params=_cparams(1),
        name="combine",
    )(ys, ri, rw, cnt, xa, g_norm, modt)


def _swap16(idx):
    g = np.asarray(idx).reshape(-1, 2, ROPE_FREQS)
    return g[:, ::-1, :].reshape(-1)


def _in_proj_columns():
    sizes = (Q_LORA, KV_LORA, QK_ROPE, M_HEADS * M_QK, M_HEADS * M_QK, M_HEADS * M_V, M_HEADS * M_V,
             2 * 2 * M_HEADS, CONV_WIDTH, CONV_WIDTH, CONV_WIDTH, 3 * D_MODEL)
    starts = np.concatenate([[0], np.cumsum(sizes)[:-1]])
    seg = lambda k: np.arange(starts[k], starts[k] + sizes[k])
    cq, ckv, kr, mq, mk, mv, mo, mif, cu, cb, cc, cg = (seg(k) for k in range(12))
    mqk = np.concatenate([np.concatenate([mq[h * M_QK:(h + 1) * M_QK], mk[h * M_QK:(h + 1) * M_QK]])
                          for h in range(M_HEADS)])
    main = np.concatenate([cq, ckv, mqk, mv, mo, cu, cb, cc, cg])
    pad = lambda n: np.full((n,), -1)
    special = np.concatenate([kr, kr, _swap16(kr), _swap16(kr), mif, pad(LANE - mif.size), pad(LANE)])
    cols = np.concatenate([main, special])
    assert cols.size == (N_MAIN_TILES + 1) * COL_TILE
    return cols


def _take_cols(w, cols, dtype):
    cols = np.asarray(cols)
    breaks = np.flatnonzero((np.diff(cols) != 1) & ~((cols[1:] < 0) & (cols[:-1] < 0))) + 1
    parts = []
    for run in np.split(cols, breaks):
        if run[0] < 0:
            parts.append(jnp.zeros(w.shape[:-1] + (run.size,), dtype))
        else:
            parts.append(w[..., int(run[0]):int(run[-1]) + 1].astype(dtype))
    return jnp.concatenate(parts, axis=-1)


def _q_up_columns():
    per = QK_NOPE + QK_ROPE
    nope = np.concatenate([np.arange(h * per, h * per + QK_NOPE) for h in range(MLA_HEADS)])
    rope = np.concatenate([np.arange(h * per + QK_NOPE, (h + 1) * per) for h in range(MLA_HEADS)])
    return np.concatenate([nope, rope, _swap16(rope)])


def _kv_up_columns():
    per = QK_NOPE + V_HEAD
    kn = np.concatenate([np.arange(h * per, h * per + QK_NOPE) for h in range(MLA_HEADS)])
    vv = np.concatenate([np.arange(h * per + QK_NOPE, (h + 1) * per) for h in range(MLA_HEADS)])
    return np.concatenate([kn, vv])


def _rope_tables(T, rt):
    pos = jnp.arange(T)
    inv_freq = ROPE_THETA ** (-jnp.arange(ROPE_FREQS, dtype=F32) / ROPE_FREQS)
    ang_r = (pos // GRID_W).astype(F32)[:, None] * inv_freq
    ang_c = (pos % GRID_W).astype(F32)[:, None] * inv_freq
    cos64 = jnp.concatenate([jnp.cos(ang_r), jnp.cos(ang_r), jnp.cos(ang_c), jnp.cos(ang_c)], axis=1)
    sin64 = jnp.concatenate([-jnp.sin(ang_r), jnp.sin(ang_r), -jnp.sin(ang_c), jnp.sin(ang_c)], axis=1)
    cos_t = jnp.concatenate([jnp.tile(cos64, (1, 2)), jnp.ones((rt, LANE), F32)], axis=0)
    sin_t = jnp.concatenate([jnp.tile(sin64, (1, 2)), jnp.zeros((rt, LANE), F32)], axis=0)
    return cos_t, sin_t


def _moe_tables(cnt, n_tiles, tok_tile):
    tm = MOE_TILE
    seg_rows = TOP_K * tok_tile + N_EXPERTS * SEG_ALIGN
    n_work = n_tiles * seg_rows // tm + N_EXPERTS
    counts = cnt[:n_tiles, 0, :N_EXPERTS].astype(jnp.int32)
    padded = (counts + SEG_ALIGN - 1) // SEG_ALIGN * SEG_ALIGN
    in_tile = jnp.cumsum(padded, axis=1) - padded
    src_row = (jnp.arange(n_tiles, dtype=jnp.int32)[:, None] * seg_rows + in_tile).T
    seg_start = jnp.concatenate([jnp.zeros((N_EXPERTS, 1), jnp.int32), jnp.cumsum(padded.T, axis=1)], axis=1)
    group_len = seg_start[:, -1]
    work_end = jnp.cumsum((group_len + tm - 1) // tm)
    work = jnp.arange(n_work, dtype=jnp.int32)
    tile_expert = jnp.minimum(jnp.sum(work_end[None, :] <= work[:, None], axis=1), N_EXPERTS - 1)
    first_work = jnp.concatenate([jnp.zeros((1,), jnp.int32), work_end[:-1]])
    tile_row0 = (work - first_work[tile_expert]) * tm
    used = jnp.clip(group_len[tile_expert] - tile_row0, 0, tm)
    starts = seg_start[tile_expert]
    t_lo = jnp.sum(starts[:, 1:] <= tile_row0[:, None], axis=1)
    t_hi = jnp.sum(starts[:, :-1] < (tile_row0 + tm)[:, None], axis=1)
    i32 = lambda a: a.astype(jnp.int32)
    return (i32(tile_expert), i32(tile_row0), i32(work_end[-1:]), i32(seg_start.reshape(-1)),
            i32(src_row.reshape(-1)), i32(used), i32(t_lo), i32(t_hi))


def _layer(xa, mods, B, T, Tc, rt, tm, update_ctx, p, cos_t, sin_t, layer):
    rows = xa.shape[0]
    n_lat = B * T
    lat_rt, all_rt = n_lat // rt, rows // rt
    lat_tm, all_tm = n_lat // tm, rows // tm
    cond_rt = np.concatenate([np.arange(lat_rt) // (T // rt), np.full((all_rt - lat_rt,), B)])
    cond_tm = np.concatenate([np.arange(lat_tm) // (T // tm), np.full((all_tm - lat_tm,), B)])
    modt_rt = mods[cond_rt][:, None, :]
    modt_tm = mods[cond_tm][:, None, :]
    g_norm = p["g_norm"].reshape(4, 1, D_MODEL)

    z, zs = _inproj(xa, g_norm, modt_rt, p["w_in_main"], p["w_in_special"], layer, rt)
    q, k, v = _mla_project(z, zs, p["g_q_lora"], p["g_kv_lora"], p["w_q_up"], p["w_kv_up"],
                           cos_t, sin_t, rt, lat_rt, T // rt)
    a, w_up_c, w_down_c = _attention_latent(q, k, v, p["w_up"], p["w_down"], layer, p["depth"],
                                            B, T, Tc, min(ATTN_Q_TILE, T), min(ATTN_KV_CHUNK, T),
                                            rows if update_ctx else n_lat)
    w_up_c = w_up_c.reshape(N_EXPERTS, D_MODEL, 2 * D_FF)
    w_down_c = w_down_c.reshape(N_EXPERTS, D_FF, D_MODEL)
    if update_ctx:
        a = _attention_ctx(q, k, v, a, B, T, Tc)
    m = _mlstm(z, zs, p["b_if"], p["g_mlstm"], B, T, Tc)
    n_rt = all_rt if update_ctx else lat_rt
    n_tm = all_tm if update_ctx else lat_tm
    t = _merge(a, m, z, p["w_proj_a"], p["w_proj_b"], p["w_proj_c"], p["w_conv"], p["b_conv"],
               rt, n_rt, lat_rt, T // rt, T, Tc)
    xa, h2, ri, rw, cnt = _post(t, p["w_o"], xa, g_norm, modt_tm, p["w_router"], p["b_router"], tm, n_tm)
    xs = _sort_rows(h2, ri, tm, n_tm)
    ys = _experts(_moe_tables(cnt, n_tm, tm), xs, w_up_c, p["b_up"], w_down_c, p["b_down"], n_tm)
    return _combine(ys, ri, rw, cnt, xa, g_norm, modt_tm, tm, n_tm)


def kernel(x, c, ctx, c_ctx, w_mod, b_mod, g_norm, w_in, g_q_lora, g_kv_lora, w_q_up, w_kv_up, b_if, g_mlstm,
           w_conv, b_conv, w_proj_a, w_proj_b, w_proj_c, w_o, w_router, b_router, w_up, b_up, w_down, b_down):
    B, T, _ = x.shape
    Tc = ctx.shape[1]
    depth = w_mod.shape[0]
    rt = min(1024, T)
    tm = min(512, T)
    assert T % rt == 0 and (B * Tc) % rt == 0 and (B * Tc) % tm == 0
    assert T & (T - 1) == 0 and Tc & (Tc - 1) == 0 and T % GRID_W == 0

    xa = jnp.concatenate([x.reshape(B * T, D_MODEL), ctx.reshape(B * Tc, D_MODEL)], axis=0)
    cond = jnp.zeros((8, D_MODEL), F32).at[:B].set(c).at[B].set(c_ctx)
    cos_t, sin_t = _rope_tables(T, rt)
    cast = lambda w: w.astype(MXU_DTYPE)
    in_cols = _in_proj_columns()
    n_main = N_MAIN_TILES * COL_TILE
    w_in_main = _take_cols(w_in, in_cols[:n_main], MXU_DTYPE)
    w_in_special = _take_cols(w_in, in_cols[n_main:], MXU_DTYPE)
    w_q_p = _take_cols(w_q_up, _q_up_columns(), MXU_DTYPE)
    w_kv_p = _take_cols(w_kv_up, _kv_up_columns(), MXU_DTYPE)
    w_up_r = w_up.reshape(depth * N_EXPERTS * D_MODEL, 2 * D_FF)
    w_down_r = w_down.reshape(depth * N_EXPERTS * D_FF, D_MODEL)
    wa_c, wb_c, wc_c, wo_c = cast(w_proj_a), cast(w_proj_b), cast(w_proj_c), cast(w_o)
    w_router_p = jnp.pad(w_router, ((0, 0), (0, 0), (0, LANE - N_EXPERTS)))
    b_router_p = jnp.pad(b_router, ((0, 0), (0, LANE - N_EXPERTS)), constant_values=NEG_BIG)
    b_if_p = jnp.pad(b_if.reshape(depth, 1, -1), ((0, 0), (0, 0), (0, LANE - 4 * M_HEADS)))
    mods = _adaln(cond, w_mod, b_mod)
    for l in range(depth):
        p = dict(
            g_norm=g_norm[l], w_in_main=w_in_main, w_in_special=w_in_special,
            g_q_lora=g_q_lora[l], g_kv_lora=g_kv_lora[l],
            w_q_up=w_q_p[l], w_kv_up=w_kv_p[l], b_if=b_if_p[l],
            g_mlstm=g_mlstm[l], w_conv=w_conv[l], b_conv=b_conv[l],
            w_proj_a=wa_c[l], w_proj_b=wb_c[l], w_proj_c=wc_c[l], w_o=wo_c[l],
            w_router=w_router_p[l], b_router=b_router_p[l],
            depth=depth, w_up=w_up_r, b_up=b_up[l].reshape(N_EXPERTS, 1, -1),
            w_down=w_down_r, b_down=b_down[l].reshape(N_EXPERTS, 1, -1),
        )
        xa = _layer(xa, mods[l], B, T, Tc, rt, tm, l < depth - 1, p, cos_t, sin_t, l)
    return xa.reshape(B, T, D_MODEL)
```

```python
import functools

import numpy as np
import jax
import jax.numpy as jnp
from jax import lax
from jax.experimental import pallas as pl
from jax.experimental.pallas import tpu as pltpu

F32 = jnp.float32
MXU_DTYPE = jnp.bfloat16

D_MODEL = 2048
GRID_W = 64
MLA_HEADS = 8
QK_NOPE = 128
QK_ROPE = 64
V_HEAD = 128
Q_LORA = 512
KV_LORA = 512
ROPE_THETA = 10000.0
ROPE_FREQS = QK_ROPE // 4
ATTN_SCALE = (QK_NOPE + QK_ROPE) ** -0.5
M_HEADS = 4
M_QK = 64
M_V = 128
CONV_WIDTH = 512
N_EXPERTS = 32
TOP_K = 4
D_FF = 1024
SWIGLU_ALPHA = 1.702
SWIGLU_LIMIT = 7.0
N_MOD = 6
EPS = 1e-6

LANE = 128
HEAD_W = 2 * LANE
MLSTM_CHUNK = 256
COL_TILE = 512
INPROJ_TILE = 1024
LOG2_E = 1.4426950408889634
MERGE_TILE = 1024
ATTN_Q_TILE = 1024
ATTN_KV_CHUNK = 512
MOE_TILE = 512
SEG_ALIGN = 16
NEG_BIG = -1e30
VMEM_LIMIT = 56 * 1024 * 1024

ZT_CQ, ZT_CKV, ZT_MQK, ZT_MV, ZT_MO, ZT_CU, ZT_CB, ZT_CC, ZT_CG = 0, 1, 2, 3, 4, 5, 6, 7, 8
N_MAIN_TILES = 8 + 3 * D_MODEL // COL_TILE
assert (N_MAIN_TILES * COL_TILE) % INPROJ_TILE == 0


def _cparams(n_axes, vmem=VMEM_LIMIT):
    return pltpu.CompilerParams(dimension_semantics=("arbitrary",) * n_axes, vmem_limit_bytes=vmem)


def _rms(x, g):
    return x * lax.rsqrt(jnp.mean(x * x, axis=-1, keepdims=True) + EPS) * g


def _sigmoid(x):
    return 0.5 * jnp.tanh(0.5 * x) + 0.5


def _lane_pick(x, idx):
    lane = lax.broadcasted_iota(jnp.int32, x.shape, 1)
    return jnp.sum(jnp.where(lane == idx, x, 0.0), axis=1, keepdims=True)


def _mod_kernel(c_ref, w_ref, b_ref, o_ref):
    c = c_ref[...]
    s = c * jax.nn.sigmoid(c)
    o_ref[0] = jnp.dot(s.astype(MXU_DTYPE), w_ref[0].astype(MXU_DTYPE),
                       preferred_element_type=F32) + b_ref[0]


def _adaln(cond, w_mod, b_mod):
    depth, _, n = w_mod.shape
    tn = 1024
    return pl.pallas_call(
        _mod_kernel,
        grid=(depth, n // tn),
        in_specs=[pl.BlockSpec(cond.shape, lambda l, j: (0, 0)),
                  pl.BlockSpec((1, D_MODEL, tn), lambda l, j: (l, 0, j)),
                  pl.BlockSpec((1, 1, tn), lambda l, j: (l, 0, j))],
        out_specs=pl.BlockSpec((1, cond.shape[0], tn), lambda l, j: (l, 0, j)),
        out_shape=jax.ShapeDtypeStruct((depth, cond.shape[0], n), F32),
        compiler_params=_cparams(2),
        name="adaln",
    )(cond, w_mod, b_mod.reshape(depth, 1, n))


def _row_sources(parts, tile):
    first, second = parts
    n_first = first.shape[0] // tile
    if second is None:
        second, second_map = first, (lambda i, *_: (0, 0))
    else:
        second_map = lambda i, *_: (jnp.maximum(i - n_first, 0), 0)
    width = first.shape[1]
    specs = [pl.BlockSpec((tile, width), lambda i, *_: (jnp.minimum(i, n_first - 1), 0)),
             pl.BlockSpec((tile, width), second_map, pipeline_mode=pl.Buffered(1))]
    return (first, second), specs, n_first


def _inproj_kernel(x_ref, g_ref, sh_ref, sc_ref, w_ref, ws_ref, z_ref, zs_ref, h_scr):
    @pl.when(pl.program_id(1) == 0)
    def _():
        h = _rms(x_ref[...], g_ref[0]) * (1.0 + sc_ref[0]) + sh_ref[0]
        h_scr[...] = h.astype(h_scr.dtype)
        zs_ref[...] = jnp.dot(h_scr[...], ws_ref[0], preferred_element_type=F32)

    z_ref[...] = jnp.dot(h_scr[...], w_ref[0], preferred_element_type=F32).astype(z_ref.dtype)


def _inproj(xa, g_norm, modt, w_main, w_special, layer, rt):
    rows = xa.shape[0]
    n_main = w_main.shape[2]
    tn = INPROJ_TILE
    return pl.pallas_call(
        _inproj_kernel,
        grid=(rows // rt, n_main // tn),
        in_specs=[pl.BlockSpec((rt, D_MODEL), lambda i, j: (i, 0)),
                  pl.BlockSpec((1, 1, D_MODEL), lambda i, j: (0, 0, 0)),
                  pl.BlockSpec((1, 1, D_MODEL), lambda i, j: (i, 0, 0)),
                  pl.BlockSpec((1, 1, D_MODEL), lambda i, j: (i, 0, 1)),
                  pl.BlockSpec((1, D_MODEL, tn), lambda i, j: (layer, 0, j)),
                  pl.BlockSpec((1, D_MODEL, COL_TILE), lambda i, j: (layer, 0, 0))],
        out_specs=[pl.BlockSpec((rt, tn), lambda i, j: (i, j)),
                   pl.BlockSpec((rt, COL_TILE), lambda i, j: (i, 0))],
        out_shape=[jax.ShapeDtypeStruct((rows, n_main), MXU_DTYPE),
                   jax.ShapeDtypeStruct((rows, COL_TILE), F32)],
        scratch_shapes=[pltpu.VMEM((rt, D_MODEL), MXU_DTYPE)],
        compiler_params=_cparams(2),
        name="inproj",
    )(xa, g_norm, modt, modt, w_main, w_special)


def _mla_kernel(cq_ref, ckv_ref, zs_ref, gq_ref, gkv_ref, wq_ref, wkv_ref, cos_ref, sin_ref,
                q_ref, k_ref, v_ref):
    def norm(z_ref, g_ref):
        return _rms(z_ref[...].astype(F32), g_ref[...]).astype(MXU_DTYPE)

    qf = jnp.dot(norm(cq_ref, gq_ref), wq_ref[...], preferred_element_type=F32) * (ATTN_SCALE * LOG2_E)
    kvf = jnp.dot(norm(ckv_ref, gkv_ref), wkv_ref[...], preferred_element_type=F32)
    cos = cos_ref[...]
    sin = sin_ref[...]
    nope_w = MLA_HEADS * QK_NOPE
    rope_w = MLA_HEADS * QK_ROPE
    for p in range(MLA_HEADS // 2):
        a = nope_w + LANE * p
        b = nope_w + rope_w + LANE * p
        rp = (qf[:, a:a + LANE] * cos + qf[:, b:b + LANE] * sin).astype(MXU_DTYPE)
        for h in (2 * p, 2 * p + 1):
            q_ref[:, HEAD_W *h:HEAD_W *h + LANE] = qf[:, LANE * h:LANE * (h + 1)].astype(MXU_DTYPE)
            q_ref[:, HEAD_W *h + LANE:HEAD_W *(h + 1)] = rp
    kr2 = zs_ref[:, 0:LANE] * cos + zs_ref[:, LANE:2 * LANE] * sin
    lane = lax.broadcasted_iota(jnp.int32, kr2.shape, 1)
    k_lo = jnp.where(lane < QK_ROPE, kr2, 0.0).astype(MXU_DTYPE)
    k_hi = jnp.where(lane >= QK_ROPE, kr2, 0.0).astype(MXU_DTYPE)
    for h in range(MLA_HEADS):
        k_ref[:, HEAD_W *h:HEAD_W *h + LANE] = kvf[:, LANE * h:LANE * (h + 1)].astype(MXU_DTYPE)
        k_ref[:, HEAD_W *h + LANE:HEAD_W *(h + 1)] = k_lo if h % 2 == 0 else k_hi
    ones = jnp.ones((kvf.shape[0], LANE), MXU_DTYPE)
    v0 = MLA_HEADS * QK_NOPE
    for h in range(MLA_HEADS):
        v_ref[:, HEAD_W *h:HEAD_W *h + LANE] = kvf[:, v0 + LANE * h:v0 + LANE * (h + 1)].astype(MXU_DTYPE)
        v_ref[:, HEAD_W *h + LANE:HEAD_W *(h + 1)] = ones


def _mla_project(z, zs, g_q, g_kv, wq_p, wkv_p, cos_t, sin_t, rt, n_lat_tiles, tiles_per_seq):
    rows = z.shape[0]
    nt = rows // rt

    def tab(i):
        return (jnp.where(i < n_lat_tiles, i % tiles_per_seq, tiles_per_seq), 0)

    return pl.pallas_call(
        _mla_kernel,
        grid=(nt,),
        in_specs=[pl.BlockSpec((rt, COL_TILE), lambda i: (i, ZT_CQ)),
                  pl.BlockSpec((rt, COL_TILE), lambda i: (i, ZT_CKV)),
                  pl.BlockSpec((rt, COL_TILE), lambda i: (i, 0)),
                  pl.BlockSpec((1, Q_LORA), lambda i: (0, 0)),
                  pl.BlockSpec((1, KV_LORA), lambda i: (0, 0)),
                  pl.BlockSpec(wq_p.shape, lambda i: (0, 0)),
                  pl.BlockSpec(wkv_p.shape, lambda i: (0, 0)),
                  pl.BlockSpec((rt, LANE), tab),
                  pl.BlockSpec((rt, LANE), tab)],
        out_specs=[pl.BlockSpec((rt, HEAD_W *MLA_HEADS), lambda i: (i, 0)),
                   pl.BlockSpec((rt, HEAD_W *MLA_HEADS), lambda i: (i, 0)),
                   pl.BlockSpec((rt, HEAD_W *MLA_HEADS), lambda i: (i, 0))],
        out_shape=[jax.ShapeDtypeStruct((rows, HEAD_W *MLA_HEADS), MXU_DTYPE)] * 3,
        compiler_params=_cparams(1),
        name="mla_project",
    )(z, z, zs, g_q.reshape(1, -1), g_kv.reshape(1, -1), wq_p, wkv_p, cos_t, sin_t)


def _attn_kernel(*refs, kv_chunks, n_cast=0):
    q_ref = refs[0]
    o_ref = refs[len(refs) - 1 - n_cast]
    for c in range(n_cast):
        src = refs[1 + 2 * len(kv_chunks) + c]
        dst = refs[len(refs) - n_cast + c]
        dst[...] = src[...].astype(dst.dtype)
    q = q_ref[...]
    tq = q.shape[0]
    m = jnp.full((tq, 1), -jnp.inf, F32)
    acc = jnp.zeros((tq, 2 * V_HEAD), F32)
    for s, n_chunks in enumerate(kv_chunks):
        k_ref, v_ref = refs[1 + 2 * s], refs[2 + 2 * s]
        ck = k_ref.shape[0] // n_chunks
        for c in range(n_chunks):
            k = k_ref[c * ck:(c + 1) * ck, :]
            v = v_ref[c * ck:(c + 1) * ck, :]
            sc = lax.dot_general(q, k, (((1,), (1,)), ((), ())), preferred_element_type=F32)
            m_new = jnp.maximum(m, jnp.max(sc, axis=-1, keepdims=True))
            p = jnp.exp2(sc - m_new).astype(MXU_DTYPE)
            acc = jnp.exp2(m - m_new) * acc + jnp.dot(p, v, preferred_element_type=F32)
            m = m_new
    o_ref[...] = (acc[:, :V_HEAD] / acc[:, V_HEAD:]).astype(o_ref.dtype)


def _attention_latent(q, k, v, w_up, w_down, layer, depth, B, T, Tc, tq, ck, rows):
    qt = T // tq
    ctx_blk0 = B * T // Tc
    steps = B * MLA_HEADS * qt
    step = lambda b, h, i: (b * MLA_HEADS + h) * qt + i
    cast_specs_in, cast_specs_out, cast_shapes = [], [], []
    for w in (w_up, w_down):
        n = w.shape[1]
        layer_rows = w.shape[0] // depth
        slab = layer_rows // steps
        assert slab * steps == layer_rows and slab % 16 == 0
        cast_specs_in.append(pl.BlockSpec((slab, n), lambda b, h, i: (layer * steps + step(b, h, i), 0)))
        cast_specs_out.append(pl.BlockSpec((slab, n), lambda b, h, i: (step(b, h, i), 0)))
        cast_shapes.append(jax.ShapeDtypeStruct((layer_rows, n), MXU_DTYPE))
    return pl.pallas_call(
        functools.partial(_attn_kernel, kv_chunks=(T // ck, 1), n_cast=2),
        grid=(B, MLA_HEADS, qt),
        in_specs=[pl.BlockSpec((tq, HEAD_W), lambda b, h, i: (b * qt + i, h)),
                  pl.BlockSpec((T, HEAD_W), lambda b, h, i: (b, h)),
                  pl.BlockSpec((T, HEAD_W), lambda b, h, i: (b, h)),
                  pl.BlockSpec((Tc, HEAD_W), lambda b, h, i: (ctx_blk0 + b, h)),
                  pl.BlockSpec((Tc, HEAD_W), lambda b, h, i: (ctx_blk0 + b, h))] + cast_specs_in,
        out_specs=[pl.BlockSpec((tq, V_HEAD), lambda b, h, i: (b * qt + i, h))] + cast_specs_out,
        out_shape=[jax.ShapeDtypeStruct((rows, V_HEAD * MLA_HEADS), MXU_DTYPE)] + cast_shapes,
        compiler_params=_cparams(3),
        name="attn_latent",
    )(q, k, v, k, v, w_up, w_down)


def _attention_ctx(q, k, v, a_buf, B, T, Tc):
    ctx_blk0 = B * T // Tc
    return pl.pallas_call(
        functools.partial(_attn_kernel, kv_chunks=(1,)),
        grid=(B, MLA_HEADS),
        in_specs=[pl.BlockSpec((Tc, HEAD_W), lambda b, h: (ctx_blk0 + b, h)),
                  pl.BlockSpec((Tc, HEAD_W), lambda b, h: (ctx_blk0 + b, h)),
                  pl.BlockSpec((Tc, HEAD_W), lambda b, h: (ctx_blk0 + b, h)),
                  pl.BlockSpec(memory_space=pl.ANY)],
        out_specs=pl.BlockSpec((Tc, V_HEAD), lambda b, h: (ctx_blk0 + b, h)),
        out_shape=jax.ShapeDtypeStruct(a_buf.shape, a_buf.dtype),
        input_output_aliases={3: 0},
        compiler_params=_cparams(2),
        name="attn_ctx",
    )(q, k, v, a_buf)


def _mlstm_chunk(qk, v, g, lf_cum, gt_ref, head, reverse, ct_ref, m_prev, consts):
    tri, row_iota, mk, ones_tile, lane = consts
    L = qk.shape[0]
    base = 8 if reverse else 0
    col_i, col_f = base + head, base + 4 + head
    ql = jnp.where(lane < M_QK, qk * (M_QK ** -0.5), jnp.zeros_like(qk))
    kl = jnp.dot(qk, mk, preferred_element_type=F32).astype(MXU_DTYPE)
    v_aug = jnp.concatenate([v, ones_tile], axis=1)
    bc = _lane_pick(lf_cum, col_f)
    li = _lane_pick(g, col_i)
    r_row = gt_ref[0, col_i:col_i + 1, :] - gt_ref[1, col_f:col_f + 1, :]
    dlog = jnp.where(tri, bc + r_row, -jnp.inf)
    inter = bc + m_prev
    m_t = jnp.maximum(inter, jnp.max(dlog, axis=1, keepdims=True))
    dexp = jnp.exp(dlog - m_t)
    s_raw = lax.dot_general(ql, kl, (((1,), (1,)), ((), ())), preferred_element_type=F32)
    s_mat = (s_raw * dexp).astype(MXU_DTYPE)
    a = jnp.exp(inter - m_t)
    ct = ct_ref[...]
    nd = (jnp.dot(s_mat, v_aug, preferred_element_type=F32)
          + a * jnp.dot(ql, ct.astype(MXU_DTYPE), preferred_element_type=F32))
    num, den = nd[:, :M_V], nd[:, M_V:]
    h_out = num / jnp.maximum(jnp.abs(den), jnp.exp(-m_t))
    last = 0 if reverse else L - 1
    total = jnp.sum(jnp.where(row_iota == last, bc, 0.0), axis=0, keepdims=True)
    w_col = total - bc + li
    m_new = jnp.maximum(total + m_prev, jnp.max(w_col, axis=0, keepdims=True))
    decay = jnp.exp(total + m_prev - m_new)
    e = jnp.exp(w_col - m_new)
    ek = (kl.astype(F32) * e).astype(MXU_DTYPE)
    upd = lax.dot_general(ek, v_aug, (((0,), (0,)), ((), ())), preferred_element_type=F32)
    ct_ref[...] = decay * ct + upd
    return h_out, m_new


def _mlstm_kernel(zl_qk, zl_v, zl_mo, gl_ref, zc_qk, zc_v, zc_mo, gc_ref, bif_ref, gm_ref,
                  ol_ref, oc_ref, acc_l, acc_c, ct_scr, gt_scr):
    L = MLSTM_CHUNK
    T = zl_qk.shape[0]
    nc = T // L
    r = lax.broadcasted_iota(jnp.int32, (L, L), 0)
    c = lax.broadcasted_iota(jnp.int32, (L, L), 1)
    tri_f = c <= r
    tri_b = c >= r
    tmat_f = tri_f.astype(jnp.bfloat16)
    tmat_b = tri_b.astype(jnp.bfloat16)
    row_iota = lax.broadcasted_iota(jnp.int32, (L, 1), 0)
    lane = lax.broadcasted_iota(jnp.int32, (L, LANE), 1)
    rr = lax.broadcasted_iota(jnp.int32, (LANE, LANE), 0)
    cc = lax.broadcasted_iota(jnp.int32, (LANE, LANE), 1)
    mk = (rr == cc + M_QK).astype(MXU_DTYPE)
    ones_tile = jnp.ones((L, LANE), MXU_DTYPE)
    consts_f = (tri_f, row_iota, mk, ones_tile, lane)
    consts_b = (tri_b, row_iota, mk, ones_tile, lane)
    bias = bif_ref[...]

    def gate_prep(g_raw, tmat, slot):
        g = g_raw + bias
        log_sig = jnp.minimum(g, 0.0) - jnp.log1p(jnp.exp(-jnp.abs(g)))
        hi = log_sig.astype(jnp.bfloat16)
        rest = log_sig - hi.astype(F32)
        mid = rest.astype(jnp.bfloat16)
        low = (rest - mid.astype(F32)).astype(jnp.bfloat16)
        sums = jnp.dot(tmat, jnp.concatenate([hi, mid, low], axis=1), preferred_element_type=F32)
        cum = sums[:, :LANE] + sums[:, LANE:2 * LANE] + sums[:, 2 * LANE:]
        gt_scr[slot, 0] = g.T
        gt_scr[slot, 1] = cum.T
        return g, cum

    def step(qk_ref, v_ref, g_ref, acc_ref, row_f, row_b, ms):
        g_f, cum_f = gate_prep(g_ref[pl.ds(row_f, L), :], tmat_f, 0)
        g_b, cum_b = gate_prep(g_ref[pl.ds(row_b, L), :], tmat_b, 1)
        new_ms = []
        for h in range(M_HEADS):
            cols = slice(h * LANE, (h + 1) * LANE)
            hf, mf = _mlstm_chunk(qk_ref[pl.ds(row_f, L), cols], v_ref[pl.ds(row_f, L), cols],
                                  g_f, cum_f, gt_scr.at[0], h, False,
                                  ct_scr.at[h], ms[h], consts_f)
            acc_ref[pl.ds(row_f, L), cols] += hf
            hb, mb = _mlstm_chunk(qk_ref[pl.ds(row_b, L), cols], v_ref[pl.ds(row_b, L), cols],
                                  g_b, cum_b, gt_scr.at[1], h, True,
                                  ct_scr.at[M_HEADS + h], ms[M_HEADS + h], consts_b)
            acc_ref[pl.ds(row_b, L), cols] += hb
            new_ms.append((mf, mb))
        return tuple(x[0] for x in new_ms) + tuple(x[1] for x in new_ms)

    ct_scr[...] = jnp.zeros_like(ct_scr)
    acc_l[...] = jnp.zeros_like(acc_l)
    acc_c[...] = jnp.zeros_like(acc_c)
    zero = jnp.zeros((1, 1), F32)
    ms = step(zc_qk, zc_v, gc_ref, acc_c, 0, 0, (zero,) * (2 * M_HEADS))

    def body(i, ms):
        row_f = pl.multiple_of(i * L, L)
        row_b = pl.multiple_of((nc - 1 - i) * L, L)
        return step(zl_qk, zl_v, gl_ref, acc_l, row_f, row_b, ms)

    lax.fori_loop(0, nc, body, ms)

    gm = gm_ref[...]

    def finish(acc_ref, mo_ref, o_ref, row):
        for h in range(M_HEADS):
            cols = slice(h * LANE, (h + 1) * LANE)
            hn = _rms(acc_ref[pl.ds(row, L), cols], gm[:, cols])
            gate = jax.nn.sigmoid(mo_ref[pl.ds(row, L), cols].astype(F32))
            o_ref[pl.ds(row, L), cols] = (hn * gate).astype(o_ref.dtype)

    finish(acc_c, zc_mo, oc_ref, 0)

    def fin_body(i, carry):
        finish(acc_l, zl_mo, ol_ref, pl.multiple_of(i * L, L))
        return carry

    lax.fori_loop(0, nc, fin_body, 0)


def _mlstm(z, zs, bif_row, g_mlstm, B, T, Tc):
    assert Tc == MLSTM_CHUNK and T % MLSTM_CHUNK == 0
    rows = z.shape[0]
    ctx_blk0 = B * T // Tc
    w = M_HEADS * LANE
    out_l, out_c = pl.pallas_call(
        _mlstm_kernel,
        grid=(B,),
        in_specs=[pl.BlockSpec((T, w), lambda b: (b, ZT_MQK)),
                  pl.BlockSpec((T, w), lambda b: (b, ZT_MV)),
                  pl.BlockSpec((T, w), lambda b: (b, ZT_MO)),
                  pl.BlockSpec((T, LANE), lambda b: (b, 2)),
                  pl.BlockSpec((Tc, w), lambda b: (ctx_blk0 + b, ZT_MQK)),
                  pl.BlockSpec((Tc, w), lambda b: (ctx_blk0 + b, ZT_MV)),
                  pl.BlockSpec((Tc, w), lambda b: (ctx_blk0 + b, ZT_MO)),
                  pl.BlockSpec((Tc, LANE), lambda b: (ctx_blk0 + b, 2)),
                  pl.BlockSpec((1, LANE), lambda b: (0, 0)),
                  pl.BlockSpec((1, w), lambda b: (0, 0))],
        out_specs=[pl.BlockSpec((T, w), lambda b: (b, 0)),
                   pl.BlockSpec((Tc, w), lambda b: (b, 0))],
        out_shape=[jax.ShapeDtypeStruct((B * T, w), MXU_DTYPE),
                   jax.ShapeDtypeStruct((B * Tc, w), MXU_DTYPE)],
        scratch_shapes=[pltpu.VMEM((T, w), F32), pltpu.VMEM((Tc, w), F32),
                        pltpu.VMEM((2 * M_HEADS, LANE, 2 * LANE), F32),
                        pltpu.VMEM((2, 2, LANE, MLSTM_CHUNK), F32)],
        compiler_params=_cparams(1),
        name="mlstm",
    )(z, z, z, zs, z, z, z, zs, bif_row, g_mlstm.reshape(1, -1))
    return out_l, out_c


def _merge_kernel(ml_ref, mc_ref, a_ref, cu_ref, cb_ref, cc_ref, cu_p, cc_p, cu_n, cc_n, ga_ref, gm_ref, gs_ref,
                  wa_ref, wb_ref, wc_ref, wconv_ref, bconv_ref, t_ref, s_scr,
                  *, n_lat_tiles, tiles_per_seq, seq_len, ctx_len):
    i = pl.program_id(0)
    j = pl.program_id(1)
    rt = a_ref.shape[0]
    m = jnp.where(i < n_lat_tiles, ml_ref[...], mc_ref[...])

    @pl.when(j == 0)
    def _():
        u = cc_ref[...].astype(F32) * cu_ref[...].astype(F32)
        hp = cc_p[...].astype(F32) * cu_p[...].astype(F32)
        hn = cc_n[...].astype(F32) * cu_n[...].astype(F32)
        row = lax.broadcasted_iota(jnp.int32, (rt, 1), 0)
        is_ctx = i >= n_lat_tiles
        seg = jnp.where(is_ctx, ctx_len, seq_len)
        off = jnp.where(is_ctx, (i - n_lat_tiles) * rt, (i % tiles_per_seq) * rt)
        pos = (off + row) & (seg - 1)
        up = pltpu.roll(u, 1, 0)
        up = jnp.where(row == 0, hp[hp.shape[0] - 1:, :], up)
        up = jnp.where(pos == 0, 0.0, up)
        un = pltpu.roll(u, rt - 1, 0)
        un = jnp.where(row == rt - 1, hn[0:1, :], un)
        un = jnp.where(pos == seg - 1, 0.0, un)
        wconv = wconv_ref[...]
        y = up * wconv[0:1, :] + u * wconv[1:2, :] + un * wconv[2:3, :] + bconv_ref[...]
        s_scr[...] = (cb_ref[...].astype(F32) * y).astype(s_scr.dtype)

    def sig(ref):
        return _sigmoid(ref[...].astype(F32))

    t = (sig(ga_ref) * jnp.dot(a_ref[...], wa_ref[...], preferred_element_type=F32)
         + sig(gm_ref) * jnp.dot(m, wb_ref[...], preferred_element_type=F32)
         + sig(gs_ref) * jnp.dot(s_scr[...], wc_ref[...], preferred_element_type=F32))
    t_ref[...] = t.astype(t_ref.dtype)


def _merge(a, m, z, wa, wb, wc, w_conv, b_conv, rt, n_tiles, n_lat_tiles, tiles_per_seq, T, Tc):
    rows = n_tiles * rt
    hb = 16
    rb = rt // hb
    last_hb = z.shape[0] // hb - 1
    tn = MERGE_TILE
    nj = D_MODEL // tn
    gt = D_MODEL // tn
    g0 = ZT_CG * COL_TILE // tn
    assert (ZT_CG * COL_TILE) % tn == 0

    def prev_blk(col):
        return lambda i, j: (jnp.maximum(i * rb - 1, 0), col)

    def next_blk(col):
        return lambda i, j: (jnp.minimum((i + 1) * rb, last_hb), col)

    kern = functools.partial(_merge_kernel, n_lat_tiles=n_lat_tiles, tiles_per_seq=tiles_per_seq,
                             seq_len=T, ctx_len=Tc)
    m_arrays, m_specs, m_first = _row_sources(m, rt)
    assert m_first == n_lat_tiles
    return pl.pallas_call(
        kern,
        grid=(n_tiles, nj),
        in_specs=m_specs + [pl.BlockSpec((rt, a.shape[1]), lambda i, j: (i, 0)),
                  pl.BlockSpec((rt, COL_TILE), lambda i, j: (i, ZT_CU)),
                  pl.BlockSpec((rt, COL_TILE), lambda i, j: (i, ZT_CB)),
                  pl.BlockSpec((rt, COL_TILE), lambda i, j: (i, ZT_CC)),
                  pl.BlockSpec((hb, COL_TILE), prev_blk(ZT_CU)),
                  pl.BlockSpec((hb, COL_TILE), prev_blk(ZT_CC)),
                  pl.BlockSpec((hb, COL_TILE), next_blk(ZT_CU)),
                  pl.BlockSpec((hb, COL_TILE), next_blk(ZT_CC)),
                  pl.BlockSpec((rt, tn), lambda i, j: (i, g0 + j)),
                  pl.BlockSpec((rt, tn), lambda i, j: (i, g0 + gt + j)),
                  pl.BlockSpec((rt, tn), lambda i, j: (i, g0 + 2 * gt + j)),
                  pl.BlockSpec((wa.shape[0], tn), lambda i, j: (0, j)),
                  pl.BlockSpec((wb.shape[0], tn), lambda i, j: (0, j)),
                  pl.BlockSpec((wc.shape[0], tn), lambda i, j: (0, j)),
                  pl.BlockSpec(w_conv.shape, lambda i, j: (0, 0)),
                  pl.BlockSpec((1, CONV_WIDTH), lambda i, j: (0, 0))],
        out_specs=pl.BlockSpec((rt, tn), lambda i, j: (i, j)),
        out_shape=jax.ShapeDtypeStruct((rows, D_MODEL), MXU_DTYPE),
        scratch_shapes=[pltpu.VMEM((rt, CONV_WIDTH), MXU_DTYPE)],
        compiler_params=_cparams(2),
        name="merge",
    )(*m_arrays, a, z, z, z, z, z, z, z, z, z, z, wa, wb, wc, w_conv, b_conv.reshape(1, -1))


def _dot3(x, w):
    xh = x.astype(jnp.bfloat16)
    xl = (x - xh.astype(F32)).astype(jnp.bfloat16)
    wh = w.astype(jnp.bfloat16)
    wl = (w - wh.astype(F32)).astype(jnp.bfloat16)
    n = w.shape[1]
    hi = jnp.dot(xh, jnp.concatenate([wh, wl], axis=1), preferred_element_type=F32)
    return hi[:, :n] + hi[:, n:] + jnp.dot(xl, wh, preferred_element_type=F32)


def _seg_pad(counts):
    return jnp.ceil(counts * (1.0 / SEG_ALIGN)) * SEG_ALIGN


def _exclusive_lane_cumsum(row):
    upper = (lax.broadcasted_iota(jnp.int32, (LANE, LANE), 0)
             < lax.broadcasted_iota(jnp.int32, (LANE, LANE), 1)).astype(jnp.bfloat16)
    out = jnp.dot(jnp.broadcast_to(row, (8, LANE)).astype(jnp.bfloat16), upper, preferred_element_type=F32)
    return out[0:1, :]


def _post_kernel(x_ref, t_ref, wo_ref, g1_ref, g2_ref, gt_ref, sh_ref, sc_ref, wr_ref, br_ref,
                 xo_ref, h2_ref, ri_ref, rw_ref, cnt_ref):
    half = t_ref.shape[0] // 2
    parts = []
    for r in (slice(0, half), slice(half, 2 * half)):
        y = jnp.dot(t_ref[r, :], wo_ref[...], preferred_element_type=F32)
        xn = x_ref[r, :] + gt_ref[0] * _rms(y, g1_ref[0])
        xo_ref[r, :] = xn
        h2 = _rms(xn, g2_ref[0]) * (1.0 + sc_ref[0]) + sh_ref[0]
        h2_ref[r, :] = h2.astype(h2_ref.dtype)
        parts.append(_dot3(h2, wr_ref[...]) + br_ref[...])
    logits = jnp.concatenate(parts, axis=0)
    lane = lax.broadcasted_iota(jnp.int32, logits.shape, 1)
    lane_f = lane.astype(F32)
    vals, ids = [], []
    for _ in range(TOP_K):
        mx = jnp.max(logits, axis=1, keepdims=True)
        idx = jnp.min(jnp.where(logits == mx, lane_f, float(LANE)), axis=1, keepdims=True)
        logits = jnp.where(lane_f == idx, -jnp.inf, logits)
        vals.append(mx)
        ids.append(idx)
    es = [jnp.exp(v - vals[0]) for v in vals]
    tot = es[0] + es[1] + es[2] + es[3]
    tm = logits.shape[0]
    onehots = [lane_f == ids[k] for k in range(TOP_K)]
    asg = jnp.zeros(logits.shape, F32)
    for k in range(TOP_K):
        asg = jnp.where(onehots[k], 1.0, asg)
    earlier = (lax.broadcasted_iota(jnp.int32, (tm, tm), 1)
               < lax.broadcasted_iota(jnp.int32, (tm, tm), 0)).astype(jnp.bfloat16)
    before = jnp.dot(earlier, asg.astype(jnp.bfloat16), preferred_element_type=F32)
    counts = jnp.sum(asg, axis=0, keepdims=True)
    seg_off = _exclusive_lane_cumsum(_seg_pad(counts))
    rw = jnp.zeros(logits.shape, F32)
    ri = jnp.zeros(logits.shape, F32)
    for k in range(TOP_K):
        pos = jnp.sum(jnp.where(onehots[k], before + seg_off, 0.0), axis=1, keepdims=True)
        rw = jnp.where(lane == k, es[k] / tot, rw)
        ri = jnp.where(lane == k, ids[k], ri)
        ri = jnp.where(lane == TOP_K + k, pos, ri)
    rw_ref[...] = rw
    ri_ref[...] = ri.astype(jnp.int32)
    cnt_ref[0] = jnp.broadcast_to(counts, cnt_ref.shape[1:])


def _post(t, wo, xa, g_norm, modt, wr_p, br_p, tm, n_tiles):
    rows = n_tiles * tm
    row_spec = lambda w: pl.BlockSpec((tm, w), lambda i: (i, 0))
    mod = lambda k: pl.BlockSpec((1, 1, D_MODEL), lambda i: (i, 0, k))
    gn = lambda k: pl.BlockSpec((1, 1, D_MODEL), lambda i: (k, 0, 0))
    return pl.pallas_call(
        _post_kernel,
        grid=(n_tiles,),
        in_specs=[row_spec(D_MODEL), row_spec(D_MODEL), pl.BlockSpec(wo.shape, lambda i: (0, 0)),
                  gn(1), gn(2), mod(2), mod(3), mod(4),
                  pl.BlockSpec(wr_p.shape, lambda i: (0, 0)), pl.BlockSpec((1, LANE), lambda i: (0, 0))],
        out_specs=[row_spec(D_MODEL), row_spec(D_MODEL), row_spec(LANE), row_spec(LANE),
                   pl.BlockSpec((1, 8, LANE), lambda i: (i, 0, 0))],
        out_shape=[jax.ShapeDtypeStruct((rows, D_MODEL), F32),
                   jax.ShapeDtypeStruct((rows, D_MODEL), MXU_DTYPE),
                   jax.ShapeDtypeStruct((rows, LANE), jnp.int32),
                   jax.ShapeDtypeStruct((rows, LANE), F32),
                   jax.ShapeDtypeStruct((rows // tm, 8, LANE), F32)],
        compiler_params=_cparams(1),
        name="post_mix",
    )(xa, t, wo, g_norm, g_norm, modt, modt, modt, wr_p, br_p.reshape(1, LANE))


def _sort_kernel(h2_ref, ri_ref, xs_ref):
    tm = h2_ref.shape[0]
    rif = ri_ref[...].astype(F32)
    pos = [_lane_pick(rif, TOP_K + k) for k in range(TOP_K)]
    h2 = h2_ref[...]
    for c in range(xs_ref.shape[0] // tm):
        col = (lax.broadcasted_iota(jnp.int32, (tm, tm), 1) + c * tm).astype(F32)
        onehot = jnp.zeros((tm, tm), F32)
        for k in range(TOP_K):
            onehot = jnp.where(pos[k] == col, 1.0, onehot)
        rows = lax.dot_general(onehot.astype(MXU_DTYPE), h2, (((0,), (0,)), ((), ())),
                               preferred_element_type=F32)
        xs_ref[c * tm:(c + 1) * tm, :] = rows.astype(xs_ref.dtype)


def _sort_rows(h2, ri, tm, n_tiles):
    seg_rows = TOP_K * tm + N_EXPERTS * SEG_ALIGN
    assert seg_rows % tm == 0
    return pl.pallas_call(
        _sort_kernel,
        grid=(n_tiles,),
        in_specs=[pl.BlockSpec((tm, D_MODEL), lambda i: (i, 0)),
                  pl.BlockSpec((tm, LANE), lambda i: (i, 0))],
        out_specs=pl.BlockSpec((seg_rows, D_MODEL), lambda i: (i, 0)),
        out_shape=jax.ShapeDtypeStruct((n_tiles * seg_rows, D_MODEL), MXU_DTYPE),
        compiler_params=_cparams(1),
        name="sort_rows",
    )(h2, ri)


def _expert_kernel(te_ref, r0_ref, na_ref, seg_ref, src_ref, used_ref, tlo_ref, thi_ref,
                   xs_hbm, wu_ref, bu_ref, wd_ref, bd_ref, ys_hbm,
                   xbuf, ybuf, sem_in, sem_out, *, n_tok_tiles):
    w = pl.program_id(0)
    n_active = na_ref[0]
    tm = ybuf.shape[0]
    nt = n_tok_tiles
    sizes = [s for s in (512, 256, 128, 64, 32, 16) if s <= tm]

    def for_pieces(length, fn):
        off = jnp.int32(0)
        for size in sizes:
            hit = (length & size) != 0

            @pl.when(hit)
            def _(off=off, size=size):
                fn(off, size)

            off = off + jnp.where(hit, size, 0)

    def for_segments(wt, fn):
        e = te_ref[wt]
        r0 = r0_ref[wt]

        def body(t, carry):
            s0 = seg_ref[e * (nt + 1) + t]
            s1 = seg_ref[e * (nt + 1) + t + 1]
            lo = jnp.maximum(s0, r0)
            length = jnp.maximum(jnp.minimum(s1, r0 + tm) - lo, 0)
            src = src_ref[e * nt + t] + (lo - s0)
            dst = lo - r0
            for_pieces(length, lambda off, size: fn(src + off, dst + off, size))
            return carry

        lax.fori_loop(tlo_ref[wt], thi_ref[wt], body, 0)

    def rows_in_use(wt):
        return used_ref[wt]

    def aligned(row):
        return row if isinstance(row, int) else pl.multiple_of(row, SEG_ALIGN)

    def in_copy(hbm_row, buf_row, size, slot):
        return pltpu.make_async_copy(xs_hbm.at[pl.ds(aligned(hbm_row), size)],
                                     xbuf.at[slot, pl.ds(aligned(buf_row), size)], sem_in.at[slot])

    def out_copy(hbm_row, buf_row, size):
        return pltpu.make_async_copy(ybuf.at[pl.ds(aligned(buf_row), size)],
                                     ys_hbm.at[pl.ds(aligned(hbm_row), size)], sem_out.at[0])

    def start_in(wt, slot):
        for_segments(wt, lambda h, b, size: in_copy(h, b, size, slot).start())

    def wait_in(wt, slot):
        for_pieces(rows_in_use(wt), lambda off, size: in_copy(0, 0, size, slot).wait())

    def wait_out(wt):
        for_pieces(rows_in_use(wt), lambda off, size: out_copy(0, 0, size).wait())

    @pl.when(w < n_active)
    def _():
        slot = w % 2

        @pl.when(w == 0)
        def _():
            xbuf[...] = jnp.zeros_like(xbuf)
            start_in(0, 0)

        wait_in(w, slot)

        @pl.when(w + 1 < n_active)
        def _():
            start_in(w + 1, 1 - slot)

        gu = jnp.dot(xbuf[slot], wu_ref[0], preferred_element_type=F32) + bu_ref[0]
        g = jnp.minimum(gu[:, :D_FF], SWIGLU_LIMIT)
        u = jnp.clip(gu[:, D_FF:], -SWIGLU_LIMIT, SWIGLU_LIMIT)
        hid = g * _sigmoid(SWIGLU_ALPHA * g) * (u + 1.0)
        y = jnp.dot(hid.astype(MXU_DTYPE), wd_ref[0], preferred_element_type=F32) + bd_ref[0]

        @pl.when(w > 0)
        def _():
            wait_out(w - 1)

        ybuf[...] = y.astype(ybuf.dtype)
        for_segments(w, lambda h, b, size: out_copy(h, b, size).start())

        @pl.when(w == n_active - 1)
        def _():
            wait_out(w)


def _experts(tables, xs, w_up, b_up, w_down, b_down, n_tok_tiles):
    tile_expert = tables[0]
    tm = MOE_TILE
    grid_spec = pltpu.PrefetchScalarGridSpec(
        num_scalar_prefetch=len(tables),
        grid=(tile_expert.shape[0],),
        in_specs=[pl.BlockSpec(memory_space=pl.ANY),
                  pl.BlockSpec((1, D_MODEL, 2 * D_FF), lambda t, te, *_: (te[t], 0, 0)),
                  pl.BlockSpec((1, 1, 2 * D_FF), lambda t, te, *_: (te[t], 0, 0)),
                  pl.BlockSpec((1, D_FF, D_MODEL), lambda t, te, *_: (te[t], 0, 0)),
                  pl.BlockSpec((1, 1, D_MODEL), lambda t, te, *_: (te[t], 0, 0))],
        out_specs=pl.BlockSpec(memory_space=pl.ANY),
        scratch_shapes=[pltpu.VMEM((2, tm, D_MODEL), xs.dtype), pltpu.VMEM((tm, D_MODEL), xs.dtype),
                        pltpu.SemaphoreType.DMA((2,)), pltpu.SemaphoreType.DMA((1,))],
    )
    return pl.pallas_call(
        functools.partial(_expert_kernel, n_tok_tiles=n_tok_tiles),
        grid_spec=grid_spec,
        out_shape=jax.ShapeDtypeStruct(xs.shape, xs.dtype),
        compiler_params=_cparams(1),
        name="experts",
    )(*tables, xs, w_up, b_up, w_down, b_down)


def _combine_kernel(ys_ref, ri_ref, rw_ref, cnt_ref, x_ref, g3_ref, gt_ref, xo_ref):
    tm = x_ref.shape[0]
    rif = ri_ref[...].astype(F32)
    rw = rw_ref[...]
    pos = [_lane_pick(rif, TOP_K + k) for k in range(TOP_K)]
    wgt = [_lane_pick(rw, k) for k in range(TOP_K)]
    used = jnp.sum(_seg_pad(cnt_ref[0][0:1, :]), axis=1, keepdims=True)
    f = jnp.zeros(x_ref.shape, F32)
    for c in range(ys_ref.shape[0] // tm):
        col = (lax.broadcasted_iota(jnp.int32, (tm, tm), 1) + c * tm).astype(F32)
        scat = jnp.zeros((tm, tm), F32)
        for k in range(TOP_K):
            scat = jnp.where(pos[k] == col, wgt[k], scat)
        row = (lax.broadcasted_iota(jnp.int32, (tm, 1), 0) + c * tm).astype(F32)
        ys = ys_ref[c * tm:(c + 1) * tm, :]
        ys = jnp.where(row < used, ys, jnp.zeros_like(ys))
        f = f + jnp.dot(scat.astype(MXU_DTYPE), ys, preferred_element_type=F32)
    xo_ref[...] = x_ref[...] + gt_ref[0] * _rms(f, g3_ref[0])


def _combine(ys, ri, rw, cnt, xa, g_norm, modt, tm, n_tiles):
    seg_rows = ys.shape[0] // n_tiles
    return pl.pallas_call(
        _combine_kernel,
        grid=(n_tiles,),
        in_specs=[pl.BlockSpec((seg_rows, D_MODEL), lambda i: (i, 0)),
                  pl.BlockSpec((tm, LANE), lambda i: (i, 0)),
                  pl.BlockSpec((tm, LANE), lambda i: (i, 0)),
                  pl.BlockSpec((1, 8, LANE), lambda i: (i, 0, 0)),
                  pl.BlockSpec((tm, D_MODEL), lambda i: (i, 0)),
                  pl.BlockSpec((1, 1, D_MODEL), lambda i: (3, 0, 0)),
                  pl.BlockSpec((1, 1, D_MODEL), lambda i: (i, 0, 5))],
        out_specs=pl.BlockSpec((tm, D_MODEL), lambda i: (i, 0)),
        out_shape=jax.ShapeDtypeStruct((n_tiles * tm, D_MODEL), F32),
        compiler_params=_cparams(1),
        name="combine",
    )(ys, ri, rw, cnt, xa, g_norm, modt)


def _swap16(idx):
    g = np.asarray(idx).reshape(-1, 2, ROPE_FREQS)
    return g[:, ::-1, :].reshape(-1)


def _in_proj_columns():
    sizes = (Q_LORA, KV_LORA, QK_ROPE, M_HEADS * M_QK, M_HEADS * M_QK, M_HEADS * M_V, M_HEADS * M_V,
             2 * 2 * M_HEADS, CONV_WIDTH, CONV_WIDTH, CONV_WIDTH, 3 * D_MODEL)
    starts = np.concatenate([[0], np.cumsum(sizes)[:-1]])
    seg = lambda k: np.arange(starts[k], starts[k] + sizes[k])
    cq, ckv, kr, mq, mk, mv, mo, mif, cu, cb, cc, cg = (seg(k) for k in range(12))
    mqk = np.concatenate([np.concatenate([mq[h * M_QK:(h + 1) * M_QK], mk[h * M_QK:(h + 1) * M_QK]])
                          for h in range(M_HEADS)])
    main = np.concatenate([cq, ckv, mqk, mv, mo, cu, cb, cc, cg])
    pad = lambda n: np.full((n,), -1)
    special = np.concatenate([kr, kr, _swap16(kr), _swap16(kr), mif, pad(LANE - mif.size), pad(LANE)])
    cols = np.concatenate([main, special])
    assert cols.size == (N_MAIN_TILES + 1) * COL_TILE
    return cols


def _take_cols(w, cols, dtype):
    cols = np.asarray(cols)
    breaks = np.flatnonzero((np.diff(cols) != 1) & ~((cols[1:] < 0) & (cols[:-1] < 0))) + 1
    parts = []
    for run in np.split(cols, breaks):
        if run[0] < 0:
            parts.append(jnp.zeros(w.shape[:-1] + (run.size,), dtype))
        else:
            parts.append(w[..., int(run[0]):int(run[-1]) + 1].astype(dtype))
    return jnp.concatenate(parts, axis=-1)


def _q_up_columns():
    per = QK_NOPE + QK_ROPE
    nope = np.concatenate([np.arange(h * per, h * per + QK_NOPE) for h in range(MLA_HEADS)])
    rope = np.concatenate([np.arange(h * per + QK_NOPE, (h + 1) * per) for h in range(MLA_HEADS)])
    return np.concatenate([nope, rope, _swap16(rope)])


def _kv_up_columns():
    per = QK_NOPE + V_HEAD
    kn = np.concatenate([np.arange(h * per, h * per + QK_NOPE) for h in range(MLA_HEADS)])
    vv = np.concatenate([np.arange(h * per + QK_NOPE, (h + 1) * per) for h in range(MLA_HEADS)])
    return np.concatenate([kn, vv])


def _rope_tables(T, rt):
    pos = jnp.arange(T)
    inv_freq = ROPE_THETA ** (-jnp.arange(ROPE_FREQS, dtype=F32) / ROPE_FREQS)
    ang_r = (pos // GRID_W).astype(F32)[:, None] * inv_freq
    ang_c = (pos % GRID_W).astype(F32)[:, None] * inv_freq
    cos64 = jnp.concatenate([jnp.cos(ang_r), jnp.cos(ang_r), jnp.cos(ang_c), jnp.cos(ang_c)], axis=1)
    sin64 = jnp.concatenate([-jnp.sin(ang_r), jnp.sin(ang_r), -jnp.sin(ang_c), jnp.sin(ang_c)], axis=1)
    cos_t = jnp.concatenate([jnp.tile(cos64, (1, 2)), jnp.ones((rt, LANE), F32)], axis=0)
    sin_t = jnp.concatenate([jnp.tile(sin64, (1, 2)), jnp.zeros((rt, LANE), F32)], axis=0)
    return cos_t, sin_t


def _prefix_sums(a):
    n = a.shape[-1]
    keep = jnp.arange(n)[:, None] <= jnp.arange(n)[None, :]
    return jnp.sum(jnp.where(keep, a[..., :, None], 0), axis=-2)


def _moe_tables(cnt, n_tiles, tok_tile):
    tm = MOE_TILE
    seg_rows = TOP_K * tok_tile + N_EXPERTS * SEG_ALIGN
    n_work = n_tiles * seg_rows // tm + N_EXPERTS
    counts = cnt[:n_tiles, 0, :N_EXPERTS].astype(jnp.int32)
    padded = (counts + SEG_ALIGN - 1) // SEG_ALIGN * SEG_ALIGN
    in_tile = _prefix_sums(padded) - padded
    src_row = (jnp.arange(n_tiles, dtype=jnp.int32)[:, None] * seg_rows + in_tile).T
    seg_start = jnp.concatenate([jnp.zeros((N_EXPERTS, 1), jnp.int32), _prefix_sums(padded.T)], axis=1)
    group_len = seg_start[:, -1]
    work_end = _prefix_sums((group_len + tm - 1) // tm)
    work = jnp.arange(n_work, dtype=jnp.int32)
    tile_expert = jnp.minimum(jnp.sum(work_end[None, :] <= work[:, None], axis=1), N_EXPERTS - 1)
    first_work = jnp.concatenate([jnp.zeros((1,), jnp.int32), work_end[:-1]])
    tile_row0 = (work - first_work[tile_expert]) * tm
    used = jnp.clip(group_len[tile_expert] - tile_row0, 0, tm)
    starts = seg_start[tile_expert]
    t_lo = jnp.sum(starts[:, 1:] <= tile_row0[:, None], axis=1)
    t_hi = jnp.sum(starts[:, :-1] < (tile_row0 + tm)[:, None], axis=1)
    i32 = lambda a: a.astype(jnp.int32)
    return (i32(tile_expert), i32(tile_row0), i32(work_end[-1:]), i32(seg_start.reshape(-1)),
            i32(src_row.reshape(-1)), i32(used), i32(t_lo), i32(t_hi))


def _layer(xa, mods, B, T, Tc, rt, tm, update_ctx, p, cos_t, sin_t, layer):
    rows = xa.shape[0]
    n_lat = B * T
    lat_rt, all_rt = n_lat // rt, rows // rt
    lat_tm, all_tm = n_lat // tm, rows // tm
    cond_rt = np.concatenate([np.arange(lat_rt) // (T // rt), np.full((all_rt - lat_rt,), B)])
    cond_tm = np.concatenate([np.arange(lat_tm) // (T // tm), np.full((all_tm - lat_tm,), B)])
    modt_rt = mods[cond_rt][:, None, :]
    modt_tm = mods[cond_tm][:, None, :]
    g_norm = p["g_norm"].reshape(4, 1, D_MODEL)

    z, zs = _inproj(xa, g_norm, modt_rt, p["w_in_main"], p["w_in_special"], layer, rt)
    q, k, v = _mla_project(z, zs, p["g_q_lora"], p["g_kv_lora"], p["w_q_up"], p["w_kv_up"],
                           cos_t, sin_t, rt, lat_rt, T // rt)
    a, w_up_c, w_down_c = _attention_latent(q, k, v, p["w_up"], p["w_down"], layer, p["depth"],
                                            B, T, Tc, min(ATTN_Q_TILE, T), min(ATTN_KV_CHUNK, T),
                                            rows if update_ctx else n_lat)
    w_up_c = w_up_c.reshape(N_EXPERTS, D_MODEL, 2 * D_FF)
    w_down_c = w_down_c.reshape(N_EXPERTS, D_FF, D_MODEL)
    if update_ctx:
        a = _attention_ctx(q, k, v, a, B, T, Tc)
    m = _mlstm(z, zs, p["b_if"], p["g_mlstm"], B, T, Tc)
    n_rt = all_rt if update_ctx else lat_rt
    n_tm = all_tm if update_ctx else lat_tm
    t = _merge(a, m, z, p["w_proj_a"], p["w_proj_b"], p["w_proj_c"], p["w_conv"], p["b_conv"],
               rt, n_rt, lat_rt, T // rt, T, Tc)
    xa, h2, ri, rw, cnt = _post(t, p["w_o"], xa, g_norm, modt_tm, p["w_router"], p["b_router"], tm, n_tm)
    xs = _sort_rows(h2, ri, tm, n_tm)
    ys = _experts(_moe_tables(cnt, n_tm, tm), xs, w_up_c, p["b_up"], w_down_c, p["b_down"], n_tm)
    return _combine(ys, ri, rw, cnt, xa, g_norm, modt_tm, tm, n_tm)


def kernel(x, c, ctx, c_ctx, w_mod, b_mod, g_norm, w_in, g_q_lora, g_kv_lora, w_q_up, w_kv_up, b_if, g_mlstm,
           w_conv, b_conv, w_proj_a, w_proj_b, w_proj_c, w_o, w_router, b_router, w_up, b_up, w_down, b_down):
    B, T, _ = x.shape
    Tc = ctx.shape[1]
    depth = w_mod.shape[0]
    rt = min(1024, T)
    tm = min(512, T)
    assert T % rt == 0 and (B * Tc) % rt == 0 and (B * Tc) % tm == 0
    assert T & (T - 1) == 0 and Tc & (Tc - 1) == 0 and T % GRID_W == 0

    xa = jnp.concatenate([x.reshape(B * T, D_MODEL), ctx.reshape(B * Tc, D_MODEL)], axis=0)
    cond = jnp.zeros((8, D_MODEL), F32).at[:B].set(c).at[B].set(c_ctx)
    cos_t, sin_t = _rope_tables(T, rt)
    cast = lambda w: w.astype(MXU_DTYPE)
    in_cols = _in_proj_columns()
    n_main = N_MAIN_TILES * COL_TILE
    w_in_main = _take_cols(w_in, in_cols[:n_main], MXU_DTYPE)
    w_in_special = _take_cols(w_in, in_cols[n_main:], MXU_DTYPE)
    w_q_p = _take_cols(w_q_up, _q_up_columns(), MXU_DTYPE)
    w_kv_p = _take_cols(w_kv_up, _kv_up_columns(), MXU_DTYPE)
    w_up_r = w_up.reshape(depth * N_EXPERTS * D_MODEL, 2 * D_FF)
    w_down_r = w_down.reshape(depth * N_EXPERTS * D_FF, D_MODEL)
    wa_c, wb_c, wc_c, wo_c = cast(w_proj_a), cast(w_proj_b), cast(w_proj_c), cast(w_o)
    w_router_p = jnp.pad(w_router, ((0, 0), (0, 0), (0, LANE - N_EXPERTS)))
    b_router_p = jnp.pad(b_router, ((0, 0), (0, LANE - N_EXPERTS)), constant_values=NEG_BIG)
    b_if_p = jnp.pad(b_if.reshape(depth, 1, -1), ((0, 0), (0, 0), (0, LANE - 4 * M_HEADS)))
    mods = _adaln(cond, w_mod, b_mod)
    for l in range(depth):
        p = dict(
            g_norm=g_norm[l], w_in_main=w_in_main, w_in_special=w_in_special,
            g_q_lora=g_q_lora[l], g_kv_lora=g_kv_lora[l],
            w_q_up=w_q_p[l], w_kv_up=w_kv_p[l], b_if=b_if_p[l],
            g_mlstm=g_mlstm[l], w_conv=w_conv[l], b_conv=b_conv[l],
            w_proj_a=wa_c[l], w_proj_b=wb_c[l], w_proj_c=wc_c[l], w_o=wo_c[l],
            w_router=w_router_p[l], b_router=b_router_p[l],
            depth=depth, w_up=w_up_r, b_up=b_up[l].reshape(N_EXPERTS, 1, -1),
            w_down=w_down_r, b_down=b_down[l].reshape(N_EXPERTS, 1, -1),
        )
        xa = _layer(xa, mods[l], B, T, Tc, rt, tm, l < depth - 1, p, cos_t, sin_t, l)
    return xa.reshape(B, T, D_MODEL)
```

```python
import functools

import numpy as np
import jax
import jax.numpy as jnp
from jax import lax
from jax.experimental import pallas as pl
from jax.experimental.pallas import tpu as pltpu

F32 = jnp.float32
MXU_DTYPE = jnp.bfloat16

D_MODEL = 2048
GRID_W = 64
MLA_HEADS = 8
QK_NOPE = 128
QK_ROPE = 64
V_HEAD = 128
Q_LORA = 512
KV_LORA = 512
ROPE_THETA = 10000.0
ROPE_FREQS = QK_ROPE // 4
ATTN_SCALE = (QK_NOPE + QK_ROPE) ** -0.5
M_HEADS = 4
M_QK = 64
M_V = 128
CONV_WIDTH = 512
N_EXPERTS = 32
TOP_K = 4
D_FF = 1024
SWIGLU_ALPHA = 1.702
SWIGLU_LIMIT = 7.0
N_MOD = 6
EPS = 1e-6

LANE = 128
HEAD_W = 2 * LANE
MLSTM_CHUNK = 256
COL_TILE = 512
INPROJ_TILE = 1024
LOG2_E = 1.4426950408889634
MERGE_TILE = 1024
ATTN_Q_TILE = 1024
ATTN_KV_CHUNK = 512
MOE_TILE = 512
SEG_ALIGN = 16
NEG_BIG = -1e30
VMEM_LIMIT = 56 * 1024 * 1024

ZT_CQ, ZT_CKV, ZT_MQK, ZT_MV, ZT_MO, ZT_CU, ZT_CB, ZT_CC, ZT_CG = 0, 1, 2, 3, 4, 5, 6, 7, 8
N_MAIN_TILES = 8 + 3 * D_MODEL // COL_TILE
assert (N_MAIN_TILES * COL_TILE) % INPROJ_TILE == 0


def _cparams(n_axes, vmem=VMEM_LIMIT):
    return pltpu.CompilerParams(dimension_semantics=("arbitrary",) * n_axes, vmem_limit_bytes=vmem)


def _rms(x, g):
    return x * lax.rsqrt(jnp.mean(x * x, axis=-1, keepdims=True) + EPS) * g


def _sigmoid(x):
    return 0.5 * jnp.tanh(0.5 * x) + 0.5


def _lane_pick(x, idx):
    lane = lax.broadcasted_iota(jnp.int32, x.shape, 1)
    return jnp.sum(jnp.where(lane == idx, x, 0.0), axis=1, keepdims=True)


def _mod_kernel(c_ref, w_ref, b_ref, o_ref):
    c = c_ref[...]
    s = c * jax.nn.sigmoid(c)
    o_ref[0] = jnp.dot(s.astype(MXU_DTYPE), w_ref[0].astype(MXU_DTYPE),
                       preferred_element_type=F32) + b_ref[0]


def _adaln(cond, w_mod, b_mod):
    depth, _, n = w_mod.shape
    tn = 1024
    return pl.pallas_call(
        _mod_kernel,
        grid=(depth, n // tn),
        in_specs=[pl.BlockSpec(cond.shape, lambda l, j: (0, 0)),
                  pl.BlockSpec((1, D_MODEL, tn), lambda l, j: (l, 0, j)),
                  pl.BlockSpec((1, 1, tn), lambda l, j: (l, 0, j))],
        out_specs=pl.BlockSpec((1, cond.shape[0], tn), lambda l, j: (l, 0, j)),
        out_shape=jax.ShapeDtypeStruct((depth, cond.shape[0], n), F32),
        compiler_params=_cparams(2),
        name="adaln",
    )(cond, w_mod, b_mod.reshape(depth, 1, n))


def _row_sources(parts, tile):
    first, second = parts
    n_first = first.shape[0] // tile
    if second is None:
        second, second_map = first, (lambda i, *_: (0, 0))
    else:
        second_map = lambda i, *_: (jnp.maximum(i - n_first, 0), 0)
    width = first.shape[1]
    specs = [pl.BlockSpec((tile, width), lambda i, *_: (jnp.minimum(i, n_first - 1), 0)),
             pl.BlockSpec((tile, width), second_map, pipeline_mode=pl.Buffered(1))]
    return (first, second), specs, n_first


def _inproj_kernel(x_ref, g_ref, sh_ref, sc_ref, w_ref, ws_ref, z_ref, zs_ref, h_scr):
    @pl.when(pl.program_id(1) == 0)
    def _():
        h = _rms(x_ref[...], g_ref[0]) * (1.0 + sc_ref[0]) + sh_ref[0]
        h_scr[...] = h.astype(h_scr.dtype)
        zs_ref[...] = jnp.dot(h_scr[...], ws_ref[0], preferred_element_type=F32)

    z_ref[...] = jnp.dot(h_scr[...], w_ref[0], preferred_element_type=F32).astype(z_ref.dtype)


def _inproj(xa, g_norm, modt, w_main, w_special, layer, rt):
    rows = xa.shape[0]
    n_main = w_main.shape[2]
    tn = INPROJ_TILE
    return pl.pallas_call(
        _inproj_kernel,
        grid=(rows // rt, n_main // tn),
        in_specs=[pl.BlockSpec((rt, D_MODEL), lambda i, j: (i, 0)),
                  pl.BlockSpec((1, 1, D_MODEL), lambda i, j: (0, 0, 0)),
                  pl.BlockSpec((1, 1, D_MODEL), lambda i, j: (i, 0, 0)),
                  pl.BlockSpec((1, 1, D_MODEL), lambda i, j: (i, 0, 1)),
                  pl.BlockSpec((1, D_MODEL, tn), lambda i, j: (layer, 0, j)),
                  pl.BlockSpec((1, D_MODEL, COL_TILE), lambda i, j: (layer, 0, 0))],
        out_specs=[pl.BlockSpec((rt, tn), lambda i, j: (i, j)),
                   pl.BlockSpec((rt, COL_TILE), lambda i, j: (i, 0))],
        out_shape=[jax.ShapeDtypeStruct((rows, n_main), MXU_DTYPE),
                   jax.ShapeDtypeStruct((rows, COL_TILE), F32)],
        scratch_shapes=[pltpu.VMEM((rt, D_MODEL), MXU_DTYPE)],
        compiler_params=_cparams(2),
        name="inproj",
    )(xa, g_norm, modt, modt, w_main, w_special)


def _mla_kernel(cq_ref, ckv_ref, zs_ref, gq_ref, gkv_ref, wq_ref, wkv_ref, cos_ref, sin_ref,
                q_ref, k_ref, v_ref):
    def norm(z_ref, g_ref):
        return _rms(z_ref[...].astype(F32), g_ref[...]).astype(MXU_DTYPE)

    qf = jnp.dot(norm(cq_ref, gq_ref), wq_ref[...], preferred_element_type=F32) * (ATTN_SCALE * LOG2_E)
    kvf = jnp.dot(norm(ckv_ref, gkv_ref), wkv_ref[...], preferred_element_type=F32)
    cos = cos_ref[...]
    sin = sin_ref[...]
    nope_w = MLA_HEADS * QK_NOPE
    rope_w = MLA_HEADS * QK_ROPE
    for p in range(MLA_HEADS // 2):
        a = nope_w + LANE * p
        b = nope_w + rope_w + LANE * p
        rp = (qf[:, a:a + LANE] * cos + qf[:, b:b + LANE] * sin).astype(MXU_DTYPE)
        for h in (2 * p, 2 * p + 1):
            q_ref[:, HEAD_W *h:HEAD_W *h + LANE] = qf[:, LANE * h:LANE * (h + 1)].astype(MXU_DTYPE)
            q_ref[:, HEAD_W *h + LANE:HEAD_W *(h + 1)] = rp
    kr2 = zs_ref[:, 0:LANE] * cos + zs_ref[:, LANE:2 * LANE] * sin
    lane = lax.broadcasted_iota(jnp.int32, kr2.shape, 1)
    k_lo = jnp.where(lane < QK_ROPE, kr2, 0.0).astype(MXU_DTYPE)
    k_hi = jnp.where(lane >= QK_ROPE, kr2, 0.0).astype(MXU_DTYPE)
    for h in range(MLA_HEADS):
        k_ref[:, HEAD_W *h:HEAD_W *h + LANE] = kvf[:, LANE * h:LANE * (h + 1)].astype(MXU_DTYPE)
        k_ref[:, HEAD_W *h + LANE:HEAD_W *(h + 1)] = k_lo if h % 2 == 0 else k_hi
    ones = jnp.ones((kvf.shape[0], LANE), MXU_DTYPE)
    v0 = MLA_HEADS * QK_NOPE
    for h in range(MLA_HEADS):
        v_ref[:, HEAD_W *h:HEAD_W *h + LANE] = kvf[:, v0 + LANE * h:v0 + LANE * (h + 1)].astype(MXU_DTYPE)
        v_ref[:, HEAD_W *h + LANE:HEAD_W *(h + 1)] = ones


def _mla_project(z, zs, g_q, g_kv, wq_p, wkv_p, cos_t, sin_t, rt, n_lat_tiles, tiles_per_seq):
    rows = z.shape[0]
    nt = rows // rt

    def tab(i):
        return (jnp.where(i < n_lat_tiles, i % tiles_per_seq, tiles_per_seq), 0)

    return pl.pallas_call(
        _mla_kernel,
        grid=(nt,),
        in_specs=[pl.BlockSpec((rt, COL_TILE), lambda i: (i, ZT_CQ)),
                  pl.BlockSpec((rt, COL_TILE), lambda i: (i, ZT_CKV)),
                  pl.BlockSpec((rt, COL_TILE), lambda i: (i, 0)),
                  pl.BlockSpec((1, Q_LORA), lambda i: (0, 0)),
                  pl.BlockSpec((1, KV_LORA), lambda i: (0, 0)),
                  pl.BlockSpec(wq_p.shape, lambda i: (0, 0)),
                  pl.BlockSpec(wkv_p.shape, lambda i: (0, 0)),
                  pl.BlockSpec((rt, LANE), tab),
                  pl.BlockSpec((rt, LANE), tab)],
        out_specs=[pl.BlockSpec((rt, HEAD_W *MLA_HEADS), lambda i: (i, 0)),
                   pl.BlockSpec((rt, HEAD_W *MLA_HEADS), lambda i: (i, 0)),
                   pl.BlockSpec((rt, HEAD_W *MLA_HEADS), lambda i: (i, 0))],
        out_shape=[jax.ShapeDtypeStruct((rows, HEAD_W *MLA_HEADS), MXU_DTYPE)] * 3,
        compiler_params=_cparams(1),
        name="mla_project",
    )(z, z, zs, g_q.reshape(1, -1), g_kv.reshape(1, -1), wq_p, wkv_p, cos_t, sin_t)


def _attn_kernel(*refs, kv_chunks, n_cast=0):
    q_ref = refs[0]
    o_ref = refs[len(refs) - 1 - n_cast]
    for c in range(n_cast):
        src = refs[1 + 2 * len(kv_chunks) + c]
        dst = refs[len(refs) - n_cast + c]
        dst[...] = src[...].astype(dst.dtype)
    q = q_ref[...]
    tq = q.shape[0]
    m = jnp.full((tq, 1), -jnp.inf, F32)
    acc = jnp.zeros((tq, 2 * V_HEAD), F32)
    for s, n_chunks in enumerate(kv_chunks):
        k_ref, v_ref = refs[1 + 2 * s], refs[2 + 2 * s]
        ck = k_ref.shape[0] // n_chunks
        for c in range(n_chunks):
            k = k_ref[c * ck:(c + 1) * ck, :]
            v = v_ref[c * ck:(c + 1) * ck, :]
            sc = lax.dot_general(q, k, (((1,), (1,)), ((), ())), preferred_element_type=F32)
            m_new = jnp.maximum(m, jnp.max(sc, axis=-1, keepdims=True))
            p = jnp.exp2(sc - m_new).astype(MXU_DTYPE)
            acc = jnp.exp2(m - m_new) * acc + jnp.dot(p, v, preferred_element_type=F32)
            m = m_new
    o_ref[...] = (acc[:, :V_HEAD] / acc[:, V_HEAD:]).astype(o_ref.dtype)


def _attention_latent(q, k, v, w_up, w_down, layer, depth, B, T, Tc, tq, ck, rows):
    qt = T // tq
    ctx_blk0 = B * T // Tc
    steps = B * MLA_HEADS * qt
    step = lambda b, h, i: (b * MLA_HEADS + h) * qt + i
    cast_specs_in, cast_specs_out, cast_shapes = [], [], []
    for w in (w_up, w_down):
        n = w.shape[1]
        layer_rows = w.shape[0] // depth
        slab = layer_rows // steps
        assert slab * steps == layer_rows and slab % 16 == 0
        cast_specs_in.append(pl.BlockSpec((slab, n), lambda b, h, i: (layer * steps + step(b, h, i), 0)))
        cast_specs_out.append(pl.BlockSpec((slab, n), lambda b, h, i: (step(b, h, i), 0)))
        cast_shapes.append(jax.ShapeDtypeStruct((layer_rows, n), MXU_DTYPE))
    return pl.pallas_call(
        functools.partial(_attn_kernel, kv_chunks=(T // ck, 1), n_cast=2),
        grid=(B, MLA_HEADS, qt),
        in_specs=[pl.BlockSpec((tq, HEAD_W), lambda b, h, i: (b * qt + i, h)),
                  pl.BlockSpec((T, HEAD_W), lambda b, h, i: (b, h)),
                  pl.BlockSpec((T, HEAD_W), lambda b, h, i: (b, h)),
                  pl.BlockSpec((Tc, HEAD_W), lambda b, h, i: (ctx_blk0 + b, h)),
                  pl.BlockSpec((Tc, HEAD_W), lambda b, h, i: (ctx_blk0 + b, h))] + cast_specs_in,
        out_specs=[pl.BlockSpec((tq, V_HEAD), lambda b, h, i: (b * qt + i, h))] + cast_specs_out,
        out_shape=[jax.ShapeDtypeStruct((rows, V_HEAD * MLA_HEADS), MXU_DTYPE)] + cast_shapes,
        compiler_params=_cparams(3),
        name="attn_latent",
    )(q, k, v, k, v, w_up, w_down)


def _attention_ctx(q, k, v, a_buf, B, T, Tc):
    ctx_blk0 = B * T // Tc
    return pl.pallas_call(
        functools.partial(_attn_kernel, kv_chunks=(1,)),
        grid=(B, MLA_HEADS),
        in_specs=[pl.BlockSpec((Tc, HEAD_W), lambda b, h: (ctx_blk0 + b, h)),
                  pl.BlockSpec((Tc, HEAD_W), lambda b, h: (ctx_blk0 + b, h)),
                  pl.BlockSpec((Tc, HEAD_W), lambda b, h: (ctx_blk0 + b, h)),
                  pl.BlockSpec(memory_space=pl.ANY)],
        out_specs=pl.BlockSpec((Tc, V_HEAD), lambda b, h: (ctx_blk0 + b, h)),
        out_shape=jax.ShapeDtypeStruct(a_buf.shape, a_buf.dtype),
        input_output_aliases={3: 0},
        compiler_params=_cparams(2),
        name="attn_ctx",
    )(q, k, v, a_buf)


def _mlstm_chunk(qk, v, g, lf_cum, gt_ref, head, reverse, ct_ref, m_prev, consts):
    tri, row_iota, mk, ones_tile, lane = consts
    L = qk.shape[0]
    base = 8 if reverse else 0
    col_i, col_f = base + head, base + 4 + head
    ql = jnp.where(lane < M_QK, qk * (M_QK ** -0.5), jnp.zeros_like(qk))
    kl = jnp.dot(qk, mk, preferred_element_type=F32).astype(MXU_DTYPE)
    v_aug = jnp.concatenate([v, ones_tile], axis=1)
    bc = _lane_pick(lf_cum, col_f)
    li = _lane_pick(g, col_i)
    r_row = gt_ref[0, col_i:col_i + 1, :] - gt_ref[1, col_f:col_f + 1, :]
    dlog = jnp.where(tri, bc + r_row, -jnp.inf)
    inter = bc + m_prev
    m_t = jnp.maximum(inter, jnp.max(dlog, axis=1, keepdims=True))
    dexp = jnp.exp(dlog - m_t)
    s_raw = lax.dot_general(ql, kl, (((1,), (1,)), ((), ())), preferred_element_type=F32)
    s_mat = (s_raw * dexp).astype(MXU_DTYPE)
    a = jnp.exp(inter - m_t)
    ct = ct_ref[...]
    nd = (jnp.dot(s_mat, v_aug, preferred_element_type=F32)
          + a * jnp.dot(ql, ct.astype(MXU_DTYPE), preferred_element_type=F32))
    num, den = nd[:, :M_V], nd[:, M_V:]
    h_out = num / jnp.maximum(jnp.abs(den), jnp.exp(-m_t))
    last = 0 if reverse else L - 1
    total = jnp.sum(jnp.where(row_iota == last, bc, 0.0), axis=0, keepdims=True)
    w_col = total - bc + li
    m_new = jnp.maximum(total + m_prev, jnp.max(w_col, axis=0, keepdims=True))
    decay = jnp.exp(total + m_prev - m_new)
    e = jnp.exp(w_col - m_new)
    ek = (kl.astype(F32) * e).astype(MXU_DTYPE)
    upd = lax.dot_general(ek, v_aug, (((0,), (0,)), ((), ())), preferred_element_type=F32)
    ct_ref[...] = decay * ct + upd
    return h_out, m_new


def _mlstm_kernel(zl_qk, zl_v, zl_mo, gl_ref, zc_qk, zc_v, zc_mo, gc_ref, bif_ref, gm_ref,
                  ol_ref, oc_ref, acc_l, acc_c, ct_scr, gt_scr):
    L = MLSTM_CHUNK
    T = zl_qk.shape[0]
    nc = T // L
    r = lax.broadcasted_iota(jnp.int32, (L, L), 0)
    c = lax.broadcasted_iota(jnp.int32, (L, L), 1)
    tri_f = c <= r
    tri_b = c >= r
    tmat_f = tri_f.astype(jnp.bfloat16)
    tmat_b = tri_b.astype(jnp.bfloat16)
    row_iota = lax.broadcasted_iota(jnp.int32, (L, 1), 0)
    lane = lax.broadcasted_iota(jnp.int32, (L, LANE), 1)
    rr = lax.broadcasted_iota(jnp.int32, (LANE, LANE), 0)
    cc = lax.broadcasted_iota(jnp.int32, (LANE, LANE), 1)
    mk = (rr == cc + M_QK).astype(MXU_DTYPE)
    ones_tile = jnp.ones((L, LANE), MXU_DTYPE)
    consts_f = (tri_f, row_iota, mk, ones_tile, lane)
    consts_b = (tri_b, row_iota, mk, ones_tile, lane)
    bias = bif_ref[...]

    def gate_prep(g_raw, tmat, slot):
        g = g_raw + bias
        log_sig = jnp.minimum(g, 0.0) - jnp.log1p(jnp.exp(-jnp.abs(g)))
        hi = log_sig.astype(jnp.bfloat16)
        rest = log_sig - hi.astype(F32)
        mid = rest.astype(jnp.bfloat16)
        low = (rest - mid.astype(F32)).astype(jnp.bfloat16)
        sums = jnp.dot(tmat, jnp.concatenate([hi, mid, low], axis=1), preferred_element_type=F32)
        cum = sums[:, :LANE] + sums[:, LANE:2 * LANE] + sums[:, 2 * LANE:]
        gt_scr[slot, 0] = g.T
        gt_scr[slot, 1] = cum.T
        return g, cum

    def step(qk_ref, v_ref, g_ref, acc_ref, row_f, row_b, ms):
        g_f, cum_f = gate_prep(g_ref[pl.ds(row_f, L), :], tmat_f, 0)
        g_b, cum_b = gate_prep(g_ref[pl.ds(row_b, L), :], tmat_b, 1)
        new_ms = []
        for h in range(M_HEADS):
            cols = slice(h * LANE, (h + 1) * LANE)
            hf, mf = _mlstm_chunk(qk_ref[pl.ds(row_f, L), cols], v_ref[pl.ds(row_f, L), cols],
                                  g_f, cum_f, gt_scr.at[0], h, False,
                                  ct_scr.at[h], ms[h], consts_f)
            acc_ref[pl.ds(row_f, L), cols] += hf
            hb, mb = _mlstm_chunk(qk_ref[pl.ds(row_b, L), cols], v_ref[pl.ds(row_b, L), cols],
                                  g_b, cum_b, gt_scr.at[1], h, True,
                                  ct_scr.at[M_HEADS + h], ms[M_HEADS + h], consts_b)
            acc_ref[pl.ds(row_b, L), cols] += hb
            new_ms.append((mf, mb))
        return tuple(x[0] for x in new_ms) + tuple(x[1] for x in new_ms)

    ct_scr[...] = jnp.zeros_like(ct_scr)
    acc_l[...] = jnp.zeros_like(acc_l)
    acc_c[...] = jnp.zeros_like(acc_c)
    zero = jnp.zeros((1, 1), F32)
    ms = step(zc_qk, zc_v, gc_ref, acc_c, 0, 0, (zero,) * (2 * M_HEADS))

    def body(i, ms):
        row_f = pl.multiple_of(i * L, L)
        row_b = pl.multiple_of((nc - 1 - i) * L, L)
        return step(zl_qk, zl_v, gl_ref, acc_l, row_f, row_b, ms)

    lax.fori_loop(0, nc, body, ms)

    gm = gm_ref[...]

    def finish(acc_ref, mo_ref, o_ref, row):
        for h in range(M_HEADS):
            cols = slice(h * LANE, (h + 1) * LANE)
            hn = _rms(acc_ref[pl.ds(row, L), cols], gm[:, cols])
            gate = jax.nn.sigmoid(mo_ref[pl.ds(row, L), cols].astype(F32))
            o_ref[pl.ds(row, L), cols] = (hn * gate).astype(o_ref.dtype)

    finish(acc_c, zc_mo, oc_ref, 0)

    def fin_body(i, carry):
        finish(acc_l, zl_mo, ol_ref, pl.multiple_of(i * L, L))
        return carry

    lax.fori_loop(0, nc, fin_body, 0)


def _mlstm(z, zs, bif_row, g_mlstm, B, T, Tc):
    assert Tc == MLSTM_CHUNK and T % MLSTM_CHUNK == 0
    rows = z.shape[0]
    ctx_blk0 = B * T // Tc
    w = M_HEADS * LANE
    out_l, out_c = pl.pallas_call(
        _mlstm_kernel,
        grid=(B,),
        in_specs=[pl.BlockSpec((T, w), lambda b: (b, ZT_MQK)),
                  pl.BlockSpec((T, w), lambda b: (b, ZT_MV)),
                  pl.BlockSpec((T, w), lambda b: (b, ZT_MO)),
                  pl.BlockSpec((T, LANE), lambda b: (b, 2)),
                  pl.BlockSpec((Tc, w), lambda b: (ctx_blk0 + b, ZT_MQK)),
                  pl.BlockSpec((Tc, w), lambda b: (ctx_blk0 + b, ZT_MV)),
                  pl.BlockSpec((Tc, w), lambda b: (ctx_blk0 + b, ZT_MO)),
                  pl.BlockSpec((Tc, LANE), lambda b: (ctx_blk0 + b, 2)),
                  pl.BlockSpec((1, LANE), lambda b: (0, 0)),
                  pl.BlockSpec((1, w), lambda b: (0, 0))],
        out_specs=[pl.BlockSpec((T, w), lambda b: (b, 0)),
                   pl.BlockSpec((Tc, w), lambda b: (b, 0))],
        out_shape=[jax.ShapeDtypeStruct((B * T, w), MXU_DTYPE),
                   jax.ShapeDtypeStruct((B * Tc, w), MXU_DTYPE)],
        scratch_shapes=[pltpu.VMEM((T, w), F32), pltpu.VMEM((Tc, w), F32),
                        pltpu.VMEM((2 * M_HEADS, LANE, 2 * LANE), F32),
                        pltpu.VMEM((2, 2, LANE, MLSTM_CHUNK), F32)],
        compiler_params=_cparams(1),
        name="mlstm",
    )(z, z, z, zs, z, z, z, zs, bif_row, g_mlstm.reshape(1, -1))
    return out_l, out_c


def _merge_kernel(ml_ref, mc_ref, a_ref, cu_ref, cb_ref, cc_ref, cu_p, cc_p, cu_n, cc_n, ga_ref, gm_ref, gs_ref,
                  wa_ref, wb_ref, wc_ref, wconv_ref, bconv_ref, t_ref, s_scr,
                  *, n_lat_tiles, tiles_per_seq, seq_len, ctx_len):
    i = pl.program_id(0)
    j = pl.program_id(1)
    rt = a_ref.shape[0]
    m = jnp.where(i < n_lat_tiles, ml_ref[...], mc_ref[...])

    @pl.when(j == 0)
    def _():
        u = cc_ref[...].astype(F32) * cu_ref[...].astype(F32)
        hp = cc_p[...].astype(F32) * cu_p[...].astype(F32)
        hn = cc_n[...].astype(F32) * cu_n[...].astype(F32)
        row = lax.broadcasted_iota(jnp.int32, (rt, 1), 0)
        is_ctx = i >= n_lat_tiles
        seg = jnp.where(is_ctx, ctx_len, seq_len)
        off = jnp.where(is_ctx, (i - n_lat_tiles) * rt, (i % tiles_per_seq) * rt)
        pos = (off + row) & (seg - 1)
        up = pltpu.roll(u, 1, 0)
        up = jnp.where(row == 0, hp[hp.shape[0] - 1:, :], up)
        up = jnp.where(pos == 0, 0.0, up)
        un = pltpu.roll(u, rt - 1, 0)
        un = jnp.where(row == rt - 1, hn[0:1, :], un)
        un = jnp.where(pos == seg - 1, 0.0, un)
        wconv = wconv_ref[...]
        y = up * wconv[0:1, :] + u * wconv[1:2, :] + un * wconv[2:3, :] + bconv_ref[...]
        s_scr[...] = (cb_ref[...].astype(F32) * y).astype(s_scr.dtype)

    def sig(ref):
        return _sigmoid(ref[...].astype(F32))

    t = (sig(ga_ref) * jnp.dot(a_ref[...], wa_ref[...], preferred_element_type=F32)
         + sig(gm_ref) * jnp.dot(m, wb_ref[...], preferred_element_type=F32)
         + sig(gs_ref) * jnp.dot(s_scr[...], wc_ref[...], preferred_element_type=F32))
    t_ref[...] = t.astype(t_ref.dtype)


def _merge(a, m, z, wa, wb, wc, w_conv, b_conv, rt, n_tiles, n_lat_tiles, tiles_per_seq, T, Tc):
    rows = n_tiles * rt
    hb = 16
    rb = rt // hb
    last_hb = z.shape[0] // hb - 1
    tn = MERGE_TILE
    nj = D_MODEL // tn
    gt = D_MODEL // tn
    g0 = ZT_CG * COL_TILE // tn
    assert (ZT_CG * COL_TILE) % tn == 0

    def prev_blk(col):
        return lambda i, j: (jnp.maximum(i * rb - 1, 0), col)

    def next_blk(col):
        return lambda i, j: (jnp.minimum((i + 1) * rb, last_hb), col)

    kern = functools.partial(_merge_kernel, n_lat_tiles=n_lat_tiles, tiles_per_seq=tiles_per_seq,
                             seq_len=T, ctx_len=Tc)
    m_arrays, m_specs, m_first = _row_sources(m, rt)
    assert m_first == n_lat_tiles
    return pl.pallas_call(
        kern,
        grid=(n_tiles, nj),
        in_specs=m_specs + [pl.BlockSpec((rt, a.shape[1]), lambda i, j: (i, 0)),
                  pl.BlockSpec((rt, COL_TILE), lambda i, j: (i, ZT_CU)),
                  pl.BlockSpec((rt, COL_TILE), lambda i, j: (i, ZT_CB)),
                  pl.BlockSpec((rt, COL_TILE), lambda i, j: (i, ZT_CC)),
                  pl.BlockSpec((hb, COL_TILE), prev_blk(ZT_CU)),
                  pl.BlockSpec((hb, COL_TILE), prev_blk(ZT_CC)),
                  pl.BlockSpec((hb, COL_TILE), next_blk(ZT_CU)),
                  pl.BlockSpec((hb, COL_TILE), next_blk(ZT_CC)),
                  pl.BlockSpec((rt, tn), lambda i, j: (i, g0 + j)),
                  pl.BlockSpec((rt, tn), lambda i, j: (i, g0 + gt + j)),
                  pl.BlockSpec((rt, tn), lambda i, j: (i, g0 + 2 * gt + j)),
                  pl.BlockSpec((wa.shape[0], tn), lambda i, j: (0, j)),
                  pl.BlockSpec((wb.shape[0], tn), lambda i, j: (0, j)),
                  pl.BlockSpec((wc.shape[0], tn), lambda i, j: (0, j)),
                  pl.BlockSpec(w_conv.shape, lambda i, j: (0, 0)),
                  pl.BlockSpec((1, CONV_WIDTH), lambda i, j: (0, 0))],
        out_specs=pl.BlockSpec((rt, tn), lambda i, j: (i, j)),
        out_shape=jax.ShapeDtypeStruct((rows, D_MODEL), MXU_DTYPE),
        scratch_shapes=[pltpu.VMEM((rt, CONV_WIDTH), MXU_DTYPE)],
        compiler_params=_cparams(2),
        name="merge",
    )(*m_arrays, a, z, z, z, z, z, z, z, z, z, z, wa, wb, wc, w_conv, b_conv.reshape(1, -1))


def _dot3(x, w):
    xh = x.astype(jnp.bfloat16)
    xl = (x - xh.astype(F32)).astype(jnp.bfloat16)
    wh = w.astype(jnp.bfloat16)
    wl = (w - wh.astype(F32)).astype(jnp.bfloat16)
    n = w.shape[1]
    hi = jnp.dot(xh, jnp.concatenate([wh, wl], axis=1), preferred_element_type=F32)
    return hi[:, :n] + hi[:, n:] + jnp.dot(xl, wh, preferred_element_type=F32)


def _seg_pad(counts):
    return jnp.ceil(counts * (1.0 / SEG_ALIGN)) * SEG_ALIGN


def _exclusive_lane_cumsum(row):
    upper = (lax.broadcasted_iota(jnp.int32, (LANE, LANE), 0)
             < lax.broadcasted_iota(jnp.int32, (LANE, LANE), 1)).astype(jnp.bfloat16)
    out = jnp.dot(jnp.broadcast_to(row, (8, LANE)).astype(jnp.bfloat16), upper, preferred_element_type=F32)
    return out[0:1, :]


def _post_kernel(x_ref, t_ref, wo_ref, g1_ref, g2_ref, gt_ref, sh_ref, sc_ref, wr_ref, br_ref,
                 xo_ref, h2_ref, ri_ref, rw_ref, cnt_ref):
    half = t_ref.shape[0] // 2
    parts = []
    for r in (slice(0, half), slice(half, 2 * half)):
        y = jnp.dot(t_ref[r, :], wo_ref[...], preferred_element_type=F32)
        xn = x_ref[r, :] + gt_ref[0] * _rms(y, g1_ref[0])
        xo_ref[r, :] = xn
        h2 = _rms(xn, g2_ref[0]) * (1.0 + sc_ref[0]) + sh_ref[0]
        h2_ref[r, :] = h2.astype(h2_ref.dtype)
        parts.append(_dot3(h2, wr_ref[...]) + br_ref[...])
    logits = jnp.concatenate(parts, axis=0)
    lane = lax.broadcasted_iota(jnp.int32, logits.shape, 1)
    lane_f = lane.astype(F32)
    vals, ids = [], []
    for _ in range(TOP_K):
        mx = jnp.max(logits, axis=1, keepdims=True)
        idx = jnp.min(jnp.where(logits == mx, lane_f, float(LANE)), axis=1, keepdims=True)
        logits = jnp.where(lane_f == idx, -jnp.inf, logits)
        vals.append(mx)
        ids.append(idx)
    es = [jnp.exp(v - vals[0]) for v in vals]
    tot = es[0] + es[1] + es[2] + es[3]
    tm = logits.shape[0]
    onehots = [lane_f == ids[k] for k in range(TOP_K)]
    asg = jnp.zeros(logits.shape, F32)
    for k in range(TOP_K):
        asg = jnp.where(onehots[k], 1.0, asg)
    earlier = (lax.broadcasted_iota(jnp.int32, (tm, tm), 1)
               < lax.broadcasted_iota(jnp.int32, (tm, tm), 0)).astype(jnp.bfloat16)
    before = jnp.dot(earlier, asg.astype(jnp.bfloat16), preferred_element_type=F32)
    counts = jnp.sum(asg, axis=0, keepdims=True)
    seg_off = _exclusive_lane_cumsum(_seg_pad(counts))
    rw = jnp.zeros(logits.shape, F32)
    ri = jnp.zeros(logits.shape, F32)
    for k in range(TOP_K):
        pos = jnp.sum(jnp.where(onehots[k], before + seg_off, 0.0), axis=1, keepdims=True)
        rw = jnp.where(lane == k, es[k] / tot, rw)
        ri = jnp.where(lane == k, ids[k], ri)
        ri = jnp.where(lane == TOP_K + k, pos, ri)
    rw_ref[...] = rw
    ri_ref[...] = ri.astype(jnp.int32)
    cnt_ref[0] = jnp.broadcast_to(counts, cnt_ref.shape[1:])


def _post(t, wo, xa, g_norm, modt, wr_p, br_p, tm, n_tiles):
    rows = n_tiles * tm
    row_spec = lambda w: pl.BlockSpec((tm, w), lambda i: (i, 0))
    mod = lambda k: pl.BlockSpec((1, 1, D_MODEL), lambda i: (i, 0, k))
    gn = lambda k: pl.BlockSpec((1, 1, D_MODEL), lambda i: (k, 0, 0))
    return pl.pallas_call(
        _post_kernel,
        grid=(n_tiles,),
        in_specs=[row_spec(D_MODEL), row_spec(D_MODEL), pl.BlockSpec(wo.shape, lambda i: (0, 0)),
                  gn(1), gn(2), mod(2), mod(3), mod(4),
                  pl.BlockSpec(wr_p.shape, lambda i: (0, 0)), pl.BlockSpec((1, LANE), lambda i: (0, 0))],
        out_specs=[row_spec(D_MODEL), row_spec(D_MODEL), row_spec(LANE), row_spec(LANE),
                   pl.BlockSpec((1, 8, LANE), lambda i: (i, 0, 0))],
        out_shape=[jax.ShapeDtypeStruct((rows, D_MODEL), F32),
                   jax.ShapeDtypeStruct((rows, D_MODEL), MXU_DTYPE),
                   jax.ShapeDtypeStruct((rows, LANE), jnp.int32),
                   jax.ShapeDtypeStruct((rows, LANE), F32),
                   jax.ShapeDtypeStruct((rows // tm, 8, LANE), F32)],
        compiler_params=_cparams(1),
        name="post_mix",
    )(xa, t, wo, g_norm, g_norm, modt, modt, modt, wr_p, br_p.reshape(1, LANE))


def _sort_kernel(h2_ref, ri_ref, xs_ref):
    tm = h2_ref.shape[0]
    rif = ri_ref[...].astype(F32)
    pos = [_lane_pick(rif, TOP_K + k) for k in range(TOP_K)]
    h2 = h2_ref[...]
    for c in range(xs_ref.shape[0] // tm):
        col = (lax.broadcasted_iota(jnp.int32, (tm, tm), 1) + c * tm).astype(F32)
        onehot = jnp.zeros((tm, tm), F32)
        for k in range(TOP_K):
            onehot = jnp.where(pos[k] == col, 1.0, onehot)
        rows = lax.dot_general(onehot.astype(MXU_DTYPE), h2, (((0,), (0,)), ((), ())),
                               preferred_element_type=F32)
        xs_ref[c * tm:(c + 1) * tm, :] = rows.astype(xs_ref.dtype)


def _sort_rows(h2, ri, tm, n_tiles):
    seg_rows = TOP_K * tm + N_EXPERTS * SEG_ALIGN
    assert seg_rows % tm == 0
    return pl.pallas_call(
        _sort_kernel,
        grid=(n_tiles,),
        in_specs=[pl.BlockSpec((tm, D_MODEL), lambda i: (i, 0)),
                  pl.BlockSpec((tm, LANE), lambda i: (i, 0))],
        out_specs=pl.BlockSpec((seg_rows, D_MODEL), lambda i: (i, 0)),
        out_shape=jax.ShapeDtypeStruct((n_tiles * seg_rows, D_MODEL), MXU_DTYPE),
        compiler_params=_cparams(1),
        name="sort_rows",
    )(h2, ri)


def _expert_kernel(te_ref, r0_ref, na_ref, seg_ref, src_ref, used_ref, tlo_ref, thi_ref,
                   xs_hbm, wu_ref, bu_ref, wd_ref, bd_ref, ys_hbm,
                   xbuf, ybuf, sem_in, sem_out, *, n_tok_tiles):
    w = pl.program_id(0)
    n_active = na_ref[0]
    tm = ybuf.shape[0]
    nt = n_tok_tiles
    sizes = [s for s in (512, 256, 128, 64, 32, 16) if s <= tm]

    def for_pieces(length, fn):
        off = jnp.int32(0)
        for size in sizes:
            hit = (length & size) != 0

            @pl.when(hit)
            def _(off=off, size=size):
                fn(off, size)

            off = off + jnp.where(hit, size, 0)

    def for_segments(wt, fn):
        e = te_ref[wt]
        r0 = r0_ref[wt]

        def body(t, carry):
            s0 = seg_ref[e * (nt + 1) + t]
            s1 = seg_ref[e * (nt + 1) + t + 1]
            lo = jnp.maximum(s0, r0)
            length = jnp.maximum(jnp.minimum(s1, r0 + tm) - lo, 0)
            src = src_ref[e * nt + t] + (lo - s0)
            dst = lo - r0
            for_pieces(length, lambda off, size: fn(src + off, dst + off, size))
            return carry

        lax.fori_loop(tlo_ref[wt], thi_ref[wt], body, 0)

    def rows_in_use(wt):
        return used_ref[wt]

    def aligned(row):
        return row if isinstance(row, int) else pl.multiple_of(row, SEG_ALIGN)

    def in_copy(hbm_row, buf_row, size, slot):
        return pltpu.make_async_copy(xs_hbm.at[pl.ds(aligned(hbm_row), size)],
                                     xbuf.at[slot, pl.ds(aligned(buf_row), size)], sem_in.at[slot])

    def out_copy(hbm_row, buf_row, size):
        return pltpu.make_async_copy(ybuf.at[pl.ds(aligned(buf_row), size)],
                                     ys_hbm.at[pl.ds(aligned(hbm_row), size)], sem_out.at[0])

    def start_in(wt, slot):
        for_segments(wt, lambda h, b, size: in_copy(h, b, size, slot).start())

    def wait_in(wt, slot):
        for_pieces(rows_in_use(wt), lambda off, size: in_copy(0, 0, size, slot).wait())

    def wait_out(wt):
        for_pieces(rows_in_use(wt), lambda off, size: out_copy(0, 0, size).wait())

    @pl.when(w < n_active)
    def _():
        slot = w % 2

        @pl.when(w == 0)
        def _():
            xbuf[...] = jnp.zeros_like(xbuf)
            start_in(0, 0)

        wait_in(w, slot)

        @pl.when(w + 1 < n_active)
        def _():
            start_in(w + 1, 1 - slot)

        gu = jnp.dot(xbuf[slot], wu_ref[0], preferred_element_type=F32) + bu_ref[0]
        g = jnp.minimum(gu[:, :D_FF], SWIGLU_LIMIT)
        u = jnp.clip(gu[:, D_FF:], -SWIGLU_LIMIT, SWIGLU_LIMIT)
        hid = g * _sigmoid(SWIGLU_ALPHA * g) * (u + 1.0)
        y = jnp.dot(hid.astype(MXU_DTYPE), wd_ref[0], preferred_element_type=F32) + bd_ref[0]

        @pl.when(w > 0)
        def _():
            wait_out(w - 1)

        ybuf[...] = y.astype(ybuf.dtype)
        for_segments(w, lambda h, b, size: out_copy(h, b, size).start())

        @pl.when(w == n_active - 1)
        def _():
            wait_out(w)


def _experts(tables, xs, w_up, b_up, w_down, b_down, n_tok_tiles):
    tile_expert = tables[0]
    tm = MOE_TILE
    grid_spec = pltpu.PrefetchScalarGridSpec(
        num_scalar_prefetch=len(tables),
        grid=(tile_expert.shape[0],),
        in_specs=[pl.BlockSpec(memory_space=pl.ANY),
                  pl.BlockSpec((1, D_MODEL, 2 * D_FF), lambda t, te, *_: (te[t], 0, 0)),
                  pl.BlockSpec((1, 1, 2 * D_FF), lambda t, te, *_: (te[t], 0, 0)),
                  pl.BlockSpec((1, D_FF, D_MODEL), lambda t, te, *_: (te[t], 0, 0)),
                  pl.BlockSpec((1, 1, D_MODEL), lambda t, te, *_: (te[t], 0, 0))],
        out_specs=pl.BlockSpec(memory_space=pl.ANY),
        scratch_shapes=[pltpu.VMEM((2, tm, D_MODEL), xs.dtype), pltpu.VMEM((tm, D_MODEL), xs.dtype),
                        pltpu.SemaphoreType.DMA((2,)), pltpu.SemaphoreType.DMA((1,))],
    )
    return pl.pallas_call(
        functools.partial(_expert_kernel, n_tok_tiles=n_tok_tiles),
        grid_spec=grid_spec,
        out_shape=jax.ShapeDtypeStruct(xs.shape, xs.dtype),
        compiler_params=_cparams(1),
        name="experts",
    )(*tables, xs, w_up, b_up, w_down, b_down)


def _combine_kernel(ys_ref, ri_ref, rw_ref, cnt_ref, x_ref, g3_ref, gt_ref, xo_ref):
    tm = x_ref.shape[0]
    rif = ri_ref[...].astype(F32)
    rw = rw_ref[...]
    pos = [_lane_pick(rif, TOP_K + k) for k in range(TOP_K)]
    wgt = [_lane_pick(rw, k) for k in range(TOP_K)]
    used = jnp.sum(_seg_pad(cnt_ref[0][0:1, :]), axis=1, keepdims=True)
    f = jnp.zeros(x_ref.shape, F32)
    for c in range(ys_ref.shape[0] // tm):
        col = (lax.broadcasted_iota(jnp.int32, (tm, tm), 1) + c * tm).astype(F32)
        scat = jnp.zeros((tm, tm), F32)
        for k in range(TOP_K):
            scat = jnp.where(pos[k] == col, wgt[k], scat)
        row = (lax.broadcasted_iota(jnp.int32, (tm, 1), 0) + c * tm).astype(F32)
        ys = ys_ref[c * tm:(c + 1) * tm, :]
        ys = jnp.where(row < used, ys, jnp.zeros_like(ys))
        f = f + jnp.dot(scat.astype(MXU_DTYPE), ys, preferred_element_type=F32)
    xo_ref[...] = x_ref[...] + gt_ref[0] * _rms(f, g3_ref[0])


def _combine(ys, ri, rw, cnt, xa, g_norm, modt, tm, n_tiles):
    seg_rows = ys.shape[0] // n_tiles
    return pl.pallas_call(
        _combine_kernel,
        grid=(n_tiles,),
        in_specs=[pl.BlockSpec((seg_rows, D_MODEL), lambda i: (i, 0)),
                  pl.BlockSpec((tm, LANE), lambda i: (i, 0)),
                  pl.BlockSpec((tm, LANE), lambda i: (i, 0)),
                  pl.BlockSpec((1, 8, LANE), lambda i: (i, 0, 0)),
                  pl.BlockSpec((tm, D_MODEL), lambda i: (i, 0)),
                  pl.BlockSpec((1, 1, D_MODEL), lambda i: (3, 0, 0)),
                  pl.BlockSpec((1, 1, D_MODEL), lambda i: (i, 0, 5))],
        out_specs=pl.BlockSpec((tm, D_MODEL), lambda i: (i, 0)),
        out_shape=jax.ShapeDtypeStruct((n_tiles * tm, D_MODEL), F32),
        compiler_params=_cparams(1),
        name="combine",
    )(ys, ri, rw, cnt, xa, g_norm, modt)


def _swap16(idx):
    g = np.asarray(idx).reshape(-1, 2, ROPE_FREQS)
    return g[:, ::-1, :].reshape(-1)


def _in_proj_columns():
    sizes = (Q_LORA, KV_LORA, QK_ROPE, M_HEADS * M_QK, M_HEADS * M_QK, M_HEADS * M_V, M_HEADS * M_V,
             2 * 2 * M_HEADS, CONV_WIDTH, CONV_WIDTH, CONV_WIDTH, 3 * D_MODEL)
    starts = np.concatenate([[0], np.cumsum(sizes)[:-1]])
    seg = lambda k: np.arange(starts[k], starts[k] + sizes[k])
    cq, ckv, kr, mq, mk, mv, mo, mif, cu, cb, cc, cg = (seg(k) for k in range(12))
    mqk = np.concatenate([np.concatenate([mq[h * M_QK:(h + 1) * M_QK], mk[h * M_QK:(h + 1) * M_QK]])
                          for h in range(M_HEADS)])
    main = np.concatenate([cq, ckv, mqk, mv, mo, cu, cb, cc, cg])
    pad = lambda n: np.full((n,), -1)
    special = np.concatenate([kr, kr, _swap16(kr), _swap16(kr), mif, pad(LANE - mif.size), pad(LANE)])
    cols = np.concatenate([main, special])
    assert cols.size == (N_MAIN_TILES + 1) * COL_TILE
    return cols


def _take_cols(w, cols, dtype):
    cols = np.asarray(cols)
    breaks = np.flatnonzero((np.diff(cols) != 1) & ~((cols[1:] < 0) & (cols[:-1] < 0))) + 1
    parts = []
    for run in np.split(cols, breaks):
        if run[0] < 0:
            parts.append(jnp.zeros(w.shape[:-1] + (run.size,), dtype))
        else:
            parts.append(w[..., int(run[0]):int(run[-1]) + 1].astype(dtype))
    return jnp.concatenate(parts, axis=-1)


def _permute_cols_kernel(w_ref, *out_refs, col_groups):
    w = w_ref[0]
    for o_ref, cols in zip(out_refs, col_groups):
        o_ref[0] = _take_cols(w, cols, o_ref.dtype)


def _permute_cols(w, col_groups, dtype):
    depth, k, n = w.shape
    bk = 256
    return pl.pallas_call(
        functools.partial(_permute_cols_kernel, col_groups=col_groups),
        grid=(depth, k // bk),
        in_specs=[pl.BlockSpec((1, bk, n), lambda l, i: (l, i, 0))],
        out_specs=[pl.BlockSpec((1, bk, len(c)), lambda l, i: (l, i, 0)) for c in col_groups],
        out_shape=[jax.ShapeDtypeStruct((depth, k, len(c)), dtype) for c in col_groups],
        compiler_params=_cparams(2),
        name="permute_cols",
    )(w)


def _q_up_columns():
    per = QK_NOPE + QK_ROPE
    nope = np.concatenate([np.arange(h * per, h * per + QK_NOPE) for h in range(MLA_HEADS)])
    rope = np.concatenate([np.arange(h * per + QK_NOPE, (h + 1) * per) for h in range(MLA_HEADS)])
    return np.concatenate([nope, rope, _swap16(rope)])


def _kv_up_columns():
    per = QK_NOPE + V_HEAD
    kn = np.concatenate([np.arange(h * per, h * per + QK_NOPE) for h in range(MLA_HEADS)])
    vv = np.concatenate([np.arange(h * per + QK_NOPE, (h + 1) * per) for h in range(MLA_HEADS)])
    return np.concatenate([kn, vv])


def _rope_tables(T, rt):
    pos = jnp.arange(T)
    inv_freq = ROPE_THETA ** (-jnp.arange(ROPE_FREQS, dtype=F32) / ROPE_FREQS)
    ang_r = (pos // GRID_W).astype(F32)[:, None] * inv_freq
    ang_c = (pos % GRID_W).astype(F32)[:, None] * inv_freq
    cos64 = jnp.concatenate([jnp.cos(ang_r), jnp.cos(ang_r), jnp.cos(ang_c), jnp.cos(ang_c)], axis=1)
    sin64 = jnp.concatenate([-jnp.sin(ang_r), jnp.sin(ang_r), -jnp.sin(ang_c), jnp.sin(ang_c)], axis=1)
    cos_t = jnp.concatenate([jnp.tile(cos64, (1, 2)), jnp.ones((rt, LANE), F32)], axis=0)
    sin_t = jnp.concatenate([jnp.tile(sin64, (1, 2)), jnp.zeros((rt, LANE), F32)], axis=0)
    return cos_t, sin_t


def _prefix_sums(a):
    n = a.shape[-1]
    keep = jnp.arange(n)[:, None] <= jnp.arange(n)[None, :]
    return jnp.sum(jnp.where(keep, a[..., :, None], 0), axis=-2)


def _moe_tables(cnt, n_tiles, tok_tile):
    tm = MOE_TILE
    seg_rows = TOP_K * tok_tile + N_EXPERTS * SEG_ALIGN
    n_work = n_tiles * seg_rows // tm + N_EXPERTS
    counts = cnt[:n_tiles, 0, :N_EXPERTS].astype(jnp.int32)
    padded = (counts + SEG_ALIGN - 1) // SEG_ALIGN * SEG_ALIGN
    in_tile = _prefix_sums(padded) - padded
    src_row = (jnp.arange(n_tiles, dtype=jnp.int32)[:, None] * seg_rows + in_tile).T
    seg_start = jnp.concatenate([jnp.zeros((N_EXPERTS, 1), jnp.int32), _prefix_sums(padded.T)], axis=1)
    group_len = seg_start[:, -1]
    work_end = _prefix_sums((group_len + tm - 1) // tm)
    work = jnp.arange(n_work, dtype=jnp.int32)
    tile_expert = jnp.minimum(jnp.sum(work_end[None, :] <= work[:, None], axis=1), N_EXPERTS - 1)
    first_work = jnp.concatenate([jnp.zeros((1,), jnp.int32), work_end[:-1]])
    tile_row0 = (work - first_work[tile_expert]) * tm
    used = jnp.clip(group_len[tile_expert] - tile_row0, 0, tm)
    starts = seg_start[tile_expert]
    t_lo = jnp.sum(starts[:, 1:] <= tile_row0[:, None], axis=1)
    t_hi = jnp.sum(starts[:, :-1] < (tile_row0 + tm)[:, None], axis=1)
    i32 = lambda a: a.astype(jnp.int32)
    return (i32(tile_expert), i32(tile_row0), i32(work_end[-1:]), i32(seg_start.reshape(-1)),
            i32(src_row.reshape(-1)), i32(used), i32(t_lo), i32(t_hi))


def _layer(xa, mods, B, T, Tc, rt, tm, update_ctx, p, cos_t, sin_t, layer):
    rows = xa.shape[0]
    n_lat = B * T
    lat_rt, all_rt = n_lat // rt, rows // rt
    lat_tm, all_tm = n_lat // tm, rows // tm
    cond_rt = np.concatenate([np.arange(lat_rt) // (T // rt), np.full((all_rt - lat_rt,), B)])
    cond_tm = np.concatenate([np.arange(lat_tm) // (T // tm), np.full((all_tm - lat_tm,), B)])
    modt_rt = mods[cond_rt][:, None, :]
    modt_tm = mods[cond_tm][:, None, :]
    g_norm = p["g_norm"].reshape(4, 1, D_MODEL)

    z, zs = _inproj(xa, g_norm, modt_rt, p["w_in_main"], p["w_in_special"], layer, rt)
    q, k, v = _mla_project(z, zs, p["g_q_lora"], p["g_kv_lora"], p["w_q_up"], p["w_kv_up"],
                           cos_t, sin_t, rt, lat_rt, T // rt)
    a, w_up_c, w_down_c = _attention_latent(q, k, v, p["w_up"], p["w_down"], layer, p["depth"],
                                            B, T, Tc, min(ATTN_Q_TILE, T), min(ATTN_KV_CHUNK, T),
                                            rows if update_ctx else n_lat)
    w_up_c = w_up_c.reshape(N_EXPERTS, D_MODEL, 2 * D_FF)
    w_down_c = w_down_c.reshape(N_EXPERTS, D_FF, D_MODEL)
    if update_ctx:
        a = _attention_ctx(q, k, v, a, B, T, Tc)
    m = _mlstm(z, zs, p["b_if"], p["g_mlstm"], B, T, Tc)
    n_rt = all_rt if update_ctx else lat_rt
    n_tm = all_tm if update_ctx else lat_tm
    t = _merge(a, m, z, p["w_proj_a"], p["w_proj_b"], p["w_proj_c"], p["w_conv"], p["b_conv"],
               rt, n_rt, lat_rt, T // rt, T, Tc)
    xa, h2, ri, rw, cnt = _post(t, p["w_o"], xa, g_norm, modt_tm, p["w_router"], p["b_router"], tm, n_tm)
    xs = _sort_rows(h2, ri, tm, n_tm)
    ys = _experts(_moe_tables(cnt, n_tm, tm), xs, w_up_c, p["b_up"], w_down_c, p["b_down"], n_tm)
    return _combine(ys, ri, rw, cnt, xa, g_norm, modt_tm, tm, n_tm)


def kernel(x, c, ctx, c_ctx, w_mod, b_mod, g_norm, w_in, g_q_lora, g_kv_lora, w_q_up, w_kv_up, b_if, g_mlstm,
           w_conv, b_conv, w_proj_a, w_proj_b, w_proj_c, w_o, w_router, b_router, w_up, b_up, w_down, b_down):
    B, T, _ = x.shape
    Tc = ctx.shape[1]
    depth = w_mod.shape[0]
    rt = min(1024, T)
    tm = min(512, T)
    assert T % rt == 0 and (B * Tc) % rt == 0 and (B * Tc) % tm == 0
    assert T & (T - 1) == 0 and Tc & (Tc - 1) == 0 and T % GRID_W == 0

    xa = jnp.concatenate([x.reshape(B * T, D_MODEL), ctx.reshape(B * Tc, D_MODEL)], axis=0)
    cond = jnp.zeros((8, D_MODEL), F32).at[:B].set(c).at[B].set(c_ctx)
    cos_t, sin_t = _rope_tables(T, rt)
    cast = lambda w: w.astype(MXU_DTYPE)
    in_cols = _in_proj_columns()
    n_main = N_MAIN_TILES * COL_TILE
    w_in_main, w_in_special = _permute_cols(w_in, (in_cols[:n_main], in_cols[n_main:]), MXU_DTYPE)
    w_q_p = _take_cols(w_q_up, _q_up_columns(), MXU_DTYPE)
    w_kv_p = _take_cols(w_kv_up, _kv_up_columns(), MXU_DTYPE)
    w_up_r = w_up.reshape(depth * N_EXPERTS * D_MODEL, 2 * D_FF)
    w_down_r = w_down.reshape(depth * N_EXPERTS * D_FF, D_MODEL)
    wa_c, wb_c, wc_c, wo_c = cast(w_proj_a), cast(w_proj_b), cast(w_proj_c), cast(w_o)
    w_router_p = jnp.pad(w_router, ((0, 0), (0, 0), (0, LANE - N_EXPERTS)))
    b_router_p = jnp.pad(b_router, ((0, 0), (0, LANE - N_EXPERTS)), constant_values=NEG_BIG)
    b_if_p = jnp.pad(b_if.reshape(depth, 1, -1), ((0, 0), (0, 0), (0, LANE - 4 * M_HEADS)))
    mods = _adaln(cond, w_mod, b_mod)
    for l in range(depth):
        p = dict(
            g_norm=g_norm[l], w_in_main=w_in_main, w_in_special=w_in_special,
            g_q_lora=g_q_lora[l], g_kv_lora=g_kv_lora[l],
            w_q_up=w_q_p[l], w_kv_up=w_kv_p[l], b_if=b_if_p[l],
            g_mlstm=g_mlstm[l], w_conv=w_conv[l], b_conv=b_conv[l],
            w_proj_a=wa_c[l], w_proj_b=wb_c[l], w_proj_c=wc_c[l], w_o=wo_c[l],
            w_router=w_router_p[l], b_router=b_router_p[l],
            depth=depth, w_up=w_up_r, b_up=b_up[l].reshape(N_EXPERTS, 1, -1),
            w_down=w_down_r, b_down=b_down[l].reshape(N_EXPERTS, 1, -1),
        )
        xa = _layer(xa, mods[l], B, T, Tc, rt, tm, l < depth - 1, p, cos_t, sin_t, l)
    return xa.reshape(B, T, D_MODEL)
```

```python
import functools

import numpy as np
import jax
import jax.numpy as jnp
from jax import lax
from jax.experimental import pallas as pl
from jax.experimental.pallas import tpu as pltpu

F32 = jnp.float32
MXU_DTYPE = jnp.bfloat16

D_MODEL = 2048
GRID_W = 64
MLA_HEADS = 8
QK_NOPE = 128
QK_ROPE = 64
V_HEAD = 128
Q_LORA = 512
KV_LORA = 512
ROPE_THETA = 10000.0
ROPE_FREQS = QK_ROPE // 4
ATTN_SCALE = (QK_NOPE + QK_ROPE) ** -0.5
M_HEADS = 4
M_QK = 64
M_V = 128
CONV_WIDTH = 512
N_EXPERTS = 32
TOP_K = 4
D_FF = 1024
SWIGLU_ALPHA = 1.702
SWIGLU_LIMIT = 7.0
EPS = 1e-6

LANE = 128
HEAD_W = 2 * LANE
MLSTM_CHUNK = 256
COL_TILE = 512
INPROJ_TILE = 1024
LOG2_E = 1.4426950408889634
MERGE_TILE = 1024
ATTN_Q_TILE = 1024
ATTN_KV_CHUNK = 256
MOE_TILE = 512
SEG_ALIGN = 16
NEG_BIG = -1e30
VMEM_LIMIT = 56 * 1024 * 1024

ZT_CQ, ZT_CKV, ZT_MQK, ZT_MV, ZT_MO, ZT_CU, ZT_CB, ZT_CC, ZT_CG = 0, 1, 2, 3, 4, 5, 6, 7, 8
N_MAIN_TILES = 8 + 3 * D_MODEL // COL_TILE
assert (N_MAIN_TILES * COL_TILE) % INPROJ_TILE == 0


def _cparams(n_axes, vmem=VMEM_LIMIT):
    return pltpu.CompilerParams(dimension_semantics=("arbitrary",) * n_axes, vmem_limit_bytes=vmem)


def _rms(x, g):
    return x * lax.rsqrt(jnp.mean(x * x, axis=-1, keepdims=True) + EPS) * g


def _sigmoid(x):
    return 0.5 * jnp.tanh(0.5 * x) + 0.5


def _lane_pick(x, idx):
    lane = lax.broadcasted_iota(jnp.int32, x.shape, 1)
    return jnp.sum(jnp.where(lane == idx, x, 0.0), axis=1, keepdims=True)


def _mod_kernel(c_ref, w_ref, b_ref, o_ref):
    c = c_ref[...]
    s = c * jax.nn.sigmoid(c)
    o_ref[0] = jnp.dot(s.astype(MXU_DTYPE), w_ref[0].astype(MXU_DTYPE),
                       preferred_element_type=F32) + b_ref[0]


def _adaln(cond, w_mod, b_mod):
    depth, _, n = w_mod.shape
    tn = 1024
    return pl.pallas_call(
        _mod_kernel,
        grid=(depth, n // tn),
        in_specs=[pl.BlockSpec(cond.shape, lambda l, j: (0, 0)),
                  pl.BlockSpec((1, D_MODEL, tn), lambda l, j: (l, 0, j)),
                  pl.BlockSpec((1, 1, tn), lambda l, j: (l, 0, j))],
        out_specs=pl.BlockSpec((1, cond.shape[0], tn), lambda l, j: (l, 0, j)),
        out_shape=jax.ShapeDtypeStruct((depth, cond.shape[0], n), F32),
        compiler_params=_cparams(2),
        name="adaln",
    )(cond, w_mod, b_mod.reshape(depth, 1, n))


def _row_sources(parts, tile):
    first, second = parts
    n_first = first.shape[0] // tile
    if second is None:
        second, second_map = first, (lambda i, *_: (0, 0))
    else:
        second_map = lambda i, *_: (jnp.maximum(i - n_first, 0), 0)
    width = first.shape[1]
    specs = [pl.BlockSpec((tile, width), lambda i, *_: (jnp.minimum(i, n_first - 1), 0)),
             pl.BlockSpec((tile, width), second_map, pipeline_mode=pl.Buffered(1))]
    return (first, second), specs, n_first


def _inproj_kernel(x_ref, g_ref, sh_ref, sc_ref, w_ref, ws_ref, z_ref, zs_ref, h_scr):
    @pl.when(pl.program_id(1) == 0)
    def _():
        h = _rms(x_ref[...], g_ref[0]) * (1.0 + sc_ref[0]) + sh_ref[0]
        h_scr[...] = h.astype(h_scr.dtype)
        zs_ref[...] = jnp.dot(h_scr[...], ws_ref[0], preferred_element_type=F32)

    z_ref[...] = jnp.dot(h_scr[...], w_ref[0], preferred_element_type=F32).astype(z_ref.dtype)


def _inproj(xa, g_norm, modt, w_main, w_special, layer, rt):
    rows = xa.shape[0]
    n_main = w_main.shape[2]
    tn = INPROJ_TILE
    return pl.pallas_call(
        _inproj_kernel,
        grid=(rows // rt, n_main // tn),
        in_specs=[pl.BlockSpec((rt, D_MODEL), lambda i, j: (i, 0)),
                  pl.BlockSpec((1, 1, D_MODEL), lambda i, j: (0, 0, 0)),
                  pl.BlockSpec((1, 1, D_MODEL), lambda i, j: (i, 0, 0)),
                  pl.BlockSpec((1, 1, D_MODEL), lambda i, j: (i, 0, 1)),
                  pl.BlockSpec((1, D_MODEL, tn), lambda i, j: (layer, 0, j)),
                  pl.BlockSpec((1, D_MODEL, COL_TILE), lambda i, j: (layer, 0, 0))],
        out_specs=[pl.BlockSpec((rt, tn), lambda i, j: (i, j)),
                   pl.BlockSpec((rt, COL_TILE), lambda i, j: (i, 0))],
        out_shape=[jax.ShapeDtypeStruct((rows, n_main), MXU_DTYPE),
                   jax.ShapeDtypeStruct((rows, COL_TILE), F32)],
        scratch_shapes=[pltpu.VMEM((rt, D_MODEL), MXU_DTYPE)],
        compiler_params=_cparams(2),
        name="inproj",
    )(xa, g_norm, modt, modt, w_main, w_special)


def _mla_kernel(cq_ref, ckv_ref, zs_ref, gq_ref, gkv_ref, wq_ref, wkv_ref, cos_ref, sin_ref,
                q_ref, k_ref, v_ref):
    def norm(z_ref, g_ref):
        return _rms(z_ref[...].astype(F32), g_ref[...]).astype(MXU_DTYPE)

    qf = jnp.dot(norm(cq_ref, gq_ref), wq_ref[...], preferred_element_type=F32) * (ATTN_SCALE * LOG2_E)
    kvf = jnp.dot(norm(ckv_ref, gkv_ref), wkv_ref[...], preferred_element_type=F32)
    cos = cos_ref[...]
    sin = sin_ref[...]
    nope_w = MLA_HEADS * QK_NOPE
    rope_w = MLA_HEADS * QK_ROPE
    for p in range(MLA_HEADS // 2):
        a = nope_w + LANE * p
        b = nope_w + rope_w + LANE * p
        rp = (qf[:, a:a + LANE] * cos + qf[:, b:b + LANE] * sin).astype(MXU_DTYPE)
        for h in (2 * p, 2 * p + 1):
            q_ref[:, HEAD_W *h:HEAD_W *h + LANE] = qf[:, LANE * h:LANE * (h + 1)].astype(MXU_DTYPE)
            q_ref[:, HEAD_W *h + LANE:HEAD_W *(h + 1)] = rp
    kr2 = zs_ref[:, 0:LANE] * cos + zs_ref[:, LANE:2 * LANE] * sin
    lane = lax.broadcasted_iota(jnp.int32, kr2.shape, 1)
    k_lo = jnp.where(lane < QK_ROPE, kr2, 0.0).astype(MXU_DTYPE)
    k_hi = jnp.where(lane >= QK_ROPE, kr2, 0.0).astype(MXU_DTYPE)
    for h in range(MLA_HEADS):
        k_ref[:, HEAD_W *h:HEAD_W *h + LANE] = kvf[:, LANE * h:LANE * (h + 1)].astype(MXU_DTYPE)
        k_ref[:, HEAD_W *h + LANE:HEAD_W *(h + 1)] = k_lo if h % 2 == 0 else k_hi
    ones = jnp.ones((kvf.shape[0], LANE), MXU_DTYPE)
    v0 = MLA_HEADS * QK_NOPE
    for h in range(MLA_HEADS):
        v_ref[:, HEAD_W *h:HEAD_W *h + LANE] = kvf[:, v0 + LANE * h:v0 + LANE * (h + 1)].astype(MXU_DTYPE)
        v_ref[:, HEAD_W *h + LANE:HEAD_W *(h + 1)] = ones


def _mla_project(z, zs, g_q, g_kv, wq_p, wkv_p, cos_t, sin_t, rt, n_lat_tiles, tiles_per_seq):
    rows = z.shape[0]
    nt = rows // rt

    def tab(i):
        return (jnp.where(i < n_lat_tiles, i % tiles_per_seq, tiles_per_seq), 0)

    return pl.pallas_call(
        _mla_kernel,
        grid=(nt,),
        in_specs=[pl.BlockSpec((rt, COL_TILE), lambda i: (i, ZT_CQ)),
                  pl.BlockSpec((rt, COL_TILE), lambda i: (i, ZT_CKV)),
                  pl.BlockSpec((rt, COL_TILE), lambda i: (i, 0)),
                  pl.BlockSpec((1, Q_LORA), lambda i: (0, 0)),
                  pl.BlockSpec((1, KV_LORA), lambda i: (0, 0)),
                  pl.BlockSpec(wq_p.shape, lambda i: (0, 0)),
                  pl.BlockSpec(wkv_p.shape, lambda i: (0, 0)),
                  pl.BlockSpec((rt, LANE), tab),
                  pl.BlockSpec((rt, LANE), tab)],
        out_specs=[pl.BlockSpec((rt, HEAD_W *MLA_HEADS), lambda i: (i, 0)),
                   pl.BlockSpec((rt, HEAD_W *MLA_HEADS), lambda i: (i, 0)),
                   pl.BlockSpec((rt, HEAD_W *MLA_HEADS), lambda i: (i, 0))],
        out_shape=[jax.ShapeDtypeStruct((rows, HEAD_W *MLA_HEADS), MXU_DTYPE)] * 3,
        compiler_params=_cparams(1),
        name="mla_project",
    )(z, z, zs, g_q.reshape(1, -1), g_kv.reshape(1, -1), wq_p, wkv_p, cos_t, sin_t)


def _attn_kernel(*refs, kv_chunks, n_cast=0):
    q_ref = refs[0]
    o_ref = refs[len(refs) - 1 - n_cast]
    for c in range(n_cast):
        src = refs[1 + 2 * len(kv_chunks) + c]
        dst = refs[len(refs) - n_cast + c]
        dst[...] = src[...].astype(dst.dtype)
    q = q_ref[...]
    tq = q.shape[0]
    m = jnp.full((tq, 1), -jnp.inf, F32)
    acc = jnp.zeros((tq, 2 * V_HEAD), F32)
    for s, n_chunks in enumerate(kv_chunks):
        k_ref, v_ref = refs[1 + 2 * s], refs[2 + 2 * s]
        ck = k_ref.shape[0] // n_chunks
        for c in range(n_chunks):
            k = k_ref[c * ck:(c + 1) * ck, :]
            v = v_ref[c * ck:(c + 1) * ck, :]
            sc = lax.dot_general(q, k, (((1,), (1,)), ((), ())), preferred_element_type=F32)
            m_new = jnp.maximum(m, jnp.max(sc, axis=-1, keepdims=True))
            p = jnp.exp2(sc - m_new).astype(MXU_DTYPE)
            acc = jnp.exp2(m - m_new) * acc + jnp.dot(p, v, preferred_element_type=F32)
            m = m_new
    o_ref[...] = (acc[:, :V_HEAD] / acc[:, V_HEAD:]).astype(o_ref.dtype)


def _attention_latent(q, k, v, w_up, w_down, layer, depth, B, T, Tc, tq, ck, rows):
    qt = T // tq
    ctx_blk0 = B * T // Tc
    steps = B * MLA_HEADS * qt
    step = lambda b, h, i: (b * MLA_HEADS + h) * qt + i
    cast_specs_in, cast_specs_out, cast_shapes = [], [], []
    for w in (w_up, w_down):
        n = w.shape[1]
        layer_rows = w.shape[0] // depth
        slab = layer_rows // steps
        assert slab * steps == layer_rows and slab % 16 == 0
        cast_specs_in.append(pl.BlockSpec((slab, n), lambda b, h, i: (layer * steps + step(b, h, i), 0)))
        cast_specs_out.append(pl.BlockSpec((slab, n), lambda b, h, i: (step(b, h, i), 0)))
        cast_shapes.append(jax.ShapeDtypeStruct((layer_rows, n), MXU_DTYPE))
    return pl.pallas_call(
        functools.partial(_attn_kernel, kv_chunks=(T // ck, 1), n_cast=2),
        grid=(B, MLA_HEADS, qt),
        in_specs=[pl.BlockSpec((tq, HEAD_W), lambda b, h, i: (b * qt + i, h)),
                  pl.BlockSpec((T, HEAD_W), lambda b, h, i: (b, h)),
                  pl.BlockSpec((T, HEAD_W), lambda b, h, i: (b, h)),
                  pl.BlockSpec((Tc, HEAD_W), lambda b, h, i: (ctx_blk0 + b, h)),
                  pl.BlockSpec((Tc, HEAD_W), lambda b, h, i: (ctx_blk0 + b, h))] + cast_specs_in,
        out_specs=[pl.BlockSpec((tq, V_HEAD), lambda b, h, i: (b * qt + i, h))] + cast_specs_out,
        out_shape=[jax.ShapeDtypeStruct((rows, V_HEAD * MLA_HEADS), MXU_DTYPE)] + cast_shapes,
        compiler_params=_cparams(3),
        name="attn_latent",
    )(q, k, v, k, v, w_up, w_down)


def _attention_ctx(q, k, v, a_buf, B, T, Tc):
    ctx_blk0 = B * T // Tc
    return pl.pallas_call(
        functools.partial(_attn_kernel, kv_chunks=(1,)),
        grid=(B, MLA_HEADS),
        in_specs=[pl.BlockSpec((Tc, HEAD_W), lambda b, h: (ctx_blk0 + b, h)),
                  pl.BlockSpec((Tc, HEAD_W), lambda b, h: (ctx_blk0 + b, h)),
                  pl.BlockSpec((Tc, HEAD_W), lambda b, h: (ctx_blk0 + b, h)),
                  pl.BlockSpec(memory_space=pl.ANY)],
        out_specs=pl.BlockSpec((Tc, V_HEAD), lambda b, h: (ctx_blk0 + b, h)),
        out_shape=jax.ShapeDtypeStruct(a_buf.shape, a_buf.dtype),
        input_output_aliases={3: 0},
        compiler_params=_cparams(2),
        name="attn_ctx",
    )(q, k, v, a_buf)


def _mlstm_chunk(qk, v, g, lf_cum, gt_ref, head, reverse, ct_ref, m_prev, consts):
    tri, row_iota, mk, ones_tile, lane = consts
    L = qk.shape[0]
    base = 8 if reverse else 0
    col_i, col_f = base + head, base + 4 + head
    ql = jnp.where(lane < M_QK, qk * (M_QK ** -0.5), jnp.zeros_like(qk))
    kl = jnp.dot(qk, mk, preferred_element_type=F32).astype(MXU_DTYPE)
    v_aug = jnp.concatenate([v, ones_tile], axis=1)
    bc = _lane_pick(lf_cum, col_f)
    li = _lane_pick(g, col_i)
    r_row = gt_ref[0, col_i:col_i + 1, :] - gt_ref[1, col_f:col_f + 1, :]
    dlog = jnp.where(tri, bc + r_row, -jnp.inf)
    inter = bc + m_prev
    m_t = jnp.maximum(inter, jnp.max(dlog, axis=1, keepdims=True))
    dexp = jnp.exp(dlog - m_t)
    s_raw = lax.dot_general(ql, kl, (((1,), (1,)), ((), ())), preferred_element_type=F32)
    s_mat = (s_raw * dexp).astype(MXU_DTYPE)
    a = jnp.exp(inter - m_t)
    ct = ct_ref[...]
    nd = (jnp.dot(s_mat, v_aug, preferred_element_type=F32)
          + a * jnp.dot(ql, ct.astype(MXU_DTYPE), preferred_element_type=F32))
    num, den = nd[:, :M_V], nd[:, M_V:]
    h_out = num / jnp.maximum(jnp.abs(den), jnp.exp(-m_t))
    last = 0 if reverse else L - 1
    total = jnp.sum(jnp.where(row_iota == last, bc, 0.0), axis=0, keepdims=True)
    w_col = total - bc + li
    m_new = jnp.maximum(total + m_prev, jnp.max(w_col, axis=0, keepdims=True))
    decay = jnp.exp(total + m_prev - m_new)
    e = jnp.exp(w_col - m_new)
    ek = (kl.astype(F32) * e).astype(MXU_DTYPE)
    upd = lax.dot_general(ek, v_aug, (((0,), (0,)), ((), ())), preferred_element_type=F32)
    ct_ref[...] = decay * ct + upd
    return h_out, m_new


def _mlstm_kernel(zl_qk, zl_v, zl_mo, gl_ref, zc_qk, zc_v, zc_mo, gc_ref, bif_ref, gm_ref,
                  ol_ref, oc_ref, acc_l, acc_c, ct_scr, gt_scr):
    L = MLSTM_CHUNK
    T = zl_qk.shape[0]
    nc = T // L
    r = lax.broadcasted_iota(jnp.int32, (L, L), 0)
    c = lax.broadcasted_iota(jnp.int32, (L, L), 1)
    tri_f = c <= r
    tri_b = c >= r
    tmat_f = tri_f.astype(jnp.bfloat16)
    tmat_b = tri_b.astype(jnp.bfloat16)
    row_iota = lax.broadcasted_iota(jnp.int32, (L, 1), 0)
    lane = lax.broadcasted_iota(jnp.int32, (L, LANE), 1)
    rr = lax.broadcasted_iota(jnp.int32, (LANE, LANE), 0)
    cc = lax.broadcasted_iota(jnp.int32, (LANE, LANE), 1)
    mk = (rr == cc + M_QK).astype(MXU_DTYPE)
    ones_tile = jnp.ones((L, LANE), MXU_DTYPE)
    consts_f = (tri_f, row_iota, mk, ones_tile, lane)
    consts_b = (tri_b, row_iota, mk, ones_tile, lane)
    bias = bif_ref[...]

    def gate_prep(g_raw, tmat, slot):
        g = g_raw + bias
        log_sig = jnp.minimum(g, 0.0) - jnp.log1p(jnp.exp(-jnp.abs(g)))
        hi = log_sig.astype(jnp.bfloat16)
        rest = log_sig - hi.astype(F32)
        mid = rest.astype(jnp.bfloat16)
        low = (rest - mid.astype(F32)).astype(jnp.bfloat16)
        sums = jnp.dot(tmat, jnp.concatenate([hi, mid, low], axis=1), preferred_element_type=F32)
        cum = sums[:, :LANE] + sums[:, LANE:2 * LANE] + sums[:, 2 * LANE:]
        gt_scr[slot, 0] = g.T
        gt_scr[slot, 1] = cum.T
        return g, cum

    def step(qk_ref, v_ref, g_ref, acc_ref, row_f, row_b, ms):
        g_f, cum_f = gate_prep(g_ref[pl.ds(row_f, L), :], tmat_f, 0)
        g_b, cum_b = gate_prep(g_ref[pl.ds(row_b, L), :], tmat_b, 1)
        new_ms = []
        for h in range(M_HEADS):
            cols = slice(h * LANE, (h + 1) * LANE)
            hf, mf = _mlstm_chunk(qk_ref[pl.ds(row_f, L), cols], v_ref[pl.ds(row_f, L), cols],
                                  g_f, cum_f, gt_scr.at[0], h, False,
                                  ct_scr.at[h], ms[h], consts_f)
            acc_ref[pl.ds(row_f, L), cols] += hf
            hb, mb = _mlstm_chunk(qk_ref[pl.ds(row_b, L), cols], v_ref[pl.ds(row_b, L), cols],
                                  g_b, cum_b, gt_scr.at[1], h, True,
                                  ct_scr.at[M_HEADS + h], ms[M_HEADS + h], consts_b)
            acc_ref[pl.ds(row_b, L), cols] += hb
            new_ms.append((mf, mb))
        return tuple(x[0] for x in new_ms) + tuple(x[1] for x in new_ms)

    ct_scr[...] = jnp.zeros_like(ct_scr)
    acc_l[...] = jnp.zeros_like(acc_l)
    acc_c[...] = jnp.zeros_like(acc_c)
    zero = jnp.zeros((1, 1), F32)
    ms = step(zc_qk, zc_v, gc_ref, acc_c, 0, 0, (zero,) * (2 * M_HEADS))

    def body(i, ms):
        row_f = pl.multiple_of(i * L, L)
        row_b = pl.multiple_of((nc - 1 - i) * L, L)
        return step(zl_qk, zl_v, gl_ref, acc_l, row_f, row_b, ms)

    lax.fori_loop(0, nc, body, ms)

    gm = gm_ref[...]

    def finish(acc_ref, mo_ref, o_ref, row):
        for h in range(M_HEADS):
            cols = slice(h * LANE, (h + 1) * LANE)
            hn = _rms(acc_ref[pl.ds(row, L), cols], gm[:, cols])
            gate = jax.nn.sigmoid(mo_ref[pl.ds(row, L), cols].astype(F32))
            o_ref[pl.ds(row, L), cols] = (hn * gate).astype(o_ref.dtype)

    finish(acc_c, zc_mo, oc_ref, 0)

    def fin_body(i, carry):
        finish(acc_l, zl_mo, ol_ref, pl.multiple_of(i * L, L))
        return carry

    lax.fori_loop(0, nc, fin_body, 0)


def _mlstm(z, zs, bif_row, g_mlstm, B, T, Tc):
    assert Tc == MLSTM_CHUNK and T % MLSTM_CHUNK == 0
    rows = z.shape[0]
    ctx_blk0 = B * T // Tc
    w = M_HEADS * LANE
    out_l, out_c = pl.pallas_call(
        _mlstm_kernel,
        grid=(B,),
        in_specs=[pl.BlockSpec((T, w), lambda b: (b, ZT_MQK)),
                  pl.BlockSpec((T, w), lambda b: (b, ZT_MV)),
                  pl.BlockSpec((T, w), lambda b: (b, ZT_MO)),
                  pl.BlockSpec((T, LANE), lambda b: (b, 2)),
                  pl.BlockSpec((Tc, w), lambda b: (ctx_blk0 + b, ZT_MQK)),
                  pl.BlockSpec((Tc, w), lambda b: (ctx_blk0 + b, ZT_MV)),
                  pl.BlockSpec((Tc, w), lambda b: (ctx_blk0 + b, ZT_MO)),
                  pl.BlockSpec((Tc, LANE), lambda b: (ctx_blk0 + b, 2)),
                  pl.BlockSpec((1, LANE), lambda b: (0, 0)),
                  pl.BlockSpec((1, w), lambda b: (0, 0))],
        out_specs=[pl.BlockSpec((T, w), lambda b: (b, 0)),
                   pl.BlockSpec((Tc, w), lambda b: (b, 0))],
        out_shape=[jax.ShapeDtypeStruct((B * T, w), MXU_DTYPE),
                   jax.ShapeDtypeStruct((B * Tc, w), MXU_DTYPE)],
        scratch_shapes=[pltpu.VMEM((T, w), F32), pltpu.VMEM((Tc, w), F32),
                        pltpu.VMEM((2 * M_HEADS, LANE, 2 * LANE), F32),
                        pltpu.VMEM((2, 2, LANE, MLSTM_CHUNK), F32)],
        compiler_params=_cparams(1),
        name="mlstm",
    )(z, z, z, zs, z, z, z, zs, bif_row, g_mlstm.reshape(1, -1))
    return out_l, out_c


def _merge_kernel(ml_ref, mc_ref, a_ref, cu_ref, cb_ref, cc_ref, cu_p, cc_p, cu_n, cc_n, ga_ref, gm_ref, gs_ref,
                  wa_ref, wb_ref, wc_ref, wconv_ref, bconv_ref, t_ref, s_scr,
                  *, n_lat_tiles, tiles_per_seq, seq_len, ctx_len):
    i = pl.program_id(0)
    j = pl.program_id(1)
    rt = a_ref.shape[0]
    m = jnp.where(i < n_lat_tiles, ml_ref[...], mc_ref[...])

    @pl.when(j == 0)
    def _():
        u = cc_ref[...].astype(F32) * cu_ref[...].astype(F32)
        hp = cc_p[...].astype(F32) * cu_p[...].astype(F32)
        hn = cc_n[...].astype(F32) * cu_n[...].astype(F32)
        row = lax.broadcasted_iota(jnp.int32, (rt, 1), 0)
        is_ctx = i >= n_lat_tiles
        seg = jnp.where(is_ctx, ctx_len, seq_len)
        off = jnp.where(is_ctx, (i - n_lat_tiles) * rt, (i % tiles_per_seq) * rt)
        pos = (off + row) & (seg - 1)
        up = pltpu.roll(u, 1, 0)
        up = jnp.where(row == 0, hp[hp.shape[0] - 1:, :], up)
        up = jnp.where(pos == 0, 0.0, up)
        un = pltpu.roll(u, rt - 1, 0)
        un = jnp.where(row == rt - 1, hn[0:1, :], un)
        un = jnp.where(pos == seg - 1, 0.0, un)
        wconv = wconv_ref[...]
        y = up * wconv[0:1, :] + u * wconv[1:2, :] + un * wconv[2:3, :] + bconv_ref[...]
        s_scr[...] = (cb_ref[...].astype(F32) * y).astype(s_scr.dtype)

    def sig(ref):
        return _sigmoid(ref[...].astype(F32))

    t = (sig(ga_ref) * jnp.dot(a_ref[...], wa_ref[...], preferred_element_type=F32)
         + sig(gm_ref) * jnp.dot(m, wb_ref[...], preferred_element_type=F32)
         + sig(gs_ref) * jnp.dot(s_scr[...], wc_ref[...], preferred_element_type=F32))
    t_ref[...] = t.astype(t_ref.dtype)


def _merge(a, m, z, wa, wb, wc, w_conv, b_conv, rt, n_tiles, n_lat_tiles, tiles_per_seq, T, Tc):
    rows = n_tiles * rt
    hb = 16
    rb = rt // hb
    last_hb = z.shape[0] // hb - 1
    tn = MERGE_TILE
    nj = D_MODEL // tn
    gt = D_MODEL // tn
    g0 = ZT_CG * COL_TILE // tn
    assert (ZT_CG * COL_TILE) % tn == 0

    def prev_blk(col):
        return lambda i, j: (jnp.maximum(i * rb - 1, 0), col)

    def next_blk(col):
        return lambda i, j: (jnp.minimum((i + 1) * rb, last_hb), col)

    kern = functools.partial(_merge_kernel, n_lat_tiles=n_lat_tiles, tiles_per_seq=tiles_per_seq,
                             seq_len=T, ctx_len=Tc)
    m_arrays, m_specs, m_first = _row_sources(m, rt)
    assert m_first == n_lat_tiles
    return pl.pallas_call(
        kern,
        grid=(n_tiles, nj),
        in_specs=m_specs + [pl.BlockSpec((rt, a.shape[1]), lambda i, j: (i, 0)),
                  pl.BlockSpec((rt, COL_TILE), lambda i, j: (i, ZT_CU)),
                  pl.BlockSpec((rt, COL_TILE), lambda i, j: (i, ZT_CB)),
                  pl.BlockSpec((rt, COL_TILE), lambda i, j: (i, ZT_CC)),
                  pl.BlockSpec((hb, COL_TILE), prev_blk(ZT_CU)),
                  pl.BlockSpec((hb, COL_TILE), prev_blk(ZT_CC)),
                  pl.BlockSpec((hb, COL_TILE), next_blk(ZT_CU)),
                  pl.BlockSpec((hb, COL_TILE), next_blk(ZT_CC)),
                  pl.BlockSpec((rt, tn), lambda i, j: (i, g0 + j)),
                  pl.BlockSpec((rt, tn), lambda i, j: (i, g0 + gt + j)),
                  pl.BlockSpec((rt, tn), lambda i, j: (i, g0 + 2 * gt + j)),
                  pl.BlockSpec((wa.shape[0], tn), lambda i, j: (0, j)),
                  pl.BlockSpec((wb.shape[0], tn), lambda i, j: (0, j)),
                  pl.BlockSpec((wc.shape[0], tn), lambda i, j: (0, j)),
                  pl.BlockSpec(w_conv.shape, lambda i, j: (0, 0)),
                  pl.BlockSpec((1, CONV_WIDTH), lambda i, j: (0, 0))],
        out_specs=pl.BlockSpec((rt, tn), lambda i, j: (i, j)),
        out_shape=jax.ShapeDtypeStruct((rows, D_MODEL), MXU_DTYPE),
        scratch_shapes=[pltpu.VMEM((rt, CONV_WIDTH), MXU_DTYPE)],
        compiler_params=_cparams(2),
        name="merge",
    )(*m_arrays, a, z, z, z, z, z, z, z, z, z, z, wa, wb, wc, w_conv, b_conv.reshape(1, -1))


def _dot3(x, w):
    xh = x.astype(jnp.bfloat16)
    xl = (x - xh.astype(F32)).astype(jnp.bfloat16)
    wh = w.astype(jnp.bfloat16)
    wl = (w - wh.astype(F32)).astype(jnp.bfloat16)
    n = w.shape[1]
    hi = jnp.dot(xh, jnp.concatenate([wh, wl], axis=1), preferred_element_type=F32)
    return hi[:, :n] + hi[:, n:] + jnp.dot(xl, wh, preferred_element_type=F32)


def _seg_pad(counts):
    return jnp.ceil(counts * (1.0 / SEG_ALIGN)) * SEG_ALIGN


def _exclusive_lane_cumsum(row):
    upper = (lax.broadcasted_iota(jnp.int32, (LANE, LANE), 0)
             < lax.broadcasted_iota(jnp.int32, (LANE, LANE), 1)).astype(jnp.bfloat16)
    out = jnp.dot(jnp.broadcast_to(row, (8, LANE)).astype(jnp.bfloat16), upper, preferred_element_type=F32)
    return out[0:1, :]


def _post_kernel(x_ref, t_ref, wo_ref, g1_ref, g2_ref, gt_ref, sh_ref, sc_ref, wr_ref, br_ref,
                 xo_ref, h2_ref, ri_ref, rw_ref, cnt_ref):
    half = t_ref.shape[0] // 2
    parts = []
    for r in (slice(0, half), slice(half, 2 * half)):
        y = jnp.dot(t_ref[r, :], wo_ref[...], preferred_element_type=F32)
        xn = x_ref[r, :] + gt_ref[0] * _rms(y, g1_ref[0])
        xo_ref[r, :] = xn
        h2 = _rms(xn, g2_ref[0]) * (1.0 + sc_ref[0]) + sh_ref[0]
        h2_ref[r, :] = h2.astype(h2_ref.dtype)
        parts.append(_dot3(h2, wr_ref[...]) + br_ref[...])
    logits = jnp.concatenate(parts, axis=0)
    lane = lax.broadcasted_iota(jnp.int32, logits.shape, 1)
    lane_f = lane.astype(F32)
    vals, ids = [], []
    for _ in range(TOP_K):
        mx = jnp.max(logits, axis=1, keepdims=True)
        idx = jnp.min(jnp.where(logits == mx, lane_f, float(LANE)), axis=1, keepdims=True)
        logits = jnp.where(lane_f == idx, -jnp.inf, logits)
        vals.append(mx)
        ids.append(idx)
    es = [jnp.exp(v - vals[0]) for v in vals]
    tot = es[0] + es[1] + es[2] + es[3]
    tm = logits.shape[0]
    onehots = [lane_f == ids[k] for k in range(TOP_K)]
    asg = jnp.zeros(logits.shape, F32)
    for k in range(TOP_K):
        asg = jnp.where(onehots[k], 1.0, asg)
    earlier = (lax.broadcasted_iota(jnp.int32, (tm, tm), 1)
               < lax.broadcasted_iota(jnp.int32, (tm, tm), 0)).astype(jnp.bfloat16)
    before = jnp.dot(earlier, asg.astype(jnp.bfloat16), preferred_element_type=F32)
    counts = jnp.sum(asg, axis=0, keepdims=True)
    seg_off = _exclusive_lane_cumsum(_seg_pad(counts))
    rw = jnp.zeros(logits.shape, F32)
    ri = jnp.zeros(logits.shape, F32)
    for k in range(TOP_K):
        pos = jnp.sum(jnp.where(onehots[k], before + seg_off, 0.0), axis=1, keepdims=True)
        rw = jnp.where(lane == k, es[k] / tot, rw)
        ri = jnp.where(lane == k, ids[k], ri)
        ri = jnp.where(lane == TOP_K + k, pos, ri)
    rw_ref[...] = rw
    ri_ref[...] = ri.astype(jnp.int32)
    cnt_ref[0] = jnp.broadcast_to(counts, cnt_ref.shape[1:])


def _post(t, wo, xa, g_norm, modt, wr_p, br_p, tm, n_tiles):
    rows = n_tiles * tm
    row_spec = lambda w: pl.BlockSpec((tm, w), lambda i: (i, 0))
    mod = lambda k: pl.BlockSpec((1, 1, D_MODEL), lambda i: (i, 0, k))
    gn = lambda k: pl.BlockSpec((1, 1, D_MODEL), lambda i: (k, 0, 0))
    return pl.pallas_call(
        _post_kernel,
        grid=(n_tiles,),
        in_specs=[row_spec(D_MODEL), row_spec(D_MODEL), pl.BlockSpec(wo.shape, lambda i: (0, 0)),
                  gn(1), gn(2), mod(2), mod(3), mod(4),
                  pl.BlockSpec(wr_p.shape, lambda i: (0, 0)), pl.BlockSpec((1, LANE), lambda i: (0, 0))],
        out_specs=[row_spec(D_MODEL), row_spec(D_MODEL), row_spec(LANE), row_spec(LANE),
                   pl.BlockSpec((1, 8, LANE), lambda i: (i, 0, 0))],
        out_shape=[jax.ShapeDtypeStruct((rows, D_MODEL), F32),
                   jax.ShapeDtypeStruct((rows, D_MODEL), MXU_DTYPE),
                   jax.ShapeDtypeStruct((rows, LANE), jnp.int32),
                   jax.ShapeDtypeStruct((rows, LANE), F32),
                   jax.ShapeDtypeStruct((rows // tm, 8, LANE), F32)],
        compiler_params=_cparams(1),
        name="post_mix",
    )(xa, t, wo, g_norm, g_norm, modt, modt, modt, wr_p, br_p.reshape(1, LANE))


def _segment_chunks(seg_rows, tm):
    slack = seg_rows - TOP_K * tm
    chunks = [(c * tm, tm) for c in range(TOP_K)]
    return chunks + [(TOP_K * tm, slack // 2), (TOP_K * tm + slack // 2, slack - slack // 2)]


def _position_onehot(pos, values, first_row, rows, tm):
    col = (lax.broadcasted_iota(jnp.int32, (tm, rows), 1) + first_row).astype(F32)
    out = jnp.zeros((tm, rows), F32)
    for p, v in zip(pos, values):
        out = jnp.where(p == col, v, out)
    return out


def _sort_kernel(used_ref, h2_ref, ri_ref, xs_ref):
    tm = h2_ref.shape[0]
    used = used_ref[pl.program_id(0)]
    rif = ri_ref[...].astype(F32)
    pos = [_lane_pick(rif, TOP_K + k) for k in range(TOP_K)]

    def emit(first_row, rows):
        onehot = _position_onehot(pos, [1.0] * TOP_K, first_row, rows, tm)
        out = lax.dot_general(onehot.astype(MXU_DTYPE), h2_ref[...], (((0,), (0,)), ((), ())),
                              preferred_element_type=F32)
        xs_ref[first_row:first_row + rows, :] = out.astype(xs_ref.dtype)

    for first_row, rows in _segment_chunks(xs_ref.shape[0], tm):
        if first_row < TOP_K * tm:
            emit(first_row, rows)
        else:
            pl.when(used > first_row)(functools.partial(emit, first_row, rows))


def _sort_rows(tile_used, h2, ri, tm, n_tiles):
    seg_rows = TOP_K * tm + N_EXPERTS * SEG_ALIGN
    grid_spec = pltpu.PrefetchScalarGridSpec(
        num_scalar_prefetch=1,
        grid=(n_tiles,),
        in_specs=[pl.BlockSpec((tm, D_MODEL), lambda i, used: (i, 0)),
                  pl.BlockSpec((tm, LANE), lambda i, used: (i, 0))],
        out_specs=pl.BlockSpec((seg_rows, D_MODEL), lambda i, used: (i, 0)),
    )
    return pl.pallas_call(
        _sort_kernel,
        grid_spec=grid_spec,
        out_shape=jax.ShapeDtypeStruct((n_tiles * seg_rows, D_MODEL), MXU_DTYPE),
        compiler_params=_cparams(1),
        name="sort_rows",
    )(tile_used, h2, ri)


def _expert_kernel(te_ref, r0_ref, na_ref, seg_ref, src_ref, used_ref, tlo_ref, thi_ref,
                   xs_hbm, wu_ref, bu_ref, wd_ref, bd_ref, ys_hbm,
                   xbuf, ybuf, sem_in, sem_out, *, n_tok_tiles):
    w = pl.program_id(0)
    n_active = na_ref[0]
    tm = ybuf.shape[0]
    nt = n_tok_tiles
    sizes = [s for s in (512, 256, 128, 64, 32, 16) if s <= tm]

    def for_pieces(length, fn):
        off = jnp.int32(0)
        for size in sizes:
            hit = (length & size) != 0

            @pl.when(hit)
            def _(off=off, size=size):
                fn(off, size)

            off = off + jnp.where(hit, size, 0)

    def for_segments(wt, fn):
        e = te_ref[wt]
        r0 = r0_ref[wt]

        def body(t, carry):
            s0 = seg_ref[e * (nt + 1) + t]
            s1 = seg_ref[e * (nt + 1) + t + 1]
            lo = jnp.maximum(s0, r0)
            length = jnp.maximum(jnp.minimum(s1, r0 + tm) - lo, 0)
            src = src_ref[e * nt + t] + (lo - s0)
            dst = lo - r0
            for_pieces(length, lambda off, size: fn(src + off, dst + off, size))
            return carry

        lax.fori_loop(tlo_ref[wt], thi_ref[wt], body, 0)

    def rows_in_use(wt):
        return used_ref[wt]

    def aligned(row):
        return row if isinstance(row, int) else pl.multiple_of(row, SEG_ALIGN)

    def in_copy(hbm_row, buf_row, size, slot):
        return pltpu.make_async_copy(xs_hbm.at[pl.ds(aligned(hbm_row), size)],
                                     xbuf.at[slot, pl.ds(aligned(buf_row), size)], sem_in.at[slot])

    def out_copy(hbm_row, buf_row, size):
        return pltpu.make_async_copy(ybuf.at[pl.ds(aligned(buf_row), size)],
                                     ys_hbm.at[pl.ds(aligned(hbm_row), size)], sem_out.at[0])

    def start_in(wt, slot):
        for_segments(wt, lambda h, b, size: in_copy(h, b, size, slot).start())

    def wait_in(wt, slot):
        for_pieces(rows_in_use(wt), lambda off, size: in_copy(0, 0, size, slot).wait())

    def wait_out(wt):
        for_pieces(rows_in_use(wt), lambda off, size: out_copy(0, 0, size).wait())

    @pl.when(w < n_active)
    def _():
        slot = w % 2

        @pl.when(w == 0)
        def _():
            xbuf[...] = jnp.zeros_like(xbuf)
            start_in(0, 0)

        wait_in(w, slot)

        @pl.when(w + 1 < n_active)
        def _():
            start_in(w + 1, 1 - slot)

        gu = jnp.dot(xbuf[slot], wu_ref[0], preferred_element_type=F32) + bu_ref[0]
        g = jnp.minimum(gu[:, :D_FF], SWIGLU_LIMIT)
        u = jnp.clip(gu[:, D_FF:], -SWIGLU_LIMIT, SWIGLU_LIMIT)
        hid = g * _sigmoid(SWIGLU_ALPHA * g) * (u + 1.0)
        y = jnp.dot(hid.astype(MXU_DTYPE), wd_ref[0], preferred_element_type=F32) + bd_ref[0]

        @pl.when(w > 0)
        def _():
            wait_out(w - 1)

        ybuf[...] = y.astype(ybuf.dtype)
        for_segments(w, lambda h, b, size: out_copy(h, b, size).start())

        @pl.when(w == n_active - 1)
        def _():
            wait_out(w)


def _experts(tables, xs, w_up, b_up, w_down, b_down, n_tok_tiles):
    tile_expert = tables[0]
    tm = MOE_TILE
    grid_spec = pltpu.PrefetchScalarGridSpec(
        num_scalar_prefetch=len(tables),
        grid=(tile_expert.shape[0],),
        in_specs=[pl.BlockSpec(memory_space=pl.ANY),
                  pl.BlockSpec((1, D_MODEL, 2 * D_FF), lambda t, te, *_: (te[t], 0, 0)),
                  pl.BlockSpec((1, 1, 2 * D_FF), lambda t, te, *_: (te[t], 0, 0)),
                  pl.BlockSpec((1, D_FF, D_MODEL), lambda t, te, *_: (te[t], 0, 0)),
                  pl.BlockSpec((1, 1, D_MODEL), lambda t, te, *_: (te[t], 0, 0))],
        out_specs=pl.BlockSpec(memory_space=pl.ANY),
        scratch_shapes=[pltpu.VMEM((2, tm, D_MODEL), xs.dtype), pltpu.VMEM((tm, D_MODEL), xs.dtype),
                        pltpu.SemaphoreType.DMA((2,)), pltpu.SemaphoreType.DMA((1,))],
    )
    return pl.pallas_call(
        functools.partial(_expert_kernel, n_tok_tiles=n_tok_tiles),
        grid_spec=grid_spec,
        out_shape=jax.ShapeDtypeStruct(xs.shape, xs.dtype),
        compiler_params=_cparams(1),
        name="experts",
    )(*tables, xs, w_up, b_up, w_down, b_down)


def _combine_kernel(used_ref, ys_ref, ri_ref, rw_ref, x_ref, g3_ref, gt_ref, xo_ref, slack_scr):
    tm = x_ref.shape[0]
    used = used_ref[pl.program_id(0)]
    rif = ri_ref[...].astype(F32)
    rw = rw_ref[...]
    pos = [_lane_pick(rif, TOP_K + k) for k in range(TOP_K)]
    wgt = [_lane_pick(rw, k) for k in range(TOP_K)]

    def part(first_row, rows):
        scat = _position_onehot(pos, wgt, first_row, rows, tm)
        row = lax.broadcasted_iota(jnp.int32, (rows, 1), 0) + first_row
        ys = ys_ref[first_row:first_row + rows, :]
        ys = jnp.where(row < used, ys, jnp.zeros_like(ys))
        return jnp.dot(scat.astype(MXU_DTYPE), ys, preferred_element_type=F32)

    def add_slack(first_row, rows):
        slack_scr[...] += part(first_row, rows)

    slack_scr[...] = jnp.zeros_like(slack_scr)
    f = None
    for first_row, rows in _segment_chunks(ys_ref.shape[0], tm):
        if first_row < TOP_K * tm:
            f = part(first_row, rows) if f is None else f + part(first_row, rows)
        else:
            pl.when(used > first_row)(functools.partial(add_slack, first_row, rows))
    f = f + slack_scr[...]
    xo_ref[...] = x_ref[...] + gt_ref[0] * _rms(f, g3_ref[0])


def _combine(tile_used, ys, ri, rw, xa, g_norm, modt, tm, n_tiles):
    seg_rows = ys.shape[0] // n_tiles
    grid_spec = pltpu.PrefetchScalarGridSpec(
        num_scalar_prefetch=1,
        grid=(n_tiles,),
        in_specs=[pl.BlockSpec((seg_rows, D_MODEL), lambda i, used: (i, 0)),
                  pl.BlockSpec((tm, LANE), lambda i, used: (i, 0)),
                  pl.BlockSpec((tm, LANE), lambda i, used: (i, 0)),
                  pl.BlockSpec((tm, D_MODEL), lambda i, used: (i, 0)),
                  pl.BlockSpec((1, 1, D_MODEL), lambda i, used: (3, 0, 0)),
                  pl.BlockSpec((1, 1, D_MODEL), lambda i, used: (i, 0, 5))],
        out_specs=pl.BlockSpec((tm, D_MODEL), lambda i, used: (i, 0)),
        scratch_shapes=[pltpu.VMEM((tm, D_MODEL), F32)],
    )
    return pl.pallas_call(
        _combine_kernel,
        grid_spec=grid_spec,
        out_shape=jax.ShapeDtypeStruct((n_tiles * tm, D_MODEL), F32),
        compiler_params=_cparams(1),
        name="combine",
    )(tile_used, ys, ri, rw, xa, g_norm, modt)


def _swap16(idx):
    g = np.asarray(idx).reshape(-1, 2, ROPE_FREQS)
    return g[:, ::-1, :].reshape(-1)


def _in_proj_columns():
    sizes = (Q_LORA, KV_LORA, QK_ROPE, M_HEADS * M_QK, M_HEADS * M_QK, M_HEADS * M_V, M_HEADS * M_V,
             2 * 2 * M_HEADS, CONV_WIDTH, CONV_WIDTH, CONV_WIDTH, 3 * D_MODEL)
    starts = np.concatenate([[0], np.cumsum(sizes)[:-1]])
    seg = lambda k: np.arange(starts[k], starts[k] + sizes[k])
    cq, ckv, kr, mq, mk, mv, mo, mif, cu, cb, cc, cg = (seg(k) for k in range(12))
    mqk = np.concatenate([np.concatenate([mq[h * M_QK:(h + 1) * M_QK], mk[h * M_QK:(h + 1) * M_QK]])
                          for h in range(M_HEADS)])
    main = np.concatenate([cq, ckv, mqk, mv, mo, cu, cb, cc, cg])
    pad = lambda n: np.full((n,), -1)
    special = np.concatenate([kr, kr, _swap16(kr), _swap16(kr), mif, pad(LANE - mif.size), pad(LANE)])
    cols = np.concatenate([main, special])
    assert cols.size == (N_MAIN_TILES + 1) * COL_TILE
    return cols


def _take_cols(w, cols, dtype):
    cols = np.asarray(cols)
    breaks = np.flatnonzero((np.diff(cols) != 1) & ~((cols[1:] < 0) & (cols[:-1] < 0))) + 1
    parts = []
    for run in np.split(cols, breaks):
        if run[0] < 0:
            parts.append(jnp.zeros(w.shape[:-1] + (run.size,), dtype))
        else:
            parts.append(w[..., int(run[0]):int(run[-1]) + 1].astype(dtype))
    return jnp.concatenate(parts, axis=-1)


def _permute_cols_kernel(w_ref, *out_refs, col_groups):
    w = w_ref[0]
    for o_ref, cols in zip(out_refs, col_groups):
        o_ref[0] = _take_cols(w, cols, o_ref.dtype)


def _permute_cols(w, col_groups, dtype):
    depth, k, n = w.shape
    bk = 256
    return pl.pallas_call(
        functools.partial(_permute_cols_kernel, col_groups=col_groups),
        grid=(depth, k // bk),
        in_specs=[pl.BlockSpec((1, bk, n), lambda l, i: (l, i, 0))],
        out_specs=[pl.BlockSpec((1, bk, len(c)), lambda l, i: (l, i, 0)) for c in col_groups],
        out_shape=[jax.ShapeDtypeStruct((depth, k, len(c)), dtype) for c in col_groups],
        compiler_params=_cparams(2),
        name="permute_cols",
    )(w)


def _q_up_columns():
    per = QK_NOPE + QK_ROPE
    nope = np.concatenate([np.arange(h * per, h * per + QK_NOPE) for h in range(MLA_HEADS)])
    rope = np.concatenate([np.arange(h * per + QK_NOPE, (h + 1) * per) for h in range(MLA_HEADS)])
    return np.concatenate([nope, rope, _swap16(rope)])


def _kv_up_columns():
    per = QK_NOPE + V_HEAD
    kn = np.concatenate([np.arange(h * per, h * per + QK_NOPE) for h in range(MLA_HEADS)])
    vv = np.concatenate([np.arange(h * per + QK_NOPE, (h + 1) * per) for h in range(MLA_HEADS)])
    return np.concatenate([kn, vv])


def _rope_tables(T, rt):
    pos = jnp.arange(T)
    inv_freq = ROPE_THETA ** (-jnp.arange(ROPE_FREQS, dtype=F32) / ROPE_FREQS)
    ang_r = (pos // GRID_W).astype(F32)[:, None] * inv_freq
    ang_c = (pos % GRID_W).astype(F32)[:, None] * inv_freq
    cos64 = jnp.concatenate([jnp.cos(ang_r), jnp.cos(ang_r), jnp.cos(ang_c), jnp.cos(ang_c)], axis=1)
    sin64 = jnp.concatenate([-jnp.sin(ang_r), jnp.sin(ang_r), -jnp.sin(ang_c), jnp.sin(ang_c)], axis=1)
    cos_t = jnp.concatenate([jnp.tile(cos64, (1, 2)), jnp.ones((rt, LANE), F32)], axis=0)
    sin_t = jnp.concatenate([jnp.tile(sin64, (1, 2)), jnp.zeros((rt, LANE), F32)], axis=0)
    return cos_t, sin_t


def _prefix_sums(a):
    n = a.shape[-1]
    keep = jnp.arange(n)[:, None] <= jnp.arange(n)[None, :]
    return jnp.sum(jnp.where(keep, a[..., :, None], 0), axis=-2)


def _moe_tables(cnt, n_tiles, tok_tile):
    tm = MOE_TILE
    seg_rows = TOP_K * tok_tile + N_EXPERTS * SEG_ALIGN
    n_work = n_tiles * seg_rows // tm + N_EXPERTS
    counts = cnt[:n_tiles, 0, :N_EXPERTS].astype(jnp.int32)
    padded = (counts + SEG_ALIGN - 1) // SEG_ALIGN * SEG_ALIGN
    in_tile = _prefix_sums(padded) - padded
    src_row = (jnp.arange(n_tiles, dtype=jnp.int32)[:, None] * seg_rows + in_tile).T
    seg_start = jnp.concatenate([jnp.zeros((N_EXPERTS, 1), jnp.int32), _prefix_sums(padded.T)], axis=1)
    group_len = seg_start[:, -1]
    work_end = _prefix_sums((group_len + tm - 1) // tm)
    work = jnp.arange(n_work, dtype=jnp.int32)
    tile_expert = jnp.minimum(jnp.sum(work_end[None, :] <= work[:, None], axis=1), N_EXPERTS - 1)
    first_work = jnp.concatenate([jnp.zeros((1,), jnp.int32), work_end[:-1]])
    tile_row0 = (work - first_work[tile_expert]) * tm
    used = jnp.clip(group_len[tile_expert] - tile_row0, 0, tm)
    starts = seg_start[tile_expert]
    t_lo = jnp.sum(starts[:, 1:] <= tile_row0[:, None], axis=1)
    t_hi = jnp.sum(starts[:, :-1] < (tile_row0 + tm)[:, None], axis=1)
    i32 = lambda a: a.astype(jnp.int32)
    tables = (i32(tile_expert), i32(tile_row0), i32(work_end[-1:]), i32(seg_start.reshape(-1)),
              i32(src_row.reshape(-1)), i32(used), i32(t_lo), i32(t_hi))
    return tables, i32(jnp.sum(padded, axis=1))


def _layer(xa, mods, B, T, Tc, rt, tm, update_ctx, p, cos_t, sin_t, layer):
    rows = xa.shape[0]
    n_lat = B * T
    lat_rt, all_rt = n_lat // rt, rows // rt
    lat_tm, all_tm = n_lat // tm, rows // tm
    cond_rt = np.concatenate([np.arange(lat_rt) // (T // rt), np.full((all_rt - lat_rt,), B)])
    cond_tm = np.concatenate([np.arange(lat_tm) // (T // tm), np.full((all_tm - lat_tm,), B)])
    modt_rt = mods[cond_rt][:, None, :]
    modt_tm = mods[cond_tm][:, None, :]
    g_norm = p["g_norm"].reshape(4, 1, D_MODEL)

    z, zs = _inproj(xa, g_norm, modt_rt, p["w_in_main"], p["w_in_special"], layer, rt)
    q, k, v = _mla_project(z, zs, p["g_q_lora"], p["g_kv_lora"], p["w_q_up"], p["w_kv_up"],
                           cos_t, sin_t, rt, lat_rt, T // rt)
    a, w_up_c, w_down_c = _attention_latent(q, k, v, p["w_up"], p["w_down"], layer, p["depth"],
                                            B, T, Tc, min(ATTN_Q_TILE, T), min(ATTN_KV_CHUNK, T),
                                            rows if update_ctx else n_lat)
    w_up_c = w_up_c.reshape(N_EXPERTS, D_MODEL, 2 * D_FF)
    w_down_c = w_down_c.reshape(N_EXPERTS, D_FF, D_MODEL)
    if update_ctx:
        a = _attention_ctx(q, k, v, a, B, T, Tc)
    m = _mlstm(z, zs, p["b_if"], p["g_mlstm"], B, T, Tc)
    n_rt = all_rt if update_ctx else lat_rt
    n_tm = all_tm if update_ctx else lat_tm
    t = _merge(a, m, z, p["w_proj_a"], p["w_proj_b"], p["w_proj_c"], p["w_conv"], p["b_conv"],
               rt, n_rt, lat_rt, T // rt, T, Tc)
    xa, h2, ri, rw, cnt = _post(t, p["w_o"], xa, g_norm, modt_tm, p["w_router"], p["b_router"], tm, n_tm)
    tables, tile_used = _moe_tables(cnt, n_tm, tm)
    xs = _sort_rows(tile_used, h2, ri, tm, n_tm)
    ys = _experts(tables, xs, w_up_c, p["b_up"], w_down_c, p["b_down"], n_tm)
    return _combine(tile_used, ys, ri, rw, xa, g_norm, modt_tm, tm, n_tm)


def kernel(x, c, ctx, c_ctx, w_mod, b_mod, g_norm, w_in, g_q_lora, g_kv_lora, w_q_up, w_kv_up, b_if, g_mlstm,
           w_conv, b_conv, w_proj_a, w_proj_b, w_proj_c, w_o, w_router, b_router, w_up, b_up, w_down, b_down):
    B, T, _ = x.shape
    Tc = ctx.shape[1]
    depth = w_mod.shape[0]
    rt = min(1024, T)
    tm = min(512, T)
    assert T % rt == 0 and (B * Tc) % rt == 0 and (B * Tc) % tm == 0
    assert T & (T - 1) == 0 and Tc & (Tc - 1) == 0 and T % GRID_W == 0

    xa = jnp.concatenate([x.reshape(B * T, D_MODEL), ctx.reshape(B * Tc, D_MODEL)], axis=0)
    cond = jnp.zeros((8, D_MODEL), F32).at[:B].set(c).at[B].set(c_ctx)
    cos_t, sin_t = _rope_tables(T, rt)
    cast = lambda w: w.astype(MXU_DTYPE)
    in_cols = _in_proj_columns()
    n_main = N_MAIN_TILES * COL_TILE
    w_in_main, w_in_special = _permute_cols(w_in, (in_cols[:n_main], in_cols[n_main:]), MXU_DTYPE)
    w_q_p = _take_cols(w_q_up, _q_up_columns(), MXU_DTYPE)
    w_kv_p = _take_cols(w_kv_up, _kv_up_columns(), MXU_DTYPE)
    w_up_r = w_up.reshape(depth * N_EXPERTS * D_MODEL, 2 * D_FF)
    w_down_r = w_down.reshape(depth * N_EXPERTS * D_FF, D_MODEL)
    wa_c, wb_c, wc_c, wo_c = cast(w_proj_a), cast(w_proj_b), cast(w_proj_c), cast(w_o)
    w_router_p = jnp.pad(w_router, ((0, 0), (0, 0), (0, LANE - N_EXPERTS)))
    b_router_p = jnp.pad(b_router, ((0, 0), (0, LANE - N_EXPERTS)), constant_values=NEG_BIG)
    b_if_p = jnp.pad(b_if.reshape(depth, 1, -1), ((0, 0), (0, 0), (0, LANE - 4 * M_HEADS)))
    mods = _adaln(cond, w_mod, b_mod)
    for l in range(depth):
        p = dict(
            g_norm=g_norm[l], w_in_main=w_in_main, w_in_special=w_in_special,
            g_q_lora=g_q_lora[l], g_kv_lora=g_kv_lora[l],
            w_q_up=w_q_p[l], w_kv_up=w_kv_p[l], b_if=b_if_p[l],
            g_mlstm=g_mlstm[l], w_conv=w_conv[l], b_conv=b_conv[l],
            w_proj_a=wa_c[l], w_proj_b=wb_c[l], w_proj_c=wc_c[l], w_o=wo_c[l],
            w_router=w_router_p[l], b_router=b_router_p[l],
            depth=depth, w_up=w_up_r, b_up=b_up[l].reshape(N_EXPERTS, 1, -1),
            w_down=w_down_r, b_down=b_down[l].reshape(N_EXPERTS, 1, -1),
        )
        xa = _layer(xa, mods[l], B, T, Tc, rt, tm, l < depth - 1, p, cos_t, sin_t, l)
    return xa.reshape(B, T, D_MODEL)
```

```python
import functools

import numpy as np
import jax
import jax.numpy as jnp
from jax import lax
from jax.experimental import pallas as pl
from jax.experimental.pallas import tpu as pltpu

F32 = jnp.float32
MXU_DTYPE = jnp.bfloat16

D_MODEL = 2048
GRID_W = 64
MLA_HEADS = 8
QK_NOPE = 128
QK_ROPE = 64
V_HEAD = 128
Q_LORA = 512
KV_LORA = 512
ROPE_THETA = 10000.0
ROPE_FREQS = QK_ROPE // 4
ATTN_SCALE = (QK_NOPE + QK_ROPE) ** -0.5
M_HEADS = 4
M_QK = 64
M_V = 128
CONV_WIDTH = 512
N_EXPERTS = 32
TOP_K = 4
D_FF = 1024
SWIGLU_ALPHA = 1.702
SWIGLU_LIMIT = 7.0
EPS = 1e-6

LANE = 128
HEAD_W = 2 * LANE
MLSTM_CHUNK = 256
COL_TILE = 512
INPROJ_TILE = 1024
LOG2_E = 1.4426950408889634
MERGE_TILE = 1024
ATTN_Q_TILE = 1024
ATTN_KV_CHUNK = 256
MOE_TILE = 512
SEG_ALIGN = 16
NEG_BIG = -1e30
VMEM_LIMIT = 56 * 1024 * 1024

ZT_CQ, ZT_CKV, ZT_MQK, ZT_MV, ZT_MO, ZT_CU, ZT_CB, ZT_CC, ZT_CG = 0, 1, 2, 3, 4, 5, 6, 7, 8
N_MAIN_TILES = 8 + 3 * D_MODEL // COL_TILE
assert (N_MAIN_TILES * COL_TILE) % INPROJ_TILE == 0


def _cparams(n_axes, vmem=VMEM_LIMIT):
    return pltpu.CompilerParams(dimension_semantics=("arbitrary",) * n_axes, vmem_limit_bytes=vmem)


def _rms(x, g):
    return x * lax.rsqrt(jnp.mean(x * x, axis=-1, keepdims=True) + EPS) * g


def _sigmoid(x):
    return 0.5 * jnp.tanh(0.5 * x) + 0.5


def _lane_pick(x, idx):
    lane = lax.broadcasted_iota(jnp.int32, x.shape, 1)
    return jnp.sum(jnp.where(lane == idx, x, 0.0), axis=1, keepdims=True)


def _mod_kernel(c_ref, w_ref, b_ref, o_ref):
    c = c_ref[...]
    s = c * jax.nn.sigmoid(c)
    o_ref[0] = jnp.dot(s.astype(MXU_DTYPE), w_ref[0].astype(MXU_DTYPE),
                       preferred_element_type=F32) + b_ref[0]


def _adaln(cond, w_mod, b_mod):
    depth, _, n = w_mod.shape
    tn = 1024
    return pl.pallas_call(
        _mod_kernel,
        grid=(depth, n // tn),
        in_specs=[pl.BlockSpec(cond.shape, lambda l, j: (0, 0)),
                  pl.BlockSpec((1, D_MODEL, tn), lambda l, j: (l, 0, j)),
                  pl.BlockSpec((1, 1, tn), lambda l, j: (l, 0, j))],
        out_specs=pl.BlockSpec((1, cond.shape[0], tn), lambda l, j: (l, 0, j)),
        out_shape=jax.ShapeDtypeStruct((depth, cond.shape[0], n), F32),
        compiler_params=_cparams(2),
        name="adaln",
    )(cond, w_mod, b_mod.reshape(depth, 1, n))


def _row_sources(parts, tile):
    first, second = parts
    n_first = first.shape[0] // tile
    if second is None:
        second, second_map = first, (lambda i, *_: (0, 0))
    else:
        second_map = lambda i, *_: (jnp.maximum(i - n_first, 0), 0)
    width = first.shape[1]
    specs = [pl.BlockSpec((tile, width), lambda i, *_: (jnp.minimum(i, n_first - 1), 0)),
             pl.BlockSpec((tile, width), second_map, pipeline_mode=pl.Buffered(1))]
    return (first, second), specs, n_first


def _inproj_kernel(x_ref, g_ref, sh_ref, sc_ref, w_ref, ws_ref, z_ref, zs_ref, h_scr):
    @pl.when(pl.program_id(1) == 0)
    def _():
        h = _rms(x_ref[...], g_ref[0]) * (1.0 + sc_ref[0]) + sh_ref[0]
        h_scr[...] = h.astype(h_scr.dtype)
        zs_ref[...] = jnp.dot(h_scr[...], ws_ref[0], preferred_element_type=F32)

    z_ref[...] = jnp.dot(h_scr[...], w_ref[0], preferred_element_type=F32).astype(z_ref.dtype)


def _inproj(xa, g_norm, modt, w_main, w_special, layer, rt):
    rows = xa.shape[0]
    n_main = w_main.shape[2]
    tn = INPROJ_TILE
    return pl.pallas_call(
        _inproj_kernel,
        grid=(rows // rt, n_main // tn),
        in_specs=[pl.BlockSpec((rt, D_MODEL), lambda i, j: (i, 0)),
                  pl.BlockSpec((1, 1, D_MODEL), lambda i, j: (0, 0, 0)),
                  pl.BlockSpec((1, 1, D_MODEL), lambda i, j: (i, 0, 0)),
                  pl.BlockSpec((1, 1, D_MODEL), lambda i, j: (i, 0, 1)),
                  pl.BlockSpec((1, D_MODEL, tn), lambda i, j: (layer, 0, j)),
                  pl.BlockSpec((1, D_MODEL, COL_TILE), lambda i, j: (layer, 0, 0))],
        out_specs=[pl.BlockSpec((rt, tn), lambda i, j: (i, j)),
                   pl.BlockSpec((rt, COL_TILE), lambda i, j: (i, 0))],
        out_shape=[jax.ShapeDtypeStruct((rows, n_main), MXU_DTYPE),
                   jax.ShapeDtypeStruct((rows, COL_TILE), F32)],
        scratch_shapes=[pltpu.VMEM((rt, D_MODEL), MXU_DTYPE)],
        compiler_params=_cparams(2),
        name="inproj",
    )(xa, g_norm, modt, modt, w_main, w_special)


def _mla_kernel(cq_ref, ckv_ref, zs_ref, gq_ref, gkv_ref, wq_ref, wkv_ref, cos_ref, sin_ref,
                q_ref, k_ref, v_ref):
    def norm(z_ref, g_ref):
        return _rms(z_ref[...].astype(F32), g_ref[...]).astype(MXU_DTYPE)

    qf = jnp.dot(norm(cq_ref, gq_ref), wq_ref[...], preferred_element_type=F32) * (ATTN_SCALE * LOG2_E)
    kvf = jnp.dot(norm(ckv_ref, gkv_ref), wkv_ref[...], preferred_element_type=F32)
    cos = cos_ref[...]
    sin = sin_ref[...]
    nope_w = MLA_HEADS * QK_NOPE
    rope_w = MLA_HEADS * QK_ROPE
    for p in range(MLA_HEADS // 2):
        a = nope_w + LANE * p
        b = nope_w + rope_w + LANE * p
        rp = (qf[:, a:a + LANE] * cos + qf[:, b:b + LANE] * sin).astype(MXU_DTYPE)
        for h in (2 * p, 2 * p + 1):
            q_ref[:, HEAD_W *h:HEAD_W *h + LANE] = qf[:, LANE * h:LANE * (h + 1)].astype(MXU_DTYPE)
            q_ref[:, HEAD_W *h + LANE:HEAD_W *(h + 1)] = rp
    kr2 = zs_ref[:, 0:LANE] * cos + zs_ref[:, LANE:2 * LANE] * sin
    lane = lax.broadcasted_iota(jnp.int32, kr2.shape, 1)
    k_lo = jnp.where(lane < QK_ROPE, kr2, 0.0).astype(MXU_DTYPE)
    k_hi = jnp.where(lane >= QK_ROPE, kr2, 0.0).astype(MXU_DTYPE)
    for h in range(MLA_HEADS):
        k_ref[:, HEAD_W *h:HEAD_W *h + LANE] = kvf[:, LANE * h:LANE * (h + 1)].astype(MXU_DTYPE)
        k_ref[:, HEAD_W *h + LANE:HEAD_W *(h + 1)] = k_lo if h % 2 == 0 else k_hi
    ones = jnp.ones((kvf.shape[0], LANE), MXU_DTYPE)
    v0 = MLA_HEADS * QK_NOPE
    for h in range(MLA_HEADS):
        v_ref[:, HEAD_W *h:HEAD_W *h + LANE] = kvf[:, v0 + LANE * h:v0 + LANE * (h + 1)].astype(MXU_DTYPE)
        v_ref[:, HEAD_W *h + LANE:HEAD_W *(h + 1)] = ones


def _mla_project(z, zs, g_q, g_kv, wq_p, wkv_p, cos_t, sin_t, rt, n_lat_tiles, tiles_per_seq):
    rows = z.shape[0]
    nt = rows // rt

    def tab(i):
        return (jnp.where(i < n_lat_tiles, i % tiles_per_seq, tiles_per_seq), 0)

    return pl.pallas_call(
        _mla_kernel,
        grid=(nt,),
        in_specs=[pl.BlockSpec((rt, COL_TILE), lambda i: (i, ZT_CQ)),
                  pl.BlockSpec((rt, COL_TILE), lambda i: (i, ZT_CKV)),
                  pl.BlockSpec((rt, COL_TILE), lambda i: (i, 0)),
                  pl.BlockSpec((1, Q_LORA), lambda i: (0, 0)),
                  pl.BlockSpec((1, KV_LORA), lambda i: (0, 0)),
                  pl.BlockSpec(wq_p.shape, lambda i: (0, 0)),
                  pl.BlockSpec(wkv_p.shape, lambda i: (0, 0)),
                  pl.BlockSpec((rt, LANE), tab),
                  pl.BlockSpec((rt, LANE), tab)],
        out_specs=[pl.BlockSpec((rt, HEAD_W *MLA_HEADS), lambda i: (i, 0)),
                   pl.BlockSpec((rt, HEAD_W *MLA_HEADS), lambda i: (i, 0)),
                   pl.BlockSpec((rt, HEAD_W *MLA_HEADS), lambda i: (i, 0))],
        out_shape=[jax.ShapeDtypeStruct((rows, HEAD_W *MLA_HEADS), MXU_DTYPE)] * 3,
        compiler_params=_cparams(1),
        name="mla_project",
    )(z, z, zs, g_q.reshape(1, -1), g_kv.reshape(1, -1), wq_p, wkv_p, cos_t, sin_t)


def _attn_kernel(*refs, kv_chunks, n_cast=0):
    q_ref = refs[0]
    o_ref = refs[len(refs) - 1 - n_cast]
    for c in range(n_cast):
        src = refs[1 + 2 * len(kv_chunks) + c]
        dst = refs[len(refs) - n_cast + c]
        dst[...] = src[...].astype(dst.dtype)
    q = q_ref[...]
    tq = q.shape[0]
    m = jnp.full((tq, 1), -jnp.inf, F32)
    acc = jnp.zeros((tq, 2 * V_HEAD), F32)
    for s, n_chunks in enumerate(kv_chunks):
        k_ref, v_ref = refs[1 + 2 * s], refs[2 + 2 * s]
        ck = k_ref.shape[0] // n_chunks
        for c in range(n_chunks):
            k = k_ref[c * ck:(c + 1) * ck, :]
            v = v_ref[c * ck:(c + 1) * ck, :]
            sc = lax.dot_general(q, k, (((1,), (1,)), ((), ())), preferred_element_type=F32)
            m_new = jnp.maximum(m, jnp.max(sc, axis=-1, keepdims=True))
            p = jnp.exp2(sc - m_new).astype(MXU_DTYPE)
            acc = jnp.exp2(m - m_new) * acc + jnp.dot(p, v, preferred_element_type=F32)
            m = m_new
    o_ref[...] = (acc[:, :V_HEAD] / acc[:, V_HEAD:]).astype(o_ref.dtype)


def _attention_latent(q, k, v, w_up, w_down, layer, depth, B, T, Tc, tq, ck, rows):
    qt = T // tq
    ctx_blk0 = B * T // Tc
    steps = B * MLA_HEADS * qt
    step = lambda b, h, i: (b * MLA_HEADS + h) * qt + i
    cast_specs_in, cast_specs_out, cast_shapes = [], [], []
    for w in (w_up, w_down):
        n = w.shape[1]
        layer_rows = w.shape[0] // depth
        slab = layer_rows // steps
        assert slab * steps == layer_rows and slab % 16 == 0
        cast_specs_in.append(pl.BlockSpec((slab, n), lambda b, h, i: (layer * steps + step(b, h, i), 0)))
        cast_specs_out.append(pl.BlockSpec((slab, n), lambda b, h, i: (step(b, h, i), 0)))
        cast_shapes.append(jax.ShapeDtypeStruct((layer_rows, n), MXU_DTYPE))
    return pl.pallas_call(
        functools.partial(_attn_kernel, kv_chunks=(T // ck, 1), n_cast=2),
        grid=(B, MLA_HEADS, qt),
        in_specs=[pl.BlockSpec((tq, HEAD_W), lambda b, h, i: (b * qt + i, h)),
                  pl.BlockSpec((T, HEAD_W), lambda b, h, i: (b, h)),
                  pl.BlockSpec((T, HEAD_W), lambda b, h, i: (b, h)),
                  pl.BlockSpec((Tc, HEAD_W), lambda b, h, i: (ctx_blk0 + b, h)),
                  pl.BlockSpec((Tc, HEAD_W), lambda b, h, i: (ctx_blk0 + b, h))] + cast_specs_in,
        out_specs=[pl.BlockSpec((tq, V_HEAD), lambda b, h, i: (b * qt + i, h))] + cast_specs_out,
        out_shape=[jax.ShapeDtypeStruct((rows, V_HEAD * MLA_HEADS), MXU_DTYPE)] + cast_shapes,
        compiler_params=_cparams(3),
        name="attn_latent",
    )(q, k, v, k, v, w_up, w_down)


def _attention_ctx(q, k, v, a_buf, B, T, Tc):
    ctx_blk0 = B * T // Tc
    return pl.pallas_call(
        functools.partial(_attn_kernel, kv_chunks=(1,)),
        grid=(B, MLA_HEADS),
        in_specs=[pl.BlockSpec((Tc, HEAD_W), lambda b, h: (ctx_blk0 + b, h)),
                  pl.BlockSpec((Tc, HEAD_W), lambda b, h: (ctx_blk0 + b, h)),
                  pl.BlockSpec((Tc, HEAD_W), lambda b, h: (ctx_blk0 + b, h)),
                  pl.BlockSpec(memory_space=pl.ANY)],
        out_specs=pl.BlockSpec((Tc, V_HEAD), lambda b, h: (ctx_blk0 + b, h)),
        out_shape=jax.ShapeDtypeStruct(a_buf.shape, a_buf.dtype),
        input_output_aliases={3: 0},
        compiler_params=_cparams(2),
        name="attn_ctx",
    )(q, k, v, a_buf)


def _mlstm_chunk(qk, v, g, lf_cum, gt_ref, head, reverse, ct_ref, m_prev, consts):
    tri, row_iota, mk, ones_tile, lane = consts
    L = qk.shape[0]
    base = 8 if reverse else 0
    col_i, col_f = base + head, base + 4 + head
    ql = jnp.where(lane < M_QK, qk * (M_QK ** -0.5), jnp.zeros_like(qk))
    kl = jnp.dot(qk, mk, preferred_element_type=F32).astype(MXU_DTYPE)
    v_aug = jnp.concatenate([v, ones_tile], axis=1)
    bc = _lane_pick(lf_cum, col_f)
    li = _lane_pick(g, col_i)
    r_row = gt_ref[0, col_i:col_i + 1, :] - gt_ref[1, col_f:col_f + 1, :]
    dlog = jnp.where(tri, bc + r_row, -jnp.inf)
    inter = bc + m_prev
    m_t = jnp.maximum(inter, jnp.max(dlog, axis=1, keepdims=True))
    dexp = jnp.exp(dlog - m_t)
    s_raw = lax.dot_general(ql, kl, (((1,), (1,)), ((), ())), preferred_element_type=F32)
    s_mat = (s_raw * dexp).astype(MXU_DTYPE)
    a = jnp.exp(inter - m_t)
    ct = ct_ref[...]
    nd = (jnp.dot(s_mat, v_aug, preferred_element_type=F32)
          + a * jnp.dot(ql, ct.astype(MXU_DTYPE), preferred_element_type=F32))
    num, den = nd[:, :M_V], nd[:, M_V:]
    h_out = num / jnp.maximum(jnp.abs(den), jnp.exp(-m_t))
    last = 0 if reverse else L - 1
    total = jnp.sum(jnp.where(row_iota == last, bc, 0.0), axis=0, keepdims=True)
    w_col = total - bc + li
    m_new = jnp.maximum(total + m_prev, jnp.max(w_col, axis=0, keepdims=True))
    decay = jnp.exp(total + m_prev - m_new)
    e = jnp.exp(w_col - m_new)
    ek = (kl.astype(F32) * e).astype(MXU_DTYPE)
    upd = lax.dot_general(ek, v_aug, (((0,), (0,)), ((), ())), preferred_element_type=F32)
    ct_ref[...] = decay * ct + upd
    return h_out, m_new


def _mlstm_kernel(zl_qk, zl_v, zl_mo, gl_ref, zc_qk, zc_v, zc_mo, gc_ref, bif_ref, gm_ref,
                  ol_ref, oc_ref, acc_l, acc_c, ct_scr, gt_scr):
    L = MLSTM_CHUNK
    T = zl_qk.shape[0]
    nc = T // L
    r = lax.broadcasted_iota(jnp.int32, (L, L), 0)
    c = lax.broadcasted_iota(jnp.int32, (L, L), 1)
    tri_f = c <= r
    tri_b = c >= r
    tmat_f = tri_f.astype(jnp.bfloat16)
    tmat_b = tri_b.astype(jnp.bfloat16)
    row_iota = lax.broadcasted_iota(jnp.int32, (L, 1), 0)
    lane = lax.broadcasted_iota(jnp.int32, (L, LANE), 1)
    rr = lax.broadcasted_iota(jnp.int32, (LANE, LANE), 0)
    cc = lax.broadcasted_iota(jnp.int32, (LANE, LANE), 1)
    mk = (rr == cc + M_QK).astype(MXU_DTYPE)
    ones_tile = jnp.ones((L, LANE), MXU_DTYPE)
    consts_f = (tri_f, row_iota, mk, ones_tile, lane)
    consts_b = (tri_b, row_iota, mk, ones_tile, lane)
    bias = bif_ref[...]

    def gate_prep(g_raw, tmat, slot):
        g = g_raw + bias
        log_sig = jnp.minimum(g, 0.0) - jnp.log1p(jnp.exp(-jnp.abs(g)))
        hi = log_sig.astype(jnp.bfloat16)
        rest = log_sig - hi.astype(F32)
        mid = rest.astype(jnp.bfloat16)
        low = (rest - mid.astype(F32)).astype(jnp.bfloat16)
        sums = jnp.dot(tmat, jnp.concatenate([hi, mid, low], axis=1), preferred_element_type=F32)
        cum = sums[:, :LANE] + sums[:, LANE:2 * LANE] + sums[:, 2 * LANE:]
        gt_scr[slot, 0] = g.T
        gt_scr[slot, 1] = cum.T
        return g, cum

    def step(qk_ref, v_ref, g_ref, acc_ref, row_f, row_b, ms):
        g_f, cum_f = gate_prep(g_ref[pl.ds(row_f, L), :], tmat_f, 0)
        g_b, cum_b = gate_prep(g_ref[pl.ds(row_b, L), :], tmat_b, 1)
        new_ms = []
        for h in range(M_HEADS):
            cols = slice(h * LANE, (h + 1) * LANE)
            hf, mf = _mlstm_chunk(qk_ref[pl.ds(row_f, L), cols], v_ref[pl.ds(row_f, L), cols],
                                  g_f, cum_f, gt_scr.at[0], h, False,
                                  ct_scr.at[h], ms[h], consts_f)
            acc_ref[pl.ds(row_f, L), cols] += hf
            hb, mb = _mlstm_chunk(qk_ref[pl.ds(row_b, L), cols], v_ref[pl.ds(row_b, L), cols],
                                  g_b, cum_b, gt_scr.at[1], h, True,
                                  ct_scr.at[M_HEADS + h], ms[M_HEADS + h], consts_b)
            acc_ref[pl.ds(row_b, L), cols] += hb
            new_ms.append((mf, mb))
        return tuple(x[0] for x in new_ms) + tuple(x[1] for x in new_ms)

    ct_scr[...] = jnp.zeros_like(ct_scr)
    acc_l[...] = jnp.zeros_like(acc_l)
    acc_c[...] = jnp.zeros_like(acc_c)
    zero = jnp.zeros((1, 1), F32)
    ms = step(zc_qk, zc_v, gc_ref, acc_c, 0, 0, (zero,) * (2 * M_HEADS))

    def body(i, ms):
        row_f = pl.multiple_of(i * L, L)
        row_b = pl.multiple_of((nc - 1 - i) * L, L)
        return step(zl_qk, zl_v, gl_ref, acc_l, row_f, row_b, ms)

    lax.fori_loop(0, nc, body, ms)

    gm = gm_ref[...]

    def finish(acc_ref, mo_ref, o_ref, row):
        for h in range(M_HEADS):
            cols = slice(h * LANE, (h + 1) * LANE)
            hn = _rms(acc_ref[pl.ds(row, L), cols], gm[:, cols])
            gate = jax.nn.sigmoid(mo_ref[pl.ds(row, L), cols].astype(F32))
            o_ref[pl.ds(row, L), cols] = (hn * gate).astype(o_ref.dtype)

    finish(acc_c, zc_mo, oc_ref, 0)

    def fin_body(i, carry):
        finish(acc_l, zl_mo, ol_ref, pl.multiple_of(i * L, L))
        return carry

    lax.fori_loop(0, nc, fin_body, 0)


def _mlstm(z, zs, bif_row, g_mlstm, B, T, Tc):
    assert Tc == MLSTM_CHUNK and T % MLSTM_CHUNK == 0
    rows = z.shape[0]
    ctx_blk0 = B * T // Tc
    w = M_HEADS * LANE
    out_l, out_c = pl.pallas_call(
        _mlstm_kernel,
        grid=(B,),
        in_specs=[pl.BlockSpec((T, w), lambda b: (b, ZT_MQK)),
                  pl.BlockSpec((T, w), lambda b: (b, ZT_MV)),
                  pl.BlockSpec((T, w), lambda b: (b, ZT_MO)),
                  pl.BlockSpec((T, LANE), lambda b: (b, 2)),
                  pl.BlockSpec((Tc, w), lambda b: (ctx_blk0 + b, ZT_MQK)),
                  pl.BlockSpec((Tc, w), lambda b: (ctx_blk0 + b, ZT_MV)),
                  pl.BlockSpec((Tc, w), lambda b: (ctx_blk0 + b, ZT_MO)),
                  pl.BlockSpec((Tc, LANE), lambda b: (ctx_blk0 + b, 2)),
                  pl.BlockSpec((1, LANE), lambda b: (0, 0)),
                  pl.BlockSpec((1, w), lambda b: (0, 0))],
        out_specs=[pl.BlockSpec((T, w), lambda b: (b, 0)),
                   pl.BlockSpec((Tc, w), lambda b: (b, 0))],
        out_shape=[jax.ShapeDtypeStruct((B * T, w), MXU_DTYPE),
                   jax.ShapeDtypeStruct((B * Tc, w), MXU_DTYPE)],
        scratch_shapes=[pltpu.VMEM((T, w), F32), pltpu.VMEM((Tc, w), F32),
                        pltpu.VMEM((2 * M_HEADS, LANE, 2 * LANE), F32),
                        pltpu.VMEM((2, 2, LANE, MLSTM_CHUNK), F32)],
        compiler_params=_cparams(1),
        name="mlstm",
    )(z, z, z, zs, z, z, z, zs, bif_row, g_mlstm.reshape(1, -1))
    return out_l, out_c


def _merge_kernel(ml_ref, mc_ref, a_ref, cu_ref, cb_ref, cc_ref, cu_p, cc_p, cu_n, cc_n, ga_ref, gm_ref, gs_ref,
                  wa_ref, wb_ref, wc_ref, wconv_ref, bconv_ref, t_ref, s_scr,
                  *, n_lat_tiles, tiles_per_seq, seq_len, ctx_len):
    i = pl.program_id(0)
    j = pl.program_id(1)
    rt = a_ref.shape[0]
    m = jnp.where(i < n_lat_tiles, ml_ref[...], mc_ref[...])

    @pl.when(j == 0)
    def _():
        u = cc_ref[...].astype(F32) * cu_ref[...].astype(F32)
        hp = cc_p[...].astype(F32) * cu_p[...].astype(F32)
        hn = cc_n[...].astype(F32) * cu_n[...].astype(F32)
        row = lax.broadcasted_iota(jnp.int32, (rt, 1), 0)
        is_ctx = i >= n_lat_tiles
        seg = jnp.where(is_ctx, ctx_len, seq_len)
        off = jnp.where(is_ctx, (i - n_lat_tiles) * rt, (i % tiles_per_seq) * rt)
        pos = (off + row) & (seg - 1)
        up = pltpu.roll(u, 1, 0)
        up = jnp.where(row == 0, hp[hp.shape[0] - 1:, :], up)
        up = jnp.where(pos == 0, 0.0, up)
        un = pltpu.roll(u, rt - 1, 0)
        un = jnp.where(row == rt - 1, hn[0:1, :], un)
        un = jnp.where(pos == seg - 1, 0.0, un)
        wconv = wconv_ref[...]
        y = up * wconv[0:1, :] + u * wconv[1:2, :] + un * wconv[2:3, :] + bconv_ref[...]
        s_scr[...] = (cb_ref[...].astype(F32) * y).astype(s_scr.dtype)

    def sig(ref):
        return _sigmoid(ref[...].astype(F32))

    t = (sig(ga_ref) * jnp.dot(a_ref[...], wa_ref[...], preferred_element_type=F32)
         + sig(gm_ref) * jnp.dot(m, wb_ref[...], preferred_element_type=F32)
         + sig(gs_ref) * jnp.dot(s_scr[...], wc_ref[...], preferred_element_type=F32))
    t_ref[...] = t.astype(t_ref.dtype)


def _merge(a, m, z, wa, wb, wc, w_conv, b_conv, rt, n_tiles, n_lat_tiles, tiles_per_seq, T, Tc):
    rows = n_tiles * rt
    hb = 16
    rb = rt // hb
    last_hb = z.shape[0] // hb - 1
    tn = MERGE_TILE
    nj = D_MODEL // tn
    gt = D_MODEL // tn
    g0 = ZT_CG * COL_TILE // tn
    assert (ZT_CG * COL_TILE) % tn == 0

    def prev_blk(col):
        return lambda i, j: (jnp.maximum(i * rb - 1, 0), col)

    def next_blk(col):
        return lambda i, j: (jnp.minimum((i + 1) * rb, last_hb), col)

    kern = functools.partial(_merge_kernel, n_lat_tiles=n_lat_tiles, tiles_per_seq=tiles_per_seq,
                             seq_len=T, ctx_len=Tc)
    m_arrays, m_specs, m_first = _row_sources(m, rt)
    assert m_first == n_lat_tiles
    return pl.pallas_call(
        kern,
        grid=(n_tiles, nj),
        in_specs=m_specs + [pl.BlockSpec((rt, a.shape[1]), lambda i, j: (i, 0)),
                  pl.BlockSpec((rt, COL_TILE), lambda i, j: (i, ZT_CU)),
                  pl.BlockSpec((rt, COL_TILE), lambda i, j: (i, ZT_CB)),
                  pl.BlockSpec((rt, COL_TILE), lambda i, j: (i, ZT_CC)),
                  pl.BlockSpec((hb, COL_TILE), prev_blk(ZT_CU)),
                  pl.BlockSpec((hb, COL_TILE), prev_blk(ZT_CC)),
                  pl.BlockSpec((hb, COL_TILE), next_blk(ZT_CU)),
                  pl.BlockSpec((hb, COL_TILE), next_blk(ZT_CC)),
                  pl.BlockSpec((rt, tn), lambda i, j: (i, g0 + j)),
                  pl.BlockSpec((rt, tn), lambda i, j: (i, g0 + gt + j)),
                  pl.BlockSpec((rt, tn), lambda i, j: (i, g0 + 2 * gt + j)),
                  pl.BlockSpec((wa.shape[0], tn), lambda i, j: (0, j)),
                  pl.BlockSpec((wb.shape[0], tn), lambda i, j: (0, j)),
                  pl.BlockSpec((wc.shape[0], tn), lambda i, j: (0, j)),
                  pl.BlockSpec(w_conv.shape, lambda i, j: (0, 0)),
                  pl.BlockSpec((1, CONV_WIDTH), lambda i, j: (0, 0))],
        out_specs=pl.BlockSpec((rt, tn), lambda i, j: (i, j)),
        out_shape=jax.ShapeDtypeStruct((rows, D_MODEL), MXU_DTYPE),
        scratch_shapes=[pltpu.VMEM((rt, CONV_WIDTH), MXU_DTYPE)],
        compiler_params=_cparams(2),
        name="merge",
    )(*m_arrays, a, z, z, z, z, z, z, z, z, z, z, wa, wb, wc, w_conv, b_conv.reshape(1, -1))


def _dot3(x, w):
    xh = x.astype(jnp.bfloat16)
    xl = (x - xh.astype(F32)).astype(jnp.bfloat16)
    wh = w.astype(jnp.bfloat16)
    wl = (w - wh.astype(F32)).astype(jnp.bfloat16)
    n = w.shape[1]
    hi = jnp.dot(xh, jnp.concatenate([wh, wl], axis=1), preferred_element_type=F32)
    return hi[:, :n] + hi[:, n:] + jnp.dot(xl, wh, preferred_element_type=F32)


def _seg_pad(counts):
    return jnp.ceil(counts * (1.0 / SEG_ALIGN)) * SEG_ALIGN


def _exclusive_lane_cumsum(row):
    upper = (lax.broadcasted_iota(jnp.int32, (LANE, LANE), 0)
             < lax.broadcasted_iota(jnp.int32, (LANE, LANE), 1)).astype(jnp.bfloat16)
    out = jnp.dot(jnp.broadcast_to(row, (8, LANE)).astype(jnp.bfloat16), upper, preferred_element_type=F32)
    return out[0:1, :]


def _post_kernel(x_ref, t_ref, wo_ref, g1_ref, g2_ref, gt_ref, sh_ref, sc_ref, wr_ref, br_ref,
                 xo_ref, h2_ref, ri_ref, rw_ref, cnt_ref):
    half = t_ref.shape[0] // 2
    parts = []
    for r in (slice(0, half), slice(half, 2 * half)):
        y = jnp.dot(t_ref[r, :], wo_ref[...], preferred_element_type=F32)
        xn = x_ref[r, :] + gt_ref[0] * _rms(y, g1_ref[0])
        xo_ref[r, :] = xn
        h2 = _rms(xn, g2_ref[0]) * (1.0 + sc_ref[0]) + sh_ref[0]
        h2_ref[r, :] = h2.astype(h2_ref.dtype)
        parts.append(_dot3(h2, wr_ref[...]) + br_ref[...])
    logits = jnp.concatenate(parts, axis=0)
    lane = lax.broadcasted_iota(jnp.int32, logits.shape, 1)
    lane_f = lane.astype(F32)
    vals, ids = [], []
    for _ in range(TOP_K):
        mx = jnp.max(logits, axis=1, keepdims=True)
        idx = jnp.min(jnp.where(logits == mx, lane_f, float(LANE)), axis=1, keepdims=True)
        logits = jnp.where(lane_f == idx, -jnp.inf, logits)
        vals.append(mx)
        ids.append(idx)
    es = [jnp.exp(v - vals[0]) for v in vals]
    tot = es[0] + es[1] + es[2] + es[3]
    tm = logits.shape[0]
    onehots = [lane_f == ids[k] for k in range(TOP_K)]
    asg = jnp.zeros(logits.shape, F32)
    for k in range(TOP_K):
        asg = jnp.where(onehots[k], 1.0, asg)
    earlier = (lax.broadcasted_iota(jnp.int32, (tm, tm), 1)
               < lax.broadcasted_iota(jnp.int32, (tm, tm), 0)).astype(jnp.bfloat16)
    before = jnp.dot(earlier, asg.astype(jnp.bfloat16), preferred_element_type=F32)
    counts = jnp.sum(asg, axis=0, keepdims=True)
    seg_off = _exclusive_lane_cumsum(_seg_pad(counts))
    rw = jnp.zeros(logits.shape, F32)
    ri = jnp.zeros(logits.shape, F32)
    for k in range(TOP_K):
        pos = jnp.sum(jnp.where(onehots[k], before + seg_off, 0.0), axis=1, keepdims=True)
        rw = jnp.where(lane == k, es[k] / tot, rw)
        ri = jnp.where(lane == k, ids[k], ri)
        ri = jnp.where(lane == TOP_K + k, pos, ri)
    rw_ref[...] = rw
    ri_ref[...] = ri.astype(jnp.int32)
    cnt_ref[0] = jnp.broadcast_to(counts, cnt_ref.shape[1:])


def _post(t, wo, xa, g_norm, modt, wr_p, br_p, tm, n_tiles):
    rows = n_tiles * tm
    row_spec = lambda w: pl.BlockSpec((tm, w), lambda i: (i, 0))
    mod = lambda k: pl.BlockSpec((1, 1, D_MODEL), lambda i: (i, 0, k))
    gn = lambda k: pl.BlockSpec((1, 1, D_MODEL), lambda i: (k, 0, 0))
    return pl.pallas_call(
        _post_kernel,
        grid=(n_tiles,),
        in_specs=[row_spec(D_MODEL), row_spec(D_MODEL), pl.BlockSpec(wo.shape, lambda i: (0, 0)),
                  gn(1), gn(2), mod(2), mod(3), mod(4),
                  pl.BlockSpec(wr_p.shape, lambda i: (0, 0)), pl.BlockSpec((1, LANE), lambda i: (0, 0))],
        out_specs=[row_spec(D_MODEL), row_spec(D_MODEL), row_spec(LANE), row_spec(LANE),
                   pl.BlockSpec((1, 8, LANE), lambda i: (i, 0, 0))],
        out_shape=[jax.ShapeDtypeStruct((rows, D_MODEL), F32),
                   jax.ShapeDtypeStruct((rows, D_MODEL), MXU_DTYPE),
                   jax.ShapeDtypeStruct((rows, LANE), jnp.int32),
                   jax.ShapeDtypeStruct((rows, LANE), F32),
                   jax.ShapeDtypeStruct((rows // tm, 8, LANE), F32)],
        compiler_params=_cparams(1),
        name="post_mix",
    )(xa, t, wo, g_norm, g_norm, modt, modt, modt, wr_p, br_p.reshape(1, LANE))


def _segment_chunks(seg_rows, tm):
    slack = seg_rows - TOP_K * tm
    chunks = [(c * tm, tm) for c in range(TOP_K)]
    return chunks + [(TOP_K * tm, slack // 2), (TOP_K * tm + slack // 2, slack - slack // 2)]


def _position_onehot(pos, values, first_row, rows, tm):
    col = (lax.broadcasted_iota(jnp.int32, (tm, rows), 1) + first_row).astype(F32)
    out = jnp.zeros((tm, rows), F32)
    for p, v in zip(pos, values):
        out = jnp.where(p == col, v, out)
    return out


def _sort_kernel(used_ref, h2_ref, ri_ref, xs_ref):
    tm = h2_ref.shape[0]
    used = used_ref[pl.program_id(0)]
    rif = ri_ref[...].astype(F32)
    pos = [_lane_pick(rif, TOP_K + k) for k in range(TOP_K)]

    def emit(first_row, rows):
        onehot = _position_onehot(pos, [1.0] * TOP_K, first_row, rows, tm)
        out = lax.dot_general(onehot.astype(MXU_DTYPE), h2_ref[...], (((0,), (0,)), ((), ())),
                              preferred_element_type=F32)
        xs_ref[first_row:first_row + rows, :] = out.astype(xs_ref.dtype)

    for first_row, rows in _segment_chunks(xs_ref.shape[0], tm):
        if first_row < TOP_K * tm:
            emit(first_row, rows)
        else:
            pl.when(used > first_row)(functools.partial(emit, first_row, rows))


def _sort_rows(tile_used, h2, ri, tm, n_tiles):
    seg_rows = TOP_K * tm + N_EXPERTS * SEG_ALIGN
    grid_spec = pltpu.PrefetchScalarGridSpec(
        num_scalar_prefetch=1,
        grid=(n_tiles,),
        in_specs=[pl.BlockSpec((tm, D_MODEL), lambda i, used: (i, 0)),
                  pl.BlockSpec((tm, LANE), lambda i, used: (i, 0))],
        out_specs=pl.BlockSpec((seg_rows, D_MODEL), lambda i, used: (i, 0)),
    )
    return pl.pallas_call(
        _sort_kernel,
        grid_spec=grid_spec,
        out_shape=jax.ShapeDtypeStruct((n_tiles * seg_rows, D_MODEL), MXU_DTYPE),
        compiler_params=_cparams(1),
        name="sort_rows",
    )(tile_used, h2, ri)


def _expert_kernel(te_ref, r0_ref, na_ref, seg_ref, src_ref, used_ref, tlo_ref, thi_ref,
                   xs_hbm, wu_ref, bu_ref, wd_ref, bd_ref, ys_hbm,
                   xbuf, ybuf, sem_in, sem_out, *, n_tok_tiles):
    w = pl.program_id(0)
    n_active = na_ref[0]
    tm = ybuf.shape[0]
    nt = n_tok_tiles
    sizes = [s for s in (512, 256, 128, 64, 32, 16) if s <= tm]

    def for_pieces(length, fn):
        off = jnp.int32(0)
        for size in sizes:
            hit = (length & size) != 0

            @pl.when(hit)
            def _(off=off, size=size):
                fn(off, size)

            off = off + jnp.where(hit, size, 0)

    def for_segments(wt, fn):
        e = te_ref[wt]
        r0 = r0_ref[wt]

        def body(t, carry):
            s0 = seg_ref[e * (nt + 1) + t]
            s1 = seg_ref[e * (nt + 1) + t + 1]
            lo = jnp.maximum(s0, r0)
            length = jnp.maximum(jnp.minimum(s1, r0 + tm) - lo, 0)
            src = src_ref[e * nt + t] + (lo - s0)
            dst = lo - r0
            for_pieces(length, lambda off, size: fn(src + off, dst + off, size))
            return carry

        lax.fori_loop(tlo_ref[wt], thi_ref[wt], body, 0)

    def rows_in_use(wt):
        return used_ref[wt]

    def aligned(row):
        return row if isinstance(row, int) else pl.multiple_of(row, SEG_ALIGN)

    def in_copy(hbm_row, buf_row, size, slot):
        return pltpu.make_async_copy(xs_hbm.at[pl.ds(aligned(hbm_row), size)],
                                     xbuf.at[slot, pl.ds(aligned(buf_row), size)], sem_in.at[slot])

    def out_copy(hbm_row, buf_row, size):
        return pltpu.make_async_copy(ybuf.at[pl.ds(aligned(buf_row), size)],
                                     ys_hbm.at[pl.ds(aligned(hbm_row), size)], sem_out.at[0])

    def start_in(wt, slot):
        for_segments(wt, lambda h, b, size: in_copy(h, b, size, slot).start())

    def wait_in(wt, slot):
        for_pieces(rows_in_use(wt), lambda off, size: in_copy(0, 0, size, slot).wait())

    def wait_out(wt):
        for_pieces(rows_in_use(wt), lambda off, size: out_copy(0, 0, size).wait())

    @pl.when(w < n_active)
    def _():
        slot = w % 2

        @pl.when(w == 0)
        def _():
            xbuf[...] = jnp.zeros_like(xbuf)
            start_in(0, 0)

        wait_in(w, slot)

        @pl.when(w + 1 < n_active)
        def _():
            start_in(w + 1, 1 - slot)

        gu = jnp.dot(xbuf[slot], wu_ref[0], preferred_element_type=F32) + bu_ref[0]
        g = jnp.minimum(gu[:, :D_FF], SWIGLU_LIMIT)
        u = jnp.clip(gu[:, D_FF:], -SWIGLU_LIMIT, SWIGLU_LIMIT)
        hid = g * _sigmoid(SWIGLU_ALPHA * g) * (u + 1.0)
        y = jnp.dot(hid.astype(MXU_DTYPE), wd_ref[0], preferred_element_type=F32) + bd_ref[0]

        @pl.when(w > 0)
        def _():
            wait_out(w - 1)

        ybuf[...] = y.astype(ybuf.dtype)
        for_segments(w, lambda h, b, size: out_copy(h, b, size).start())

        @pl.when(w == n_active - 1)
        def _():
            wait_out(w)


def _experts(tables, xs, w_up, b_up, w_down, b_down, n_tok_tiles):
    tile_expert = tables[0]
    tm = MOE_TILE
    grid_spec = pltpu.PrefetchScalarGridSpec(
        num_scalar_prefetch=len(tables),
        grid=(tile_expert.shape[0],),
        in_specs=[pl.BlockSpec(memory_space=pl.ANY),
                  pl.BlockSpec((1, D_MODEL, 2 * D_FF), lambda t, te, *_: (te[t], 0, 0)),
                  pl.BlockSpec((1, 1, 2 * D_FF), lambda t, te, *_: (te[t], 0, 0)),
                  pl.BlockSpec((1, D_FF, D_MODEL), lambda t, te, *_: (te[t], 0, 0)),
                  pl.BlockSpec((1, 1, D_MODEL), lambda t, te, *_: (te[t], 0, 0))],
        out_specs=pl.BlockSpec(memory_space=pl.ANY),
        scratch_shapes=[pltpu.VMEM((2, tm, D_MODEL), xs.dtype), pltpu.VMEM((tm, D_MODEL), xs.dtype),
                        pltpu.SemaphoreType.DMA((2,)), pltpu.SemaphoreType.DMA((1,))],
    )
    return pl.pallas_call(
        functools.partial(_expert_kernel, n_tok_tiles=n_tok_tiles),
        grid_spec=grid_spec,
        out_shape=jax.ShapeDtypeStruct(xs.shape, xs.dtype),
        compiler_params=_cparams(1),
        name="experts",
    )(*tables, xs, w_up, b_up, w_down, b_down)


def _combine_kernel(used_ref, ys_ref, ri_ref, rw_ref, x_ref, g3_ref, gt_ref, xo_ref):
    tm = x_ref.shape[0]
    used = used_ref[pl.program_id(0)]
    rif = ri_ref[...].astype(F32)
    rw = rw_ref[...]
    pos = [_lane_pick(rif, TOP_K + k) for k in range(TOP_K)]
    wgt = [_lane_pick(rw, k) for k in range(TOP_K)]
    f = jnp.zeros(x_ref.shape, F32)
    for c in range(ys_ref.shape[0] // tm):
        scat = _position_onehot(pos, wgt, c * tm, tm, tm)
        row = lax.broadcasted_iota(jnp.int32, (tm, 1), 0) + c * tm
        ys = ys_ref[c * tm:(c + 1) * tm, :]
        ys = jnp.where(row < used, ys, jnp.zeros_like(ys))
        f = f + jnp.dot(scat.astype(MXU_DTYPE), ys, preferred_element_type=F32)
    xo_ref[...] = x_ref[...] + gt_ref[0] * _rms(f, g3_ref[0])


def _combine(tile_used, ys, ri, rw, xa, g_norm, modt, tm, n_tiles):
    seg_rows = ys.shape[0] // n_tiles
    grid_spec = pltpu.PrefetchScalarGridSpec(
        num_scalar_prefetch=1,
        grid=(n_tiles,),
        in_specs=[pl.BlockSpec((seg_rows, D_MODEL), lambda i, used: (i, 0)),
                  pl.BlockSpec((tm, LANE), lambda i, used: (i, 0)),
                  pl.BlockSpec((tm, LANE), lambda i, used: (i, 0)),
                  pl.BlockSpec((tm, D_MODEL), lambda i, used: (i, 0)),
                  pl.BlockSpec((1, 1, D_MODEL), lambda i, used: (3, 0, 0)),
                  pl.BlockSpec((1, 1, D_MODEL), lambda i, used: (i, 0, 5))],
        out_specs=pl.BlockSpec((tm, D_MODEL), lambda i, used: (i, 0)),
    )
    return pl.pallas_call(
        _combine_kernel,
        grid_spec=grid_spec,
        out_shape=jax.ShapeDtypeStruct((n_tiles * tm, D_MODEL), F32),
        compiler_params=_cparams(1),
        name="combine",
    )(tile_used, ys, ri, rw, xa, g_norm, modt)


def _swap16(idx):
    g = np.asarray(idx).reshape(-1, 2, ROPE_FREQS)
    return g[:, ::-1, :].reshape(-1)


def _in_proj_columns():
    sizes = (Q_LORA, KV_LORA, QK_ROPE, M_HEADS * M_QK, M_HEADS * M_QK, M_HEADS * M_V, M_HEADS * M_V,
             2 * 2 * M_HEADS, CONV_WIDTH, CONV_WIDTH, CONV_WIDTH, 3 * D_MODEL)
    starts = np.concatenate([[0], np.cumsum(sizes)[:-1]])
    seg = lambda k: np.arange(starts[k], starts[k] + sizes[k])
    cq, ckv, kr, mq, mk, mv, mo, mif, cu, cb, cc, cg = (seg(k) for k in range(12))
    mqk = np.concatenate([np.concatenate([mq[h * M_QK:(h + 1) * M_QK], mk[h * M_QK:(h + 1) * M_QK]])
                          for h in range(M_HEADS)])
    main = np.concatenate([cq, ckv, mqk, mv, mo, cu, cb, cc, cg])
    pad = lambda n: np.full((n,), -1)
    special = np.concatenate([kr, kr, _swap16(kr), _swap16(kr), mif, pad(LANE - mif.size), pad(LANE)])
    cols = np.concatenate([main, special])
    assert cols.size == (N_MAIN_TILES + 1) * COL_TILE
    return cols


def _take_cols(w, cols, dtype):
    cols = np.asarray(cols)
    breaks = np.flatnonzero((np.diff(cols) != 1) & ~((cols[1:] < 0) & (cols[:-1] < 0))) + 1
    parts = []
    for run in np.split(cols, breaks):
        if run[0] < 0:
            parts.append(jnp.zeros(w.shape[:-1] + (run.size,), dtype))
        else:
            parts.append(w[..., int(run[0]):int(run[-1]) + 1].astype(dtype))
    return jnp.concatenate(parts, axis=-1)


def _permute_cols_kernel(w_ref, *out_refs, col_groups):
    w = w_ref[0]
    for o_ref, cols in zip(out_refs, col_groups):
        o_ref[0] = _take_cols(w, cols, o_ref.dtype)


def _permute_cols(w, col_groups, dtype):
    depth, k, n = w.shape
    bk = 256
    return pl.pallas_call(
        functools.partial(_permute_cols_kernel, col_groups=col_groups),
        grid=(depth, k // bk),
        in_specs=[pl.BlockSpec((1, bk, n), lambda l, i: (l, i, 0))],
        out_specs=[pl.BlockSpec((1, bk, len(c)), lambda l, i: (l, i, 0)) for c in col_groups],
        out_shape=[jax.ShapeDtypeStruct((depth, k, len(c)), dtype) for c in col_groups],
        compiler_params=_cparams(2),
        name="permute_cols",
    )(w)


def _q_up_columns():
    per = QK_NOPE + QK_ROPE
    nope = np.concatenate([np.arange(h * per, h * per + QK_NOPE) for h in range(MLA_HEADS)])
    rope = np.concatenate([np.arange(h * per + QK_NOPE, (h + 1) * per) for h in range(MLA_HEADS)])
    return np.concatenate([nope, rope, _swap16(rope)])


def _kv_up_columns():
    per = QK_NOPE + V_HEAD
    kn = np.concatenate([np.arange(h * per, h * per + QK_NOPE) for h in range(MLA_HEADS)])
    vv = np.concatenate([np.arange(h * per + QK_NOPE, (h + 1) * per) for h in range(MLA_HEADS)])
    return np.concatenate([kn, vv])


def _rope_tables(T, rt):
    pos = jnp.arange(T)
    inv_freq = ROPE_THETA ** (-jnp.arange(ROPE_FREQS, dtype=F32) / ROPE_FREQS)
    ang_r = (pos // GRID_W).astype(F32)[:, None] * inv_freq
    ang_c = (pos % GRID_W).astype(F32)[:, None] * inv_freq
    cos64 = jnp.concatenate([jnp.cos(ang_r), jnp.cos(ang_r), jnp.cos(ang_c), jnp.cos(ang_c)], axis=1)
    sin64 = jnp.concatenate([-jnp.sin(ang_r), jnp.sin(ang_r), -jnp.sin(ang_c), jnp.sin(ang_c)], axis=1)
    cos_t = jnp.concatenate([jnp.tile(cos64, (1, 2)), jnp.ones((rt, LANE), F32)], axis=0)
    sin_t = jnp.concatenate([jnp.tile(sin64, (1, 2)), jnp.zeros((rt, LANE), F32)], axis=0)
    return cos_t, sin_t


def _prefix_sums(a):
    n = a.shape[-1]
    keep = jnp.arange(n)[:, None] <= jnp.arange(n)[None, :]
    return jnp.sum(jnp.where(keep, a[..., :, None], 0), axis=-2)


def _moe_tables(cnt, n_tiles, tok_tile):
    tm = MOE_TILE
    seg_rows = TOP_K * tok_tile + N_EXPERTS * SEG_ALIGN
    n_work = n_tiles * seg_rows // tm + N_EXPERTS
    counts = cnt[:n_tiles, 0, :N_EXPERTS].astype(jnp.int32)
    padded = (counts + SEG_ALIGN - 1) // SEG_ALIGN * SEG_ALIGN
    in_tile = _prefix_sums(padded) - padded
    src_row = (jnp.arange(n_tiles, dtype=jnp.int32)[:, None] * seg_rows + in_tile).T
    seg_start = jnp.concatenate([jnp.zeros((N_EXPERTS, 1), jnp.int32), _prefix_sums(padded.T)], axis=1)
    group_len = seg_start[:, -1]
    work_end = _prefix_sums((group_len + tm - 1) // tm)
    work = jnp.arange(n_work, dtype=jnp.int32)
    tile_expert = jnp.minimum(jnp.sum(work_end[None, :] <= work[:, None], axis=1), N_EXPERTS - 1)
    first_work = jnp.concatenate([jnp.zeros((1,), jnp.int32), work_end[:-1]])
    tile_row0 = (work - first_work[tile_expert]) * tm
    used = jnp.clip(group_len[tile_expert] - tile_row0, 0, tm)
    starts = seg_start[tile_expert]
    t_lo = jnp.sum(starts[:, 1:] <= tile_row0[:, None], axis=1)
    t_hi = jnp.sum(starts[:, :-1] < (tile_row0 + tm)[:, None], axis=1)
    i32 = lambda a: a.astype(jnp.int32)
    tables = (i32(tile_expert), i32(tile_row0), i32(work_end[-1:]), i32(seg_start.reshape(-1)),
              i32(src_row.reshape(-1)), i32(used), i32(t_lo), i32(t_hi))
    return tables, i32(jnp.sum(padded, axis=1))


def _layer(xa, mods, B, T, Tc, rt, tm, update_ctx, p, cos_t, sin_t, layer):
    rows = xa.shape[0]
    n_lat = B * T
    lat_rt, all_rt = n_lat // rt, rows // rt
    lat_tm, all_tm = n_lat // tm, rows // tm
    cond_rt = np.concatenate([np.arange(lat_rt) // (T // rt), np.full((all_rt - lat_rt,), B)])
    cond_tm = np.concatenate([np.arange(lat_tm) // (T // tm), np.full((all_tm - lat_tm,), B)])
    modt_rt = mods[cond_rt][:, None, :]
    modt_tm = mods[cond_tm][:, None, :]
    g_norm = p["g_norm"].reshape(4, 1, D_MODEL)

    z, zs = _inproj(xa, g_norm, modt_rt, p["w_in_main"], p["w_in_special"], layer, rt)
    q, k, v = _mla_project(z, zs, p["g_q_lora"], p["g_kv_lora"], p["w_q_up"], p["w_kv_up"],
                           cos_t, sin_t, rt, lat_rt, T // rt)
    a, w_up_c, w_down_c = _attention_latent(q, k, v, p["w_up"], p["w_down"], layer, p["depth"],
                                            B, T, Tc, min(ATTN_Q_TILE, T), min(ATTN_KV_CHUNK, T),
                                            rows if update_ctx else n_lat)
    w_up_c = w_up_c.reshape(N_EXPERTS, D_MODEL, 2 * D_FF)
    w_down_c = w_down_c.reshape(N_EXPERTS, D_FF, D_MODEL)
    if update_ctx:
        a = _attention_ctx(q, k, v, a, B, T, Tc)
    m = _mlstm(z, zs, p["b_if"], p["g_mlstm"], B, T, Tc)
    n_rt = all_rt if update_ctx else lat_rt
    n_tm = all_tm if update_ctx else lat_tm
    t = _merge(a, m, z, p["w_proj_a"], p["w_proj_b"], p["w_proj_c"], p["w_conv"], p["b_conv"],
               rt, n_rt, lat_rt, T // rt, T, Tc)
    xa, h2, ri, rw, cnt = _post(t, p["w_o"], xa, g_norm, modt_tm, p["w_router"], p["b_router"], tm, n_tm)
    tables, tile_used = _moe_tables(cnt, n_tm, tm)
    xs = _sort_rows(tile_used, h2, ri, tm, n_tm)
    ys = _experts(tables, xs, w_up_c, p["b_up"], w_down_c, p["b_down"], n_tm)
    return _combine(tile_used, ys, ri, rw, xa, g_norm, modt_tm, tm, n_tm)


def kernel(x, c, ctx, c_ctx, w_mod, b_mod, g_norm, w_in, g_q_lora, g_kv_lora, w_q_up, w_kv_up, b_if, g_mlstm,
           w_conv, b_conv, w_proj_a, w_proj_b, w_proj_c, w_o, w_router, b_router, w_up, b_up, w_down, b_down):
    B, T, _ = x.shape
    Tc = ctx.shape[1]
    depth = w_mod.shape[0]
    rt = min(1024, T)
    tm = min(512, T)
    assert T % rt == 0 and (B * Tc) % rt == 0 and (B * Tc) % tm == 0
    assert T & (T - 1) == 0 and Tc & (Tc - 1) == 0 and T % GRID_W == 0

    xa = jnp.concatenate([x.reshape(B * T, D_MODEL), ctx.reshape(B * Tc, D_MODEL)], axis=0)
    cond = jnp.zeros((8, D_MODEL), F32).at[:B].set(c).at[B].set(c_ctx)
    cos_t, sin_t = _rope_tables(T, rt)
    cast = lambda w: w.astype(MXU_DTYPE)
    in_cols = _in_proj_columns()
    n_main = N_MAIN_TILES * COL_TILE
    w_in_main, w_in_special = _permute_cols(w_in, (in_cols[:n_main], in_cols[n_main:]), MXU_DTYPE)
    w_q_p = _take_cols(w_q_up, _q_up_columns(), MXU_DTYPE)
    w_kv_p = _take_cols(w_kv_up, _kv_up_columns(), MXU_DTYPE)
    w_up_r = w_up.reshape(depth * N_EXPERTS * D_MODEL, 2 * D_FF)
    w_down_r = w_down.reshape(depth * N_EXPERTS * D_FF, D_MODEL)
    wa_c, wb_c, wc_c, wo_c = cast(w_proj_a), cast(w_proj_b), cast(w_proj_c), cast(w_o)
    w_router_p = jnp.pad(w_router, ((0, 0), (0, 0), (0, LANE - N_EXPERTS)))
    b_router_p = jnp.pad(b_router, ((0, 0), (0, LANE - N_EXPERTS)), constant_values=NEG_BIG)
    b_if_p = jnp.pad(b_if.reshape(depth, 1, -1), ((0, 0), (0, 0), (0, LANE - 4 * M_HEADS)))
    mods = _adaln(cond, w_mod, b_mod)
    for l in range(depth):
        p = dict(
            g_norm=g_norm[l], w_in_main=w_in_main, w_in_special=w_in_special,
            g_q_lora=g_q_lora[l], g_kv_lora=g_kv_lora[l],
            w_q_up=w_q_p[l], w_kv_up=w_kv_p[l], b_if=b_if_p[l],
            g_mlstm=g_mlstm[l], w_conv=w_conv[l], b_conv=b_conv[l],
            w_proj_a=wa_c[l], w_proj_b=wb_c[l], w_proj_c=wc_c[l], w_o=wo_c[l],
            w_router=w_router_p[l], b_router=b_router_p[l],
            depth=depth, w_up=w_up_r, b_up=b_up[l].reshape(N_EXPERTS, 1, -1),
            w_down=w_down_r, b_down=b_down[l].reshape(N_EXPERTS, 1, -1),
        )
        xa = _layer(xa, mods[l], B, T, Tc, rt, tm, l < depth - 1, p, cos_t, sin_t, l)
    return xa.reshape(B, T, D_MODEL)
```

```python
import functools

import numpy as np
import jax
import jax.numpy as jnp
from jax import lax
from jax.experimental import pallas as pl
from jax.experimental.pallas import tpu as pltpu

F32 = jnp.float32
MXU_DTYPE = jnp.bfloat16

D_MODEL = 2048
GRID_W = 64
MLA_HEADS = 8
QK_NOPE = 128
QK_ROPE = 64
V_HEAD = 128
Q_LORA = 512
KV_LORA = 512
ROPE_THETA = 10000.0
ROPE_FREQS = QK_ROPE // 4
ATTN_SCALE = (QK_NOPE + QK_ROPE) ** -0.5
M_HEADS = 4
M_QK = 64
M_V = 128
CONV_WIDTH = 512
N_EXPERTS = 32
TOP_K = 4
D_FF = 1024
SWIGLU_ALPHA = 1.702
SWIGLU_LIMIT = 7.0
EPS = 1e-6

LANE = 128
HEAD_W = 2 * LANE
MLSTM_CHUNK = 256
COL_TILE = 512
INPROJ_TILE = 1024
LOG2_E = 1.4426950408889634
MERGE_TILE = 1024
ATTN_Q_TILE = 1024
ATTN_KV_CHUNK = 256
MOE_TILE = 512
SEG_ALIGN = 16
NEG_BIG = -1e30
VMEM_LIMIT = 56 * 1024 * 1024

ZT_CQ, ZT_CKV, ZT_MQK, ZT_MV, ZT_MO, ZT_CU, ZT_CB, ZT_CC, ZT_CG = 0, 1, 2, 3, 4, 5, 6, 7, 8
N_MAIN_TILES = 8 + 3 * D_MODEL // COL_TILE
assert (N_MAIN_TILES * COL_TILE) % INPROJ_TILE == 0


def _cparams(n_axes, vmem=VMEM_LIMIT):
    return pltpu.CompilerParams(dimension_semantics=("arbitrary",) * n_axes, vmem_limit_bytes=vmem)


def _dot_nt(x, wt):
    return lax.dot_general(x, wt, (((1,), (1,)), ((), ())), preferred_element_type=F32)


def _rms(x, g):
    return x * lax.rsqrt(jnp.mean(x * x, axis=-1, keepdims=True) + EPS) * g


def _sigmoid(x):
    return 0.5 * jnp.tanh(0.5 * x) + 0.5


def _lane_pick(x, idx):
    lane = lax.broadcasted_iota(jnp.int32, x.shape, 1)
    return jnp.sum(jnp.where(lane == idx, x, 0.0), axis=1, keepdims=True)


def _mod_kernel(c_ref, w_ref, b_ref, o_ref):
    c = c_ref[...]
    s = c * jax.nn.sigmoid(c)
    o_ref[0] = jnp.dot(s.astype(MXU_DTYPE), w_ref[0].astype(MXU_DTYPE),
                       preferred_element_type=F32) + b_ref[0]


def _adaln(cond, w_mod, b_mod):
    depth, _, n = w_mod.shape
    tn = 1024
    return pl.pallas_call(
        _mod_kernel,
        grid=(depth, n // tn),
        in_specs=[pl.BlockSpec(cond.shape, lambda l, j: (0, 0)),
                  pl.BlockSpec((1, D_MODEL, tn), lambda l, j: (l, 0, j)),
                  pl.BlockSpec((1, 1, tn), lambda l, j: (l, 0, j))],
        out_specs=pl.BlockSpec((1, cond.shape[0], tn), lambda l, j: (l, 0, j)),
        out_shape=jax.ShapeDtypeStruct((depth, cond.shape[0], n), F32),
        compiler_params=_cparams(2),
        name="adaln",
    )(cond, w_mod, b_mod.reshape(depth, 1, n))


def _row_sources(parts, tile):
    first, second = parts
    n_first = first.shape[0] // tile
    if second is None:
        second, second_map = first, (lambda i, *_: (0, 0))
    else:
        second_map = lambda i, *_: (jnp.maximum(i - n_first, 0), 0)
    width = first.shape[1]
    specs = [pl.BlockSpec((tile, width), lambda i, *_: (jnp.minimum(i, n_first - 1), 0)),
             pl.BlockSpec((tile, width), second_map, pipeline_mode=pl.Buffered(1))]
    return (first, second), specs, n_first


def _inproj_kernel(x_ref, g_ref, sh_ref, sc_ref, w_ref, ws_ref, z_ref, zs_ref, h_scr):
    @pl.when(pl.program_id(1) == 0)
    def _():
        h = _rms(x_ref[...], g_ref[0]) * (1.0 + sc_ref[0]) + sh_ref[0]
        h_scr[...] = h.astype(h_scr.dtype)
        zs_ref[...] = _dot_nt(h_scr[...], ws_ref[0])

    z_ref[...] = _dot_nt(h_scr[...], w_ref[0]).astype(z_ref.dtype)


def _inproj(xa, g_norm, modt, w_main, w_special, layer, rt):
    rows = xa.shape[0]
    n_main = w_main.shape[1]
    tn = INPROJ_TILE
    return pl.pallas_call(
        _inproj_kernel,
        grid=(rows // rt, n_main // tn),
        in_specs=[pl.BlockSpec((rt, D_MODEL), lambda i, j: (i, 0)),
                  pl.BlockSpec((1, 1, D_MODEL), lambda i, j: (0, 0, 0)),
                  pl.BlockSpec((1, 1, D_MODEL), lambda i, j: (i, 0, 0)),
                  pl.BlockSpec((1, 1, D_MODEL), lambda i, j: (i, 0, 1)),
                  pl.BlockSpec((1, tn, D_MODEL), lambda i, j: (layer, j, 0)),
                  pl.BlockSpec((1, COL_TILE, D_MODEL), lambda i, j: (layer, 0, 0))],
        out_specs=[pl.BlockSpec((rt, tn), lambda i, j: (i, j)),
                   pl.BlockSpec((rt, COL_TILE), lambda i, j: (i, 0))],
        out_shape=[jax.ShapeDtypeStruct((rows, n_main), MXU_DTYPE),
                   jax.ShapeDtypeStruct((rows, COL_TILE), F32)],
        scratch_shapes=[pltpu.VMEM((rt, D_MODEL), MXU_DTYPE)],
        compiler_params=_cparams(2),
        name="inproj",
    )(xa, g_norm, modt, modt, w_main, w_special)


def _mla_kernel(cq_ref, ckv_ref, zs_ref, gq_ref, gkv_ref, wq_ref, wkv_ref, cos_ref, sin_ref,
                q_ref, k_ref, v_ref):
    def norm(z_ref, g_ref):
        return _rms(z_ref[...].astype(F32), g_ref[...]).astype(MXU_DTYPE)

    qf = jnp.dot(norm(cq_ref, gq_ref), wq_ref[...], preferred_element_type=F32) * (ATTN_SCALE * LOG2_E)
    kvf = jnp.dot(norm(ckv_ref, gkv_ref), wkv_ref[...], preferred_element_type=F32)
    cos = cos_ref[...]
    sin = sin_ref[...]
    nope_w = MLA_HEADS * QK_NOPE
    rope_w = MLA_HEADS * QK_ROPE
    for p in range(MLA_HEADS // 2):
        a = nope_w + LANE * p
        b = nope_w + rope_w + LANE * p
        rp = (qf[:, a:a + LANE] * cos + qf[:, b:b + LANE] * sin).astype(MXU_DTYPE)
        for h in (2 * p, 2 * p + 1):
            q_ref[:, HEAD_W *h:HEAD_W *h + LANE] = qf[:, LANE * h:LANE * (h + 1)].astype(MXU_DTYPE)
            q_ref[:, HEAD_W *h + LANE:HEAD_W *(h + 1)] = rp
    kr2 = zs_ref[:, 0:LANE] * cos + zs_ref[:, LANE:2 * LANE] * sin
    lane = lax.broadcasted_iota(jnp.int32, kr2.shape, 1)
    k_lo = jnp.where(lane < QK_ROPE, kr2, 0.0).astype(MXU_DTYPE)
    k_hi = jnp.where(lane >= QK_ROPE, kr2, 0.0).astype(MXU_DTYPE)
    for h in range(MLA_HEADS):
        k_ref[:, HEAD_W *h:HEAD_W *h + LANE] = kvf[:, LANE * h:LANE * (h + 1)].astype(MXU_DTYPE)
        k_ref[:, HEAD_W *h + LANE:HEAD_W *(h + 1)] = k_lo if h % 2 == 0 else k_hi
    ones = jnp.ones((kvf.shape[0], LANE), MXU_DTYPE)
    v0 = MLA_HEADS * QK_NOPE
    for h in range(MLA_HEADS):
        v_ref[:, HEAD_W *h:HEAD_W *h + LANE] = kvf[:, v0 + LANE * h:v0 + LANE * (h + 1)].astype(MXU_DTYPE)
        v_ref[:, HEAD_W *h + LANE:HEAD_W *(h + 1)] = ones


def _mla_project(z, zs, g_q, g_kv, wq_p, wkv_p, cos_t, sin_t, rt, n_lat_tiles, tiles_per_seq):
    rows = z.shape[0]
    nt = rows // rt

    def tab(i):
        return (jnp.where(i < n_lat_tiles, i % tiles_per_seq, tiles_per_seq), 0)

    return pl.pallas_call(
        _mla_kernel,
        grid=(nt,),
        in_specs=[pl.BlockSpec((rt, COL_TILE), lambda i: (i, ZT_CQ)),
                  pl.BlockSpec((rt, COL_TILE), lambda i: (i, ZT_CKV)),
                  pl.BlockSpec((rt, COL_TILE), lambda i: (i, 0)),
                  pl.BlockSpec((1, Q_LORA), lambda i: (0, 0)),
                  pl.BlockSpec((1, KV_LORA), lambda i: (0, 0)),
                  pl.BlockSpec(wq_p.shape, lambda i: (0, 0)),
                  pl.BlockSpec(wkv_p.shape, lambda i: (0, 0)),
                  pl.BlockSpec((rt, LANE), tab),
                  pl.BlockSpec((rt, LANE), tab)],
        out_specs=[pl.BlockSpec((rt, HEAD_W *MLA_HEADS), lambda i: (i, 0)),
                   pl.BlockSpec((rt, HEAD_W *MLA_HEADS), lambda i: (i, 0)),
                   pl.BlockSpec((rt, HEAD_W *MLA_HEADS), lambda i: (i, 0))],
        out_shape=[jax.ShapeDtypeStruct((rows, HEAD_W *MLA_HEADS), MXU_DTYPE)] * 3,
        compiler_params=_cparams(1),
        name="mla_project",
    )(z, z, zs, g_q.reshape(1, -1), g_kv.reshape(1, -1), wq_p, wkv_p, cos_t, sin_t)


def _attn_kernel(*refs, kv_chunks, n_cast=0):
    q_ref = refs[0]
    o_ref = refs[len(refs) - 1 - n_cast]
    for c in range(n_cast):
        src = refs[1 + 2 * len(kv_chunks) + c]
        dst = refs[len(refs) - n_cast + c]
        dst[...] = src[...].astype(dst.dtype)
    q = q_ref[...]
    tq = q.shape[0]
    m = jnp.full((tq, 1), -jnp.inf, F32)
    acc = jnp.zeros((tq, 2 * V_HEAD), F32)
    for s, n_chunks in enumerate(kv_chunks):
        k_ref, v_ref = refs[1 + 2 * s], refs[2 + 2 * s]
        ck = k_ref.shape[0] // n_chunks
        for c in range(n_chunks):
            k = k_ref[c * ck:(c + 1) * ck, :]
            v = v_ref[c * ck:(c + 1) * ck, :]
            sc = lax.dot_general(q, k, (((1,), (1,)), ((), ())), preferred_element_type=F32)
            m_new = jnp.maximum(m, jnp.max(sc, axis=-1, keepdims=True))
            p = jnp.exp2(sc - m_new).astype(MXU_DTYPE)
            acc = jnp.exp2(m - m_new) * acc + jnp.dot(p, v, preferred_element_type=F32)
            m = m_new
    o_ref[...] = (acc[:, :V_HEAD] / acc[:, V_HEAD:]).astype(o_ref.dtype)


def _attention_latent(q, k, v, w_up, w_down, layer, depth, B, T, Tc, tq, ck, rows):
    qt = T // tq
    ctx_blk0 = B * T // Tc
    steps = B * MLA_HEADS * qt
    step = lambda b, h, i: (b * MLA_HEADS + h) * qt + i
    cast_specs_in, cast_specs_out, cast_shapes = [], [], []
    for w in (w_up, w_down):
        n = w.shape[1]
        layer_rows = w.shape[0] // depth
        slab = layer_rows // steps
        assert slab * steps == layer_rows and slab % 16 == 0
        cast_specs_in.append(pl.BlockSpec((slab, n), lambda b, h, i: (layer * steps + step(b, h, i), 0)))
        cast_specs_out.append(pl.BlockSpec((slab, n), lambda b, h, i: (step(b, h, i), 0)))
        cast_shapes.append(jax.ShapeDtypeStruct((layer_rows, n), MXU_DTYPE))
    return pl.pallas_call(
        functools.partial(_attn_kernel, kv_chunks=(T // ck, 1), n_cast=2),
        grid=(B, MLA_HEADS, qt),
        in_specs=[pl.BlockSpec((tq, HEAD_W), lambda b, h, i: (b * qt + i, h)),
                  pl.BlockSpec((T, HEAD_W), lambda b, h, i: (b, h)),
                  pl.BlockSpec((T, HEAD_W), lambda b, h, i: (b, h)),
                  pl.BlockSpec((Tc, HEAD_W), lambda b, h, i: (ctx_blk0 + b, h)),
                  pl.BlockSpec((Tc, HEAD_W), lambda b, h, i: (ctx_blk0 + b, h))] + cast_specs_in,
        out_specs=[pl.BlockSpec((tq, V_HEAD), lambda b, h, i: (b * qt + i, h))] + cast_specs_out,
        out_shape=[jax.ShapeDtypeStruct((rows, V_HEAD * MLA_HEADS), MXU_DTYPE)] + cast_shapes,
        compiler_params=_cparams(3),
        name="attn_latent",
    )(q, k, v, k, v, w_up, w_down)


def _attention_ctx(q, k, v, a_buf, B, T, Tc):
    ctx_blk0 = B * T // Tc
    return pl.pallas_call(
        functools.partial(_attn_kernel, kv_chunks=(1,)),
        grid=(B, MLA_HEADS),
        in_specs=[pl.BlockSpec((Tc, HEAD_W), lambda b, h: (ctx_blk0 + b, h)),
                  pl.BlockSpec((Tc, HEAD_W), lambda b, h: (ctx_blk0 + b, h)),
                  pl.BlockSpec((Tc, HEAD_W), lambda b, h: (ctx_blk0 + b, h)),
                  pl.BlockSpec(memory_space=pl.ANY)],
        out_specs=pl.BlockSpec((Tc, V_HEAD), lambda b, h: (ctx_blk0 + b, h)),
        out_shape=jax.ShapeDtypeStruct(a_buf.shape, a_buf.dtype),
        input_output_aliases={3: 0},
        compiler_params=_cparams(2),
        name="attn_ctx",
    )(q, k, v, a_buf)


def _mlstm_chunk(qk, v, g, lf_cum, gt_ref, head, reverse, ct_ref, m_prev, consts):
    tri, row_iota, mk, ones_tile, lane = consts
    L = qk.shape[0]
    base = 8 if reverse else 0
    col_i, col_f = base + head, base + 4 + head
    ql = jnp.where(lane < M_QK, qk * (M_QK ** -0.5), jnp.zeros_like(qk))
    kl = jnp.dot(qk, mk, preferred_element_type=F32).astype(MXU_DTYPE)
    v_aug = jnp.concatenate([v, ones_tile], axis=1)
    bc = _lane_pick(lf_cum, col_f)
    li = _lane_pick(g, col_i)
    r_row = gt_ref[0, col_i:col_i + 1, :] - gt_ref[1, col_f:col_f + 1, :]
    dlog = jnp.where(tri, bc + r_row, -jnp.inf)
    inter = bc + m_prev
    m_t = jnp.maximum(inter, jnp.max(dlog, axis=1, keepdims=True))
    dexp = jnp.exp(dlog - m_t)
    s_raw = lax.dot_general(ql, kl, (((1,), (1,)), ((), ())), preferred_element_type=F32)
    s_mat = (s_raw * dexp).astype(MXU_DTYPE)
    a = jnp.exp(inter - m_t)
    ct = ct_ref[...]
    nd = (jnp.dot(s_mat, v_aug, preferred_element_type=F32)
          + a * jnp.dot(ql, ct.astype(MXU_DTYPE), preferred_element_type=F32))
    num, den = nd[:, :M_V], nd[:, M_V:]
    h_out = num / jnp.maximum(jnp.abs(den), jnp.exp(-m_t))
    last = 0 if reverse else L - 1
    total = jnp.sum(jnp.where(row_iota == last, bc, 0.0), axis=0, keepdims=True)
    w_col = total - bc + li
    m_new = jnp.maximum(total + m_prev, jnp.max(w_col, axis=0, keepdims=True))
    decay = jnp.exp(total + m_prev - m_new)
    e = jnp.exp(w_col - m_new)
    ek = (kl.astype(F32) * e).astype(MXU_DTYPE)
    upd = lax.dot_general(ek, v_aug, (((0,), (0,)), ((), ())), preferred_element_type=F32)
    ct_ref[...] = decay * ct + upd
    return h_out, m_new


def _mlstm_kernel(zl_qk, zl_v, zl_mo, gl_ref, zc_qk, zc_v, zc_mo, gc_ref, bif_ref, gm_ref,
                  ol_ref, oc_ref, acc_l, acc_c, ct_scr, gt_scr):
    L = MLSTM_CHUNK
    T = zl_qk.shape[0]
    nc = T // L
    r = lax.broadcasted_iota(jnp.int32, (L, L), 0)
    c = lax.broadcasted_iota(jnp.int32, (L, L), 1)
    tri_f = c <= r
    tri_b = c >= r
    tmat_f = tri_f.astype(jnp.bfloat16)
    tmat_b = tri_b.astype(jnp.bfloat16)
    row_iota = lax.broadcasted_iota(jnp.int32, (L, 1), 0)
    lane = lax.broadcasted_iota(jnp.int32, (L, LANE), 1)
    rr = lax.broadcasted_iota(jnp.int32, (LANE, LANE), 0)
    cc = lax.broadcasted_iota(jnp.int32, (LANE, LANE), 1)
    mk = (rr == cc + M_QK).astype(MXU_DTYPE)
    ones_tile = jnp.ones((L, LANE), MXU_DTYPE)
    consts_f = (tri_f, row_iota, mk, ones_tile, lane)
    consts_b = (tri_b, row_iota, mk, ones_tile, lane)
    bias = bif_ref[...]

    def gate_prep(g_raw, tmat, slot):
        g = g_raw + bias
        log_sig = jnp.minimum(g, 0.0) - jnp.log1p(jnp.exp(-jnp.abs(g)))
        hi = log_sig.astype(jnp.bfloat16)
        rest = log_sig - hi.astype(F32)
        mid = rest.astype(jnp.bfloat16)
        low = (rest - mid.astype(F32)).astype(jnp.bfloat16)
        sums = jnp.dot(tmat, jnp.concatenate([hi, mid, low], axis=1), preferred_element_type=F32)
        cum = sums[:, :LANE] + sums[:, LANE:2 * LANE] + sums[:, 2 * LANE:]
        gt_scr[slot, 0] = g.T
        gt_scr[slot, 1] = cum.T
        return g, cum

    def step(qk_ref, v_ref, g_ref, acc_ref, row_f, row_b, ms):
        g_f, cum_f = gate_prep(g_ref[pl.ds(row_f, L), :], tmat_f, 0)
        g_b, cum_b = gate_prep(g_ref[pl.ds(row_b, L), :], tmat_b, 1)
        new_ms = []
        for h in range(M_HEADS):
            cols = slice(h * LANE, (h + 1) * LANE)
            hf, mf = _mlstm_chunk(qk_ref[pl.ds(row_f, L), cols], v_ref[pl.ds(row_f, L), cols],
                                  g_f, cum_f, gt_scr.at[0], h, False,
                                  ct_scr.at[h], ms[h], consts_f)
            acc_ref[pl.ds(row_f, L), cols] += hf
            hb, mb = _mlstm_chunk(qk_ref[pl.ds(row_b, L), cols], v_ref[pl.ds(row_b, L), cols],
                                  g_b, cum_b, gt_scr.at[1], h, True,
                                  ct_scr.at[M_HEADS + h], ms[M_HEADS + h], consts_b)
            acc_ref[pl.ds(row_b, L), cols] += hb
            new_ms.append((mf, mb))
        return tuple(x[0] for x in new_ms) + tuple(x[1] for x in new_ms)

    ct_scr[...] = jnp.zeros_like(ct_scr)
    acc_l[...] = jnp.zeros_like(acc_l)
    acc_c[...] = jnp.zeros_like(acc_c)
    zero = jnp.zeros((1, 1), F32)
    ms = step(zc_qk, zc_v, gc_ref, acc_c, 0, 0, (zero,) * (2 * M_HEADS))

    def body(i, ms):
        row_f = pl.multiple_of(i * L, L)
        row_b = pl.multiple_of((nc - 1 - i) * L, L)
        return step(zl_qk, zl_v, gl_ref, acc_l, row_f, row_b, ms)

    lax.fori_loop(0, nc, body, ms)

    gm = gm_ref[...]

    def finish(acc_ref, mo_ref, o_ref, row):
        for h in range(M_HEADS):
            cols = slice(h * LANE, (h + 1) * LANE)
            hn = _rms(acc_ref[pl.ds(row, L), cols], gm[:, cols])
            gate = jax.nn.sigmoid(mo_ref[pl.ds(row, L), cols].astype(F32))
            o_ref[pl.ds(row, L), cols] = (hn * gate).astype(o_ref.dtype)

    finish(acc_c, zc_mo, oc_ref, 0)

    def fin_body(i, carry):
        finish(acc_l, zl_mo, ol_ref, pl.multiple_of(i * L, L))
        return carry

    lax.fori_loop(0, nc, fin_body, 0)


def _mlstm(z, zs, bif_row, g_mlstm, B, T, Tc):
    assert Tc == MLSTM_CHUNK and T % MLSTM_CHUNK == 0
    rows = z.shape[0]
    ctx_blk0 = B * T // Tc
    w = M_HEADS * LANE
    out_l, out_c = pl.pallas_call(
        _mlstm_kernel,
        grid=(B,),
        in_specs=[pl.BlockSpec((T, w), lambda b: (b, ZT_MQK)),
                  pl.BlockSpec((T, w), lambda b: (b, ZT_MV)),
                  pl.BlockSpec((T, w), lambda b: (b, ZT_MO)),
                  pl.BlockSpec((T, LANE), lambda b: (b, 2)),
                  pl.BlockSpec((Tc, w), lambda b: (ctx_blk0 + b, ZT_MQK)),
                  pl.BlockSpec((Tc, w), lambda b: (ctx_blk0 + b, ZT_MV)),
                  pl.BlockSpec((Tc, w), lambda b: (ctx_blk0 + b, ZT_MO)),
                  pl.BlockSpec((Tc, LANE), lambda b: (ctx_blk0 + b, 2)),
                  pl.BlockSpec((1, LANE), lambda b: (0, 0)),
                  pl.BlockSpec((1, w), lambda b: (0, 0))],
        out_specs=[pl.BlockSpec((T, w), lambda b: (b, 0)),
                   pl.BlockSpec((Tc, w), lambda b: (b, 0))],
        out_shape=[jax.ShapeDtypeStruct((B * T, w), MXU_DTYPE),
                   jax.ShapeDtypeStruct((B * Tc, w), MXU_DTYPE)],
        scratch_shapes=[pltpu.VMEM((T, w), F32), pltpu.VMEM((Tc, w), F32),
                        pltpu.VMEM((2 * M_HEADS, LANE, 2 * LANE), F32),
                        pltpu.VMEM((2, 2, LANE, MLSTM_CHUNK), F32)],
        compiler_params=_cparams(1),
        name="mlstm",
    )(z, z, z, zs, z, z, z, zs, bif_row, g_mlstm.reshape(1, -1))
    return out_l, out_c


def _merge_kernel(ml_ref, mc_ref, a_ref, cu_ref, cb_ref, cc_ref, cu_p, cc_p, cu_n, cc_n, ga_ref, gm_ref, gs_ref,
                  wa_ref, wb_ref, wc_ref, wconv_ref, bconv_ref, t_ref, s_scr,
                  *, n_lat_tiles, tiles_per_seq, seq_len, ctx_len):
    i = pl.program_id(0)
    j = pl.program_id(1)
    rt = a_ref.shape[0]
    m = jnp.where(i < n_lat_tiles, ml_ref[...], mc_ref[...])

    @pl.when(j == 0)
    def _():
        u = cc_ref[...].astype(F32) * cu_ref[...].astype(F32)
        hp = cc_p[...].astype(F32) * cu_p[...].astype(F32)
        hn = cc_n[...].astype(F32) * cu_n[...].astype(F32)
        row = lax.broadcasted_iota(jnp.int32, (rt, 1), 0)
        is_ctx = i >= n_lat_tiles
        seg = jnp.where(is_ctx, ctx_len, seq_len)
        off = jnp.where(is_ctx, (i - n_lat_tiles) * rt, (i % tiles_per_seq) * rt)
        pos = (off + row) & (seg - 1)
        up = pltpu.roll(u, 1, 0)
        up = jnp.where(row == 0, hp[hp.shape[0] - 1:, :], up)
        up = jnp.where(pos == 0, 0.0, up)
        un = pltpu.roll(u, rt - 1, 0)
        un = jnp.where(row == rt - 1, hn[0:1, :], un)
        un = jnp.where(pos == seg - 1, 0.0, un)
        wconv = wconv_ref[...]
        y = up * wconv[0:1, :] + u * wconv[1:2, :] + un * wconv[2:3, :] + bconv_ref[...]
        s_scr[...] = (cb_ref[...].astype(F32) * y).astype(s_scr.dtype)

    def sig(ref):
        return _sigmoid(ref[...].astype(F32))

    t = (sig(ga_ref) * jnp.dot(a_ref[...], wa_ref[...], preferred_element_type=F32)
         + sig(gm_ref) * jnp.dot(m, wb_ref[...], preferred_element_type=F32)
         + sig(gs_ref) * jnp.dot(s_scr[...], wc_ref[...], preferred_element_type=F32))
    t_ref[...] = t.astype(t_ref.dtype)


def _merge(a, m, z, wa, wb, wc, w_conv, b_conv, rt, n_tiles, n_lat_tiles, tiles_per_seq, T, Tc):
    rows = n_tiles * rt
    hb = 16
    rb = rt // hb
    last_hb = z.shape[0] // hb - 1
    tn = MERGE_TILE
    nj = D_MODEL // tn
    gt = D_MODEL // tn
    g0 = ZT_CG * COL_TILE // tn
    assert (ZT_CG * COL_TILE) % tn == 0

    def prev_blk(col):
        return lambda i, j: (jnp.maximum(i * rb - 1, 0), col)

    def next_blk(col):
        return lambda i, j: (jnp.minimum((i + 1) * rb, last_hb), col)

    kern = functools.partial(_merge_kernel, n_lat_tiles=n_lat_tiles, tiles_per_seq=tiles_per_seq,
                             seq_len=T, ctx_len=Tc)
    m_arrays, m_specs, m_first = _row_sources(m, rt)
    assert m_first == n_lat_tiles
    return pl.pallas_call(
        kern,
        grid=(n_tiles, nj),
        in_specs=m_specs + [pl.BlockSpec((rt, a.shape[1]), lambda i, j: (i, 0)),
                  pl.BlockSpec((rt, COL_TILE), lambda i, j: (i, ZT_CU)),
                  pl.BlockSpec((rt, COL_TILE), lambda i, j: (i, ZT_CB)),
                  pl.BlockSpec((rt, COL_TILE), lambda i, j: (i, ZT_CC)),
                  pl.BlockSpec((hb, COL_TILE), prev_blk(ZT_CU)),
                  pl.BlockSpec((hb, COL_TILE), prev_blk(ZT_CC)),
                  pl.BlockSpec((hb, COL_TILE), next_blk(ZT_CU)),
                  pl.BlockSpec((hb, COL_TILE), next_blk(ZT_CC)),
                  pl.BlockSpec((rt, tn), lambda i, j: (i, g0 + j)),
                  pl.BlockSpec((rt, tn), lambda i, j: (i, g0 + gt + j)),
                  pl.BlockSpec((rt, tn), lambda i, j: (i, g0 + 2 * gt + j)),
                  pl.BlockSpec((wa.shape[0], tn), lambda i, j: (0, j)),
                  pl.BlockSpec((wb.shape[0], tn), lambda i, j: (0, j)),
                  pl.BlockSpec((wc.shape[0], tn), lambda i, j: (0, j)),
                  pl.BlockSpec(w_conv.shape, lambda i, j: (0, 0)),
                  pl.BlockSpec((1, CONV_WIDTH), lambda i, j: (0, 0))],
        out_specs=pl.BlockSpec((rt, tn), lambda i, j: (i, j)),
        out_shape=jax.ShapeDtypeStruct((rows, D_MODEL), MXU_DTYPE),
        scratch_shapes=[pltpu.VMEM((rt, CONV_WIDTH), MXU_DTYPE)],
        compiler_params=_cparams(2),
        name="merge",
    )(*m_arrays, a, z, z, z, z, z, z, z, z, z, z, wa, wb, wc, w_conv, b_conv.reshape(1, -1))


def _dot3(x, w):
    xh = x.astype(jnp.bfloat16)
    xl = (x - xh.astype(F32)).astype(jnp.bfloat16)
    wh = w.astype(jnp.bfloat16)
    wl = (w - wh.astype(F32)).astype(jnp.bfloat16)
    n = w.shape[1]
    hi = jnp.dot(xh, jnp.concatenate([wh, wl], axis=1), preferred_element_type=F32)
    return hi[:, :n] + hi[:, n:] + jnp.dot(xl, wh, preferred_element_type=F32)


def _seg_pad(counts):
    return jnp.ceil(counts * (1.0 / SEG_ALIGN)) * SEG_ALIGN


def _exclusive_lane_cumsum(row):
    upper = (lax.broadcasted_iota(jnp.int32, (LANE, LANE), 0)
             < lax.broadcasted_iota(jnp.int32, (LANE, LANE), 1)).astype(jnp.bfloat16)
    out = jnp.dot(jnp.broadcast_to(row, (8, LANE)).astype(jnp.bfloat16), upper, preferred_element_type=F32)
    return out[0:1, :]


def _post_kernel(x_ref, t_ref, wo_ref, g1_ref, g2_ref, gt_ref, sh_ref, sc_ref, wr_ref, br_ref,
                 xo_ref, h2_ref, ri_ref, rw_ref, cnt_ref):
    half = t_ref.shape[0] // 2
    parts = []
    for r in (slice(0, half), slice(half, 2 * half)):
        y = jnp.dot(t_ref[r, :], wo_ref[...], preferred_element_type=F32)
        xn = x_ref[r, :] + gt_ref[0] * _rms(y, g1_ref[0])
        xo_ref[r, :] = xn
        h2 = _rms(xn, g2_ref[0]) * (1.0 + sc_ref[0]) + sh_ref[0]
        h2_ref[r, :] = h2.astype(h2_ref.dtype)
        parts.append(_dot3(h2, wr_ref[...]) + br_ref[...])
    logits = jnp.concatenate(parts, axis=0)
    lane = lax.broadcasted_iota(jnp.int32, logits.shape, 1)
    lane_f = lane.astype(F32)
    vals, ids = [], []
    for _ in range(TOP_K):
        mx = jnp.max(logits, axis=1, keepdims=True)
        idx = jnp.min(jnp.where(logits == mx, lane_f, float(LANE)), axis=1, keepdims=True)
        logits = jnp.where(lane_f == idx, -jnp.inf, logits)
        vals.append(mx)
        ids.append(idx)
    es = [jnp.exp(v - vals[0]) for v in vals]
    tot = es[0] + es[1] + es[2] + es[3]
    tm = logits.shape[0]
    onehots = [lane_f == ids[k] for k in range(TOP_K)]
    asg = jnp.zeros(logits.shape, F32)
    for k in range(TOP_K):
        asg = jnp.where(onehots[k], 1.0, asg)
    earlier = (lax.broadcasted_iota(jnp.int32, (tm, tm), 1)
               < lax.broadcasted_iota(jnp.int32, (tm, tm), 0)).astype(jnp.bfloat16)
    before = jnp.dot(earlier, asg.astype(jnp.bfloat16), preferred_element_type=F32)
    counts = jnp.sum(asg, axis=0, keepdims=True)
    seg_off = _exclusive_lane_cumsum(_seg_pad(counts))
    rw = jnp.zeros(logits.shape, F32)
    ri = jnp.zeros(logits.shape, F32)
    for k in range(TOP_K):
        pos = jnp.sum(jnp.where(onehots[k], before + seg_off, 0.0), axis=1, keepdims=True)
        rw = jnp.where(lane == k, es[k] / tot, rw)
        ri = jnp.where(lane == k, ids[k], ri)
        ri = jnp.where(lane == TOP_K + k, pos, ri)
    rw_ref[...] = rw
    ri_ref[...] = ri.astype(jnp.int32)
    cnt_ref[0] = jnp.broadcast_to(counts, cnt_ref.shape[1:])


def _post(t, wo, xa, g_norm, modt, wr_p, br_p, tm, n_tiles):
    rows = n_tiles * tm
    row_spec = lambda w: pl.BlockSpec((tm, w), lambda i: (i, 0))
    mod = lambda k: pl.BlockSpec((1, 1, D_MODEL), lambda i: (i, 0, k))
    gn = lambda k: pl.BlockSpec((1, 1, D_MODEL), lambda i: (k, 0, 0))
    return pl.pallas_call(
        _post_kernel,
        grid=(n_tiles,),
        in_specs=[row_spec(D_MODEL), row_spec(D_MODEL), pl.BlockSpec(wo.shape, lambda i: (0, 0)),
                  gn(1), gn(2), mod(2), mod(3), mod(4),
                  pl.BlockSpec(wr_p.shape, lambda i: (0, 0)), pl.BlockSpec((1, LANE), lambda i: (0, 0))],
        out_specs=[row_spec(D_MODEL), row_spec(D_MODEL), row_spec(LANE), row_spec(LANE),
                   pl.BlockSpec((1, 8, LANE), lambda i: (i, 0, 0))],
        out_shape=[jax.ShapeDtypeStruct((rows, D_MODEL), F32),
                   jax.ShapeDtypeStruct((rows, D_MODEL), MXU_DTYPE),
                   jax.ShapeDtypeStruct((rows, LANE), jnp.int32),
                   jax.ShapeDtypeStruct((rows, LANE), F32),
                   jax.ShapeDtypeStruct((rows // tm, 8, LANE), F32)],
        compiler_params=_cparams(1),
        name="post_mix",
    )(xa, t, wo, g_norm, g_norm, modt, modt, modt, wr_p, br_p.reshape(1, LANE))


def _segment_chunks(seg_rows, tm):
    slack = seg_rows - TOP_K * tm
    chunks = [(c * tm, tm) for c in range(TOP_K)]
    return chunks + [(TOP_K * tm, slack // 2), (TOP_K * tm + slack // 2, slack - slack // 2)]


def _position_onehot(pos, values, first_row, rows, tm):
    col = (lax.broadcasted_iota(jnp.int32, (tm, rows), 1) + first_row).astype(F32)
    out = jnp.zeros((tm, rows), F32)
    for p, v in zip(pos, values):
        out = jnp.where(p == col, v, out)
    return out


def _sort_kernel(used_ref, h2_ref, ri_ref, xs_ref):
    tm = h2_ref.shape[0]
    used = used_ref[pl.program_id(0)]
    rif = ri_ref[...].astype(F32)
    pos = [_lane_pick(rif, TOP_K + k) for k in range(TOP_K)]

    def emit(first_row, rows):
        onehot = _position_onehot(pos, [1.0] * TOP_K, first_row, rows, tm)
        out = lax.dot_general(onehot.astype(MXU_DTYPE), h2_ref[...], (((0,), (0,)), ((), ())),
                              preferred_element_type=F32)
        xs_ref[first_row:first_row + rows, :] = out.astype(xs_ref.dtype)

    for first_row, rows in _segment_chunks(xs_ref.shape[0], tm):
        if first_row < TOP_K * tm:
            emit(first_row, rows)
        else:
            pl.when(used > first_row)(functools.partial(emit, first_row, rows))


def _sort_rows(tile_used, h2, ri, tm, n_tiles):
    seg_rows = TOP_K * tm + N_EXPERTS * SEG_ALIGN
    grid_spec = pltpu.PrefetchScalarGridSpec(
        num_scalar_prefetch=1,
        grid=(n_tiles,),
        in_specs=[pl.BlockSpec((tm, D_MODEL), lambda i, used: (i, 0)),
                  pl.BlockSpec((tm, LANE), lambda i, used: (i, 0))],
        out_specs=pl.BlockSpec((seg_rows, D_MODEL), lambda i, used: (i, 0)),
    )
    return pl.pallas_call(
        _sort_kernel,
        grid_spec=grid_spec,
        out_shape=jax.ShapeDtypeStruct((n_tiles * seg_rows, D_MODEL), MXU_DTYPE),
        compiler_params=_cparams(1),
        name="sort_rows",
    )(tile_used, h2, ri)


def _expert_kernel(te_ref, r0_ref, na_ref, seg_ref, src_ref, used_ref, tlo_ref, thi_ref,
                   xs_hbm, wu_ref, bu_ref, wd_ref, bd_ref, ys_hbm,
                   xbuf, ybuf, sem_in, sem_out, *, n_tok_tiles):
    w = pl.program_id(0)
    n_active = na_ref[0]
    tm = ybuf.shape[0]
    nt = n_tok_tiles
    sizes = [s for s in (512, 256, 128, 64, 32, 16) if s <= tm]

    def for_pieces(length, fn):
        off = jnp.int32(0)
        for size in sizes:
            hit = (length & size) != 0

            @pl.when(hit)
            def _(off=off, size=size):
                fn(off, size)

            off = off + jnp.where(hit, size, 0)

    def for_segments(wt, fn):
        e = te_ref[wt]
        r0 = r0_ref[wt]

        def body(t, carry):
            s0 = seg_ref[e * (nt + 1) + t]
            s1 = seg_ref[e * (nt + 1) + t + 1]
            lo = jnp.maximum(s0, r0)
            length = jnp.maximum(jnp.minimum(s1, r0 + tm) - lo, 0)
            src = src_ref[e * nt + t] + (lo - s0)
            dst = lo - r0
            for_pieces(length, lambda off, size: fn(src + off, dst + off, size))
            return carry

        lax.fori_loop(tlo_ref[wt], thi_ref[wt], body, 0)

    def rows_in_use(wt):
        return used_ref[wt]

    def aligned(row):
        return row if isinstance(row, int) else pl.multiple_of(row, SEG_ALIGN)

    def in_copy(hbm_row, buf_row, size, slot):
        return pltpu.make_async_copy(xs_hbm.at[pl.ds(aligned(hbm_row), size)],
                                     xbuf.at[slot, pl.ds(aligned(buf_row), size)], sem_in.at[slot])

    def out_copy(hbm_row, buf_row, size):
        return pltpu.make_async_copy(ybuf.at[pl.ds(aligned(buf_row), size)],
                                     ys_hbm.at[pl.ds(aligned(hbm_row), size)], sem_out.at[0])

    def start_in(wt, slot):
        for_segments(wt, lambda h, b, size: in_copy(h, b, size, slot).start())

    def wait_in(wt, slot):
        for_pieces(rows_in_use(wt), lambda off, size: in_copy(0, 0, size, slot).wait())

    def wait_out(wt):
        for_pieces(rows_in_use(wt), lambda off, size: out_copy(0, 0, size).wait())

    @pl.when(w < n_active)
    def _():
        slot = w % 2

        @pl.when(w == 0)
        def _():
            xbuf[...] = jnp.zeros_like(xbuf)
            start_in(0, 0)

        wait_in(w, slot)

        @pl.when(w + 1 < n_active)
        def _():
            start_in(w + 1, 1 - slot)

        gu = jnp.dot(xbuf[slot], wu_ref[0], preferred_element_type=F32) + bu_ref[0]
        g = jnp.minimum(gu[:, :D_FF], SWIGLU_LIMIT)
        u = jnp.clip(gu[:, D_FF:], -SWIGLU_LIMIT, SWIGLU_LIMIT)
        hid = g * _sigmoid(SWIGLU_ALPHA * g) * (u + 1.0)
        y = jnp.dot(hid.astype(MXU_DTYPE), wd_ref[0], preferred_element_type=F32) + bd_ref[0]

        @pl.when(w > 0)
        def _():
            wait_out(w - 1)

        ybuf[...] = y.astype(ybuf.dtype)
        for_segments(w, lambda h, b, size: out_copy(h, b, size).start())

        @pl.when(w == n_active - 1)
        def _():
            wait_out(w)


def _experts(tables, xs, w_up, b_up, w_down, b_down, n_tok_tiles):
    tile_expert = tables[0]
    tm = MOE_TILE
    grid_spec = pltpu.PrefetchScalarGridSpec(
        num_scalar_prefetch=len(tables),
        grid=(tile_expert.shape[0],),
        in_specs=[pl.BlockSpec(memory_space=pl.ANY),
                  pl.BlockSpec((1, D_MODEL, 2 * D_FF), lambda t, te, *_: (te[t], 0, 0)),
                  pl.BlockSpec((1, 1, 2 * D_FF), lambda t, te, *_: (te[t], 0, 0)),
                  pl.BlockSpec((1, D_FF, D_MODEL), lambda t, te, *_: (te[t], 0, 0)),
                  pl.BlockSpec((1, 1, D_MODEL), lambda t, te, *_: (te[t], 0, 0))],
        out_specs=pl.BlockSpec(memory_space=pl.ANY),
        scratch_shapes=[pltpu.VMEM((2, tm, D_MODEL), xs.dtype), pltpu.VMEM((tm, D_MODEL), xs.dtype),
                        pltpu.SemaphoreType.DMA((2,)), pltpu.SemaphoreType.DMA((1,))],
    )
    return pl.pallas_call(
        functools.partial(_expert_kernel, n_tok_tiles=n_tok_tiles),
        grid_spec=grid_spec,
        out_shape=jax.ShapeDtypeStruct(xs.shape, xs.dtype),
        compiler_params=_cparams(1),
        name="experts",
    )(*tables, xs, w_up, b_up, w_down, b_down)


def _combine_kernel(used_ref, ys_ref, ri_ref, rw_ref, x_ref, g3_ref, gt_ref, xo_ref):
    tm = x_ref.shape[0]
    used = used_ref[pl.program_id(0)]
    rif = ri_ref[...].astype(F32)
    rw = rw_ref[...]
    pos = [_lane_pick(rif, TOP_K + k) for k in range(TOP_K)]
    wgt = [_lane_pick(rw, k) for k in range(TOP_K)]
    f = jnp.zeros(x_ref.shape, F32)
    for c in range(ys_ref.shape[0] // tm):
        scat = _position_onehot(pos, wgt, c * tm, tm, tm)
        row = lax.broadcasted_iota(jnp.int32, (tm, 1), 0) + c * tm
        ys = ys_ref[c * tm:(c + 1) * tm, :]
        ys = jnp.where(row < used, ys, jnp.zeros_like(ys))
        f = f + jnp.dot(scat.astype(MXU_DTYPE), ys, preferred_element_type=F32)
    xo_ref[...] = x_ref[...] + gt_ref[0] * _rms(f, g3_ref[0])


def _combine(tile_used, ys, ri, rw, xa, g_norm, modt, tm, n_tiles):
    seg_rows = ys.shape[0] // n_tiles
    grid_spec = pltpu.PrefetchScalarGridSpec(
        num_scalar_prefetch=1,
        grid=(n_tiles,),
        in_specs=[pl.BlockSpec((seg_rows, D_MODEL), lambda i, used: (i, 0)),
                  pl.BlockSpec((tm, LANE), lambda i, used: (i, 0)),
                  pl.BlockSpec((tm, LANE), lambda i, used: (i, 0)),
                  pl.BlockSpec((tm, D_MODEL), lambda i, used: (i, 0)),
                  pl.BlockSpec((1, 1, D_MODEL), lambda i, used: (3, 0, 0)),
                  pl.BlockSpec((1, 1, D_MODEL), lambda i, used: (i, 0, 5))],
        out_specs=pl.BlockSpec((tm, D_MODEL), lambda i, used: (i, 0)),
    )
    return pl.pallas_call(
        _combine_kernel,
        grid_spec=grid_spec,
        out_shape=jax.ShapeDtypeStruct((n_tiles * tm, D_MODEL), F32),
        compiler_params=_cparams(1),
        name="combine",
    )(tile_used, ys, ri, rw, xa, g_norm, modt)


def _swap16(idx):
    g = np.asarray(idx).reshape(-1, 2, ROPE_FREQS)
    return g[:, ::-1, :].reshape(-1)


def _in_proj_columns():
    sizes = (Q_LORA, KV_LORA, QK_ROPE, M_HEADS * M_QK, M_HEADS * M_QK, M_HEADS * M_V, M_HEADS * M_V,
             2 * 2 * M_HEADS, CONV_WIDTH, CONV_WIDTH, CONV_WIDTH, 3 * D_MODEL)
    starts = np.concatenate([[0], np.cumsum(sizes)[:-1]])
    seg = lambda k: np.arange(starts[k], starts[k] + sizes[k])
    cq, ckv, kr, mq, mk, mv, mo, mif, cu, cb, cc, cg = (seg(k) for k in range(12))
    mqk = np.concatenate([np.concatenate([mq[h * M_QK:(h + 1) * M_QK], mk[h * M_QK:(h + 1) * M_QK]])
                          for h in range(M_HEADS)])
    main = np.concatenate([cq, ckv, mqk, mv, mo, cu, cb, cc, cg])
    pad = lambda n: np.full((n,), -1)
    special = np.concatenate([kr, kr, _swap16(kr), _swap16(kr), mif, pad(LANE - mif.size), pad(LANE)])
    cols = np.concatenate([main, special])
    assert cols.size == (N_MAIN_TILES + 1) * COL_TILE
    return cols


def _take_cols(w, cols, dtype):
    cols = np.asarray(cols)
    breaks = np.flatnonzero((np.diff(cols) != 1) & ~((cols[1:] < 0) & (cols[:-1] < 0))) + 1
    parts = []
    for run in np.split(cols, breaks):
        if run[0] < 0:
            parts.append(jnp.zeros(w.shape[:-1] + (run.size,), dtype))
        else:
            parts.append(w[..., int(run[0]):int(run[-1]) + 1].astype(dtype))
    return jnp.concatenate(parts, axis=-1)


def _permute_rows_kernel(w_ref, *out_refs, row_groups):
    w = w_ref[0]
    for o_ref, rows in zip(out_refs, row_groups):
        o_ref[0] = _take_rows(w, rows, o_ref.dtype)


def _take_rows(w, rows, dtype):
    rows = np.asarray(rows)
    breaks = np.flatnonzero((np.diff(rows) != 1) & ~((rows[1:] < 0) & (rows[:-1] < 0))) + 1
    parts = []
    for run in np.split(rows, breaks):
        if run[0] < 0:
            parts.append(jnp.zeros((run.size, w.shape[1]), dtype))
        else:
            parts.append(w[int(run[0]):int(run[-1]) + 1, :].astype(dtype))
    return jnp.concatenate(parts, axis=0)


def _permute_rows(wt, row_groups, dtype):
    depth, n, k = wt.shape
    bk = 256
    return pl.pallas_call(
        functools.partial(_permute_rows_kernel, row_groups=row_groups),
        grid=(depth, k // bk),
        in_specs=[pl.BlockSpec((1, n, bk), lambda l, i: (l, 0, i))],
        out_specs=[pl.BlockSpec((1, len(r), bk), lambda l, i: (l, 0, i)) for r in row_groups],
        out_shape=[jax.ShapeDtypeStruct((depth, len(r), k), dtype) for r in row_groups],
        compiler_params=_cparams(2),
        name="permute_rows",
    )(wt)


def _q_up_columns():
    per = QK_NOPE + QK_ROPE
    nope = np.concatenate([np.arange(h * per, h * per + QK_NOPE) for h in range(MLA_HEADS)])
    rope = np.concatenate([np.arange(h * per + QK_NOPE, (h + 1) * per) for h in range(MLA_HEADS)])
    return np.concatenate([nope, rope, _swap16(rope)])


def _kv_up_columns():
    per = QK_NOPE + V_HEAD
    kn = np.concatenate([np.arange(h * per, h * per + QK_NOPE) for h in range(MLA_HEADS)])
    vv = np.concatenate([np.arange(h * per + QK_NOPE, (h + 1) * per) for h in range(MLA_HEADS)])
    return np.concatenate([kn, vv])


def _rope_tables(T, rt):
    pos = jnp.arange(T)
    inv_freq = ROPE_THETA ** (-jnp.arange(ROPE_FREQS, dtype=F32) / ROPE_FREQS)
    ang_r = (pos // GRID_W).astype(F32)[:, None] * inv_freq
    ang_c = (pos % GRID_W).astype(F32)[:, None] * inv_freq
    cos64 = jnp.concatenate([jnp.cos(ang_r), jnp.cos(ang_r), jnp.cos(ang_c), jnp.cos(ang_c)], axis=1)
    sin64 = jnp.concatenate([-jnp.sin(ang_r), jnp.sin(ang_r), -jnp.sin(ang_c), jnp.sin(ang_c)], axis=1)
    cos_t = jnp.concatenate([jnp.tile(cos64, (1, 2)), jnp.ones((rt, LANE), F32)], axis=0)
    sin_t = jnp.concatenate([jnp.tile(sin64, (1, 2)), jnp.zeros((rt, LANE), F32)], axis=0)
    return cos_t, sin_t


def _prefix_sums(a):
    n = a.shape[-1]
    keep = jnp.arange(n)[:, None] <= jnp.arange(n)[None, :]
    return jnp.sum(jnp.where(keep, a[..., :, None], 0), axis=-2)


def _moe_tables(cnt, n_tiles, tok_tile):
    tm = MOE_TILE
    seg_rows = TOP_K * tok_tile + N_EXPERTS * SEG_ALIGN
    n_work = n_tiles * seg_rows // tm + N_EXPERTS
    counts = cnt[:n_tiles, 0, :N_EXPERTS].astype(jnp.int32)
    padded = (counts + SEG_ALIGN - 1) // SEG_ALIGN * SEG_ALIGN
    in_tile = _prefix_sums(padded) - padded
    src_row = (jnp.arange(n_tiles, dtype=jnp.int32)[:, None] * seg_rows + in_tile).T
    seg_start = jnp.concatenate([jnp.zeros((N_EXPERTS, 1), jnp.int32), _prefix_sums(padded.T)], axis=1)
    group_len = seg_start[:, -1]
    work_end = _prefix_sums((group_len + tm - 1) // tm)
    work = jnp.arange(n_work, dtype=jnp.int32)
    tile_expert = jnp.minimum(jnp.sum(work_end[None, :] <= work[:, None], axis=1), N_EXPERTS - 1)
    first_work = jnp.concatenate([jnp.zeros((1,), jnp.int32), work_end[:-1]])
    tile_row0 = (work - first_work[tile_expert]) * tm
    used = jnp.clip(group_len[tile_expert] - tile_row0, 0, tm)
    starts = seg_start[tile_expert]
    t_lo = jnp.sum(starts[:, 1:] <= tile_row0[:, None], axis=1)
    t_hi = jnp.sum(starts[:, :-1] < (tile_row0 + tm)[:, None], axis=1)
    i32 = lambda a: a.astype(jnp.int32)
    tables = (i32(tile_expert), i32(tile_row0), i32(work_end[-1:]), i32(seg_start.reshape(-1)),
              i32(src_row.reshape(-1)), i32(used), i32(t_lo), i32(t_hi))
    return tables, i32(jnp.sum(padded, axis=1))


def _layer(xa, mods, B, T, Tc, rt, tm, update_ctx, p, cos_t, sin_t, layer):
    rows = xa.shape[0]
    n_lat = B * T
    lat_rt, all_rt = n_lat // rt, rows // rt
    lat_tm, all_tm = n_lat // tm, rows // tm
    cond_rt = np.concatenate([np.arange(lat_rt) // (T // rt), np.full((all_rt - lat_rt,), B)])
    cond_tm = np.concatenate([np.arange(lat_tm) // (T // tm), np.full((all_tm - lat_tm,), B)])
    modt_rt = mods[cond_rt][:, None, :]
    modt_tm = mods[cond_tm][:, None, :]
    g_norm = p["g_norm"].reshape(4, 1, D_MODEL)

    z, zs = _inproj(xa, g_norm, modt_rt, p["w_in_main"], p["w_in_special"], layer, rt)
    q, k, v = _mla_project(z, zs, p["g_q_lora"], p["g_kv_lora"], p["w_q_up"], p["w_kv_up"],
                           cos_t, sin_t, rt, lat_rt, T // rt)
    a, w_up_c, w_down_c = _attention_latent(q, k, v, p["w_up"], p["w_down"], layer, p["depth"],
                                            B, T, Tc, min(ATTN_Q_TILE, T), min(ATTN_KV_CHUNK, T),
                                            rows if update_ctx else n_lat)
    w_up_c = w_up_c.reshape(N_EXPERTS, D_MODEL, 2 * D_FF)
    w_down_c = w_down_c.reshape(N_EXPERTS, D_FF, D_MODEL)
    if update_ctx:
        a = _attention_ctx(q, k, v, a, B, T, Tc)
    m = _mlstm(z, zs, p["b_if"], p["g_mlstm"], B, T, Tc)
    n_rt = all_rt if update_ctx else lat_rt
    n_tm = all_tm if update_ctx else lat_tm
    t = _merge(a, m, z, p["w_proj_a"], p["w_proj_b"], p["w_proj_c"], p["w_conv"], p["b_conv"],
               rt, n_rt, lat_rt, T // rt, T, Tc)
    xa, h2, ri, rw, cnt = _post(t, p["w_o"], xa, g_norm, modt_tm, p["w_router"], p["b_router"], tm, n_tm)
    tables, tile_used = _moe_tables(cnt, n_tm, tm)
    xs = _sort_rows(tile_used, h2, ri, tm, n_tm)
    ys = _experts(tables, xs, w_up_c, p["b_up"], w_down_c, p["b_down"], n_tm)
    return _combine(tile_used, ys, ri, rw, xa, g_norm, modt_tm, tm, n_tm)


def kernel(x, c, ctx, c_ctx, w_mod, b_mod, g_norm, w_in, g_q_lora, g_kv_lora, w_q_up, w_kv_up, b_if, g_mlstm,
           w_conv, b_conv, w_proj_a, w_proj_b, w_proj_c, w_o, w_router, b_router, w_up, b_up, w_down, b_down):
    B, T, _ = x.shape
    Tc = ctx.shape[1]
    depth = w_mod.shape[0]
    rt = min(1024, T)
    tm = min(512, T)
    assert T % rt == 0 and (B * Tc) % rt == 0 and (B * Tc) % tm == 0
    assert T & (T - 1) == 0 and Tc & (Tc - 1) == 0 and T % GRID_W == 0

    xa = jnp.concatenate([x.reshape(B * T, D_MODEL), ctx.reshape(B * Tc, D_MODEL)], axis=0)
    cond = jnp.zeros((8, D_MODEL), F32).at[:B].set(c).at[B].set(c_ctx)
    cos_t, sin_t = _rope_tables(T, rt)
    cast = lambda w: w.astype(MXU_DTYPE)
    in_cols = _in_proj_columns()
    n_main = N_MAIN_TILES * COL_TILE
    w_in_main, w_in_special = _permute_rows(jnp.swapaxes(w_in, 1, 2), (in_cols[:n_main], in_cols[n_main:]),
                                            MXU_DTYPE)
    w_q_p = _take_cols(w_q_up, _q_up_columns(), MXU_DTYPE)
    w_kv_p = _take_cols(w_kv_up, _kv_up_columns(), MXU_DTYPE)
    w_up_r = w_up.reshape(depth * N_EXPERTS * D_MODEL, 2 * D_FF)
    w_down_r = w_down.reshape(depth * N_EXPERTS * D_FF, D_MODEL)
    wa_c, wb_c, wc_c, wo_c = cast(w_proj_a), cast(w_proj_b), cast(w_proj_c), cast(w_o)
    w_router_p = jnp.pad(w_router, ((0, 0), (0, 0), (0, LANE - N_EXPERTS)))
    b_router_p = jnp.pad(b_router, ((0, 0), (0, LANE - N_EXPERTS)), constant_values=NEG_BIG)
    b_if_p = jnp.pad(b_if.reshape(depth, 1, -1), ((0, 0), (0, 0), (0, LANE - 4 * M_HEADS)))
    mods = _adaln(cond, w_mod, b_mod)
    for l in range(depth):
        p = dict(
            g_norm=g_norm[l], w_in_main=w_in_main, w_in_special=w_in_special,
            g_q_lora=g_q_lora[l], g_kv_lora=g_kv_lora[l],
            w_q_up=w_q_p[l], w_kv_up=w_kv_p[l], b_if=b_if_p[l],
            g_mlstm=g_mlstm[l], w_conv=w_conv[l], b_conv=b_conv[l],
            w_proj_a=wa_c[l], w_proj_b=wb_c[l], w_proj_c=wc_c[l], w_o=wo_c[l],
            w_router=w_router_p[l], b_router=b_router_p[l],
            depth=depth, w_up=w_up_r, b_up=b_up[l].reshape(N_EXPERTS, 1, -1),
            w_down=w_down_r, b_down=b_down[l].reshape(N_EXPERTS, 1, -1),
        )
        xa = _layer(xa, mods[l], B, T, Tc, rt, tm, l < depth - 1, p, cos_t, sin_t, l)
    return xa.reshape(B, T, D_MODEL)
```
